```python
import math
import jax, jax.numpy as jnp
from jax import lax
import numpy as np

D_MODEL = 1024
BATCH = 8
SEQ = 4096
DEPTH = 1

CHUNK = 64
D_MIX = D_MODEL
S5_WIDTH = D_MIX // 2
S5_GROUP_CH = 16
S5_GROUPS = S5_WIDTH // S5_GROUP_CH
S5_STATE = 64
CONV_WIDTH = D_MIX - S5_WIDTH
CONV_HEAD_DIM = 64
CONV_HEADS = CONV_WIDTH // CONV_HEAD_DIM
CONV_K = 31
D_FF = 2816
IN_COLS = S5_WIDTH + 2 * CONV_WIDTH
EPS = 1e-6

kernel_name = "hybrid_s5_conformer_conv_macaron"


def rms_norm(x, g):
    xf = x.astype(jnp.float32)
    y = xf * lax.rsqrt(jnp.mean(xf * xf, axis=-1, keepdims=True) + EPS)
    return (y * g.astype(jnp.float32)).astype(x.dtype)


def swiglu_ffn(h, w_gate, w_up, w_down):
    return (jax.nn.silu(h @ w_gate) * (h @ w_up)) @ w_down


def _complex_affine_combine(left, right):
    a_re_i, a_im_i, b_re_i, b_im_i = left
    a_re_j, a_im_j, b_re_j, b_im_j = right
    a_re = a_re_j * a_re_i - a_im_j * a_im_i
    a_im = a_re_j * a_im_i + a_im_j * a_re_i
    b_re = a_re_j * b_re_i - a_im_j * b_im_i + b_re_j
    b_im = a_re_j * b_im_i + a_im_j * b_re_i + b_im_j
    return (a_re, a_im, b_re, b_im)


def s5_mixer(u, lam_re, lam_im, log_dt, b_re, b_im, c_re, c_im, d_skip, w_glu, b_glu):
    bsz, seq_len, _ = u.shape
    uf = u.astype(jnp.float32).reshape(bsz, seq_len, S5_GROUPS, S5_GROUP_CH)
    lr = lam_re.astype(jnp.float32)
    li = lam_im.astype(jnp.float32)
    dt = jnp.exp(log_dt.astype(jnp.float32))[:, None]
    mag = jnp.exp(lr * dt)
    abar_re = mag * jnp.cos(li * dt)
    abar_im = mag * jnp.sin(li * dt)
    den = lr * lr + li * li
    num_re = abar_re - 1.0
    num_im = abar_im
    f_re = ((num_re * lr + num_im * li) / den)[..., None]
    f_im = ((num_im * lr - num_re * li) / den)[..., None]
    br = b_re.astype(jnp.float32)
    bi = b_im.astype(jnp.float32)
    bbar_re = f_re * br - f_im * bi
    bbar_im = f_re * bi + f_im * br
    bu_re = jnp.einsum('blgc,gpc->blgp', uf, bbar_re)
    bu_im = jnp.einsum('blgc,gpc->blgp', uf, bbar_im)
    a_re = jnp.broadcast_to(abar_re[None, None], (1, seq_len, S5_GROUPS, S5_STATE))
    a_im = jnp.broadcast_to(abar_im[None, None], (1, seq_len, S5_GROUPS, S5_STATE))
    _, _, s_re, s_im = lax.associative_scan(_complex_affine_combine,
                                            (a_re, a_im, bu_re, bu_im), axis=1)
    y = (jnp.einsum('blgp,gcp->blgc', s_re, c_re.astype(jnp.float32))
         - jnp.einsum('blgp,gcp->blgc', s_im, c_im.astype(jnp.float32)))
    y = y + d_skip.astype(jnp.float32).reshape(S5_GROUPS, S5_GROUP_CH) * uf
    y = jax.nn.gelu(y.reshape(bsz, seq_len, S5_WIDTH)).astype(u.dtype)
    return y * jax.nn.sigmoid(y @ w_glu + b_glu)


def conv_module_mixer(v, w_dw, b_dw, ln_g, ln_b):
    bsz, seq_len, _ = v.shape
    z = v[..., :CONV_WIDTH] * jax.nn.sigmoid(v[..., CONV_WIDTH:])
    z = lax.conv_general_dilated(
        z, w_dw[:, None, :], window_strides=(1,), padding=[(CONV_K - 1, 0)],
        dimension_numbers=('NWC', 'WIO', 'NWC'), feature_group_count=CONV_WIDTH) + b_dw
    zf = z.astype(jnp.float32).reshape(bsz, seq_len, CONV_HEADS, CONV_HEAD_DIM)
    mu = jnp.mean(zf, axis=-1, keepdims=True)
    var = jnp.mean(jnp.square(zf - mu), axis=-1, keepdims=True)
    zn = ((zf - mu) * lax.rsqrt(var + EPS)).reshape(bsz, seq_len, CONV_WIDTH)
    zn = zn * ln_g.astype(jnp.float32) + ln_b.astype(jnp.float32)
    return jax.nn.silu(zn).astype(v.dtype)


def setup_inputs(seed: int = 0) -> dict:
    key = jax.random.key(seed)
    ks = jax.random.split(key, 32)
    f32 = jnp.float32

    def nrm(k, shape, scale):
        return jax.random.normal(k, shape, f32) * scale

    def gain(k, shape):
        return 1.0 + 0.02 * jax.random.normal(k, shape, f32)

    L = DEPTH
    n_idx = jnp.arange(S5_STATE, dtype=f32)
    lam_re = -0.5 * (1.0 + 0.05 * jax.random.normal(ks[10], (L, S5_GROUPS, S5_STATE), f32))
    lam_im = jnp.broadcast_to(math.pi * n_idx, (L, S5_GROUPS, S5_STATE)) \
        + 0.01 * jax.random.normal(ks[11], (L, S5_GROUPS, S5_STATE), f32)
    log_dt = jax.random.uniform(ks[12], (L, S5_GROUPS), f32, math.log(1e-3), math.log(1e-1))
    return {
        "x": jax.random.normal(ks[0], (BATCH, SEQ, D_MODEL), f32),
        "ffn1_norm": gain(ks[1], (L, D_MODEL)),
        "ffn1_w_gate": nrm(ks[2], (L, D_MODEL, D_FF), D_MODEL ** -0.5),
        "ffn1_w_up": nrm(ks[3], (L, D_MODEL, D_FF), D_MODEL ** -0.5),
        "ffn1_w_down": nrm(ks[4], (L, D_FF, D_MODEL), D_FF ** -0.5),
        "mix_norm": gain(ks[5], (L, D_MODEL)),
        "w_in": nrm(ks[6], (L, D_MODEL, IN_COLS), D_MODEL ** -0.5),
        "s5_lam_re": lam_re,
        "s5_lam_im": lam_im,
        "s5_log_dt": log_dt,
        "s5_b_re": nrm(ks[13], (L, S5_GROUPS, S5_STATE, S5_GROUP_CH), (2 * S5_GROUP_CH) ** -0.5),
        "s5_b_im": nrm(ks[14], (L, S5_GROUPS, S5_STATE, S5_GROUP_CH), (2 * S5_GROUP_CH) ** -0.5),
        "s5_c_re": nrm(ks[15], (L, S5_GROUPS, S5_GROUP_CH, S5_STATE), (2 * S5_STATE) ** -0.5),
        "s5_c_im": nrm(ks[16], (L, S5_GROUPS, S5_GROUP_CH, S5_STATE), (2 * S5_STATE) ** -0.5),
        "s5_d": gain(ks[17], (L, S5_WIDTH)),
        "s5_w_glu": nrm(ks[18], (L, S5_WIDTH, S5_WIDTH), S5_WIDTH ** -0.5),
        "s5_b_glu": nrm(ks[19], (L, S5_WIDTH), 0.02),
        "conv_w_dw": nrm(ks[20], (L, CONV_K, CONV_WIDTH), CONV_K ** -0.5),
        "conv_b_dw": nrm(ks[21], (L, CONV_WIDTH), 0.02),
        "conv_ln_g": gain(ks[22], (L, CONV_WIDTH)),
        "conv_ln_b": nrm(ks[23], (L, CONV_WIDTH), 0.02),
        "w_out": nrm(ks[24], (L, D_MIX, D_MODEL), D_MIX ** -0.5),
        "ffn2_norm": gain(ks[25], (L, D_MODEL)),
        "ffn2_w_gate": nrm(ks[26], (L, D_MODEL, D_FF), D_MODEL ** -0.5),
        "ffn2_w_up": nrm(ks[27], (L, D_MODEL, D_FF), D_MODEL ** -0.5),
        "ffn2_w_down": nrm(ks[28], (L, D_FF, D_MODEL), D_FF ** -0.5),
        "final_norm": gain(ks[29], (D_MODEL,)),
    }


def reference(x, ffn1_norm, ffn1_w_gate, ffn1_w_up, ffn1_w_down, mix_norm, w_in,
              s5_lam_re, s5_lam_im, s5_log_dt, s5_b_re, s5_b_im, s5_c_re, s5_c_im,
              s5_d, s5_w_glu, s5_b_glu, conv_w_dw, conv_b_dw, conv_ln_g, conv_ln_b,
              w_out, ffn2_norm, ffn2_w_gate, ffn2_w_up, ffn2_w_down, final_norm):
    for l in range(DEPTH):
        h = rms_norm(x, ffn1_norm[l])
        x = x + 0.5 * swiglu_ffn(h, ffn1_w_gate[l], ffn1_w_up[l], ffn1_w_down[l])
        h = rms_norm(x, mix_norm[l])
        u = h @ w_in[l]
        y_s5 = s5_mixer(u[..., :S5_WIDTH], s5_lam_re[l], s5_lam_im[l], s5_log_dt[l],
                        s5_b_re[l], s5_b_im[l], s5_c_re[l], s5_c_im[l], s5_d[l],
                        s5_w_glu[l], s5_b_glu[l])
        y_conv = conv_module_mixer(u[..., S5_WIDTH:], conv_w_dw[l], conv_b_dw[l],
                                   conv_ln_g[l], conv_ln_b[l])
        x = x + jnp.concatenate([y_s5, y_conv], axis=-1) @ w_out[l]
        h = rms_norm(x, ffn2_norm[l])
        x = x + 0.5 * swiglu_ffn(h, ffn2_w_gate[l], ffn2_w_up[l], ffn2_w_down[l])
    return rms_norm(x, final_norm)
```

```python
import functools
import math

import jax
import jax.numpy as jnp
from jax import lax
from jax.experimental import pallas as pl
from jax.experimental.pallas import tpu as pltpu

D_MODEL = 1024
D_FF = 2816
S5_WIDTH = 512
S5_GROUP_CH = 16
S5_GROUPS = S5_WIDTH // S5_GROUP_CH
S5_STATE = 64
CONV_WIDTH = 512
CONV_HEAD_DIM = 64
CONV_K = 31
IN_COLS = S5_WIDTH + 2 * CONV_WIDTH
EPS = 1e-6

S5_CHUNK = 64
S5_LANES = S5_CHUNK * S5_GROUP_CH
CONV_HALO = 32

ROW_TILE = 512
FF_SPLITS = ((0, 1536), (1536, 2816))
CONV_TILE = 512
VMEM_LIMIT = 56 * 1024 * 1024

BF16 = jnp.bfloat16
F32 = jnp.float32


def _rms_norm(x, g):
    return x * lax.rsqrt(jnp.mean(x * x, axis=-1, keepdims=True) + EPS) * g


def _dot(a, b):
    return jnp.dot(a, b, preferred_element_type=F32)


def _swiglu(h, wg_ref, wu_ref, wd_ref):
    acc = None
    for lo, hi in FF_SPLITS:
        g = _dot(h, wg_ref[:, lo:hi])
        u = _dot(h, wu_ref[:, lo:hi])
        a = (g * jax.nn.sigmoid(g) * u).astype(BF16)
        part = _dot(a, wd_ref[lo:hi, :])
        acc = part if acc is None else acc + part
    return acc


def _ffn1_in_proj_kernel(x_ref, g1_ref, wg_ref, wu_ref, wd_ref, gm_ref, win_ref,
                         x1_ref, u_ref, z_ref):
    x = x_ref[...]
    h = _rms_norm(x, g1_ref[...]).astype(BF16)
    x1 = x + 0.5 * _swiglu(h, wg_ref, wu_ref, wd_ref)
    x1_ref[...] = x1
    h2 = _rms_norm(x1, gm_ref[...]).astype(BF16)
    uv = _dot(h2, win_ref[...])
    u_ref[...] = uv[:, :S5_WIDTH]
    z_ref[...] = uv[:, S5_WIDTH:S5_WIDTH + CONV_WIDTH] * jax.nn.sigmoid(uv[:, S5_WIDTH + CONV_WIDTH:])


def _s5_mix_kernel(u_ref, m_ref, win_ref, wout_ref, a1_ref, a2_ref, d_ref, y_ref,
                   sprev_ref, *, n_chunks, batch):
    u = u_ref[...]
    ub = u.astype(BF16)
    y = _dot(ub, m_ref[...])
    e = _dot(ub, win_ref[...])
    a1 = a1_ref[...]
    a2 = a2_ref[...]
    s = jnp.zeros((batch, 2 * S5_STATE), F32)
    s_sw = jnp.zeros((batch, 2 * S5_STATE), F32)
    for c in range(n_chunks):
        rows = slice(c * batch, (c + 1) * batch)
        sprev_ref[rows, :] = s
        e_c = e[rows, :]
        s, s_sw = (a1 * s + a2 * s_sw + e_c[:, :2 * S5_STATE],
                   a1 * s_sw - a2 * s + e_c[:, 2 * S5_STATE:])
    y = y + _dot(sprev_ref[...].astype(BF16), wout_ref[...])
    y = y + d_ref[...] * u
    y_ref[...] = jax.nn.gelu(y, approximate=True)


def _conv_mix_kernel(z_ref, halo_ref, w_ref, b_ref, avg_ref, g_ref, beta_ref, o_ref, buf_ref):
    i = pl.program_id(1)
    tile = z_ref.shape[0]

    @pl.when(i == 0)
    def _():
        buf_ref[0:CONV_HALO, :] = jnp.zeros((CONV_HALO, CONV_WIDTH), F32)

    @pl.when(i > 0)
    def _():
        buf_ref[0:CONV_HALO, :] = halo_ref[...]

    buf_ref[CONV_HALO:, :] = z_ref[...]
    first = CONV_HALO - (CONV_K - 1)
    acc = jnp.broadcast_to(b_ref[...], (tile, CONV_WIDTH))
    for j in range(CONV_K):
        acc = acc + w_ref[j:j + 1, :] * buf_ref[first + j:first + j + tile, :]
    avg = avg_ref[...]
    mu = _dot(acc.astype(BF16), avg)
    d = acc - mu
    var = _dot((d * d).astype(BF16), avg)
    zn = d * lax.rsqrt(var + EPS) * g_ref[...] + beta_ref[...]
    o_ref[...] = zn * jax.nn.sigmoid(zn)


def _out_ffn2_norm_kernel(x1_ref, ys_ref, yc_ref, wglu_ref, bglu_ref, wo_s_ref, wo_c_ref,
                          g2_ref, wg_ref, wu_ref, wd_ref, gf_ref, o_ref):
    ys = ys_ref[...]
    gate = jax.nn.sigmoid(_dot(ys.astype(BF16), wglu_ref[...]) + bglu_ref[...])
    mix = _dot((ys * gate).astype(BF16), wo_s_ref[...]) + _dot(yc_ref[...].astype(BF16), wo_c_ref[...])
    x2 = x1_ref[...] + mix
    h = _rms_norm(x2, g2_ref[...]).astype(BF16)
    x3 = x2 + 0.5 * _swiglu(h, wg_ref, wu_ref, wd_ref)
    o_ref[...] = _rms_norm(x3, gf_ref[...])


def _resident(shape):
    nd = len(shape)
    return pl.BlockSpec(shape, lambda *_: (0,) * nd, pipeline_mode=pl.Buffered(1))


def _rows(tile, width):
    return pl.BlockSpec((tile, width), lambda i: (i, 0))


def _s5_tables(lam_re, lam_im, log_dt, b_re, b_im, c_re, c_im, d_skip):
    hp = lax.Precision.HIGHEST
    T, C, P, G = S5_CHUNK, S5_GROUP_CH, S5_STATE, S5_GROUPS
    dt = jnp.exp(log_dt)[:, None]
    mag = jnp.exp(lam_re * dt)
    abar_re = mag * jnp.cos(lam_im * dt)
    abar_im = mag * jnp.sin(lam_im * dt)
    den = lam_re * lam_re + lam_im * lam_im
    num_re = abar_re - 1.0
    num_im = abar_im
    f_re = ((num_re * lam_re + num_im * lam_im) / den)[..., None]
    f_im = ((num_im * lam_re - num_re * lam_im) / den)[..., None]
    bb_re = f_re * b_re - f_im * b_im
    bb_im = f_re * b_im + f_im * b_re
    j = jnp.arange(T + 1, dtype=F32)[None, :, None]
    pmag = jnp.exp(j * (lam_re * dt)[:, None, :])
    pang = j * (lam_im * dt)[:, None, :]
    p_re = pmag * jnp.cos(pang)
    p_im = pmag * jnp.sin(pang)
    ab_re = p_re[..., None] * bb_re[:, None] - p_im[..., None] * bb_im[:, None]
    ab_im = p_re[..., None] * bb_im[:, None] + p_im[..., None] * bb_re[:, None]
    kern = (jnp.einsum('gop,gjpi->gjoi', c_re, ab_re, precision=hp)
            - jnp.einsum('gop,gjpi->gjoi', c_im, ab_im, precision=hp))
    t_idx = jnp.arange(T)
    lag = t_idx[None, :] - t_idx[:, None]
    toe = jnp.where((lag >= 0)[None, :, :, None, None], kern[:, jnp.clip(lag, 0, T)], 0.0)
    toe = toe.transpose(0, 1, 4, 2, 3).reshape(G, T * C, T * C)
    w_re = ab_re[:, T - 1 - t_idx].transpose(0, 1, 3, 2).reshape(G, T * C, P)
    w_im = ab_im[:, T - 1 - t_idx].transpose(0, 1, 3, 2).reshape(G, T * C, P)
    w_in = jnp.concatenate([w_re, w_im, w_im, w_re], axis=-1)
    q_re = p_re[:, 1:, None, :] * c_re[:, None] - p_im[:, 1:, None, :] * c_im[:, None]
    q_im = p_re[:, 1:, None, :] * c_im[:, None] + p_im[:, 1:, None, :] * c_re[:, None]
    w_out = jnp.concatenate([q_re, -q_im], axis=-1).reshape(G, T * C, 2 * P).transpose(0, 2, 1)
    a1 = jnp.concatenate([p_re[:, T], p_re[:, T]], axis=-1)[:, None, :]
    a2 = jnp.concatenate([-p_im[:, T], p_im[:, T]], axis=-1)[:, None, :]
    d_lanes = jnp.tile(d_skip.reshape(G, 1, C), (1, T, 1)).reshape(G, 1, T * C)
    return toe.astype(BF16), w_in.astype(BF16), w_out.astype(BF16), a1, a2, d_lanes


def kernel(x, ffn1_norm, ffn1_w_gate, ffn1_w_up, ffn1_w_down, mix_norm, w_in, s5_lam_re, s5_lam_im, s5_log_dt, s5_b_re, s5_b_im, s5_c_re, s5_c_im, s5_d, s5_w_glu, s5_b_glu, conv_w_dw, conv_b_dw, conv_ln_g, conv_ln_b, w_out, ffn2_norm, ffn2_w_gate, ffn2_w_up, ffn2_w_down, final_norm):
    batch, seq, d_model = x.shape
    depth = ffn1_norm.shape[0]
    assert depth == 1 and d_model == D_MODEL
    n_tok = batch * seq
    assert n_tok % ROW_TILE == 0 and seq % S5_CHUNK == 0 and seq % CONV_TILE == 0
    n_chunks = seq // S5_CHUNK
    l = 0
    params = pltpu.CompilerParams(dimension_semantics=("arbitrary",), vmem_limit_bytes=VMEM_LIMIT)
    row = lambda v: v.reshape(1, -1)

    xt = x.reshape(n_tok, D_MODEL)

    x1, u_s5, z = pl.pallas_call(
        _ffn1_in_proj_kernel,
        grid=(n_tok // ROW_TILE,),
        in_specs=[
            _rows(ROW_TILE, D_MODEL),
            _resident((1, D_MODEL)),
            _resident((D_MODEL, D_FF)),
            _resident((D_MODEL, D_FF)),
            _resident((D_FF, D_MODEL)),
            _resident((1, D_MODEL)),
            _resident((D_MODEL, IN_COLS)),
        ],
        out_specs=[_rows(ROW_TILE, D_MODEL), _rows(ROW_TILE, S5_WIDTH), _rows(ROW_TILE, CONV_WIDTH)],
        out_shape=[
            jax.ShapeDtypeStruct((n_tok, D_MODEL), F32),
            jax.ShapeDtypeStruct((n_tok, S5_WIDTH), F32),
            jax.ShapeDtypeStruct((n_tok, CONV_WIDTH), F32),
        ],
        compiler_params=params,
        name="ffn1_in_proj",
    )(xt, row(ffn1_norm[l]), ffn1_w_gate[l].astype(BF16), ffn1_w_up[l].astype(BF16),
      ffn1_w_down[l].astype(BF16), row(mix_norm[l]), w_in[l].astype(BF16))

    toe, s_in, s_out, a1, a2, d_lanes = _s5_tables(
        s5_lam_re[l], s5_lam_im[l], s5_log_dt[l], s5_b_re[l], s5_b_im[l], s5_c_re[l], s5_c_im[l], s5_d[l])
    u_g = (u_s5.reshape(batch, n_chunks, S5_CHUNK, S5_GROUPS, S5_GROUP_CH)
           .transpose(3, 1, 0, 2, 4).reshape(S5_GROUPS, n_chunks * batch, S5_LANES))
    grp = lambda r, c: pl.BlockSpec((None, r, c), lambda g: (g, 0, 0))
    y_g = pl.pallas_call(
        functools.partial(_s5_mix_kernel, n_chunks=n_chunks, batch=batch),
        grid=(S5_GROUPS,),
        in_specs=[
            grp(n_chunks * batch, S5_LANES),
            grp(S5_LANES, S5_LANES),
            grp(S5_LANES, 4 * S5_STATE),
            grp(2 * S5_STATE, S5_LANES),
            grp(1, 2 * S5_STATE),
            grp(1, 2 * S5_STATE),
            grp(1, S5_LANES),
        ],
        out_specs=grp(n_chunks * batch, S5_LANES),
        out_shape=jax.ShapeDtypeStruct((S5_GROUPS, n_chunks * batch, S5_LANES), F32),
        scratch_shapes=[pltpu.VMEM((n_chunks * batch, 2 * S5_STATE), F32)],
        compiler_params=params,
        name="s5_mix",
    )(u_g, toe, s_in, s_out, a1, a2, d_lanes)
    y_s5 = (y_g.reshape(S5_GROUPS, n_chunks, batch, S5_CHUNK, S5_GROUP_CH)
            .transpose(2, 1, 3, 0, 4).reshape(n_tok, S5_WIDTH))

    head = jnp.arange(CONV_WIDTH) // CONV_HEAD_DIM
    avg = ((head[:, None] == head[None, :]).astype(F32) / CONV_HEAD_DIM).astype(BF16)
    halo_per_tile = CONV_TILE // CONV_HALO
    z3 = z.reshape(batch, seq, CONV_WIDTH)
    const2 = lambda shape: pl.BlockSpec(shape, lambda b, i: (0, 0))
    y_conv = pl.pallas_call(
        _conv_mix_kernel,
        grid=(batch, seq // CONV_TILE),
        in_specs=[
            pl.BlockSpec((None, CONV_TILE, CONV_WIDTH), lambda b, i: (b, i, 0)),
            pl.BlockSpec((None, CONV_HALO, CONV_WIDTH),
                         lambda b, i: (b, jnp.maximum(i * halo_per_tile - 1, 0), 0)),
            const2((CONV_K, CONV_WIDTH)),
            const2((1, CONV_WIDTH)),
            const2((CONV_WIDTH, CONV_WIDTH)),
            const2((1, CONV_WIDTH)),
            const2((1, CONV_WIDTH)),
        ],
        out_specs=pl.BlockSpec((None, CONV_TILE, CONV_WIDTH), lambda b, i: (b, i, 0)),
        out_shape=jax.ShapeDtypeStruct((batch, seq, CONV_WIDTH), F32),
        scratch_shapes=[pltpu.VMEM((CONV_HALO + CONV_TILE, CONV_WIDTH), F32)],
        compiler_params=pltpu.CompilerParams(dimension_semantics=("arbitrary", "arbitrary"),
                                             vmem_limit_bytes=VMEM_LIMIT),
        name="conv_mix",
    )(z3, z3, conv_w_dw[l], row(conv_b_dw[l]), avg, row(conv_ln_g[l]), row(conv_ln_b[l]))
    y_conv = y_conv.reshape(n_tok, CONV_WIDTH)

    w_out_b = w_out[l].astype(BF16)
    out = pl.pallas_call(
        _out_ffn2_norm_kernel,
        grid=(n_tok // ROW_TILE,),
        in_specs=[
            _rows(ROW_TILE, D_MODEL),
            _rows(ROW_TILE, S5_WIDTH),
            _rows(ROW_TILE, CONV_WIDTH),
            _resident((S5_WIDTH, S5_WIDTH)),
            _resident((1, S5_WIDTH)),
            _resident((S5_WIDTH, D_MODEL)),
            _resident((CONV_WIDTH, D_MODEL)),
            _resident((1, D_MODEL)),
            _resident((D_MODEL, D_FF)),
            _resident((D_MODEL, D_FF)),
            _resident((D_FF, D_MODEL)),
            _resident((1, D_MODEL)),
        ],
        out_specs=_rows(ROW_TILE, D_MODEL),
        out_shape=jax.ShapeDtypeStruct((n_tok, D_MODEL), F32),
        compiler_params=params,
        name="out_ffn2_norm",
    )(x1, y_s5, y_conv, s5_w_glu[l].astype(BF16), row(s5_b_glu[l]), w_out_b[:S5_WIDTH],
      w_out_b[S5_WIDTH:], row(ffn2_norm[l]), ffn2_w_gate[l].astype(BF16), ffn2_w_up[l].astype(BF16),
      ffn2_w_down[l].astype(BF16), row(final_norm))
    return out.reshape(batch, seq, D_MODEL)
```

```python
import functools

import jax
import jax.numpy as jnp
from jax import lax
from jax.experimental import pallas as pl
from jax.experimental.pallas import tpu as pltpu

D_MODEL = 1024
D_FF = 2816
S5_WIDTH = 512
S5_GROUP_CH = 16
S5_GROUPS = S5_WIDTH // S5_GROUP_CH
S5_STATE = 64
CONV_WIDTH = 512
CONV_HEAD_DIM = 64
CONV_HEADS = CONV_WIDTH // CONV_HEAD_DIM
CONV_K = 31
IN_COLS = S5_WIDTH + 2 * CONV_WIDTH
EPS = 1e-6

LANES = 128
SUBLANES = 8
CHUNK = LANES
S5_LANES = S5_GROUP_CH * CHUNK
TILE_SEQS = 4
ROW_TILE = TILE_SEQS * CHUNK
FF_SPLITS = ((0, 1536), (1536, 2816))
VMEM_LIMIT = 56 * 1024 * 1024

BF16 = jnp.bfloat16
F32 = jnp.float32

_NT = (((1,), (1,)), ((), ()))
_TN = (((0,), (0,)), ((), ()))


def _rms_norm(x, g):
    return x * lax.rsqrt(jnp.mean(x * x, axis=-1, keepdims=True) + EPS) * g


def _dot(a, b):
    return jnp.dot(a, b, preferred_element_type=F32)


def _swiglu(h, wg_ref, wu_ref, wd_ref):
    acc = None
    for lo, hi in FF_SPLITS:
        g = _dot(h, wg_ref[:, lo:hi])
        u = _dot(h, wu_ref[:, lo:hi])
        a = (g * jax.nn.sigmoid(g) * u).astype(BF16)
        part = _dot(a, wd_ref[lo:hi, :])
        acc = part if acc is None else acc + part
    return acc


def _store_slabs(slab_ref, val, first_seq):
    n_ch = slab_ref.shape[0]
    rows = slab_ref.reshape(n_ch * SUBLANES, LANES)
    for ct in range(n_ch // SUBLANES):
        for s in range(TILE_SEQS):
            rows[pl.ds(ct * SUBLANES * SUBLANES + first_seq + s, SUBLANES, stride=SUBLANES), :] = (
                val[ct * SUBLANES:(ct + 1) * SUBLANES, s * LANES:(s + 1) * LANES])


def _load_slabs(slab_ref, first_seq):
    n_ch = slab_ref.shape[0]
    rows = slab_ref.reshape(n_ch * SUBLANES, LANES)
    bands = []
    for ct in range(n_ch // SUBLANES):
        bands.append(jnp.concatenate(
            [rows[pl.ds(ct * SUBLANES * SUBLANES + first_seq + s, SUBLANES, stride=SUBLANES), :]
             for s in range(TILE_SEQS)], axis=1))
    return jnp.concatenate(bands, axis=0)


def _ffn1_in_proj_kernel(x_ref, g1_ref, wg_ref, wu_ref, wd_ref, gm_ref, win_t_ref,
                         x1_ref, us_ref, zs_ref):
    first_seq = pl.program_id(1) * TILE_SEQS
    x = x_ref[...].reshape(ROW_TILE, D_MODEL)
    h = _rms_norm(x, g1_ref[...]).astype(BF16)
    x1 = x + 0.5 * _swiglu(h, wg_ref, wu_ref, wd_ref)
    x1_ref[...] = x1.reshape(TILE_SEQS, CHUNK, D_MODEL)
    h2 = _rms_norm(x1, gm_ref[...]).astype(BF16)
    uv_t = lax.dot_general(win_t_ref[...], h2, _NT, preferred_element_type=F32)
    _store_slabs(us_ref, uv_t[:S5_WIDTH], first_seq)
    z_t = uv_t[S5_WIDTH:S5_WIDTH + CONV_WIDTH] * jax.nn.sigmoid(uv_t[S5_WIDTH + CONV_WIDTH:])
    _store_slabs(zs_ref, z_t, first_seq)


def _s5_mix_kernel(us_ref, f_ref, win_ref, wout_ref, a1_ref, a2_ref, d_ref, ys_ref,
                   m_ref, sprev_ref, *, n_chunks, batch):
    lag_row = lax.broadcasted_iota(jnp.int32, (CHUNK, CHUNK), 0)
    lag_col = lax.broadcasted_iota(jnp.int32, (CHUNK, CHUNK), 1)
    causal = lag_col >= lag_row

    def gen_toeplitz(ci, carry):
        for co in range(S5_GROUP_CH):
            f = jnp.broadcast_to(f_ref[pl.ds(ci * S5_GROUP_CH + co, 1), :], (CHUNK, CHUNK))
            blk = jnp.where(causal, pltpu.roll(f, 0, 1, stride=1, stride_axis=0), 0.0)
            m_ref[pl.ds(pl.multiple_of(ci * CHUNK, CHUNK), CHUNK), co * CHUNK:(co + 1) * CHUNK] = blk.astype(BF16)
        return carry

    lax.fori_loop(0, S5_GROUP_CH, gen_toeplitz, 0)

    u = jnp.concatenate([us_ref[ci] for ci in range(S5_GROUP_CH)], axis=1)
    ub = u.astype(BF16)
    y = _dot(ub, m_ref[...])
    e = _dot(ub, win_ref[...])
    a1 = a1_ref[...]
    a2 = a2_ref[...]
    s = jnp.zeros((batch, 2 * S5_STATE), F32)
    s_sw = jnp.zeros((batch, 2 * S5_STATE), F32)
    for c in range(n_chunks):
        rows = slice(c * batch, (c + 1) * batch)
        sprev_ref[rows, :] = s
        e_c = e[rows, :]
        s, s_sw = (a1 * s + a2 * s_sw + e_c[:, :2 * S5_STATE],
                   a1 * s_sw - a2 * s + e_c[:, 2 * S5_STATE:])
    y = y + _dot(sprev_ref[...].astype(BF16), wout_ref[...])
    y = y + d_ref[...] * u
    y = jax.nn.gelu(y, approximate=True)
    for co in range(S5_GROUP_CH):
        ys_ref[co] = y[:, co * CHUNK:(co + 1) * CHUNK]


def _conv_mix_kernel(zs_ref, q_ref, b_ref, g_ref, beta_ref, o_ref, *, batch):
    n_rows = zs_ref.shape[1]
    row = lax.broadcasted_iota(jnp.int32, (CHUNK, CHUNK), 0)
    col = lax.broadcasted_iota(jnp.int32, (CHUNK, CHUNK), 1)
    lag = row - col
    band_cur = (lag >= -(CONV_K - 1)) & (lag <= 0)
    band_prev = lag >= CHUNK - (CONV_K - 1)

    def conv_channel(ch, carry):
        z = zs_ref[ch]
        z_prev = jnp.concatenate([jnp.zeros((batch, CHUNK), F32), z[:n_rows - batch]], axis=0)
        lhs = jnp.concatenate([z_prev, z], axis=1).astype(BF16)
        taps = jnp.broadcast_to(q_ref[ch], (CHUNK, CHUNK))
        diag = pltpu.roll(taps, 0, 1, stride=1, stride_axis=0)
        toe = jnp.concatenate([jnp.where(band_prev, diag, 0.0), jnp.where(band_cur, diag, 0.0)],
                              axis=0).astype(BF16)
        o_ref[ch] = _dot(lhs, toe) + b_ref[ch]
        return carry

    lax.fori_loop(0, CONV_HEAD_DIM, conv_channel, 0, unroll=2)

    inv_n = 1.0 / CONV_HEAD_DIM
    mu = lax.fori_loop(0, CONV_HEAD_DIM, lambda ch, acc: acc + o_ref[ch],
                       jnp.zeros((n_rows, CHUNK), F32)) * inv_n
    var = lax.fori_loop(0, CONV_HEAD_DIM, lambda ch, acc: acc + jnp.square(o_ref[ch] - mu),
                        jnp.zeros((n_rows, CHUNK), F32)) * inv_n
    rstd = lax.rsqrt(var + EPS)

    def norm_channel(ch, carry):
        zn = (o_ref[ch] - mu) * rstd * g_ref[ch] + beta_ref[ch]
        o_ref[ch] = zn * jax.nn.sigmoid(zn)
        return carry

    lax.fori_loop(0, CONV_HEAD_DIM, norm_channel, 0)


def _out_ffn2_norm_kernel(x1_ref, ys_ref, yc_ref, wglu_t_ref, bglu_ref, wo_s_ref, wo_c_ref,
                          g2_ref, wg_ref, wu_ref, wd_ref, gf_ref, o_ref):
    first_seq = pl.program_id(1) * TILE_SEQS
    ys_t = _load_slabs(ys_ref, first_seq)
    yc_t = _load_slabs(yc_ref, first_seq)
    gate_t = jax.nn.sigmoid(_dot(wglu_t_ref[...], ys_t.astype(BF16)) + bglu_ref[...])
    mix = (lax.dot_general((ys_t * gate_t).astype(BF16), wo_s_ref[...], _TN, preferred_element_type=F32)
           + lax.dot_general(yc_t.astype(BF16), wo_c_ref[...], _TN, preferred_element_type=F32))
    x2 = x1_ref[...].reshape(ROW_TILE, D_MODEL) + mix
    h = _rms_norm(x2, g2_ref[...]).astype(BF16)
    x3 = x2 + 0.5 * _swiglu(h, wg_ref, wu_ref, wd_ref)
    o_ref[...] = _rms_norm(x3, gf_ref[...]).reshape(TILE_SEQS, CHUNK, D_MODEL)


def _resident(shape):
    nd = len(shape)
    return pl.BlockSpec(shape, lambda *_: (0,) * nd, pipeline_mode=pl.Buffered(1))


def _tile_spec():
    return pl.BlockSpec((TILE_SEQS, CHUNK, D_MODEL), lambda c, h: (h, c, 0))


def _slab_spec(n_ch):
    return pl.BlockSpec((n_ch, SUBLANES, LANES), lambda c, h: (0, c, 0))


_DENSE_PARAMS = pltpu.CompilerParams(dimension_semantics=("arbitrary", "arbitrary"),
                                     vmem_limit_bytes=VMEM_LIMIT)
_MIX_PARAMS = pltpu.CompilerParams(dimension_semantics=("arbitrary",), vmem_limit_bytes=VMEM_LIMIT)


def _ffn1_in_proj(x, g1, wg, wu, wd, gm, win_t):
    batch, seq, _ = x.shape
    assert batch == SUBLANES and seq % CHUNK == 0
    n_rows = (seq // CHUNK) * batch
    return pl.pallas_call(
        _ffn1_in_proj_kernel,
        grid=(seq // CHUNK, batch // TILE_SEQS),
        in_specs=[
            _tile_spec(),
            _resident((1, D_MODEL)),
            _resident((D_MODEL, D_FF)),
            _resident((D_MODEL, D_FF)),
            _resident((D_FF, D_MODEL)),
            _resident((1, D_MODEL)),
            _resident((IN_COLS, D_MODEL)),
        ],
        out_specs=[_tile_spec(), _slab_spec(S5_WIDTH), _slab_spec(CONV_WIDTH)],
        out_shape=[
            jax.ShapeDtypeStruct((batch, seq, D_MODEL), F32),
            jax.ShapeDtypeStruct((S5_WIDTH, n_rows, LANES), F32),
            jax.ShapeDtypeStruct((CONV_WIDTH, n_rows, LANES), F32),
        ],
        compiler_params=_DENSE_PARAMS,
        name="ffn1_in_proj",
    )(x, g1, wg, wu, wd, gm, win_t)


def _s5_mix(us, f_tab, s_in, s_out, a1, a2, d_lanes, *, batch):
    n_rows = us.shape[1]
    grp = lambda r, c: pl.BlockSpec((None, r, c), lambda g: (g, 0, 0))
    slab = pl.BlockSpec((S5_GROUP_CH, n_rows, LANES), lambda g: (g, 0, 0))
    return pl.pallas_call(
        functools.partial(_s5_mix_kernel, n_chunks=n_rows // batch, batch=batch),
        grid=(S5_GROUPS,),
        in_specs=[
            slab,
            grp(S5_GROUP_CH * S5_GROUP_CH, CHUNK),
            grp(S5_LANES, 4 * S5_STATE),
            grp(2 * S5_STATE, S5_LANES),
            grp(1, 2 * S5_STATE),
            grp(1, 2 * S5_STATE),
            grp(1, S5_LANES),
        ],
        out_specs=slab,
        out_shape=jax.ShapeDtypeStruct((S5_WIDTH, n_rows, LANES), F32),
        scratch_shapes=[pltpu.VMEM((S5_LANES, S5_LANES), BF16),
                        pltpu.VMEM((n_rows, 2 * S5_STATE), F32)],
        compiler_params=_MIX_PARAMS,
        name="s5_mix",
    )(us, f_tab, s_in, s_out, a1, a2, d_lanes)


def _conv_mix(zs, taps, bias, gain, beta, *, batch):
    n_rows = zs.shape[1]
    slab = pl.BlockSpec((CONV_HEAD_DIM, n_rows, LANES), lambda h: (h, 0, 0))
    per_ch = pl.BlockSpec((CONV_HEAD_DIM, 1, LANES), lambda h: (h, 0, 0))
    return pl.pallas_call(
        functools.partial(_conv_mix_kernel, batch=batch),
        grid=(CONV_HEADS,),
        in_specs=[slab, per_ch, per_ch, per_ch, per_ch],
        out_specs=slab,
        out_shape=jax.ShapeDtypeStruct((CONV_WIDTH, n_rows, LANES), F32),
        compiler_params=_MIX_PARAMS,
        name="conv_mix",
    )(zs, taps, bias, gain, beta)


def _out_ffn2_norm(x1, ys, yc, wglu_t, bglu, wo_s, wo_c, g2, wg, wu, wd, gf):
    batch, seq, _ = x1.shape
    return pl.pallas_call(
        _out_ffn2_norm_kernel,
        grid=(seq // CHUNK, batch // TILE_SEQS),
        in_specs=[
            _tile_spec(),
            _slab_spec(S5_WIDTH),
            _slab_spec(CONV_WIDTH),
            _resident((S5_WIDTH, S5_WIDTH)),
            _resident((S5_WIDTH, 1)),
            _resident((S5_WIDTH, D_MODEL)),
            _resident((CONV_WIDTH, D_MODEL)),
            _resident((1, D_MODEL)),
            _resident((D_MODEL, D_FF)),
            _resident((D_MODEL, D_FF)),
            _resident((D_FF, D_MODEL)),
            _resident((1, D_MODEL)),
        ],
        out_specs=_tile_spec(),
        out_shape=jax.ShapeDtypeStruct((batch, seq, D_MODEL), F32),
        compiler_params=_DENSE_PARAMS,
        name="out_ffn2_norm",
    )(x1, ys, yc, wglu_t, bglu, wo_s, wo_c, g2, wg, wu, wd, gf)


def _s5_tables(lam_re, lam_im, log_dt, b_re, b_im, c_re, c_im, d_skip):
    hp = lax.Precision.HIGHEST
    T, C, P, G = CHUNK, S5_GROUP_CH, S5_STATE, S5_GROUPS
    dt = jnp.exp(log_dt)[:, None]
    mag = jnp.exp(lam_re * dt)
    abar_re = mag * jnp.cos(lam_im * dt)
    abar_im = mag * jnp.sin(lam_im * dt)
    den = lam_re * lam_re + lam_im * lam_im
    num_re = abar_re - 1.0
    num_im = abar_im
    f_re = ((num_re * lam_re + num_im * lam_im) / den)[..., None]
    f_im = ((num_im * lam_re - num_re * lam_im) / den)[..., None]
    bb_re = f_re * b_re - f_im * b_im
    bb_im = f_re * b_im + f_im * b_re
    j = jnp.arange(T + 1, dtype=F32)[None, :, None]
    pmag = jnp.exp(j * (lam_re * dt)[:, None, :])
    pang = j * (lam_im * dt)[:, None, :]
    p_re = pmag * jnp.cos(pang)
    p_im = pmag * jnp.sin(pang)
    ab_re = p_re[..., None] * bb_re[:, None] - p_im[..., None] * bb_im[:, None]
    ab_im = p_re[..., None] * bb_im[:, None] + p_im[..., None] * bb_re[:, None]
    f_tab = (jnp.einsum('gop,gjpi->gioj', c_re, ab_re[:, :T], precision=hp)
             - jnp.einsum('gop,gjpi->gioj', c_im, ab_im[:, :T], precision=hp)).reshape(G, C * C, T)
    k_rev = T - 1 - jnp.arange(T)
    w_re = ab_re[:, k_rev].transpose(0, 3, 1, 2).reshape(G, C * T, P)
    w_im = ab_im[:, k_rev].transpose(0, 3, 1, 2).reshape(G, C * T, P)
    s_in = jnp.concatenate([w_re, w_im, w_im, w_re], axis=-1)
    q_re = p_re[:, 1:, None, :] * c_re[:, None] - p_im[:, 1:, None, :] * c_im[:, None]
    q_im = p_re[:, 1:, None, :] * c_im[:, None] + p_im[:, 1:, None, :] * c_re[:, None]
    s_out = jnp.concatenate([q_re, -q_im], axis=-1).transpose(0, 3, 2, 1).reshape(G, 2 * P, C * T)
    a1 = jnp.concatenate([p_re[:, T], p_re[:, T]], axis=-1)[:, None, :]
    a2 = jnp.concatenate([-p_im[:, T], p_im[:, T]], axis=-1)[:, None, :]
    d_lanes = jnp.repeat(d_skip.reshape(G, 1, C), T, axis=-1)
    return f_tab, s_in.astype(BF16), s_out.astype(BF16), a1, a2, d_lanes


def _per_channel_lanes(v):
    return jnp.broadcast_to(v[:, None, None], (v.shape[0], 1, LANES))


def kernel(x, ffn1_norm, ffn1_w_gate, ffn1_w_up, ffn1_w_down, mix_norm, w_in, s5_lam_re, s5_lam_im, s5_log_dt, s5_b_re, s5_b_im, s5_c_re, s5_c_im, s5_d, s5_w_glu, s5_b_glu, conv_w_dw, conv_b_dw, conv_ln_g, conv_ln_b, w_out, ffn2_norm, ffn2_w_gate, ffn2_w_up, ffn2_w_down, final_norm):
    batch, seq, d_model = x.shape
    assert ffn1_norm.shape[0] == 1 and d_model == D_MODEL
    l = 0
    row = lambda v: v.reshape(1, -1)

    x1, us, zs = _ffn1_in_proj(
        x, row(ffn1_norm[l]), ffn1_w_gate[l].astype(BF16), ffn1_w_up[l].astype(BF16),
        ffn1_w_down[l].astype(BF16), row(mix_norm[l]), w_in[l].T.astype(BF16))

    f_tab, s_in, s_out, a1, a2, d_lanes = _s5_tables(
        s5_lam_re[l], s5_lam_im[l], s5_log_dt[l], s5_b_re[l], s5_b_im[l], s5_c_re[l], s5_c_im[l], s5_d[l])
    ys = _s5_mix(us, f_tab, s_in, s_out, a1, a2, d_lanes, batch=batch)

    taps = jnp.zeros((CONV_WIDTH, LANES), F32).at[:, :CONV_K].set(conv_w_dw[l][::-1].T)[:, None, :]
    yc = _conv_mix(zs, taps, _per_channel_lanes(conv_b_dw[l]), _per_channel_lanes(conv_ln_g[l]),
                   _per_channel_lanes(conv_ln_b[l]), batch=batch)

    w_out_b = w_out[l].astype(BF16)
    return _out_ffn2_norm(
        x1, ys, yc, s5_w_glu[l].T.astype(BF16), s5_b_glu[l].reshape(-1, 1), w_out_b[:S5_WIDTH],
        w_out_b[S5_WIDTH:], row(ffn2_norm[l]), ffn2_w_gate[l].astype(BF16), ffn2_w_up[l].astype(BF16),
        ffn2_w_down[l].astype(BF16), row(final_norm))
```

```python
import functools

import jax
import jax.numpy as jnp
from jax import lax
from jax.experimental import pallas as pl
from jax.experimental.pallas import tpu as pltpu

D_MODEL = 1024
D_FF = 2816
S5_WIDTH = 512
S5_GROUP_CH = 16
S5_GROUPS = S5_WIDTH // S5_GROUP_CH
S5_STATE = 64
CONV_WIDTH = 512
CONV_HEAD_DIM = 64
CONV_HEADS = CONV_WIDTH // CONV_HEAD_DIM
CONV_K = 31
IN_COLS = S5_WIDTH + 2 * CONV_WIDTH
EPS = 1e-6

LANES = 128
SUBLANES = 8
CHUNK = LANES
S5_LANES = S5_GROUP_CH * CHUNK
TILE_SEQS = 4
ROW_TILE = TILE_SEQS * CHUNK
FF_SPLITS = ((0, 1536), (1536, 2816))
VMEM_LIMIT = 56 * 1024 * 1024

BF16 = jnp.bfloat16
F32 = jnp.float32

_NT = (((1,), (1,)), ((), ()))
_TN = (((0,), (0,)), ((), ()))


def _rms_norm(x, g):
    return x * lax.rsqrt(jnp.mean(x * x, axis=-1, keepdims=True) + EPS) * g


def _dot(a, b):
    return jnp.dot(a, b, preferred_element_type=F32)


def _swiglu(h, wg_ref, wu_ref, wd_ref):
    acc = None
    for lo, hi in FF_SPLITS:
        g = _dot(h, wg_ref[:, lo:hi])
        u = _dot(h, wu_ref[:, lo:hi])
        a = (g * jax.nn.sigmoid(g) * u).astype(BF16)
        part = _dot(a, wd_ref[lo:hi, :])
        acc = part if acc is None else acc + part
    return acc


def _store_slabs(slab_ref, val, first_seq):
    n_ch = slab_ref.shape[0]
    rows = slab_ref.reshape(n_ch * SUBLANES, LANES)
    for ct in range(n_ch // SUBLANES):
        for s in range(TILE_SEQS):
            rows[pl.ds(ct * SUBLANES * SUBLANES + first_seq + s, SUBLANES, stride=SUBLANES), :] = (
                val[ct * SUBLANES:(ct + 1) * SUBLANES, s * LANES:(s + 1) * LANES])


def _load_slabs(slab_ref, first_seq):
    n_ch = slab_ref.shape[0]
    rows = slab_ref.reshape(n_ch * SUBLANES, LANES)
    bands = []
    for ct in range(n_ch // SUBLANES):
        bands.append(jnp.concatenate(
            [rows[pl.ds(ct * SUBLANES * SUBLANES + first_seq + s, SUBLANES, stride=SUBLANES), :]
             for s in range(TILE_SEQS)], axis=1))
    return jnp.concatenate(bands, axis=0)


def _ffn1_in_proj_kernel(x_ref, g1_ref, wg_ref, wu_ref, wd_ref, gm_ref, win_t_ref,
                         x1_ref, us_ref, zs_ref):
    first_seq = pl.program_id(1) * TILE_SEQS
    x = x_ref[...].reshape(ROW_TILE, D_MODEL)
    h = _rms_norm(x, g1_ref[...]).astype(BF16)
    x1 = x + 0.5 * _swiglu(h, wg_ref, wu_ref, wd_ref)
    x1_ref[...] = x1.reshape(TILE_SEQS, CHUNK, D_MODEL)
    h2 = _rms_norm(x1, gm_ref[...]).astype(BF16)
    uv_t = lax.dot_general(win_t_ref[...], h2, _NT, preferred_element_type=F32)
    _store_slabs(us_ref, uv_t[:S5_WIDTH], first_seq)
    z_t = uv_t[S5_WIDTH:S5_WIDTH + CONV_WIDTH] * jax.nn.sigmoid(uv_t[S5_WIDTH + CONV_WIDTH:])
    _store_slabs(zs_ref, z_t, first_seq)


def _s5_mix_kernel(us_ref, cb_re_ref, cb_im_ref, pt_re_ref, pt_im_ref, pa_rev_ref, pb_rev_ref,
                   pa_fwd_ref, pb_fwd_ref, bvec_ref, cvec_ref, a1_ref, a2_ref, d_ref, ys_ref,
                   f_ref, lhs_ref, win_ref, wout_ref, sprev_ref, *, n_chunks, batch):
    hp = lax.Precision.HIGHEST
    f_ref[...] = (jnp.dot(cb_re_ref[...], pt_re_ref[...], precision=hp, preferred_element_type=F32)
                  - jnp.dot(cb_im_ref[...], pt_im_ref[...], precision=hp, preferred_element_type=F32))
    pa_rev, pb_rev = pa_rev_ref[...], pb_rev_ref[...]
    pa_fwd, pb_fwd = pa_fwd_ref[...], pb_fwd_ref[...]
    for c in range(S5_GROUP_CH):
        rows = slice(c * CHUNK, (c + 1) * CHUNK)
        b_r, b_i, b_i2 = bvec_ref[0, c:c + 1, :], bvec_ref[1, c:c + 1, :], bvec_ref[2, c:c + 1, :]
        win_ref[rows, 0:2 * S5_STATE] = (pa_rev * b_r + pb_rev * b_i).astype(BF16)
        win_ref[rows, 2 * S5_STATE:] = (pb_rev * b_r + pa_rev * b_i2).astype(BF16)
        wout_ref[rows, :] = (pa_fwd * cvec_ref[0, c:c + 1, :] + pb_fwd * cvec_ref[1, c:c + 1, :]).astype(BF16)
        lhs_ref[:, rows] = us_ref[c].astype(BF16)

    e = _dot(lhs_ref[...], win_ref[...])
    a1 = a1_ref[...]
    a2 = a2_ref[...]
    s = jnp.zeros((batch, 2 * S5_STATE), F32)
    s_sw = jnp.zeros((batch, 2 * S5_STATE), F32)
    for c in range(n_chunks):
        rows = slice(c * batch, (c + 1) * batch)
        sprev_ref[rows, :] = s
        e_c = e[rows, :]
        s, s_sw = (a1 * s + a2 * s_sw + e_c[:, :2 * S5_STATE],
                   a1 * s_sw - a2 * s + e_c[:, 2 * S5_STATE:])
    sprev = sprev_ref[...].astype(BF16)

    lag_row = lax.broadcasted_iota(jnp.int32, (CHUNK, CHUNK), 0)
    lag_col = lax.broadcasted_iota(jnp.int32, (CHUNK, CHUNK), 1)
    causal = lag_col >= lag_row

    def toeplitz(ci, co):
        f = jnp.broadcast_to(f_ref[pl.ds(ci * S5_GROUP_CH + co, 1), :], (CHUNK, CHUNK))
        return jnp.where(causal, pltpu.roll(f, 0, 1, stride=1, stride_axis=0), 0.0).astype(BF16)

    def out_pair(n, carry):
        co0 = 2 * n
        w_state = wout_ref[pl.ds(pl.multiple_of(co0 * CHUNK, 2 * CHUNK), 2 * CHUNK), :]
        acc = lax.dot_general(sprev, w_state, _NT, preferred_element_type=F32)
        for kt in range(S5_GROUP_CH // 2):
            w = jnp.concatenate(
                [jnp.concatenate([toeplitz(2 * kt + r, co0), toeplitz(2 * kt + r, co0 + 1)], axis=1)
                 for r in range(2)], axis=0)
            acc = acc + _dot(lhs_ref[:, 2 * kt * CHUNK:(2 * kt + 2) * CHUNK], w)
        for r in range(2):
            y = acc[:, r * CHUNK:(r + 1) * CHUNK] + d_ref[pl.ds(co0 + r, 1), :] * us_ref[co0 + r]
            ys_ref[co0 + r] = jax.nn.gelu(y, approximate=True)
        return carry

    lax.fori_loop(0, S5_GROUP_CH // 2, out_pair, 0, unroll=True)


def _conv_mix_kernel(zs_ref, q_ref, b_ref, g_ref, beta_ref, o_ref, *, batch):
    n_rows = zs_ref.shape[1]
    row = lax.broadcasted_iota(jnp.int32, (CHUNK, CHUNK), 0)
    col = lax.broadcasted_iota(jnp.int32, (CHUNK, CHUNK), 1)
    lag = row - col
    band_cur = (lag >= -(CONV_K - 1)) & (lag <= 0)
    band_prev = lag >= CHUNK - (CONV_K - 1)

    def conv_channel(ch, carry):
        z = zs_ref[ch]
        z_prev = jnp.concatenate([jnp.zeros((batch, CHUNK), F32), z[:n_rows - batch]], axis=0)
        lhs = jnp.concatenate([z_prev, z], axis=1).astype(BF16)
        taps = jnp.broadcast_to(q_ref[ch], (CHUNK, CHUNK))
        diag = pltpu.roll(taps, 0, 1, stride=1, stride_axis=0)
        toe = jnp.concatenate([jnp.where(band_prev, diag, 0.0), jnp.where(band_cur, diag, 0.0)],
                              axis=0).astype(BF16)
        o_ref[ch] = _dot(lhs, toe) + b_ref[ch]
        return carry

    lax.fori_loop(0, CONV_HEAD_DIM, conv_channel, 0, unroll=8)

    inv_n = 1.0 / CONV_HEAD_DIM
    mu = lax.fori_loop(0, CONV_HEAD_DIM, lambda ch, acc: acc + o_ref[ch],
                       jnp.zeros((n_rows, CHUNK), F32)) * inv_n
    var = lax.fori_loop(0, CONV_HEAD_DIM, lambda ch, acc: acc + jnp.square(o_ref[ch] - mu),
                        jnp.zeros((n_rows, CHUNK), F32)) * inv_n
    rstd = lax.rsqrt(var + EPS)

    def norm_channel(ch, carry):
        zn = (o_ref[ch] - mu) * rstd * g_ref[ch] + beta_ref[ch]
        o_ref[ch] = zn * jax.nn.sigmoid(zn)
        return carry

    lax.fori_loop(0, CONV_HEAD_DIM, norm_channel, 0)


def _out_ffn2_norm_kernel(x1_ref, ys_ref, yc_ref, wglu_t_ref, bglu_ref, wo_s_ref, wo_c_ref,
                          g2_ref, wg_ref, wu_ref, wd_ref, gf_ref, o_ref):
    first_seq = pl.program_id(1) * TILE_SEQS
    ys_t = _load_slabs(ys_ref, first_seq)
    yc_t = _load_slabs(yc_ref, first_seq)
    gate_t = jax.nn.sigmoid(_dot(wglu_t_ref[...], ys_t.astype(BF16)) + bglu_ref[...])
    mix = (lax.dot_general((ys_t * gate_t).astype(BF16), wo_s_ref[...], _TN, preferred_element_type=F32)
           + lax.dot_general(yc_t.astype(BF16), wo_c_ref[...], _TN, preferred_element_type=F32))
    x2 = x1_ref[...].reshape(ROW_TILE, D_MODEL) + mix
    h = _rms_norm(x2, g2_ref[...]).astype(BF16)
    x3 = x2 + 0.5 * _swiglu(h, wg_ref, wu_ref, wd_ref)
    o_ref[...] = _rms_norm(x3, gf_ref[...]).reshape(TILE_SEQS, CHUNK, D_MODEL)


def _resident(shape):
    nd = len(shape)
    return pl.BlockSpec(shape, lambda *_: (0,) * nd, pipeline_mode=pl.Buffered(1))


def _tile_spec():
    return pl.BlockSpec((TILE_SEQS, CHUNK, D_MODEL), lambda c, h: (h, c, 0))


def _slab_spec(n_ch):
    return pl.BlockSpec((n_ch, SUBLANES, LANES), lambda c, h: (0, c, 0))


_DENSE_PARAMS = pltpu.CompilerParams(dimension_semantics=("arbitrary", "arbitrary"),
                                     vmem_limit_bytes=VMEM_LIMIT)
_MIX_PARAMS = pltpu.CompilerParams(dimension_semantics=("arbitrary",), vmem_limit_bytes=VMEM_LIMIT)


def _ffn1_in_proj(x, g1, wg, wu, wd, gm, win_t):
    batch, seq, _ = x.shape
    assert batch == SUBLANES and seq % CHUNK == 0
    n_rows = (seq // CHUNK) * batch
    return pl.pallas_call(
        _ffn1_in_proj_kernel,
        grid=(seq // CHUNK, batch // TILE_SEQS),
        in_specs=[
            _tile_spec(),
            _resident((1, D_MODEL)),
            _resident((D_MODEL, D_FF)),
            _resident((D_MODEL, D_FF)),
            _resident((D_FF, D_MODEL)),
            _resident((1, D_MODEL)),
            _resident((IN_COLS, D_MODEL)),
        ],
        out_specs=[_tile_spec(), _slab_spec(S5_WIDTH), _slab_spec(CONV_WIDTH)],
        out_shape=[
            jax.ShapeDtypeStruct((batch, seq, D_MODEL), F32),
            jax.ShapeDtypeStruct((S5_WIDTH, n_rows, LANES), F32),
            jax.ShapeDtypeStruct((CONV_WIDTH, n_rows, LANES), F32),
        ],
        compiler_params=_DENSE_PARAMS,
        name="ffn1_in_proj",
    )(x, g1, wg, wu, wd, gm, win_t)


def _s5_mix(us, tables, *, batch):
    n_rows = us.shape[1]
    slab = pl.BlockSpec((S5_GROUP_CH, n_rows, LANES), lambda g: (g, 0, 0))

    def per_group(t):
        nd = t.ndim - 1
        return pl.BlockSpec((None,) + t.shape[1:], lambda g: (g,) + (0,) * nd)

    return pl.pallas_call(
        functools.partial(_s5_mix_kernel, n_chunks=n_rows // batch, batch=batch),
        grid=(S5_GROUPS,),
        in_specs=[slab] + [per_group(t) for t in tables],
        out_specs=slab,
        out_shape=jax.ShapeDtypeStruct((S5_WIDTH, n_rows, LANES), F32),
        scratch_shapes=[pltpu.VMEM((S5_GROUP_CH * S5_GROUP_CH, CHUNK), F32),
                        pltpu.VMEM((n_rows, S5_LANES), BF16),
                        pltpu.VMEM((S5_LANES, 4 * S5_STATE), BF16),
                        pltpu.VMEM((S5_LANES, 2 * S5_STATE), BF16),
                        pltpu.VMEM((n_rows, 2 * S5_STATE), F32)],
        compiler_params=_MIX_PARAMS,
        name="s5_mix",
    )(us, *tables)


def _conv_mix(zs, taps, bias, gain, beta, *, batch):
    n_rows = zs.shape[1]
    slab = pl.BlockSpec((CONV_HEAD_DIM, n_rows, LANES), lambda h: (h, 0, 0))
    per_ch = pl.BlockSpec((CONV_HEAD_DIM, 1, LANES), lambda h: (h, 0, 0))
    return pl.pallas_call(
        functools.partial(_conv_mix_kernel, batch=batch),
        grid=(CONV_HEADS,),
        in_specs=[slab, per_ch, per_ch, per_ch, per_ch],
        out_specs=slab,
        out_shape=jax.ShapeDtypeStruct((CONV_WIDTH, n_rows, LANES), F32),
        compiler_params=_MIX_PARAMS,
        name="conv_mix",
    )(zs, taps, bias, gain, beta)


def _out_ffn2_norm(x1, ys, yc, wglu_t, bglu, wo_s, wo_c, g2, wg, wu, wd, gf):
    batch, seq, _ = x1.shape
    return pl.pallas_call(
        _out_ffn2_norm_kernel,
        grid=(seq // CHUNK, batch // TILE_SEQS),
        in_specs=[
            _tile_spec(),
            _slab_spec(S5_WIDTH),
            _slab_spec(CONV_WIDTH),
            _resident((S5_WIDTH, S5_WIDTH)),
            _resident((S5_WIDTH, 1)),
            _resident((S5_WIDTH, D_MODEL)),
            _resident((CONV_WIDTH, D_MODEL)),
            _resident((1, D_MODEL)),
            _resident((D_MODEL, D_FF)),
            _resident((D_MODEL, D_FF)),
            _resident((D_FF, D_MODEL)),
            _resident((1, D_MODEL)),
        ],
        out_specs=_tile_spec(),
        out_shape=jax.ShapeDtypeStruct((batch, seq, D_MODEL), F32),
        compiler_params=_DENSE_PARAMS,
        name="out_ffn2_norm",
    )(x1, ys, yc, wglu_t, bglu, wo_s, wo_c, g2, wg, wu, wd, gf)


def _s5_tables(lam_re, lam_im, log_dt, b_re, b_im, c_re, c_im, d_skip):
    T, C, P, G = CHUNK, S5_GROUP_CH, S5_STATE, S5_GROUPS
    dt = jnp.exp(log_dt)[:, None]
    mag = jnp.exp(lam_re * dt)
    abar_re = mag * jnp.cos(lam_im * dt)
    abar_im = mag * jnp.sin(lam_im * dt)
    den = lam_re * lam_re + lam_im * lam_im
    num_re = abar_re - 1.0
    num_im = abar_im
    f_re = ((num_re * lam_re + num_im * lam_im) / den)[..., None]
    f_im = ((num_im * lam_re - num_re * lam_im) / den)[..., None]
    bb_re = f_re * b_re - f_im * b_im
    bb_im = f_re * b_im + f_im * b_re
    j = jnp.arange(T + 1, dtype=F32)[None, :, None]
    pmag = jnp.exp(j * (lam_re * dt)[:, None, :])
    pang = j * (lam_im * dt)[:, None, :]
    p_re = pmag * jnp.cos(pang)
    p_im = pmag * jnp.sin(pang)
    cat = lambda a, b: jnp.concatenate([a, b], axis=-1)
    bt_re = bb_re.transpose(0, 2, 1)
    bt_im = bb_im.transpose(0, 2, 1)
    cb_re = (c_re[:, None] * bt_re[:, :, None] - c_im[:, None] * bt_im[:, :, None]).reshape(G, C * C, P)
    cb_im = (c_re[:, None] * bt_im[:, :, None] + c_im[:, None] * bt_re[:, :, None]).reshape(G, C * C, P)
    pt_re = p_re[:, :T].transpose(0, 2, 1)
    pt_im = p_im[:, :T].transpose(0, 2, 1)
    pr_rev, pi_rev = p_re[:, T - 1::-1], p_im[:, T - 1::-1]
    pr_fwd, pi_fwd = p_re[:, 1:], p_im[:, 1:]
    bvec = jnp.stack([cat(bt_re, bt_re), cat(-bt_im, bt_im), cat(bt_im, -bt_im)], axis=1)
    cvec = jnp.stack([cat(c_re, -c_re), cat(-c_im, -c_im)], axis=1)
    a1 = cat(p_re[:, T], p_re[:, T])[:, None, :]
    a2 = cat(-p_im[:, T], p_im[:, T])[:, None, :]
    d_rows = jnp.broadcast_to(d_skip.reshape(G, C, 1), (G, C, LANES))
    return (cb_re, cb_im, pt_re, pt_im, cat(pr_rev, pi_rev), cat(pi_rev, pr_rev),
            cat(pr_fwd, pi_fwd), cat(pi_fwd, pr_fwd), bvec, cvec, a1, a2, d_rows)


def _per_channel_lanes(v):
    return jnp.broadcast_to(v[:, None, None], (v.shape[0], 1, LANES))


def kernel(x, ffn1_norm, ffn1_w_gate, ffn1_w_up, ffn1_w_down, mix_norm, w_in, s5_lam_re, s5_lam_im, s5_log_dt, s5_b_re, s5_b_im, s5_c_re, s5_c_im, s5_d, s5_w_glu, s5_b_glu, conv_w_dw, conv_b_dw, conv_ln_g, conv_ln_b, w_out, ffn2_norm, ffn2_w_gate, ffn2_w_up, ffn2_w_down, final_norm):
    batch, seq, d_model = x.shape
    assert ffn1_norm.shape[0] == 1 and d_model == D_MODEL
    l = 0
    row = lambda v: v.reshape(1, -1)

    x1, us, zs = _ffn1_in_proj(
        x, row(ffn1_norm[l]), ffn1_w_gate[l].astype(BF16), ffn1_w_up[l].astype(BF16),
        ffn1_w_down[l].astype(BF16), row(mix_norm[l]), w_in[l].T.astype(BF16))

    tables = _s5_tables(
        s5_lam_re[l], s5_lam_im[l], s5_log_dt[l], s5_b_re[l], s5_b_im[l], s5_c_re[l], s5_c_im[l], s5_d[l])
    ys = _s5_mix(us, tables, batch=batch)

    taps = jnp.zeros((CONV_WIDTH, LANES), F32).at[:, :CONV_K].set(conv_w_dw[l][::-1].T)[:, None, :]
    yc = _conv_mix(zs, taps, _per_channel_lanes(conv_b_dw[l]), _per_channel_lanes(conv_ln_g[l]),
                   _per_channel_lanes(conv_ln_b[l]), batch=batch)

    w_out_b = w_out[l].astype(BF16)
    return _out_ffn2_norm(
        x1, ys, yc, s5_w_glu[l].T.astype(BF16), s5_b_glu[l].reshape(-1, 1), w_out_b[:S5_WIDTH],
        w_out_b[S5_WIDTH:], row(ffn2_norm[l]), ffn2_w_gate[l].astype(BF16), ffn2_w_up[l].astype(BF16),
        ffn2_w_down[l].astype(BF16), row(final_norm))
```

```python
import functools

import jax
import jax.numpy as jnp
from jax import lax
from jax.experimental import pallas as pl
from jax.experimental.pallas import tpu as pltpu

D_MODEL = 1024
D_FF = 2816
S5_WIDTH = 512
S5_GROUP_CH = 16
S5_GROUPS = S5_WIDTH // S5_GROUP_CH
S5_STATE = 64
CONV_WIDTH = 512
CONV_HEAD_DIM = 64
CONV_HEADS = CONV_WIDTH // CONV_HEAD_DIM
CONV_K = 31
IN_COLS = S5_WIDTH + 2 * CONV_WIDTH
EPS = 1e-6

LANES = 128
SUBLANES = 8
CHUNK = LANES
S5_LANES = S5_GROUP_CH * CHUNK
TILE_SEQS = 4
ROW_TILE = TILE_SEQS * CHUNK
FF_SPLITS = ((0, 1536), (1536, 2816))
LN_ROWS = 64
VMEM_LIMIT = 56 * 1024 * 1024

BF16 = jnp.bfloat16
F32 = jnp.float32

_NT = (((1,), (1,)), ((), ()))
_TN = (((0,), (0,)), ((), ()))


def _rms_norm(x, g):
    return x * lax.rsqrt(jnp.mean(x * x, axis=-1, keepdims=True) + EPS) * g


def _dot(a, b):
    return jnp.dot(a, b, preferred_element_type=F32)


def _rms_split(x, g):
    return (x * g).astype(BF16), lax.rsqrt(jnp.mean(x * x, axis=-1, keepdims=True) + EPS)


def _swiglu(h, row_scale, wg_ref, wu_ref, wd_ref):
    acc = None
    for lo, hi in FF_SPLITS:
        g = _dot(h, wg_ref[:, lo:hi]) * row_scale
        u = _dot(h, wu_ref[:, lo:hi]) * row_scale
        a = (g * jax.nn.sigmoid(g) * u).astype(BF16)
        part = _dot(a, wd_ref[lo:hi, :])
        acc = part if acc is None else acc + part
    return acc


def _store_slabs(slab_ref, val, first_seq):
    n_ch = slab_ref.shape[0]
    rows = slab_ref.reshape(n_ch * SUBLANES, LANES)
    for ct in range(n_ch // SUBLANES):
        for s in range(TILE_SEQS):
            rows[pl.ds(ct * SUBLANES * SUBLANES + first_seq + s, SUBLANES, stride=SUBLANES), :] = (
                val[ct * SUBLANES:(ct + 1) * SUBLANES, s * LANES:(s + 1) * LANES])


def _load_slabs(slab_ref, first_seq):
    n_ch = slab_ref.shape[0]
    rows = slab_ref.reshape(n_ch * SUBLANES, LANES)
    bands = []
    for ct in range(n_ch // SUBLANES):
        bands.append(jnp.concatenate(
            [rows[pl.ds(ct * SUBLANES * SUBLANES + first_seq + s, SUBLANES, stride=SUBLANES), :]
             for s in range(TILE_SEQS)], axis=1))
    return jnp.concatenate(bands, axis=0)


def _ffn1_in_proj_kernel(x_ref, g1_ref, wg_ref, wu_ref, wd_ref, gm_ref, win_t_ref,
                         x1_ref, us_ref, zs_ref):
    first_seq = pl.program_id(1) * TILE_SEQS
    x = x_ref[...].reshape(ROW_TILE, D_MODEL)
    h, r = _rms_split(x, g1_ref[...])
    x1 = x + 0.5 * _swiglu(h, r, wg_ref, wu_ref, wd_ref)
    x1_ref[...] = x1.reshape(TILE_SEQS, CHUNK, D_MODEL)
    h2 = _rms_norm(x1, gm_ref[...]).astype(BF16)

    def in_proj_t(lo, hi):
        return lax.dot_general(win_t_ref[lo:hi, :], h2, _NT, preferred_element_type=F32)

    z_t = (in_proj_t(S5_WIDTH, S5_WIDTH + CONV_WIDTH)
           * jax.nn.sigmoid(in_proj_t(S5_WIDTH + CONV_WIDTH, IN_COLS)))
    _store_slabs(zs_ref, z_t, first_seq)
    _store_slabs(us_ref, in_proj_t(0, S5_WIDTH), first_seq)


def _s5_mix_kernel(us_ref, cb_re_ref, cb_im_ref, pt_re_ref, pt_im_ref, pa_rev_ref, pb_rev_ref,
                   pa_fwd_ref, pb_fwd_ref, bvec_ref, cvec_ref, a1_ref, a2_ref, d_ref, ys_ref,
                   f_ref, lhs_ref, win_ref, wout_ref, sprev_ref, *, n_chunks, batch):
    hp = lax.Precision.HIGHEST
    f_ref[...] = (jnp.dot(cb_re_ref[...], pt_re_ref[...], precision=hp, preferred_element_type=F32)
                  - jnp.dot(cb_im_ref[...], pt_im_ref[...], precision=hp, preferred_element_type=F32))
    pa_rev, pb_rev = pa_rev_ref[...], pb_rev_ref[...]
    pa_fwd, pb_fwd = pa_fwd_ref[...], pb_fwd_ref[...]
    for c in range(S5_GROUP_CH):
        rows = slice(c * CHUNK, (c + 1) * CHUNK)
        b_r, b_i, b_i2 = bvec_ref[0, c:c + 1, :], bvec_ref[1, c:c + 1, :], bvec_ref[2, c:c + 1, :]
        win_ref[rows, 0:2 * S5_STATE] = (pa_rev * b_r + pb_rev * b_i).astype(BF16)
        win_ref[rows, 2 * S5_STATE:] = (pb_rev * b_r + pa_rev * b_i2).astype(BF16)
        wout_ref[rows, :] = (pa_fwd * cvec_ref[0, c:c + 1, :] + pb_fwd * cvec_ref[1, c:c + 1, :]).astype(BF16)
        lhs_ref[:, rows] = us_ref[c].astype(BF16)

    e = _dot(lhs_ref[...], win_ref[...])
    a1 = a1_ref[...]
    a2 = a2_ref[...]
    s = jnp.zeros((batch, 2 * S5_STATE), F32)
    s_sw = jnp.zeros((batch, 2 * S5_STATE), F32)
    for c in range(n_chunks):
        rows = slice(c * batch, (c + 1) * batch)
        sprev_ref[rows, :] = s
        e_c = e[rows, :]
        s, s_sw = (a1 * s + a2 * s_sw + e_c[:, :2 * S5_STATE],
                   a1 * s_sw - a2 * s + e_c[:, 2 * S5_STATE:])
    sprev = sprev_ref[...].astype(BF16)

    lag_row = lax.broadcasted_iota(jnp.int32, (CHUNK, CHUNK), 0)
    lag_col = lax.broadcasted_iota(jnp.int32, (CHUNK, CHUNK), 1)
    causal = lag_col >= lag_row

    def toeplitz(ci, co):
        f = jnp.broadcast_to(f_ref[pl.ds(ci * S5_GROUP_CH + co, 1), :], (CHUNK, CHUNK))
        return jnp.where(causal, pltpu.roll(f, 0, 1, stride=1, stride_axis=0), 0.0).astype(BF16)

    def out_pair(n, carry):
        co0 = 2 * n
        w_state = wout_ref[pl.ds(pl.multiple_of(co0 * CHUNK, 2 * CHUNK), 2 * CHUNK), :]
        acc = lax.dot_general(sprev, w_state, _NT, preferred_element_type=F32)
        for kt in range(S5_GROUP_CH // 2):
            w = jnp.concatenate(
                [jnp.concatenate([toeplitz(2 * kt + r, co0), toeplitz(2 * kt + r, co0 + 1)], axis=1)
                 for r in range(2)], axis=0)
            acc = acc + _dot(lhs_ref[:, 2 * kt * CHUNK:(2 * kt + 2) * CHUNK], w)
        for r in range(2):
            y = acc[:, r * CHUNK:(r + 1) * CHUNK] + d_ref[pl.ds(co0 + r, 1), :] * us_ref[co0 + r]
            ys_ref[co0 + r] = jax.nn.gelu(y, approximate=True)
        return carry

    lax.fori_loop(0, S5_GROUP_CH // 2, out_pair, 0, unroll=True)


def _conv_mix_kernel(zs_ref, q_ref, b_ref, g_ref, beta_ref, o_ref, *, batch):
    n_rows = zs_ref.shape[1]
    row = lax.broadcasted_iota(jnp.int32, (CHUNK, CHUNK), 0)
    col = lax.broadcasted_iota(jnp.int32, (CHUNK, CHUNK), 1)
    lag = row - col
    band_cur = (lag >= -(CONV_K - 1)) & (lag <= 0)
    band_prev = lag >= CHUNK - (CONV_K - 1)

    def conv_channel(ch, carry):
        z = zs_ref[ch]
        z_prev = jnp.concatenate([jnp.zeros((batch, CHUNK), F32), z[:n_rows - batch]], axis=0)
        lhs = jnp.concatenate([z_prev, z], axis=1).astype(BF16)
        taps = jnp.broadcast_to(q_ref[ch], (CHUNK, CHUNK))
        diag = pltpu.roll(taps, 0, 1, stride=1, stride_axis=0)
        toe = jnp.concatenate([jnp.where(band_prev, diag, 0.0), jnp.where(band_cur, diag, 0.0)],
                              axis=0).astype(BF16)
        o_ref[ch] = _dot(lhs, toe) + b_ref[ch]
        return carry

    lax.fori_loop(0, CONV_HEAD_DIM, conv_channel, 0, unroll=8)

    inv_n = 1.0 / CONV_HEAD_DIM
    for r0 in range(0, n_rows, LN_ROWS):
        rows = pl.ds(r0, LN_ROWS)
        zero = jnp.zeros((LN_ROWS, CHUNK), F32)
        mu = lax.fori_loop(0, CONV_HEAD_DIM, lambda ch, acc: acc + o_ref[ch, rows, :], zero,
                           unroll=4) * inv_n
        var = lax.fori_loop(0, CONV_HEAD_DIM, lambda ch, acc: acc + jnp.square(o_ref[ch, rows, :] - mu),
                            zero, unroll=4) * inv_n
        rstd = lax.rsqrt(var + EPS)

        def norm_channel(ch, carry):
            zn = (o_ref[ch, rows, :] - mu) * rstd * g_ref[ch] + beta_ref[ch]
            o_ref[ch, rows, :] = zn * jax.nn.sigmoid(zn)
            return carry

        lax.fori_loop(0, CONV_HEAD_DIM, norm_channel, 0, unroll=4)


def _out_ffn2_norm_kernel(x1_ref, ys_ref, yc_ref, wglu_t_ref, bglu_ref, wo_s_ref, wo_c_ref,
                          g2_ref, wg_ref, wu_ref, wd_ref, gf_ref, o_ref):
    first_seq = pl.program_id(1) * TILE_SEQS
    ys_t = _load_slabs(ys_ref, first_seq)
    yc_t = _load_slabs(yc_ref, first_seq)
    gate_t = jax.nn.sigmoid(_dot(wglu_t_ref[...], ys_t.astype(BF16)) + bglu_ref[...])
    mix = (lax.dot_general((ys_t * gate_t).astype(BF16), wo_s_ref[...], _TN, preferred_element_type=F32)
           + lax.dot_general(yc_t.astype(BF16), wo_c_ref[...], _TN, preferred_element_type=F32))
    x2 = x1_ref[...].reshape(ROW_TILE, D_MODEL) + mix
    h, r = _rms_split(x2, g2_ref[...])
    x3 = x2 + 0.5 * _swiglu(h, r, wg_ref, wu_ref, wd_ref)
    o_ref[...] = _rms_norm(x3, gf_ref[...]).reshape(TILE_SEQS, CHUNK, D_MODEL)


def _resident(shape):
    nd = len(shape)
    return pl.BlockSpec(shape, lambda *_: (0,) * nd, pipeline_mode=pl.Buffered(1))


def _tile_spec():
    return pl.BlockSpec((TILE_SEQS, CHUNK, D_MODEL), lambda c, h: (h, c, 0))


def _slab_spec(n_ch):
    return pl.BlockSpec((n_ch, SUBLANES, LANES), lambda c, h: (0, c, 0))


_DENSE_PARAMS = pltpu.CompilerParams(dimension_semantics=("arbitrary", "arbitrary"),
                                     vmem_limit_bytes=VMEM_LIMIT)
_MIX_PARAMS = pltpu.CompilerParams(dimension_semantics=("arbitrary",), vmem_limit_bytes=VMEM_LIMIT)


def _ffn1_in_proj(x, g1, wg, wu, wd, gm, win_t):
    batch, seq, _ = x.shape
    assert batch == SUBLANES and seq % CHUNK == 0
    n_rows = (seq // CHUNK) * batch
    return pl.pallas_call(
        _ffn1_in_proj_kernel,
        grid=(seq // CHUNK, batch // TILE_SEQS),
        in_specs=[
            _tile_spec(),
            _resident((1, D_MODEL)),
            _resident((D_MODEL, D_FF)),
            _resident((D_MODEL, D_FF)),
            _resident((D_FF, D_MODEL)),
            _resident((1, D_MODEL)),
            _resident((IN_COLS, D_MODEL)),
        ],
        out_specs=[_tile_spec(), _slab_spec(S5_WIDTH), _slab_spec(CONV_WIDTH)],
        out_shape=[
            jax.ShapeDtypeStruct((batch, seq, D_MODEL), F32),
            jax.ShapeDtypeStruct((S5_WIDTH, n_rows, LANES), F32),
            jax.ShapeDtypeStruct((CONV_WIDTH, n_rows, LANES), F32),
        ],
        compiler_params=_DENSE_PARAMS,
        name="ffn1_in_proj",
    )(x, g1, wg, wu, wd, gm, win_t)


def _s5_mix(us, tables, *, batch):
    n_rows = us.shape[1]
    slab = pl.BlockSpec((S5_GROUP_CH, n_rows, LANES), lambda g: (g, 0, 0))

    def per_group(t):
        nd = t.ndim - 1
        return pl.BlockSpec((None,) + t.shape[1:], lambda g: (g,) + (0,) * nd)

    return pl.pallas_call(
        functools.partial(_s5_mix_kernel, n_chunks=n_rows // batch, batch=batch),
        grid=(S5_GROUPS,),
        in_specs=[slab] + [per_group(t) for t in tables],
        out_specs=slab,
        out_shape=jax.ShapeDtypeStruct((S5_WIDTH, n_rows, LANES), F32),
        scratch_shapes=[pltpu.VMEM((S5_GROUP_CH * S5_GROUP_CH, CHUNK), F32),
                        pltpu.VMEM((n_rows, S5_LANES), BF16),
                        pltpu.VMEM((S5_LANES, 4 * S5_STATE), BF16),
                        pltpu.VMEM((S5_LANES, 2 * S5_STATE), BF16),
                        pltpu.VMEM((n_rows, 2 * S5_STATE), F32)],
        compiler_params=_MIX_PARAMS,
        name="s5_mix",
    )(us, *tables)


def _conv_mix(zs, taps, bias, gain, beta, *, batch):
    n_rows = zs.shape[1]
    slab = pl.BlockSpec((CONV_HEAD_DIM, n_rows, LANES), lambda h: (h, 0, 0))
    per_ch = pl.BlockSpec((CONV_HEAD_DIM, 1, LANES), lambda h: (h, 0, 0))
    return pl.pallas_call(
        functools.partial(_conv_mix_kernel, batch=batch),
        grid=(CONV_HEADS,),
        in_specs=[slab, per_ch, per_ch, per_ch, per_ch],
        out_specs=slab,
        out_shape=jax.ShapeDtypeStruct((CONV_WIDTH, n_rows, LANES), F32),
        compiler_params=_MIX_PARAMS,
        name="conv_mix",
    )(zs, taps, bias, gain, beta)


def _out_ffn2_norm(x1, ys, yc, wglu_t, bglu, wo_s, wo_c, g2, wg, wu, wd, gf):
    batch, seq, _ = x1.shape
    return pl.pallas_call(
        _out_ffn2_norm_kernel,
        grid=(seq // CHUNK, batch // TILE_SEQS),
        in_specs=[
            _tile_spec(),
            _slab_spec(S5_WIDTH),
            _slab_spec(CONV_WIDTH),
            _resident((S5_WIDTH, S5_WIDTH)),
            _resident((S5_WIDTH, 1)),
            _resident((S5_WIDTH, D_MODEL)),
            _resident((CONV_WIDTH, D_MODEL)),
            _resident((1, D_MODEL)),
            _resident((D_MODEL, D_FF)),
            _resident((D_MODEL, D_FF)),
            _resident((D_FF, D_MODEL)),
            _resident((1, D_MODEL)),
        ],
        out_specs=_tile_spec(),
        out_shape=jax.ShapeDtypeStruct((batch, seq, D_MODEL), F32),
        compiler_params=_DENSE_PARAMS,
        name="out_ffn2_norm",
    )(x1, ys, yc, wglu_t, bglu, wo_s, wo_c, g2, wg, wu, wd, gf)


def _s5_tables(lam_re, lam_im, log_dt, b_re, b_im, c_re, c_im, d_skip):
    T, C, P, G = CHUNK, S5_GROUP_CH, S5_STATE, S5_GROUPS
    dt = jnp.exp(log_dt)[:, None]
    mag = jnp.exp(lam_re * dt)
    abar_re = mag * jnp.cos(lam_im * dt)
    abar_im = mag * jnp.sin(lam_im * dt)
    den = lam_re * lam_re + lam_im * lam_im
    num_re = abar_re - 1.0
    num_im = abar_im
    f_re = ((num_re * lam_re + num_im * lam_im) / den)[..., None]
    f_im = ((num_im * lam_re - num_re * lam_im) / den)[..., None]
    bb_re = f_re * b_re - f_im * b_im
    bb_im = f_re * b_im + f_im * b_re
    j = jnp.arange(T + 1, dtype=F32)[None, :, None]
    pmag = jnp.exp(j * (lam_re * dt)[:, None, :])
    pang = j * (lam_im * dt)[:, None, :]
    p_re = pmag * jnp.cos(pang)
    p_im = pmag * jnp.sin(pang)
    cat = lambda a, b: jnp.concatenate([a, b], axis=-1)
    bt_re = bb_re.transpose(0, 2, 1)
    bt_im = bb_im.transpose(0, 2, 1)
    cb_re = (c_re[:, None] * bt_re[:, :, None] - c_im[:, None] * bt_im[:, :, None]).reshape(G, C * C, P)
    cb_im = (c_re[:, None] * bt_im[:, :, None] + c_im[:, None] * bt_re[:, :, None]).reshape(G, C * C, P)
    pt_re = p_re[:, :T].transpose(0, 2, 1)
    pt_im = p_im[:, :T].transpose(0, 2, 1)
    pr_rev, pi_rev = p_re[:, T - 1::-1], p_im[:, T - 1::-1]
    pr_fwd, pi_fwd = p_re[:, 1:], p_im[:, 1:]
    bvec = jnp.stack([cat(bt_re, bt_re), cat(-bt_im, bt_im), cat(bt_im, -bt_im)], axis=1)
    cvec = jnp.stack([cat(c_re, -c_re), cat(-c_im, -c_im)], axis=1)
    a1 = cat(p_re[:, T], p_re[:, T])[:, None, :]
    a2 = cat(-p_im[:, T], p_im[:, T])[:, None, :]
    d_rows = jnp.broadcast_to(d_skip.reshape(G, C, 1), (G, C, LANES))
    return (cb_re, cb_im, pt_re, pt_im, cat(pr_rev, pi_rev), cat(pi_rev, pr_rev),
            cat(pr_fwd, pi_fwd), cat(pi_fwd, pr_fwd), bvec, cvec, a1, a2, d_rows)


def _per_channel_lanes(v):
    return jnp.broadcast_to(v[:, None, None], (v.shape[0], 1, LANES))


def kernel(x, ffn1_norm, ffn1_w_gate, ffn1_w_up, ffn1_w_down, mix_norm, w_in, s5_lam_re, s5_lam_im, s5_log_dt, s5_b_re, s5_b_im, s5_c_re, s5_c_im, s5_d, s5_w_glu, s5_b_glu, conv_w_dw, conv_b_dw, conv_ln_g, conv_ln_b, w_out, ffn2_norm, ffn2_w_gate, ffn2_w_up, ffn2_w_down, final_norm):
    batch, seq, d_model = x.shape
    assert ffn1_norm.shape[0] == 1 and d_model == D_MODEL
    l = 0
    row = lambda v: v.reshape(1, -1)

    x1, us, zs = _ffn1_in_proj(
        x, row(ffn1_norm[l]), ffn1_w_gate[l].astype(BF16), ffn1_w_up[l].astype(BF16),
        ffn1_w_down[l].astype(BF16), row(mix_norm[l]), w_in[l].T.astype(BF16))

    tables = _s5_tables(
        s5_lam_re[l], s5_lam_im[l], s5_log_dt[l], s5_b_re[l], s5_b_im[l], s5_c_re[l], s5_c_im[l], s5_d[l])
    ys = _s5_mix(us, tables, batch=batch)

    taps = jnp.zeros((CONV_WIDTH, LANES), F32).at[:, :CONV_K].set(conv_w_dw[l][::-1].T)[:, None, :]
    yc = _conv_mix(zs, taps, _per_channel_lanes(conv_b_dw[l]), _per_channel_lanes(conv_ln_g[l]),
                   _per_channel_lanes(conv_ln_b[l]), batch=batch)

    w_out_b = w_out[l].astype(BF16)
    return _out_ffn2_norm(
        x1, ys, yc, s5_w_glu[l].T.astype(BF16), s5_b_glu[l].reshape(-1, 1), w_out_b[:S5_WIDTH],
        w_out_b[S5_WIDTH:], row(ffn2_norm[l]), ffn2_w_gate[l].astype(BF16), ffn2_w_up[l].astype(BF16),
        ffn2_w_down[l].astype(BF16), row(final_norm))
```

```python
import functools

import jax
import jax.numpy as jnp
from jax import lax
from jax.experimental import pallas as pl
from jax.experimental.pallas import tpu as pltpu

D_MODEL = 1024
D_FF = 2816
S5_WIDTH = 512
S5_GROUP_CH = 16
S5_GROUPS = S5_WIDTH // S5_GROUP_CH
S5_STATE = 64
CONV_WIDTH = 512
CONV_HEAD_DIM = 64
CONV_HEADS = CONV_WIDTH // CONV_HEAD_DIM
CONV_K = 31
IN_COLS = S5_WIDTH + 2 * CONV_WIDTH
EPS = 1e-6

LANES = 128
SUBLANES = 8
CHUNK = LANES
S5_LANES = S5_GROUP_CH * CHUNK
TILE_SEQS = 4
ROW_TILE = TILE_SEQS * CHUNK
FF_SPLITS = ((0, 1536), (1536, 2816))
LN_ROWS = 64
VMEM_LIMIT = 56 * 1024 * 1024

BF16 = jnp.bfloat16
F32 = jnp.float32

(_V_BR,
 _V_BI_NP,
 _V_BI_PN,
 _V_BI_PP,
 _V_CR_PN,
 _V_CI_NN,
 _V_C_RE_NIM,
 _V_C_NIM_NRE,
 _V_D,
 _V_COUNT) = range(10)

_PC_TAPS, _PC_BIAS, _PC_GAIN, _PC_BETA, _PC_COUNT = range(5)

_NT = (((1,), (1,)), ((), ()))
_TN = (((0,), (0,)), ((), ()))


def _rms_norm(x, g):
    return x * lax.rsqrt(jnp.mean(x * x, axis=-1, keepdims=True) + EPS) * g


def _dot(a, b):
    return jnp.dot(a, b, preferred_element_type=F32)


def _rms_split(x, g):
    return (x * g).astype(BF16), lax.rsqrt(jnp.mean(x * x, axis=-1, keepdims=True) + EPS)


def _swiglu(h, row_scale, wg_ref, wu_ref, wd_ref):
    acc = None
    for lo, hi in FF_SPLITS:
        g = _dot(h, wg_ref[:, lo:hi]) * row_scale
        u = _dot(h, wu_ref[:, lo:hi]) * row_scale
        a = (g * jax.nn.sigmoid(g) * u).astype(BF16)
        part = _dot(a, wd_ref[lo:hi, :])
        acc = part if acc is None else acc + part
    return acc


def _store_slabs(slab_ref, val, first_seq):
    n_ch = slab_ref.shape[0]
    rows = slab_ref.reshape(n_ch * SUBLANES, LANES)
    for ct in range(n_ch // SUBLANES):
        for s in range(TILE_SEQS):
            rows[pl.ds(ct * SUBLANES * SUBLANES + first_seq + s, SUBLANES, stride=SUBLANES), :] = (
                val[ct * SUBLANES:(ct + 1) * SUBLANES, s * LANES:(s + 1) * LANES])


def _load_slabs(slab_ref, first_seq):
    n_ch = slab_ref.shape[0]
    rows = slab_ref.reshape(n_ch * SUBLANES, LANES)
    bands = []
    for ct in range(n_ch // SUBLANES):
        bands.append(jnp.concatenate(
            [rows[pl.ds(ct * SUBLANES * SUBLANES + first_seq + s, SUBLANES, stride=SUBLANES), :]
             for s in range(TILE_SEQS)], axis=1))
    return jnp.concatenate(bands, axis=0)


def _ffn1_in_proj_kernel(x_ref, g1_ref, wg_ref, wu_ref, wd_ref, gm_ref, win_t_ref,
                         x1_ref, us_ref, zs_ref):
    first_seq = pl.program_id(1) * TILE_SEQS
    x = x_ref[...].reshape(ROW_TILE, D_MODEL)
    h, r = _rms_split(x, g1_ref[...])
    x1 = x + 0.5 * _swiglu(h, r, wg_ref, wu_ref, wd_ref)
    x1_ref[...] = x1.reshape(TILE_SEQS, CHUNK, D_MODEL)
    h2 = _rms_norm(x1, gm_ref[...]).astype(BF16)

    def in_proj_t(lo, hi):
        return lax.dot_general(win_t_ref[lo:hi, :], h2, _NT, preferred_element_type=F32)

    z_t = (in_proj_t(S5_WIDTH, S5_WIDTH + CONV_WIDTH)
           * jax.nn.sigmoid(in_proj_t(S5_WIDTH + CONV_WIDTH, IN_COLS)))
    _store_slabs(zs_ref, z_t, first_seq)
    _store_slabs(us_ref, in_proj_t(0, S5_WIDTH), first_seq)


def _s5_mix_kernel(us_ref, pow_ref, pow_rev_ref, vec_ref, ys_ref,
                   cb_ref, f_ref, lhs_ref, win_ref, wout_ref, sprev_ref, *, n_chunks, batch):
    swap = lambda v: pltpu.roll(v, S5_STATE, 1)
    lo_half = lax.broadcasted_iota(jnp.int32, (1, 2 * S5_STATE), 1) < S5_STATE
    pa_rev = pow_rev_ref[...]
    pb_rev = swap(pa_rev)
    pa_fwd = pow_ref[1:CHUNK + 1, :]
    pb_fwd = swap(pa_fwd)
    for c in range(S5_GROUP_CH):
        rows = slice(c * CHUNK, (c + 1) * CHUNK)
        vec = lambda i: vec_ref[i, c:c + 1, :]
        win_ref[rows, 0:2 * S5_STATE] = (pa_rev * vec(_V_BR) + pb_rev * vec(_V_BI_NP)).astype(BF16)
        win_ref[rows, 2 * S5_STATE:] = (pb_rev * vec(_V_BR) + pa_rev * vec(_V_BI_PN)).astype(BF16)
        wout_ref[rows, :] = (pa_fwd * vec(_V_CR_PN) + pb_fwd * vec(_V_CI_NN)).astype(BF16)
        lhs_ref[:, rows] = us_ref[c].astype(BF16)
        cb_ref[c * S5_GROUP_CH:(c + 1) * S5_GROUP_CH, :] = (
            vec_ref[_V_C_RE_NIM] * vec(_V_BR) + vec_ref[_V_C_NIM_NRE] * vec(_V_BI_PP))
    f_ref[...] = lax.dot_general(cb_ref[...], pow_ref[0:CHUNK, :], _NT, precision=lax.Precision.HIGHEST,
                                 preferred_element_type=F32)

    e = _dot(lhs_ref[...], win_ref[...])
    a_chunk = pow_ref[CHUNK:CHUNK + 1, :]
    a_sw = swap(a_chunk)
    a1 = jnp.where(lo_half, a_chunk, a_sw)
    a2 = jnp.where(lo_half, -a_sw, a_chunk)
    s = jnp.zeros((batch, 2 * S5_STATE), F32)
    s_sw = jnp.zeros((batch, 2 * S5_STATE), F32)
    for c in range(n_chunks):
        rows = slice(c * batch, (c + 1) * batch)
        sprev_ref[rows, :] = s
        e_c = e[rows, :]
        s, s_sw = (a1 * s + a2 * s_sw + e_c[:, :2 * S5_STATE],
                   a1 * s_sw - a2 * s + e_c[:, 2 * S5_STATE:])
    sprev = sprev_ref[...].astype(BF16)

    lag_row = lax.broadcasted_iota(jnp.int32, (CHUNK, CHUNK), 0)
    lag_col = lax.broadcasted_iota(jnp.int32, (CHUNK, CHUNK), 1)
    causal = lag_col >= lag_row

    def toeplitz(ci, co):
        f = jnp.broadcast_to(f_ref[pl.ds(ci * S5_GROUP_CH + co, 1), :], (CHUNK, CHUNK))
        return jnp.where(causal, pltpu.roll(f, 0, 1, stride=1, stride_axis=0), 0.0).astype(BF16)

    def out_pair(n, carry):
        co0 = 2 * n
        w_state = wout_ref[pl.ds(pl.multiple_of(co0 * CHUNK, 2 * CHUNK), 2 * CHUNK), :]
        acc = lax.dot_general(sprev, w_state, _NT, preferred_element_type=F32)
        for kt in range(S5_GROUP_CH // 2):
            w = jnp.concatenate(
                [jnp.concatenate([toeplitz(2 * kt + r, co0), toeplitz(2 * kt + r, co0 + 1)], axis=1)
                 for r in range(2)], axis=0)
            acc = acc + _dot(lhs_ref[:, 2 * kt * CHUNK:(2 * kt + 2) * CHUNK], w)
        for r in range(2):
            y = acc[:, r * CHUNK:(r + 1) * CHUNK] + vec_ref[_V_D, pl.ds(co0 + r, 1), :] * us_ref[co0 + r]
            ys_ref[co0 + r] = jax.nn.gelu(y, approximate=True)
        return carry

    lax.fori_loop(0, S5_GROUP_CH // 2, out_pair, 0, unroll=True)


def _conv_mix_kernel(zs_ref, pc_ref, o_ref, *, batch):
    n_rows = zs_ref.shape[1]
    row = lax.broadcasted_iota(jnp.int32, (CHUNK, CHUNK), 0)
    col = lax.broadcasted_iota(jnp.int32, (CHUNK, CHUNK), 1)
    lag = row - col
    band_cur = (lag >= -(CONV_K - 1)) & (lag <= 0)
    band_prev = lag >= CHUNK - (CONV_K - 1)

    def conv_channel(ch, carry):
        z = zs_ref[ch]
        z_prev = jnp.concatenate([jnp.zeros((batch, CHUNK), F32), z[:n_rows - batch]], axis=0)
        lhs = jnp.concatenate([z_prev, z], axis=1).astype(BF16)
        taps = jnp.broadcast_to(pc_ref[ch, _PC_TAPS:_PC_TAPS + 1, :], (CHUNK, CHUNK))
        diag = pltpu.roll(taps, 0, 1, stride=1, stride_axis=0)
        toe = jnp.concatenate([jnp.where(band_prev, diag, 0.0), jnp.where(band_cur, diag, 0.0)],
                              axis=0).astype(BF16)
        o_ref[ch] = _dot(lhs, toe) + pc_ref[ch, _PC_BIAS:_PC_BIAS + 1, :]
        return carry

    lax.fori_loop(0, CONV_HEAD_DIM, conv_channel, 0, unroll=8)

    inv_n = 1.0 / CONV_HEAD_DIM
    for r0 in range(0, n_rows, LN_ROWS):
        rows = pl.ds(r0, LN_ROWS)
        zero = jnp.zeros((LN_ROWS, CHUNK), F32)
        mu = lax.fori_loop(0, CONV_HEAD_DIM, lambda ch, acc: acc + o_ref[ch, rows, :], zero,
                           unroll=4) * inv_n
        var = lax.fori_loop(0, CONV_HEAD_DIM, lambda ch, acc: acc + jnp.square(o_ref[ch, rows, :] - mu),
                            zero, unroll=4) * inv_n
        rstd = lax.rsqrt(var + EPS)

        def norm_channel(ch, carry):
            zn = ((o_ref[ch, rows, :] - mu) * rstd * pc_ref[ch, _PC_GAIN:_PC_GAIN + 1, :]
                  + pc_ref[ch, _PC_BETA:_PC_BETA + 1, :])
            o_ref[ch, rows, :] = zn * jax.nn.sigmoid(zn)
            return carry

        lax.fori_loop(0, CONV_HEAD_DIM, norm_channel, 0, unroll=4)


def _out_ffn2_norm_kernel(x1_ref, ys_ref, yc_ref, wglu_t_ref, bglu_ref, wo_ref,
                          g2_ref, wg_ref, wu_ref, wd_ref, gf_ref, o_ref):
    first_seq = pl.program_id(1) * TILE_SEQS
    ys_t = _load_slabs(ys_ref, first_seq)
    yc_t = _load_slabs(yc_ref, first_seq)
    gate_t = jax.nn.sigmoid(_dot(wglu_t_ref[...], ys_t.astype(BF16)) + bglu_ref[...])
    mix = (lax.dot_general((ys_t * gate_t).astype(BF16), wo_ref[0:S5_WIDTH, :], _TN, preferred_element_type=F32)
           + lax.dot_general(yc_t.astype(BF16), wo_ref[S5_WIDTH:, :], _TN, preferred_element_type=F32))
    x2 = x1_ref[...].reshape(ROW_TILE, D_MODEL) + mix
    h, r = _rms_split(x2, g2_ref[...])
    x3 = x2 + 0.5 * _swiglu(h, r, wg_ref, wu_ref, wd_ref)
    o_ref[...] = _rms_norm(x3, gf_ref[...]).reshape(TILE_SEQS, CHUNK, D_MODEL)


def _resident(shape):
    nd = len(shape)
    return pl.BlockSpec(shape, lambda *_: (0,) * nd, pipeline_mode=pl.Buffered(1))


def _tile_spec():
    return pl.BlockSpec((TILE_SEQS, CHUNK, D_MODEL), lambda c, h: (h, c, 0))


def _slab_spec(n_ch):
    return pl.BlockSpec((n_ch, SUBLANES, LANES), lambda c, h: (0, c, 0))


_DENSE_PARAMS = pltpu.CompilerParams(dimension_semantics=("arbitrary", "arbitrary"),
                                     vmem_limit_bytes=VMEM_LIMIT)
_MIX_PARAMS = pltpu.CompilerParams(dimension_semantics=("arbitrary",), vmem_limit_bytes=VMEM_LIMIT)


def _ffn1_in_proj(x, g1, wg, wu, wd, gm, win_t):
    batch, seq, _ = x.shape
    assert batch == SUBLANES and seq % CHUNK == 0
    n_rows = (seq // CHUNK) * batch
    return pl.pallas_call(
        _ffn1_in_proj_kernel,
        grid=(seq // CHUNK, batch // TILE_SEQS),
        in_specs=[
            _tile_spec(),
            _resident((1, D_MODEL)),
            _resident((D_MODEL, D_FF)),
            _resident((D_MODEL, D_FF)),
            _resident((D_FF, D_MODEL)),
            _resident((1, D_MODEL)),
            _resident((IN_COLS, D_MODEL)),
        ],
        out_specs=[_tile_spec(), _slab_spec(S5_WIDTH), _slab_spec(CONV_WIDTH)],
        out_shape=[
            jax.ShapeDtypeStruct((batch, seq, D_MODEL), F32),
            jax.ShapeDtypeStruct((S5_WIDTH, n_rows, LANES), F32),
            jax.ShapeDtypeStruct((CONV_WIDTH, n_rows, LANES), F32),
        ],
        compiler_params=_DENSE_PARAMS,
        name="ffn1_in_proj",
    )(x, g1, wg, wu, wd, gm, win_t)


def _s5_mix(us, tables, *, batch):
    n_rows = us.shape[1]
    slab = pl.BlockSpec((S5_GROUP_CH, n_rows, LANES), lambda g: (g, 0, 0))

    def per_group(t):
        nd = t.ndim - 1
        return pl.BlockSpec((None,) + t.shape[1:], lambda g: (g,) + (0,) * nd)

    return pl.pallas_call(
        functools.partial(_s5_mix_kernel, n_chunks=n_rows // batch, batch=batch),
        grid=(S5_GROUPS,),
        in_specs=[slab] + [per_group(t) for t in tables],
        out_specs=slab,
        out_shape=jax.ShapeDtypeStruct((S5_WIDTH, n_rows, LANES), F32),
        scratch_shapes=[pltpu.VMEM((S5_GROUP_CH * S5_GROUP_CH, 2 * S5_STATE), F32),
                        pltpu.VMEM((S5_GROUP_CH * S5_GROUP_CH, CHUNK), F32),
                        pltpu.VMEM((n_rows, S5_LANES), BF16),
                        pltpu.VMEM((S5_LANES, 4 * S5_STATE), BF16),
                        pltpu.VMEM((S5_LANES, 2 * S5_STATE), BF16),
                        pltpu.VMEM((n_rows, 2 * S5_STATE), F32)],
        compiler_params=_MIX_PARAMS,
        name="s5_mix",
    )(us, *tables)


def _conv_mix(zs, per_channel, *, batch):
    n_rows = zs.shape[1]
    slab = pl.BlockSpec((CONV_HEAD_DIM, n_rows, LANES), lambda h: (h, 0, 0))
    per_ch = pl.BlockSpec((CONV_HEAD_DIM, _PC_COUNT, LANES), lambda h: (h, 0, 0))
    return pl.pallas_call(
        functools.partial(_conv_mix_kernel, batch=batch),
        grid=(CONV_HEADS,),
        in_specs=[slab, per_ch],
        out_specs=slab,
        out_shape=jax.ShapeDtypeStruct((CONV_WIDTH, n_rows, LANES), F32),
        compiler_params=_MIX_PARAMS,
        name="conv_mix",
    )(zs, per_channel)


def _out_ffn2_norm(x1, ys, yc, wglu_t, bglu, wo, g2, wg, wu, wd, gf):
    batch, seq, _ = x1.shape
    return pl.pallas_call(
        _out_ffn2_norm_kernel,
        grid=(seq // CHUNK, batch // TILE_SEQS),
        in_specs=[
            _tile_spec(),
            _slab_spec(S5_WIDTH),
            _slab_spec(CONV_WIDTH),
            _resident((S5_WIDTH, S5_WIDTH)),
            _resident((S5_WIDTH, 1)),
            _resident((S5_WIDTH + CONV_WIDTH, D_MODEL)),
            _resident((1, D_MODEL)),
            _resident((D_MODEL, D_FF)),
            _resident((D_MODEL, D_FF)),
            _resident((D_FF, D_MODEL)),
            _resident((1, D_MODEL)),
        ],
        out_specs=_tile_spec(),
        out_shape=jax.ShapeDtypeStruct((batch, seq, D_MODEL), F32),
        compiler_params=_DENSE_PARAMS,
        name="out_ffn2_norm",
    )(x1, ys, yc, wglu_t, bglu, wo, g2, wg, wu, wd, gf)


def _s5_tables(lam_re, lam_im, log_dt, b_re, b_im, c_re, c_im, d_skip):
    T, C, P, G = CHUNK, S5_GROUP_CH, S5_STATE, S5_GROUPS
    dt = jnp.exp(log_dt)[:, None]
    mag = jnp.exp(lam_re * dt)
    abar_re = mag * jnp.cos(lam_im * dt)
    abar_im = mag * jnp.sin(lam_im * dt)
    den = lam_re * lam_re + lam_im * lam_im
    num_re = abar_re - 1.0
    num_im = abar_im
    f_re = ((num_re * lam_re + num_im * lam_im) / den)[..., None]
    f_im = ((num_im * lam_re - num_re * lam_im) / den)[..., None]
    bb_re = f_re * b_re - f_im * b_im
    bb_im = f_re * b_im + f_im * b_re
    j = jnp.arange(T + 1, dtype=F32)[None, :, None]
    pmag = jnp.exp(j * (lam_re * dt)[:, None, :])
    pang = j * (lam_im * dt)[:, None, :]
    p_re = pmag * jnp.cos(pang)
    p_im = pmag * jnp.sin(pang)
    cat = lambda a, b: jnp.concatenate([a, b], axis=-1)
    powers = cat(p_re, p_im)
    powers_rev = powers[:, T - 1::-1]
    bt_re = bb_re.transpose(0, 2, 1)
    bt_im = bb_im.transpose(0, 2, 1)
    d_rows = jnp.broadcast_to(d_skip.reshape(G, C, 1), (G, C, LANES))
    vecs = [None] * _V_COUNT
    vecs[_V_BR] = cat(bt_re, bt_re)
    vecs[_V_BI_NP] = cat(-bt_im, bt_im)
    vecs[_V_BI_PN] = cat(bt_im, -bt_im)
    vecs[_V_BI_PP] = cat(bt_im, bt_im)
    vecs[_V_CR_PN] = cat(c_re, -c_re)
    vecs[_V_CI_NN] = cat(-c_im, -c_im)
    vecs[_V_C_RE_NIM] = cat(c_re, -c_im)
    vecs[_V_C_NIM_NRE] = cat(-c_im, -c_re)
    vecs[_V_D] = d_rows
    return powers, powers_rev, jnp.stack(vecs, axis=1)


def _conv_tables(w_dw, b_dw, ln_g, ln_b):
    taps = jnp.pad(w_dw[::-1].T, ((0, 0), (0, LANES - CONV_K)))
    lanes = lambda v: jnp.broadcast_to(v[:, None], (CONV_WIDTH, LANES))
    rows = [None] * _PC_COUNT
    rows[_PC_TAPS], rows[_PC_BIAS], rows[_PC_GAIN], rows[_PC_BETA] = taps, lanes(b_dw), lanes(ln_g), lanes(ln_b)
    return jnp.stack(rows, axis=1)


def kernel(x, ffn1_norm, ffn1_w_gate, ffn1_w_up, ffn1_w_down, mix_norm, w_in, s5_lam_re, s5_lam_im, s5_log_dt, s5_b_re, s5_b_im, s5_c_re, s5_c_im, s5_d, s5_w_glu, s5_b_glu, conv_w_dw, conv_b_dw, conv_ln_g, conv_ln_b, w_out, ffn2_norm, ffn2_w_gate, ffn2_w_up, ffn2_w_down, final_norm):
    batch, seq, d_model = x.shape
    assert ffn1_norm.shape[0] == 1 and d_model == D_MODEL
    l = 0
    row = lambda v: v.reshape(1, -1)

    x1, us, zs = _ffn1_in_proj(
        x, row(ffn1_norm[l]), ffn1_w_gate[l].astype(BF16), ffn1_w_up[l].astype(BF16),
        ffn1_w_down[l].astype(BF16), row(mix_norm[l]), w_in[l].T.astype(BF16))

    tables = _s5_tables(
        s5_lam_re[l], s5_lam_im[l], s5_log_dt[l], s5_b_re[l], s5_b_im[l], s5_c_re[l], s5_c_im[l], s5_d[l])
    ys = _s5_mix(us, tables, batch=batch)

    yc = _conv_mix(zs, _conv_tables(conv_w_dw[l], conv_b_dw[l], conv_ln_g[l], conv_ln_b[l]), batch=batch)

    return _out_ffn2_norm(
        x1, ys, yc, s5_w_glu[l].T.astype(BF16), s5_b_glu[l].reshape(-1, 1), w_out[l].astype(BF16),
        row(ffn2_norm[l]), ffn2_w_gate[l].astype(BF16), ffn2_w_up[l].astype(BF16),
        ffn2_w_down[l].astype(BF16), row(final_norm))
```

```python
import functools

import jax
import jax.numpy as jnp
from jax import lax
from jax.experimental import pallas as pl
from jax.experimental.pallas import tpu as pltpu

D_MODEL = 1024
D_FF = 2816
S5_WIDTH = 512
S5_GROUP_CH = 16
S5_GROUPS = S5_WIDTH // S5_GROUP_CH
S5_STATE = 64
CONV_WIDTH = 512
CONV_HEAD_DIM = 64
CONV_HEADS = CONV_WIDTH // CONV_HEAD_DIM
CONV_K = 31
IN_COLS = S5_WIDTH + 2 * CONV_WIDTH
EPS = 1e-6

LANES = 128
SUBLANES = 8
CHUNK = LANES
S5_LANES = S5_GROUP_CH * CHUNK
S5_STEP_GROUPS = 2
TILE_SEQS = 4
ROW_TILE = TILE_SEQS * CHUNK
FF_SPLITS = ((0, 1536), (1536, 2816))
LN_ROWS = 64
VMEM_LIMIT = 56 * 1024 * 1024

BF16 = jnp.bfloat16
F32 = jnp.float32

(_V_BR,
 _V_BI_NP,
 _V_BI_PN,
 _V_BI_PP,
 _V_CR_PN,
 _V_CI_NN,
 _V_C_RE_NIM,
 _V_C_NIM_NRE,
 _V_D,
 _V_COUNT) = range(10)

_PC_TAPS, _PC_BIAS, _PC_GAIN, _PC_BETA, _PC_COUNT = range(5)

_NT = (((1,), (1,)), ((), ()))
_TN = (((0,), (0,)), ((), ()))


def _rms_norm(x, g):
    return x * lax.rsqrt(jnp.mean(x * x, axis=-1, keepdims=True) + EPS) * g


def _dot(a, b):
    return jnp.dot(a, b, preferred_element_type=F32)


def _rms_split(x, g):
    return (x * g).astype(BF16), lax.rsqrt(jnp.mean(x * x, axis=-1, keepdims=True) + EPS)


def _swiglu(h, row_scale, wg_ref, wu_ref, wd_ref):
    acc = None
    for lo, hi in FF_SPLITS:
        g = _dot(h, wg_ref[:, lo:hi]) * row_scale
        u = _dot(h, wu_ref[:, lo:hi]) * row_scale
        a = (g * jax.nn.sigmoid(g) * u).astype(BF16)
        part = _dot(a, wd_ref[lo:hi, :])
        acc = part if acc is None else acc + part
    return acc


def _store_slabs(slab_ref, val, first_seq):
    n_ch = slab_ref.shape[0]
    rows = slab_ref.reshape(n_ch * SUBLANES, LANES)
    for ct in range(n_ch // SUBLANES):
        for s in range(TILE_SEQS):
            rows[pl.ds(ct * SUBLANES * SUBLANES + first_seq + s, SUBLANES, stride=SUBLANES), :] = (
                val[ct * SUBLANES:(ct + 1) * SUBLANES, s * LANES:(s + 1) * LANES])


def _load_slabs(slab_ref, first_seq):
    n_ch = slab_ref.shape[0]
    rows = slab_ref.reshape(n_ch * SUBLANES, LANES)
    bands = []
    for ct in range(n_ch // SUBLANES):
        bands.append(jnp.concatenate(
            [rows[pl.ds(ct * SUBLANES * SUBLANES + first_seq + s, SUBLANES, stride=SUBLANES), :]
             for s in range(TILE_SEQS)], axis=1))
    return jnp.concatenate(bands, axis=0)


def _ffn1_in_proj_kernel(x_ref, g1_ref, wg_ref, wu_ref, wd_ref, gm_ref, win_t_ref,
                         x1_ref, us_ref, zs_ref):
    first_seq = pl.program_id(1) * TILE_SEQS
    x = x_ref[...].reshape(ROW_TILE, D_MODEL)
    h, r = _rms_split(x, g1_ref[...])
    x1 = x + 0.5 * _swiglu(h, r, wg_ref, wu_ref, wd_ref)
    x1_ref[...] = x1.reshape(TILE_SEQS, CHUNK, D_MODEL)
    h2, r2 = _rms_split(x1, gm_ref[...])
    r2_lanes = jnp.transpose(jnp.broadcast_to(r2, (ROW_TILE, LANES)))[0:1, :]

    def in_proj_t(lo, hi):
        return lax.dot_general(win_t_ref[lo:hi, :], h2, _NT, preferred_element_type=F32) * r2_lanes

    z_t = (in_proj_t(S5_WIDTH, S5_WIDTH + CONV_WIDTH)
           * jax.nn.sigmoid(in_proj_t(S5_WIDTH + CONV_WIDTH, IN_COLS)))
    _store_slabs(zs_ref, z_t, first_seq)
    _store_slabs(us_ref, in_proj_t(0, S5_WIDTH), first_seq)


def _s5_mix_kernel(us_ref, pow_ref, vec_ref, ys_ref, *scratch, n_chunks, batch):
    for g in range(S5_STEP_GROUPS):
        ch = pl.ds(g * S5_GROUP_CH, S5_GROUP_CH)
        _s5_group(us_ref.at[ch], pow_ref.at[g], vec_ref.at[g], ys_ref.at[ch], *scratch,
                  n_chunks=n_chunks, batch=batch)


def _s5_group(us_ref, pow_ref, vec_ref, ys_ref,
              cb_ref, f_ref, lhs_ref, win_ref, wout_ref, sprev_ref, *, n_chunks, batch):
    swap = lambda v: pltpu.roll(v, S5_STATE, 1)
    lo_half = lax.broadcasted_iota(jnp.int32, (1, 2 * S5_STATE), 1) < S5_STATE
    pa_rev = pow_ref[0:CHUNK, :]
    pb_rev = swap(pa_rev)
    pa_fwd = pow_ref[CHUNK + 1:2 * CHUNK + 1, :]
    pb_fwd = swap(pa_fwd)
    for c in range(S5_GROUP_CH):
        rows = slice(c * CHUNK, (c + 1) * CHUNK)
        vec = lambda i: vec_ref[i, c:c + 1, :]
        win_ref[rows, 0:2 * S5_STATE] = (pa_rev * vec(_V_BR) + pb_rev * vec(_V_BI_NP)).astype(BF16)
        win_ref[rows, 2 * S5_STATE:] = (pb_rev * vec(_V_BR) + pa_rev * vec(_V_BI_PN)).astype(BF16)
        wout_ref[rows, :] = (pa_fwd * vec(_V_CR_PN) + pb_fwd * vec(_V_CI_NN)).astype(BF16)
        lhs_ref[:, rows] = us_ref[c].astype(BF16)
        cb_ref[c * S5_GROUP_CH:(c + 1) * S5_GROUP_CH, :] = (
            vec_ref[_V_C_RE_NIM] * vec(_V_BR) + vec_ref[_V_C_NIM_NRE] * vec(_V_BI_PP))
    f_ref[...] = lax.dot_general(cb_ref[...], pow_ref[CHUNK:2 * CHUNK, :], _NT, precision=lax.Precision.HIGHEST,
                                 preferred_element_type=F32)

    e = _dot(lhs_ref[...], win_ref[...])
    a_chunk = pow_ref[2 * CHUNK:2 * CHUNK + 1, :]
    a_sw = swap(a_chunk)
    a1 = jnp.where(lo_half, a_chunk, a_sw)
    a2 = jnp.where(lo_half, -a_sw, a_chunk)
    s = jnp.zeros((batch, 2 * S5_STATE), F32)
    s_sw = jnp.zeros((batch, 2 * S5_STATE), F32)
    for c in range(n_chunks):
        rows = slice(c * batch, (c + 1) * batch)
        sprev_ref[rows, :] = s
        e_c = e[rows, :]
        s, s_sw = (a1 * s + a2 * s_sw + e_c[:, :2 * S5_STATE],
                   a1 * s_sw - a2 * s + e_c[:, 2 * S5_STATE:])
    sprev = sprev_ref[...].astype(BF16)

    lag_row = lax.broadcasted_iota(jnp.int32, (CHUNK, CHUNK), 0)
    lag_col = lax.broadcasted_iota(jnp.int32, (CHUNK, CHUNK), 1)
    causal = lag_col >= lag_row

    def toeplitz(ci, co):
        f = jnp.broadcast_to(f_ref[pl.ds(ci * S5_GROUP_CH + co, 1), :], (CHUNK, CHUNK))
        return jnp.where(causal, pltpu.roll(f, 0, 1, stride=1, stride_axis=0), 0.0).astype(BF16)

    def out_pair(n, carry):
        co0 = 2 * n
        w_state = wout_ref[pl.ds(pl.multiple_of(co0 * CHUNK, 2 * CHUNK), 2 * CHUNK), :]
        acc = lax.dot_general(sprev, w_state, _NT, preferred_element_type=F32)
        for kt in range(S5_GROUP_CH // 2):
            w = jnp.concatenate(
                [jnp.concatenate([toeplitz(2 * kt + r, co0), toeplitz(2 * kt + r, co0 + 1)], axis=1)
                 for r in range(2)], axis=0)
            acc = acc + _dot(lhs_ref[:, 2 * kt * CHUNK:(2 * kt + 2) * CHUNK], w)
        for r in range(2):
            y = acc[:, r * CHUNK:(r + 1) * CHUNK] + vec_ref[_V_D, pl.ds(co0 + r, 1), :] * us_ref[co0 + r]
            ys_ref[co0 + r] = jax.nn.gelu(y, approximate=True)
        return carry

    lax.fori_loop(0, S5_GROUP_CH // 2, out_pair, 0, unroll=True)


def _conv_mix_kernel(zs_ref, pc_ref, o_ref, *, batch):
    n_rows = zs_ref.shape[1]
    row = lax.broadcasted_iota(jnp.int32, (CHUNK, CHUNK), 0)
    col = lax.broadcasted_iota(jnp.int32, (CHUNK, CHUNK), 1)
    lag = row - col
    band_cur = (lag >= -(CONV_K - 1)) & (lag <= 0)
    band_prev = lag >= CHUNK - (CONV_K - 1)

    def conv_channel(ch, carry):
        z = zs_ref[ch]
        z_prev = jnp.concatenate([jnp.zeros((batch, CHUNK), F32), z[:n_rows - batch]], axis=0)
        lhs = jnp.concatenate([z_prev, z], axis=1).astype(BF16)
        taps = jnp.broadcast_to(pc_ref[ch, _PC_TAPS:_PC_TAPS + 1, :], (CHUNK, CHUNK))
        diag = pltpu.roll(taps, 0, 1, stride=1, stride_axis=0)
        toe = jnp.concatenate([jnp.where(band_prev, diag, 0.0), jnp.where(band_cur, diag, 0.0)],
                              axis=0).astype(BF16)
        o_ref[ch] = _dot(lhs, toe) + pc_ref[ch, _PC_BIAS:_PC_BIAS + 1, :]
        return carry

    lax.fori_loop(0, CONV_HEAD_DIM, conv_channel, 0, unroll=8)

    inv_n = 1.0 / CONV_HEAD_DIM
    for r0 in range(0, n_rows, LN_ROWS):
        rows = pl.ds(r0, LN_ROWS)
        zero = jnp.zeros((LN_ROWS, CHUNK), F32)
        mu = lax.fori_loop(0, CONV_HEAD_DIM, lambda ch, acc: acc + o_ref[ch, rows, :], zero,
                           unroll=4) * inv_n
        var = lax.fori_loop(0, CONV_HEAD_DIM, lambda ch, acc: acc + jnp.square(o_ref[ch, rows, :] - mu),
                            zero, unroll=4) * inv_n
        rstd = lax.rsqrt(var + EPS)

        def norm_channel(ch, carry):
            zn = ((o_ref[ch, rows, :] - mu) * rstd * pc_ref[ch, _PC_GAIN:_PC_GAIN + 1, :]
                  + pc_ref[ch, _PC_BETA:_PC_BETA + 1, :])
            o_ref[ch, rows, :] = zn * jax.nn.sigmoid(zn)
            return carry

        lax.fori_loop(0, CONV_HEAD_DIM, norm_channel, 0, unroll=4)


def _out_ffn2_norm_kernel(x1_ref, ys_ref, yc_ref, wglu_t_ref, bglu_ref, wo_ref,
                          g2_ref, wg_ref, wu_ref, wd_ref, gf_ref, o_ref):
    first_seq = pl.program_id(1) * TILE_SEQS
    ys_t = _load_slabs(ys_ref, first_seq)
    yc_t = _load_slabs(yc_ref, first_seq)
    gate_t = jax.nn.sigmoid(_dot(wglu_t_ref[...], ys_t.astype(BF16)) + bglu_ref[...])
    mix = (lax.dot_general((ys_t * gate_t).astype(BF16), wo_ref[0:S5_WIDTH, :], _TN, preferred_element_type=F32)
           + lax.dot_general(yc_t.astype(BF16), wo_ref[S5_WIDTH:, :], _TN, preferred_element_type=F32))
    x2 = x1_ref[...].reshape(ROW_TILE, D_MODEL) + mix
    h, r = _rms_split(x2, g2_ref[...])
    x3 = x2 + 0.5 * _swiglu(h, r, wg_ref, wu_ref, wd_ref)
    o_ref[...] = _rms_norm(x3, gf_ref[...]).reshape(TILE_SEQS, CHUNK, D_MODEL)


def _resident(shape):
    nd = len(shape)
    return pl.BlockSpec(shape, lambda *_: (0,) * nd, pipeline_mode=pl.Buffered(1))


def _tile_spec():
    return pl.BlockSpec((TILE_SEQS, CHUNK, D_MODEL), lambda c, h: (h, c, 0))


def _slab_spec(n_ch):
    return pl.BlockSpec((n_ch, SUBLANES, LANES), lambda c, h: (0, c, 0))


_DENSE_PARAMS = pltpu.CompilerParams(dimension_semantics=("arbitrary", "arbitrary"),
                                     vmem_limit_bytes=VMEM_LIMIT)
_MIX_PARAMS = pltpu.CompilerParams(dimension_semantics=("arbitrary",), vmem_limit_bytes=VMEM_LIMIT)


def _ffn1_in_proj(x, g1, wg, wu, wd, gm, win_t):
    batch, seq, _ = x.shape
    assert batch == SUBLANES and seq % CHUNK == 0
    n_rows = (seq // CHUNK) * batch
    return pl.pallas_call(
        _ffn1_in_proj_kernel,
        grid=(seq // CHUNK, batch // TILE_SEQS),
        in_specs=[
            _tile_spec(),
            _resident((1, D_MODEL)),
            _resident((D_MODEL, D_FF)),
            _resident((D_MODEL, D_FF)),
            _resident((D_FF, D_MODEL)),
            _resident((1, D_MODEL)),
            _resident((IN_COLS, D_MODEL)),
        ],
        out_specs=[_tile_spec(), _slab_spec(S5_WIDTH), _slab_spec(CONV_WIDTH)],
        out_shape=[
            jax.ShapeDtypeStruct((batch, seq, D_MODEL), F32),
            jax.ShapeDtypeStruct((S5_WIDTH, n_rows, LANES), F32),
            jax.ShapeDtypeStruct((CONV_WIDTH, n_rows, LANES), F32),
        ],
        compiler_params=_DENSE_PARAMS,
        name="ffn1_in_proj",
    )(x, g1, wg, wu, wd, gm, win_t)


def _s5_mix(us, tables, *, batch):
    n_rows = us.shape[1]
    slab = pl.BlockSpec((S5_STEP_GROUPS * S5_GROUP_CH, n_rows, LANES), lambda g: (g, 0, 0))

    def per_group(t):
        nd = t.ndim - 1
        return pl.BlockSpec((S5_STEP_GROUPS,) + t.shape[1:], lambda g: (g,) + (0,) * nd)

    return pl.pallas_call(
        functools.partial(_s5_mix_kernel, n_chunks=n_rows // batch, batch=batch),
        grid=(S5_GROUPS // S5_STEP_GROUPS,),
        in_specs=[slab] + [per_group(t) for t in tables],
        out_specs=slab,
        out_shape=jax.ShapeDtypeStruct((S5_WIDTH, n_rows, LANES), F32),
        scratch_shapes=[pltpu.VMEM((S5_GROUP_CH * S5_GROUP_CH, 2 * S5_STATE), F32),
                        pltpu.VMEM((S5_GROUP_CH * S5_GROUP_CH, CHUNK), F32),
                        pltpu.VMEM((n_rows, S5_LANES), BF16),
                        pltpu.VMEM((S5_LANES, 4 * S5_STATE), BF16),
                        pltpu.VMEM((S5_LANES, 2 * S5_STATE), BF16),
                        pltpu.VMEM((n_rows, 2 * S5_STATE), F32)],
        compiler_params=_MIX_PARAMS,
        name="s5_mix",
    )(us, *tables)


def _conv_mix(zs, per_channel, *, batch):
    n_rows = zs.shape[1]
    slab = pl.BlockSpec((CONV_HEAD_DIM, n_rows, LANES), lambda h: (h, 0, 0))
    per_ch = pl.BlockSpec((CONV_HEAD_DIM, _PC_COUNT, LANES), lambda h: (h, 0, 0))
    return pl.pallas_call(
        functools.partial(_conv_mix_kernel, batch=batch),
        grid=(CONV_HEADS,),
        in_specs=[slab, per_ch],
        out_specs=slab,
        out_shape=jax.ShapeDtypeStruct((CONV_WIDTH, n_rows, LANES), F32),
        compiler_params=_MIX_PARAMS,
        name="conv_mix",
    )(zs, per_channel)


def _out_ffn2_norm(x1, ys, yc, wglu_t, bglu, wo, g2, wg, wu, wd, gf):
    batch, seq, _ = x1.shape
    return pl.pallas_call(
        _out_ffn2_norm_kernel,
        grid=(seq // CHUNK, batch // TILE_SEQS),
        in_specs=[
            _tile_spec(),
            _slab_spec(S5_WIDTH),
            _slab_spec(CONV_WIDTH),
            _resident((S5_WIDTH, S5_WIDTH)),
            _resident((S5_WIDTH, 1)),
            _resident((S5_WIDTH + CONV_WIDTH, D_MODEL)),
            _resident((1, D_MODEL)),
            _resident((D_MODEL, D_FF)),
            _resident((D_MODEL, D_FF)),
            _resident((D_FF, D_MODEL)),
            _resident((1, D_MODEL)),
        ],
        out_specs=_tile_spec(),
        out_shape=jax.ShapeDtypeStruct((batch, seq, D_MODEL), F32),
        compiler_params=_DENSE_PARAMS,
        name="out_ffn2_norm",
    )(x1, ys, yc, wglu_t, bglu, wo, g2, wg, wu, wd, gf)


def _s5_tables(lam_re, lam_im, log_dt, b_re, b_im, c_re, c_im, d_skip):
    T, C, P, G = CHUNK, S5_GROUP_CH, S5_STATE, S5_GROUPS
    dt = jnp.exp(log_dt)[:, None]
    mag = jnp.exp(lam_re * dt)
    abar_re = mag * jnp.cos(lam_im * dt)
    abar_im = mag * jnp.sin(lam_im * dt)
    den = lam_re * lam_re + lam_im * lam_im
    num_re = abar_re - 1.0
    num_im = abar_im
    f_re = ((num_re * lam_re + num_im * lam_im) / den)[..., None]
    f_im = ((num_im * lam_re - num_re * lam_im) / den)[..., None]
    bb_re = f_re * b_re - f_im * b_im
    bb_im = f_re * b_im + f_im * b_re
    j = jnp.concatenate([jnp.arange(T - 1, -1, -1), jnp.arange(T + 1)]).astype(F32)[None, :, None]
    pmag = jnp.exp(j * (lam_re * dt)[:, None, :])
    pang = j * (lam_im * dt)[:, None, :]
    p_re = pmag * jnp.cos(pang)
    p_im = pmag * jnp.sin(pang)
    cat = lambda a, b: jnp.concatenate([a, b], axis=-1)
    powers = cat(p_re, p_im)
    bt_re = bb_re.transpose(0, 2, 1)
    bt_im = bb_im.transpose(0, 2, 1)
    d_rows = jnp.broadcast_to(d_skip.reshape(G, C, 1), (G, C, LANES))
    vecs = [None] * _V_COUNT
    vecs[_V_BR] = cat(bt_re, bt_re)
    vecs[_V_BI_NP] = cat(-bt_im, bt_im)
    vecs[_V_BI_PN] = cat(bt_im, -bt_im)
    vecs[_V_BI_PP] = cat(bt_im, bt_im)
    vecs[_V_CR_PN] = cat(c_re, -c_re)
    vecs[_V_CI_NN] = cat(-c_im, -c_im)
    vecs[_V_C_RE_NIM] = cat(c_re, -c_im)
    vecs[_V_C_NIM_NRE] = cat(-c_im, -c_re)
    vecs[_V_D] = d_rows
    return powers, jnp.stack(vecs, axis=1)


def _conv_tables(w_dw, b_dw, ln_g, ln_b):
    taps = jnp.pad(w_dw[::-1].T, ((0, 0), (0, LANES - CONV_K)))
    lanes = lambda v: jnp.broadcast_to(v[:, None], (CONV_WIDTH, LANES))
    rows = [None] * _PC_COUNT
    rows[_PC_TAPS], rows[_PC_BIAS], rows[_PC_GAIN], rows[_PC_BETA] = taps, lanes(b_dw), lanes(ln_g), lanes(ln_b)
    return jnp.stack(rows, axis=1)


def kernel(x, ffn1_norm, ffn1_w_gate, ffn1_w_up, ffn1_w_down, mix_norm, w_in, s5_lam_re, s5_lam_im, s5_log_dt, s5_b_re, s5_b_im, s5_c_re, s5_c_im, s5_d, s5_w_glu, s5_b_glu, conv_w_dw, conv_b_dw, conv_ln_g, conv_ln_b, w_out, ffn2_norm, ffn2_w_gate, ffn2_w_up, ffn2_w_down, final_norm):
    batch, seq, d_model = x.shape
    assert ffn1_norm.shape[0] == 1 and d_model == D_MODEL
    l = 0
    row = lambda v: v.reshape(1, -1)

    x1, us, zs = _ffn1_in_proj(
        x, row(ffn1_norm[l]), ffn1_w_gate[l].astype(BF16), ffn1_w_up[l].astype(BF16),
        ffn1_w_down[l].astype(BF16), row(mix_norm[l]), w_in[l].T.astype(BF16))

    tables = _s5_tables(
        s5_lam_re[l], s5_lam_im[l], s5_log_dt[l], s5_b_re[l], s5_b_im[l], s5_c_re[l], s5_c_im[l], s5_d[l])
    ys = _s5_mix(us, tables, batch=batch)

    yc = _conv_mix(zs, _conv_tables(conv_w_dw[l], conv_b_dw[l], conv_ln_g[l], conv_ln_b[l]), batch=batch)

    return _out_ffn2_norm(
        x1, ys, yc, s5_w_glu[l].T.astype(BF16), s5_b_glu[l].reshape(-1, 1), w_out[l].astype(BF16),
        row(ffn2_norm[l]), ffn2_w_gate[l].astype(BF16), ffn2_w_up[l].astype(BF16),
        ffn2_w_down[l].astype(BF16), row(final_norm))
```

```python
import functools

import jax
import jax.numpy as jnp
from jax import lax
from jax.experimental import pallas as pl
from jax.experimental.pallas import tpu as pltpu

D_MODEL = 1024
D_FF = 2816
S5_WIDTH = 512
S5_GROUP_CH = 16
S5_GROUPS = S5_WIDTH // S5_GROUP_CH
S5_STATE = 64
CONV_WIDTH = 512
CONV_HEAD_DIM = 64
CONV_HEADS = CONV_WIDTH // CONV_HEAD_DIM
CONV_K = 31
IN_COLS = S5_WIDTH + 2 * CONV_WIDTH
EPS = 1e-6

LANES = 128
SUBLANES = 8
CHUNK = LANES
S5_LANES = S5_GROUP_CH * CHUNK
S5_STEP_GROUPS = 2
TILE_SEQS = 4
ROW_TILE = TILE_SEQS * CHUNK
FF_SPLITS = ((0, 1536), (1536, 2816))
LN_ROWS = 64
VMEM_LIMIT = 56 * 1024 * 1024

BF16 = jnp.bfloat16
F32 = jnp.float32

(_V_BR,
 _V_BI_NP,
 _V_BI_PN,
 _V_BI_PP,
 _V_CR_PN,
 _V_CI_NN,
 _V_C_RE_NIM,
 _V_C_NIM_NRE,
 _V_D,
 _V_COUNT) = range(10)

_PC_TAPS, _PC_BIAS, _PC_HALF_GAIN, _PC_HALF_BETA, _PC_COUNT = range(5)

_NT = (((1,), (1,)), ((), ()))
_TN = (((0,), (0,)), ((), ()))


def _rms_norm(x, g):
    return x * lax.rsqrt(jnp.mean(x * x, axis=-1, keepdims=True) + EPS) * g


def _dot(a, b):
    return jnp.dot(a, b, preferred_element_type=F32)


def _rms_split(x, g):
    return (x * g).astype(BF16), lax.rsqrt(jnp.mean(x * x, axis=-1, keepdims=True) + EPS)


def _swiglu(h, row_scale, wg_ref, wu_ref, wd_ref):
    acc = None
    for lo, hi in FF_SPLITS:
        g = _dot(h, wg_ref[:, lo:hi]) * row_scale
        u = _dot(h, wu_ref[:, lo:hi]) * row_scale
        a = (g * jax.nn.sigmoid(g) * u).astype(BF16)
        part = _dot(a, wd_ref[lo:hi, :])
        acc = part if acc is None else acc + part
    return acc


def _store_slabs(slab_ref, val, first_seq):
    n_ch = slab_ref.shape[0]
    rows = slab_ref.reshape(n_ch * SUBLANES, LANES)
    for ct in range(n_ch // SUBLANES):
        for s in range(TILE_SEQS):
            rows[pl.ds(ct * SUBLANES * SUBLANES + first_seq + s, SUBLANES, stride=SUBLANES), :] = (
                val[ct * SUBLANES:(ct + 1) * SUBLANES, s * LANES:(s + 1) * LANES])


def _load_slabs(slab_ref, first_seq):
    n_ch = slab_ref.shape[0]
    rows = slab_ref.reshape(n_ch * SUBLANES, LANES)
    bands = []
    for ct in range(n_ch // SUBLANES):
        bands.append(jnp.concatenate(
            [rows[pl.ds(ct * SUBLANES * SUBLANES + first_seq + s, SUBLANES, stride=SUBLANES), :]
             for s in range(TILE_SEQS)], axis=1))
    return jnp.concatenate(bands, axis=0)


def _ffn1_in_proj_kernel(x_ref, g1_ref, wg_ref, wu_ref, wd_ref, gm_ref, win_t_ref,
                         x1_ref, us_ref, zs_ref):
    first_seq = pl.program_id(1) * TILE_SEQS
    x = x_ref[...].reshape(ROW_TILE, D_MODEL)
    h, r = _rms_split(x, g1_ref[...])
    x1 = x + 0.5 * _swiglu(h, r, wg_ref, wu_ref, wd_ref)
    x1_ref[...] = x1.reshape(TILE_SEQS, CHUNK, D_MODEL)
    h2, r2 = _rms_split(x1, gm_ref[...])
    r2_lanes = jnp.transpose(jnp.broadcast_to(r2, (ROW_TILE, LANES)))[0:1, :]

    def in_proj_t(lo, hi):
        return lax.dot_general(win_t_ref[lo:hi, :], h2, _NT, preferred_element_type=F32) * r2_lanes

    z_t = (in_proj_t(S5_WIDTH, S5_WIDTH + CONV_WIDTH)
           * jax.nn.sigmoid(in_proj_t(S5_WIDTH + CONV_WIDTH, IN_COLS)))
    _store_slabs(zs_ref, z_t, first_seq)
    _store_slabs(us_ref, in_proj_t(0, S5_WIDTH), first_seq)


def _s5_mix_kernel(us_ref, pow_ref, vec_ref, ys_ref, *scratch, n_chunks, batch):
    for g in range(S5_STEP_GROUPS):
        ch = pl.ds(g * S5_GROUP_CH, S5_GROUP_CH)
        _s5_group(us_ref.at[ch], pow_ref.at[g], vec_ref.at[g], ys_ref.at[ch], *scratch,
                  n_chunks=n_chunks, batch=batch)


def _s5_group(us_ref, pow_ref, vec_ref, ys_ref,
              cb_ref, f_ref, lhs_ref, win_ref, wout_ref, sprev_ref, *, n_chunks, batch):
    swap = lambda v: pltpu.roll(v, S5_STATE, 1)
    lo_half = lax.broadcasted_iota(jnp.int32, (1, 2 * S5_STATE), 1) < S5_STATE
    pa_rev = pow_ref[0:CHUNK, :]
    pb_rev = swap(pa_rev)
    pa_fwd = pow_ref[CHUNK + 1:2 * CHUNK + 1, :]
    pb_fwd = swap(pa_fwd)
    for c in range(S5_GROUP_CH):
        rows = slice(c * CHUNK, (c + 1) * CHUNK)
        vec = lambda i: vec_ref[i, c:c + 1, :]
        win_ref[rows, 0:2 * S5_STATE] = (pa_rev * vec(_V_BR) + pb_rev * vec(_V_BI_NP)).astype(BF16)
        win_ref[rows, 2 * S5_STATE:] = (pb_rev * vec(_V_BR) + pa_rev * vec(_V_BI_PN)).astype(BF16)
        wout_ref[rows, :] = (pa_fwd * vec(_V_CR_PN) + pb_fwd * vec(_V_CI_NN)).astype(BF16)
        lhs_ref[:, rows] = us_ref[c].astype(BF16)
        cb_ref[c * S5_GROUP_CH:(c + 1) * S5_GROUP_CH, :] = (
            vec_ref[_V_C_RE_NIM] * vec(_V_BR) + vec_ref[_V_C_NIM_NRE] * vec(_V_BI_PP))
    f_ref[...] = lax.dot_general(cb_ref[...], pow_ref[CHUNK:2 * CHUNK, :], _NT, precision=lax.Precision.HIGHEST,
                                 preferred_element_type=F32)

    e = _dot(lhs_ref[...], win_ref[...])
    a_chunk = pow_ref[2 * CHUNK:2 * CHUNK + 1, :]
    a_sw = swap(a_chunk)
    a1 = jnp.where(lo_half, a_chunk, a_sw)
    a2 = jnp.where(lo_half, -a_sw, a_chunk)
    s = jnp.zeros((batch, 2 * S5_STATE), F32)
    s_sw = jnp.zeros((batch, 2 * S5_STATE), F32)
    for c in range(n_chunks):
        rows = slice(c * batch, (c + 1) * batch)
        sprev_ref[rows, :] = s
        e_c = e[rows, :]
        s, s_sw = (a1 * s + a2 * s_sw + e_c[:, :2 * S5_STATE],
                   a1 * s_sw - a2 * s + e_c[:, 2 * S5_STATE:])
    sprev = sprev_ref[...].astype(BF16)

    lag_row = lax.broadcasted_iota(jnp.int32, (CHUNK, CHUNK), 0)
    lag_col = lax.broadcasted_iota(jnp.int32, (CHUNK, CHUNK), 1)
    causal = lag_col >= lag_row

    def toeplitz(ci, co):
        f = jnp.broadcast_to(f_ref[pl.ds(ci * S5_GROUP_CH + co, 1), :], (CHUNK, CHUNK))
        return jnp.where(causal, pltpu.roll(f, 0, 1, stride=1, stride_axis=0), 0.0).astype(BF16)

    def out_pair(n, carry):
        co0 = 2 * n
        w_state = wout_ref[pl.ds(pl.multiple_of(co0 * CHUNK, 2 * CHUNK), 2 * CHUNK), :]
        acc = lax.dot_general(sprev, w_state, _NT, preferred_element_type=F32)
        for kt in range(S5_GROUP_CH // 2):
            w = jnp.concatenate(
                [jnp.concatenate([toeplitz(2 * kt + r, co0), toeplitz(2 * kt + r, co0 + 1)], axis=1)
                 for r in range(2)], axis=0)
            acc = acc + _dot(lhs_ref[:, 2 * kt * CHUNK:(2 * kt + 2) * CHUNK], w)
        for r in range(2):
            y = acc[:, r * CHUNK:(r + 1) * CHUNK] + vec_ref[_V_D, pl.ds(co0 + r, 1), :] * us_ref[co0 + r]
            ys_ref[co0 + r] = jax.nn.gelu(y, approximate=True)
        return carry

    lax.fori_loop(0, S5_GROUP_CH // 2, out_pair, 0, unroll=True)


def _conv_mix_kernel(zs_ref, pc_ref, o_ref, *, batch):
    n_rows = zs_ref.shape[1]
    row = lax.broadcasted_iota(jnp.int32, (CHUNK, CHUNK), 0)
    col = lax.broadcasted_iota(jnp.int32, (CHUNK, CHUNK), 1)
    lag = row - col
    band_cur = (lag >= -(CONV_K - 1)) & (lag <= 0)
    band_prev = lag >= CHUNK - (CONV_K - 1)

    def conv_channel(ch, carry):
        z = zs_ref[ch]
        z_prev = jnp.concatenate([jnp.zeros((batch, CHUNK), F32), z[:n_rows - batch]], axis=0)
        lhs = jnp.concatenate([z_prev, z], axis=1).astype(BF16)
        taps = jnp.broadcast_to(pc_ref[ch, _PC_TAPS:_PC_TAPS + 1, :], (CHUNK, CHUNK))
        diag = pltpu.roll(taps, 0, 1, stride=1, stride_axis=0)
        toe = jnp.concatenate([jnp.where(band_prev, diag, 0.0), jnp.where(band_cur, diag, 0.0)],
                              axis=0).astype(BF16)
        o_ref[ch] = _dot(lhs, toe) + pc_ref[ch, _PC_BIAS:_PC_BIAS + 1, :]
        return carry

    lax.fori_loop(0, CONV_HEAD_DIM, conv_channel, 0, unroll=8)

    inv_n = 1.0 / CONV_HEAD_DIM
    for r0 in range(0, n_rows, LN_ROWS):
        rows = pl.ds(r0, LN_ROWS)
        zero = jnp.zeros((LN_ROWS, CHUNK), F32)
        mu = lax.fori_loop(0, CONV_HEAD_DIM, lambda ch, acc: acc + o_ref[ch, rows, :], zero,
                           unroll=4) * inv_n
        var = lax.fori_loop(0, CONV_HEAD_DIM, lambda ch, acc: acc + jnp.square(o_ref[ch, rows, :] - mu),
                            zero, unroll=4) * inv_n
        rstd = lax.rsqrt(var + EPS)

        def norm_channel(ch, carry):
            half = ((o_ref[ch, rows, :] - mu) * rstd * pc_ref[ch, _PC_HALF_GAIN:_PC_HALF_GAIN + 1, :]
                    + pc_ref[ch, _PC_HALF_BETA:_PC_HALF_BETA + 1, :])
            o_ref[ch, rows, :] = half + half * jnp.tanh(half)
            return carry

        lax.fori_loop(0, CONV_HEAD_DIM, norm_channel, 0, unroll=4)


def _out_ffn2_norm_kernel(x1_ref, ys_ref, yc_ref, wglu_t_ref, bglu_ref, wo_ref,
                          g2_ref, wg_ref, wu_ref, wd_ref, gf_ref, o_ref):
    first_seq = pl.program_id(1) * TILE_SEQS
    ys_t = _load_slabs(ys_ref, first_seq)
    yc_t = _load_slabs(yc_ref, first_seq)
    gate_t = jax.nn.sigmoid(_dot(wglu_t_ref[...], ys_t.astype(BF16)) + bglu_ref[...])
    mix = (lax.dot_general((ys_t * gate_t).astype(BF16), wo_ref[0:S5_WIDTH, :], _TN, preferred_element_type=F32)
           + lax.dot_general(yc_t.astype(BF16), wo_ref[S5_WIDTH:, :], _TN, preferred_element_type=F32))
    x2 = x1_ref[...].reshape(ROW_TILE, D_MODEL) + mix
    h, r = _rms_split(x2, g2_ref[...])
    x3 = x2 + 0.5 * _swiglu(h, r, wg_ref, wu_ref, wd_ref)
    o_ref[...] = _rms_norm(x3, gf_ref[...]).reshape(TILE_SEQS, CHUNK, D_MODEL)


def _resident(shape):
    nd = len(shape)
    return pl.BlockSpec(shape, lambda *_: (0,) * nd, pipeline_mode=pl.Buffered(1))


def _tile_spec():
    return pl.BlockSpec((TILE_SEQS, CHUNK, D_MODEL), lambda c, h: (h, c, 0))


def _slab_spec(n_ch):
    return pl.BlockSpec((n_ch, SUBLANES, LANES), lambda c, h: (0, c, 0))


_DENSE_PARAMS = pltpu.CompilerParams(dimension_semantics=("arbitrary", "arbitrary"),
                                     vmem_limit_bytes=VMEM_LIMIT)
_MIX_PARAMS = pltpu.CompilerParams(dimension_semantics=("arbitrary",), vmem_limit_bytes=VMEM_LIMIT)


def _ffn1_in_proj(x, g1, wg, wu, wd, gm, win_t):
    batch, seq, _ = x.shape
    assert batch == SUBLANES and seq % CHUNK == 0
    n_rows = (seq // CHUNK) * batch
    return pl.pallas_call(
        _ffn1_in_proj_kernel,
        grid=(seq // CHUNK, batch // TILE_SEQS),
        in_specs=[
            _tile_spec(),
            _resident((1, D_MODEL)),
            _resident((D_MODEL, D_FF)),
            _resident((D_MODEL, D_FF)),
            _resident((D_FF, D_MODEL)),
            _resident((1, D_MODEL)),
            _resident((IN_COLS, D_MODEL)),
        ],
        out_specs=[_tile_spec(), _slab_spec(S5_WIDTH), _slab_spec(CONV_WIDTH)],
        out_shape=[
            jax.ShapeDtypeStruct((batch, seq, D_MODEL), F32),
            jax.ShapeDtypeStruct((S5_WIDTH, n_rows, LANES), F32),
            jax.ShapeDtypeStruct((CONV_WIDTH, n_rows, LANES), F32),
        ],
        compiler_params=_DENSE_PARAMS,
        name="ffn1_in_proj",
    )(x, g1, wg, wu, wd, gm, win_t)


def _s5_mix(us, tables, *, batch):
    n_rows = us.shape[1]
    slab = pl.BlockSpec((S5_STEP_GROUPS * S5_GROUP_CH, n_rows, LANES), lambda g: (g, 0, 0))

    def per_group(t):
        nd = t.ndim - 1
        return pl.BlockSpec((S5_STEP_GROUPS,) + t.shape[1:], lambda g: (g,) + (0,) * nd)

    return pl.pallas_call(
        functools.partial(_s5_mix_kernel, n_chunks=n_rows // batch, batch=batch),
        grid=(S5_GROUPS // S5_STEP_GROUPS,),
        in_specs=[slab] + [per_group(t) for t in tables],
        out_specs=slab,
        out_shape=jax.ShapeDtypeStruct((S5_WIDTH, n_rows, LANES), F32),
        scratch_shapes=[pltpu.VMEM((S5_GROUP_CH * S5_GROUP_CH, 2 * S5_STATE), F32),
                        pltpu.VMEM((S5_GROUP_CH * S5_GROUP_CH, CHUNK), F32),
                        pltpu.VMEM((n_rows, S5_LANES), BF16),
                        pltpu.VMEM((S5_LANES, 4 * S5_STATE), BF16),
                        pltpu.VMEM((S5_LANES, 2 * S5_STATE), BF16),
                        pltpu.VMEM((n_rows, 2 * S5_STATE), F32)],
        compiler_params=_MIX_PARAMS,
        name="s5_mix",
    )(us, *tables)


def _conv_mix(zs, per_channel, *, batch):
    n_rows = zs.shape[1]
    slab = pl.BlockSpec((CONV_HEAD_DIM, n_rows, LANES), lambda h: (h, 0, 0))
    per_ch = pl.BlockSpec((CONV_HEAD_DIM, _PC_COUNT, LANES), lambda h: (h, 0, 0))
    return pl.pallas_call(
        functools.partial(_conv_mix_kernel, batch=batch),
        grid=(CONV_HEADS,),
        in_specs=[slab, per_ch],
        out_specs=slab,
        out_shape=jax.ShapeDtypeStruct((CONV_WIDTH, n_rows, LANES), F32),
        compiler_params=_MIX_PARAMS,
        name="conv_mix",
    )(zs, per_channel)


def _out_ffn2_norm(x1, ys, yc, wglu_t, bglu, wo, g2, wg, wu, wd, gf):
    batch, seq, _ = x1.shape
    return pl.pallas_call(
        _out_ffn2_norm_kernel,
        grid=(seq // CHUNK, batch // TILE_SEQS),
        in_specs=[
            _tile_spec(),
            _slab_spec(S5_WIDTH),
            _slab_spec(CONV_WIDTH),
            _resident((S5_WIDTH, S5_WIDTH)),
            _resident((S5_WIDTH, 1)),
            _resident((S5_WIDTH + CONV_WIDTH, D_MODEL)),
            _resident((1, D_MODEL)),
            _resident((D_MODEL, D_FF)),
            _resident((D_MODEL, D_FF)),
            _resident((D_FF, D_MODEL)),
            _resident((1, D_MODEL)),
        ],
        out_specs=_tile_spec(),
        out_shape=jax.ShapeDtypeStruct((batch, seq, D_MODEL), F32),
        compiler_params=_DENSE_PARAMS,
        name="out_ffn2_norm",
    )(x1, ys, yc, wglu_t, bglu, wo, g2, wg, wu, wd, gf)


def _s5_tables(lam_re, lam_im, log_dt, b_re, b_im, c_re, c_im, d_skip):
    T, C, P, G = CHUNK, S5_GROUP_CH, S5_STATE, S5_GROUPS
    dt = jnp.exp(log_dt)[:, None]
    mag = jnp.exp(lam_re * dt)
    abar_re = mag * jnp.cos(lam_im * dt)
    abar_im = mag * jnp.sin(lam_im * dt)
    den = lam_re * lam_re + lam_im * lam_im
    num_re = abar_re - 1.0
    num_im = abar_im
    f_re = ((num_re * lam_re + num_im * lam_im) / den)[..., None]
    f_im = ((num_im * lam_re - num_re * lam_im) / den)[..., None]
    bb_re = f_re * b_re - f_im * b_im
    bb_im = f_re * b_im + f_im * b_re
    n_q = T // SUBLANES
    expo = jnp.concatenate([jnp.arange(SUBLANES), SUBLANES * jnp.arange(1, n_q + 1)]).astype(F32)[None, :, None]
    bmag = jnp.exp(expo * (lam_re * dt)[:, None, :])
    bang = expo * (lam_im * dt)[:, None, :]
    base_re, base_im = bmag * jnp.cos(bang), bmag * jnp.sin(bang)
    lo_re, lo_im = base_re[:, :SUBLANES], base_im[:, :SUBLANES]
    hi_re = jnp.concatenate([base_re[:, :1], base_re[:, SUBLANES:]], axis=1)
    hi_im = jnp.concatenate([base_im[:, :1], base_im[:, SUBLANES:]], axis=1)

    def outer(q_re, q_im, r_re, r_im):
        re = q_re[:, :, None] * r_re[:, None] - q_im[:, :, None] * r_im[:, None]
        im = q_re[:, :, None] * r_im[:, None] + q_im[:, :, None] * r_re[:, None]
        return jnp.concatenate([re, im], axis=-1).reshape(G, -1, 2 * P)

    cat = lambda a, b: jnp.concatenate([a, b], axis=-1)
    powers = jnp.concatenate([
        outer(hi_re[:, n_q - 1::-1], hi_im[:, n_q - 1::-1], lo_re[:, ::-1], lo_im[:, ::-1]),
        outer(hi_re, hi_im, lo_re, lo_im)], axis=1)
    bt_re = bb_re.transpose(0, 2, 1)
    bt_im = bb_im.transpose(0, 2, 1)
    d_rows = jnp.broadcast_to(d_skip.reshape(G, C, 1), (G, C, LANES))
    vecs = [None] * _V_COUNT
    vecs[_V_BR] = cat(bt_re, bt_re)
    vecs[_V_BI_NP] = cat(-bt_im, bt_im)
    vecs[_V_BI_PN] = cat(bt_im, -bt_im)
    vecs[_V_BI_PP] = cat(bt_im, bt_im)
    vecs[_V_CR_PN] = cat(c_re, -c_re)
    vecs[_V_CI_NN] = cat(-c_im, -c_im)
    vecs[_V_C_RE_NIM] = cat(c_re, -c_im)
    vecs[_V_C_NIM_NRE] = cat(-c_im, -c_re)
    vecs[_V_D] = d_rows
    return powers, jnp.stack(vecs, axis=1)


def _conv_tables(w_dw, b_dw, ln_g, ln_b):
    taps = jnp.pad(w_dw[::-1].T, ((0, 0), (0, LANES - CONV_K)))
    lanes = lambda v: jnp.broadcast_to(v[:, None], (CONV_WIDTH, LANES))
    rows = [None] * _PC_COUNT
    rows[_PC_TAPS], rows[_PC_BIAS] = taps, lanes(b_dw)
    rows[_PC_HALF_GAIN], rows[_PC_HALF_BETA] = lanes(0.5 * ln_g), lanes(0.5 * ln_b)
    return jnp.stack(rows, axis=1)


def kernel(x, ffn1_norm, ffn1_w_gate, ffn1_w_up, ffn1_w_down, mix_norm, w_in, s5_lam_re, s5_lam_im, s5_log_dt, s5_b_re, s5_b_im, s5_c_re, s5_c_im, s5_d, s5_w_glu, s5_b_glu, conv_w_dw, conv_b_dw, conv_ln_g, conv_ln_b, w_out, ffn2_norm, ffn2_w_gate, ffn2_w_up, ffn2_w_down, final_norm):
    batch, seq, d_model = x.shape
    assert ffn1_norm.shape[0] == 1 and d_model == D_MODEL
    l = 0
    row = lambda v: v.reshape(1, -1)

    x1, us, zs = _ffn1_in_proj(
        x, row(ffn1_norm[l]), ffn1_w_gate[l].astype(BF16), ffn1_w_up[l].astype(BF16),
        ffn1_w_down[l].astype(BF16), row(mix_norm[l]), w_in[l].T.astype(BF16))

    tables = _s5_tables(
        s5_lam_re[l], s5_lam_im[l], s5_log_dt[l], s5_b_re[l], s5_b_im[l], s5_c_re[l], s5_c_im[l], s5_d[l])
    ys = _s5_mix(us, tables, batch=batch)

    yc = _conv_mix(zs, _conv_tables(conv_w_dw[l], conv_b_dw[l], conv_ln_g[l], conv_ln_b[l]), batch=batch)

    return _out_ffn2_norm(
        x1, ys, yc, s5_w_glu[l].T.astype(BF16), s5_b_glu[l].reshape(-1, 1), w_out[l].astype(BF16),
        row(ffn2_norm[l]), ffn2_w_gate[l].astype(BF16), ffn2_w_up[l].astype(BF16),
        ffn2_w_down[l].astype(BF16), row(final_norm))
```

```python
import functools

import jax
import jax.numpy as jnp
from jax import lax
from jax.experimental import pallas as pl
from jax.experimental.pallas import tpu as pltpu

D_MODEL = 1024
D_FF = 2816
S5_WIDTH = 512
S5_GROUP_CH = 16
S5_GROUPS = S5_WIDTH // S5_GROUP_CH
S5_STATE = 64
CONV_WIDTH = 512
CONV_HEAD_DIM = 64
CONV_HEADS = CONV_WIDTH // CONV_HEAD_DIM
CONV_K = 31
IN_COLS = S5_WIDTH + 2 * CONV_WIDTH
EPS = 1e-6

LANES = 128
SUBLANES = 8
CHUNK = LANES
S5_LANES = S5_GROUP_CH * CHUNK
S5_STEP_GROUPS = 2
TILE_SEQS = 4
ROW_TILE = TILE_SEQS * CHUNK
FF_SPLITS = ((0, 1536), (1536, 2816))
LN_ROWS = 64
VMEM_LIMIT = 56 * 1024 * 1024

BF16 = jnp.bfloat16
F32 = jnp.float32

(_V_BR,
 _V_BI_NP,
 _V_BI_PN,
 _V_BI_PP,
 _V_CR_PN,
 _V_CI_NN,
 _V_C_RE_NIM,
 _V_C_NIM_NRE,
 _V_D,
 _V_COUNT) = range(10)

_PC_TAPS, _PC_BIAS, _PC_HALF_GAIN, _PC_HALF_BETA, _PC_COUNT = range(5)

_NT = (((1,), (1,)), ((), ()))
_TN = (((0,), (0,)), ((), ()))


def _rms_norm(x, g):
    return x * lax.rsqrt(jnp.mean(x * x, axis=-1, keepdims=True) + EPS) * g


def _dot(a, b):
    return jnp.dot(a, b, preferred_element_type=F32)


def _rms_split(x, g):
    return (x * g).astype(BF16), lax.rsqrt(jnp.mean(x * x, axis=-1, keepdims=True) + EPS)


def _swiglu(h, row_scale, wg_ref, wu_ref, wd_ref):
    acc = None
    for lo, hi in FF_SPLITS:
        g = _dot(h, wg_ref[:, lo:hi]) * row_scale
        u = _dot(h, wu_ref[:, lo:hi]) * row_scale
        a = (g * jax.nn.sigmoid(g) * u).astype(BF16)
        part = _dot(a, wd_ref[lo:hi, :])
        acc = part if acc is None else acc + part
    return acc


def _store_slabs(slab_ref, val, first_seq):
    n_ch = slab_ref.shape[0]
    rows = slab_ref.reshape(n_ch * SUBLANES, LANES)
    for ct in range(n_ch // SUBLANES):
        for s in range(TILE_SEQS):
            rows[pl.ds(ct * SUBLANES * SUBLANES + first_seq + s, SUBLANES, stride=SUBLANES), :] = (
                val[ct * SUBLANES:(ct + 1) * SUBLANES, s * LANES:(s + 1) * LANES])


def _load_slabs(slab_ref, first_seq):
    n_ch = slab_ref.shape[0]
    rows = slab_ref.reshape(n_ch * SUBLANES, LANES)
    bands = []
    for ct in range(n_ch // SUBLANES):
        bands.append(jnp.concatenate(
            [rows[pl.ds(ct * SUBLANES * SUBLANES + first_seq + s, SUBLANES, stride=SUBLANES), :]
             for s in range(TILE_SEQS)], axis=1))
    return jnp.concatenate(bands, axis=0)


def _ffn1_in_proj_kernel(x_ref, g1_ref, wg_ref, wu_ref, wd_ref, gm_ref, win_t_ref, *rest):
    n_cast = (len(rest) - 3) // 2
    x1_ref, us_ref, zs_ref = rest[n_cast:n_cast + 3]
    first_seq = pl.program_id(1) * TILE_SEQS
    x = x_ref[...].reshape(ROW_TILE, D_MODEL)
    h, r = _rms_split(x, g1_ref[...])
    x1 = x + 0.5 * _swiglu(h, r, wg_ref, wu_ref, wd_ref)
    x1_ref[...] = x1.reshape(TILE_SEQS, CHUNK, D_MODEL)
    h2, r2 = _rms_split(x1, gm_ref[...])
    r2_lanes = jnp.transpose(jnp.broadcast_to(r2, (ROW_TILE, LANES)))[0:1, :]

    def in_proj_t(lo, hi):
        return lax.dot_general(win_t_ref[lo:hi, :], h2, _NT, preferred_element_type=F32) * r2_lanes

    z_t = (in_proj_t(S5_WIDTH, S5_WIDTH + CONV_WIDTH)
           * jax.nn.sigmoid(in_proj_t(S5_WIDTH + CONV_WIDTH, IN_COLS)))
    _store_slabs(zs_ref, z_t, first_seq)
    _store_slabs(us_ref, in_proj_t(0, S5_WIDTH), first_seq)
    for src_ref, dst_ref in zip(rest[:n_cast], rest[n_cast + 3:]):
        dst_ref[...] = src_ref[...].astype(BF16)


def _s5_mix_kernel(us_ref, pow_ref, vec_ref, ys_ref, *scratch, n_chunks, batch):
    for g in range(S5_STEP_GROUPS):
        ch = pl.ds(g * S5_GROUP_CH, S5_GROUP_CH)
        _s5_group(us_ref.at[ch], pow_ref.at[g], vec_ref.at[g], ys_ref.at[ch], *scratch,
                  n_chunks=n_chunks, batch=batch)


def _s5_group(us_ref, pow_ref, vec_ref, ys_ref,
              cb_ref, f_ref, lhs_ref, win_ref, wout_ref, sprev_ref, *, n_chunks, batch):
    swap = lambda v: pltpu.roll(v, S5_STATE, 1)
    lo_half = lax.broadcasted_iota(jnp.int32, (1, 2 * S5_STATE), 1) < S5_STATE
    pa_rev = pow_ref[0:CHUNK, :]
    pb_rev = swap(pa_rev)
    pa_fwd = pow_ref[CHUNK + 1:2 * CHUNK + 1, :]
    pb_fwd = swap(pa_fwd)
    for c in range(S5_GROUP_CH):
        rows = slice(c * CHUNK, (c + 1) * CHUNK)
        vec = lambda i: vec_ref[i, c:c + 1, :]
        win_ref[rows, 0:2 * S5_STATE] = (pa_rev * vec(_V_BR) + pb_rev * vec(_V_BI_NP)).astype(BF16)
        win_ref[rows, 2 * S5_STATE:] = (pb_rev * vec(_V_BR) + pa_rev * vec(_V_BI_PN)).astype(BF16)
        wout_ref[rows, :] = (pa_fwd * vec(_V_CR_PN) + pb_fwd * vec(_V_CI_NN)).astype(BF16)
        lhs_ref[:, rows] = us_ref[c].astype(BF16)
        cb_ref[c * S5_GROUP_CH:(c + 1) * S5_GROUP_CH, :] = (
            vec_ref[_V_C_RE_NIM] * vec(_V_BR) + vec_ref[_V_C_NIM_NRE] * vec(_V_BI_PP))
    f_ref[...] = lax.dot_general(cb_ref[...], pow_ref[CHUNK:2 * CHUNK, :], _NT, precision=lax.Precision.HIGHEST,
                                 preferred_element_type=F32)

    e = _dot(lhs_ref[...], win_ref[...])
    a_chunk = pow_ref[2 * CHUNK:2 * CHUNK + 1, :]
    a_sw = swap(a_chunk)
    a1 = jnp.where(lo_half, a_chunk, a_sw)
    a2 = jnp.where(lo_half, -a_sw, a_chunk)
    s = jnp.zeros((batch, 2 * S5_STATE), F32)
    s_sw = jnp.zeros((batch, 2 * S5_STATE), F32)
    for c in range(n_chunks):
        rows = slice(c * batch, (c + 1) * batch)
        sprev_ref[rows, :] = s
        e_c = e[rows, :]
        s, s_sw = (a1 * s + a2 * s_sw + e_c[:, :2 * S5_STATE],
                   a1 * s_sw - a2 * s + e_c[:, 2 * S5_STATE:])
    sprev = sprev_ref[...].astype(BF16)

    lag_row = lax.broadcasted_iota(jnp.int32, (CHUNK, CHUNK), 0)
    lag_col = lax.broadcasted_iota(jnp.int32, (CHUNK, CHUNK), 1)
    causal = lag_col >= lag_row

    def toeplitz(ci, co):
        f = jnp.broadcast_to(f_ref[pl.ds(ci * S5_GROUP_CH + co, 1), :], (CHUNK, CHUNK))
        return jnp.where(causal, pltpu.roll(f, 0, 1, stride=1, stride_axis=0), 0.0).astype(BF16)

    def out_pair(n, carry):
        co0 = 2 * n
        w_state = wout_ref[pl.ds(pl.multiple_of(co0 * CHUNK, 2 * CHUNK), 2 * CHUNK), :]
        acc = lax.dot_general(sprev, w_state, _NT, preferred_element_type=F32)
        for kt in range(S5_GROUP_CH // 2):
            w = jnp.concatenate(
                [jnp.concatenate([toeplitz(2 * kt + r, co0), toeplitz(2 * kt + r, co0 + 1)], axis=1)
                 for r in range(2)], axis=0)
            acc = acc + _dot(lhs_ref[:, 2 * kt * CHUNK:(2 * kt + 2) * CHUNK], w)
        for r in range(2):
            y = acc[:, r * CHUNK:(r + 1) * CHUNK] + vec_ref[_V_D, pl.ds(co0 + r, 1), :] * us_ref[co0 + r]
            ys_ref[co0 + r] = jax.nn.gelu(y, approximate=True)
        return carry

    lax.fori_loop(0, S5_GROUP_CH // 2, out_pair, 0, unroll=True)


def _conv_mix_kernel(zs_ref, pc_ref, o_ref, *, batch):
    n_rows = zs_ref.shape[1]
    row = lax.broadcasted_iota(jnp.int32, (CHUNK, CHUNK), 0)
    col = lax.broadcasted_iota(jnp.int32, (CHUNK, CHUNK), 1)
    lag = row - col
    band_cur = (lag >= -(CONV_K - 1)) & (lag <= 0)
    band_prev = lag >= CHUNK - (CONV_K - 1)

    def conv_channel(ch, carry):
        z = zs_ref[ch]
        z_prev = jnp.concatenate([jnp.zeros((batch, CHUNK), F32), z[:n_rows - batch]], axis=0)
        lhs = jnp.concatenate([z_prev, z], axis=1).astype(BF16)
        taps = jnp.broadcast_to(pc_ref[ch, _PC_TAPS:_PC_TAPS + 1, :], (CHUNK, CHUNK))
        diag = pltpu.roll(taps, 0, 1, stride=1, stride_axis=0)
        toe = jnp.concatenate([jnp.where(band_prev, diag, 0.0), jnp.where(band_cur, diag, 0.0)],
                              axis=0).astype(BF16)
        o_ref[ch] = _dot(lhs, toe) + pc_ref[ch, _PC_BIAS:_PC_BIAS + 1, :]
        return carry

    lax.fori_loop(0, CONV_HEAD_DIM, conv_channel, 0, unroll=8)

    inv_n = 1.0 / CONV_HEAD_DIM
    for r0 in range(0, n_rows, LN_ROWS):
        rows = pl.ds(r0, LN_ROWS)
        zero = jnp.zeros((LN_ROWS, CHUNK), F32)
        mu = lax.fori_loop(0, CONV_HEAD_DIM, lambda ch, acc: acc + o_ref[ch, rows, :], zero,
                           unroll=4) * inv_n
        var = lax.fori_loop(0, CONV_HEAD_DIM, lambda ch, acc: acc + jnp.square(o_ref[ch, rows, :] - mu),
                            zero, unroll=4) * inv_n
        rstd = lax.rsqrt(var + EPS)

        def norm_channel(ch, carry):
            half = ((o_ref[ch, rows, :] - mu) * rstd * pc_ref[ch, _PC_HALF_GAIN:_PC_HALF_GAIN + 1, :]
                    + pc_ref[ch, _PC_HALF_BETA:_PC_HALF_BETA + 1, :])
            o_ref[ch, rows, :] = half + half * jnp.tanh(half)
            return carry

        lax.fori_loop(0, CONV_HEAD_DIM, norm_channel, 0, unroll=4)


def _out_ffn2_norm_kernel(x1_ref, ys_ref, yc_ref, wglu_t_ref, bglu_ref, wo_ref,
                          g2_ref, wg_ref, wu_ref, wd_ref, gf_ref, o_ref):
    first_seq = pl.program_id(1) * TILE_SEQS
    ys_t = _load_slabs(ys_ref, first_seq)
    yc_t = _load_slabs(yc_ref, first_seq)
    gate_t = jax.nn.sigmoid(_dot(wglu_t_ref[...], ys_t.astype(BF16)) + bglu_ref[...])
    mix = (lax.dot_general((ys_t * gate_t).astype(BF16), wo_ref[0:S5_WIDTH, :], _TN, preferred_element_type=F32)
           + lax.dot_general(yc_t.astype(BF16), wo_ref[S5_WIDTH:, :], _TN, preferred_element_type=F32))
    x2 = x1_ref[...].reshape(ROW_TILE, D_MODEL) + mix
    h, r = _rms_split(x2, g2_ref[...])
    x3 = x2 + 0.5 * _swiglu(h, r, wg_ref, wu_ref, wd_ref)
    o_ref[...] = _rms_norm(x3, gf_ref[...]).reshape(TILE_SEQS, CHUNK, D_MODEL)


def _resident(shape):
    nd = len(shape)
    return pl.BlockSpec(shape, lambda *_: (0,) * nd, pipeline_mode=pl.Buffered(1))


def _tile_spec():
    return pl.BlockSpec((TILE_SEQS, CHUNK, D_MODEL), lambda c, h: (h, c, 0))


def _slab_spec(n_ch):
    return pl.BlockSpec((n_ch, SUBLANES, LANES), lambda c, h: (0, c, 0))


_DENSE_PARAMS = pltpu.CompilerParams(dimension_semantics=("arbitrary", "arbitrary"),
                                     vmem_limit_bytes=VMEM_LIMIT)
_MIX_PARAMS = pltpu.CompilerParams(dimension_semantics=("arbitrary",), vmem_limit_bytes=VMEM_LIMIT)


def _cast_spec(w, n_steps):
    n_blocks = n_steps
    while w.shape[0] % n_blocks or (w.shape[0] // n_blocks) % (2 * SUBLANES):
        n_blocks //= 2
    steps_per_block = n_steps // n_blocks
    return pl.BlockSpec((w.shape[0] // n_blocks, w.shape[1]), lambda c, h: (c // steps_per_block, 0))


def _ffn1_in_proj(x, g1, wg, wu, wd, gm, win_t, later_weights):
    batch, seq, _ = x.shape
    assert batch == SUBLANES and seq % CHUNK == 0
    n_rows = (seq // CHUNK) * batch
    cast_specs = [_cast_spec(w, seq // CHUNK) for w in later_weights]
    return pl.pallas_call(
        _ffn1_in_proj_kernel,
        grid=(seq // CHUNK, batch // TILE_SEQS),
        in_specs=[
            _tile_spec(),
            _resident((1, D_MODEL)),
            _resident((D_MODEL, D_FF)),
            _resident((D_MODEL, D_FF)),
            _resident((D_FF, D_MODEL)),
            _resident((1, D_MODEL)),
            _resident((IN_COLS, D_MODEL)),
        ] + cast_specs,
        out_specs=[_tile_spec(), _slab_spec(S5_WIDTH), _slab_spec(CONV_WIDTH)] + cast_specs,
        out_shape=[
            jax.ShapeDtypeStruct((batch, seq, D_MODEL), F32),
            jax.ShapeDtypeStruct((S5_WIDTH, n_rows, LANES), F32),
            jax.ShapeDtypeStruct((CONV_WIDTH, n_rows, LANES), F32),
        ] + [jax.ShapeDtypeStruct(w.shape, BF16) for w in later_weights],
        compiler_params=_DENSE_PARAMS,
        name="ffn1_in_proj",
    )(x, g1, wg, wu, wd, gm, win_t, *later_weights)


def _s5_mix(us, tables, *, batch):
    n_rows = us.shape[1]
    slab = pl.BlockSpec((S5_STEP_GROUPS * S5_GROUP_CH, n_rows, LANES), lambda g: (g, 0, 0))

    def per_group(t):
        nd = t.ndim - 1
        return pl.BlockSpec((S5_STEP_GROUPS,) + t.shape[1:], lambda g: (g,) + (0,) * nd)

    return pl.pallas_call(
        functools.partial(_s5_mix_kernel, n_chunks=n_rows // batch, batch=batch),
        grid=(S5_GROUPS // S5_STEP_GROUPS,),
        in_specs=[slab] + [per_group(t) for t in tables],
        out_specs=slab,
        out_shape=jax.ShapeDtypeStruct((S5_WIDTH, n_rows, LANES), F32),
        scratch_shapes=[pltpu.VMEM((S5_GROUP_CH * S5_GROUP_CH, 2 * S5_STATE), F32),
                        pltpu.VMEM((S5_GROUP_CH * S5_GROUP_CH, CHUNK), F32),
                        pltpu.VMEM((n_rows, S5_LANES), BF16),
                        pltpu.VMEM((S5_LANES, 4 * S5_STATE), BF16),
                        pltpu.VMEM((S5_LANES, 2 * S5_STATE), BF16),
                        pltpu.VMEM((n_rows, 2 * S5_STATE), F32)],
        compiler_params=_MIX_PARAMS,
        name="s5_mix",
    )(us, *tables)


def _conv_mix(zs, per_channel, *, batch):
    n_rows = zs.shape[1]
    slab = pl.BlockSpec((CONV_HEAD_DIM, n_rows, LANES), lambda h: (h, 0, 0))
    per_ch = pl.BlockSpec((CONV_HEAD_DIM, _PC_COUNT, LANES), lambda h: (h, 0, 0))
    return pl.pallas_call(
        functools.partial(_conv_mix_kernel, batch=batch),
        grid=(CONV_HEADS,),
        in_specs=[slab, per_ch],
        out_specs=slab,
        out_shape=jax.ShapeDtypeStruct((CONV_WIDTH, n_rows, LANES), F32),
        compiler_params=_MIX_PARAMS,
        name="conv_mix",
    )(zs, per_channel)


def _out_ffn2_norm(x1, ys, yc, wglu_t, bglu, wo, g2, wg, wu, wd, gf):
    batch, seq, _ = x1.shape
    return pl.pallas_call(
        _out_ffn2_norm_kernel,
        grid=(seq // CHUNK, batch // TILE_SEQS),
        in_specs=[
            _tile_spec(),
            _slab_spec(S5_WIDTH),
            _slab_spec(CONV_WIDTH),
            _resident((S5_WIDTH, S5_WIDTH)),
            _resident((S5_WIDTH, 1)),
            _resident((S5_WIDTH + CONV_WIDTH, D_MODEL)),
            _resident((1, D_MODEL)),
            _resident((D_MODEL, D_FF)),
            _resident((D_MODEL, D_FF)),
            _resident((D_FF, D_MODEL)),
            _resident((1, D_MODEL)),
        ],
        out_specs=_tile_spec(),
        out_shape=jax.ShapeDtypeStruct((batch, seq, D_MODEL), F32),
        compiler_params=_DENSE_PARAMS,
        name="out_ffn2_norm",
    )(x1, ys, yc, wglu_t, bglu, wo, g2, wg, wu, wd, gf)


def _s5_tables(lam_re, lam_im, log_dt, b_re, b_im, c_re, c_im, d_skip):
    T, C, P, G = CHUNK, S5_GROUP_CH, S5_STATE, S5_GROUPS
    dt = jnp.exp(log_dt)[:, None]
    mag = jnp.exp(lam_re * dt)
    abar_re = mag * jnp.cos(lam_im * dt)
    abar_im = mag * jnp.sin(lam_im * dt)
    den = lam_re * lam_re + lam_im * lam_im
    num_re = abar_re - 1.0
    num_im = abar_im
    f_re = ((num_re * lam_re + num_im * lam_im) / den)[..., None]
    f_im = ((num_im * lam_re - num_re * lam_im) / den)[..., None]
    bb_re = f_re * b_re - f_im * b_im
    bb_im = f_re * b_im + f_im * b_re
    n_q = T // SUBLANES
    expo = jnp.concatenate([jnp.arange(SUBLANES), SUBLANES * jnp.arange(1, n_q + 1)]).astype(F32)[None, :, None]
    bmag = jnp.exp(expo * (lam_re * dt)[:, None, :])
    bang = expo * (lam_im * dt)[:, None, :]
    base_re, base_im = bmag * jnp.cos(bang), bmag * jnp.sin(bang)
    lo_re, lo_im = base_re[:, :SUBLANES], base_im[:, :SUBLANES]
    hi_re = jnp.concatenate([base_re[:, :1], base_re[:, SUBLANES:]], axis=1)
    hi_im = jnp.concatenate([base_im[:, :1], base_im[:, SUBLANES:]], axis=1)

    def outer(q_re, q_im, r_re, r_im):
        re = q_re[:, :, None] * r_re[:, None] - q_im[:, :, None] * r_im[:, None]
        im = q_re[:, :, None] * r_im[:, None] + q_im[:, :, None] * r_re[:, None]
        return jnp.concatenate([re, im], axis=-1).reshape(G, -1, 2 * P)

    cat = lambda a, b: jnp.concatenate([a, b], axis=-1)
    powers = jnp.concatenate([
        outer(hi_re[:, n_q - 1::-1], hi_im[:, n_q - 1::-1], lo_re[:, ::-1], lo_im[:, ::-1]),
        outer(hi_re, hi_im, lo_re, lo_im)], axis=1)
    bt_re = bb_re.transpose(0, 2, 1)
    bt_im = bb_im.transpose(0, 2, 1)
    d_rows = jnp.broadcast_to(d_skip.reshape(G, C, 1), (G, C, LANES))
    vecs = [None] * _V_COUNT
    vecs[_V_BR] = cat(bt_re, bt_re)
    vecs[_V_BI_NP] = cat(-bt_im, bt_im)
    vecs[_V_BI_PN] = cat(bt_im, -bt_im)
    vecs[_V_BI_PP] = cat(bt_im, bt_im)
    vecs[_V_CR_PN] = cat(c_re, -c_re)
    vecs[_V_CI_NN] = cat(-c_im, -c_im)
    vecs[_V_C_RE_NIM] = cat(c_re, -c_im)
    vecs[_V_C_NIM_NRE] = cat(-c_im, -c_re)
    vecs[_V_D] = d_rows
    return powers, jnp.stack(vecs, axis=1)


def _conv_tables(w_dw, b_dw, ln_g, ln_b):
    taps = jnp.pad(w_dw[::-1].T, ((0, 0), (0, LANES - CONV_K)))
    lanes = lambda v: jnp.broadcast_to(v[:, None], (CONV_WIDTH, LANES))
    rows = [None] * _PC_COUNT
    rows[_PC_TAPS], rows[_PC_BIAS] = taps, lanes(b_dw)
    rows[_PC_HALF_GAIN], rows[_PC_HALF_BETA] = lanes(0.5 * ln_g), lanes(0.5 * ln_b)
    return jnp.stack(rows, axis=1)


def kernel(x, ffn1_norm, ffn1_w_gate, ffn1_w_up, ffn1_w_down, mix_norm, w_in, s5_lam_re, s5_lam_im, s5_log_dt, s5_b_re, s5_b_im, s5_c_re, s5_c_im, s5_d, s5_w_glu, s5_b_glu, conv_w_dw, conv_b_dw, conv_ln_g, conv_ln_b, w_out, ffn2_norm, ffn2_w_gate, ffn2_w_up, ffn2_w_down, final_norm):
    batch, seq, d_model = x.shape
    assert ffn1_norm.shape[0] == 1 and d_model == D_MODEL
    l = 0
    row = lambda v: v.reshape(1, -1)

    x1, us, zs, wg2, wu2, wd2, wo = _ffn1_in_proj(
        x, row(ffn1_norm[l]), ffn1_w_gate[l].astype(BF16), ffn1_w_up[l].astype(BF16),
        ffn1_w_down[l].astype(BF16), row(mix_norm[l]), w_in[l].T.astype(BF16),
        (ffn2_w_gate[l], ffn2_w_up[l], ffn2_w_down[l], w_out[l]))

    tables = _s5_tables(
        s5_lam_re[l], s5_lam_im[l], s5_log_dt[l], s5_b_re[l], s5_b_im[l], s5_c_re[l], s5_c_im[l], s5_d[l])
    ys = _s5_mix(us, tables, batch=batch)

    yc = _conv_mix(zs, _conv_tables(conv_w_dw[l], conv_b_dw[l], conv_ln_g[l], conv_ln_b[l]), batch=batch)

    return _out_ffn2_norm(
        x1, ys, yc, s5_w_glu[l].T.astype(BF16), s5_b_glu[l].reshape(-1, 1), wo,
        row(ffn2_norm[l]), wg2, wu2, wd2, row(final_norm))
```

```python
import functools

import jax
import jax.numpy as jnp
from jax import lax
from jax.experimental import pallas as pl
from jax.experimental.pallas import tpu as pltpu

D_MODEL = 1024
D_FF = 2816
S5_WIDTH = 512
S5_GROUP_CH = 16
S5_GROUPS = S5_WIDTH // S5_GROUP_CH
S5_STATE = 64
CONV_WIDTH = 512
CONV_HEAD_DIM = 64
CONV_HEADS = CONV_WIDTH // CONV_HEAD_DIM
CONV_K = 31
IN_COLS = S5_WIDTH + 2 * CONV_WIDTH
EPS = 1e-6

LANES = 128
SUBLANES = 8
CHUNK = LANES
S5_LANES = S5_GROUP_CH * CHUNK
S5_STEP_GROUPS = 4
TILE_SEQS = 4
ROW_TILE = TILE_SEQS * CHUNK
FF_SPLITS = ((0, 1536), (1536, 2816))
LN_ROWS = 64
VMEM_LIMIT = 56 * 1024 * 1024

BF16 = jnp.bfloat16
F32 = jnp.float32

(_V_BR,
 _V_BI_NP,
 _V_BI_PN,
 _V_BI_PP,
 _V_CR_PN,
 _V_CI_NN,
 _V_C_RE_NIM,
 _V_C_NIM_NRE,
 _V_D,
 _V_COUNT) = range(10)

_PC_TAPS, _PC_BIAS, _PC_HALF_GAIN, _PC_HALF_BETA, _PC_COUNT = range(5)

_NT = (((1,), (1,)), ((), ()))
_TN = (((0,), (0,)), ((), ()))


def _rms_norm(x, g):
    return x * lax.rsqrt(jnp.mean(x * x, axis=-1, keepdims=True) + EPS) * g


def _dot(a, b):
    return jnp.dot(a, b, preferred_element_type=F32)


def _rms_split(x, g):
    return (x * g).astype(BF16), lax.rsqrt(jnp.mean(x * x, axis=-1, keepdims=True) + EPS)


def _swiglu(h, row_scale, wg_ref, wu_ref, wd_ref):
    acc = None
    for lo, hi in FF_SPLITS:
        g = _dot(h, wg_ref[:, lo:hi]) * row_scale
        u = _dot(h, wu_ref[:, lo:hi]) * row_scale
        a = (g * jax.nn.sigmoid(g) * u).astype(BF16)
        part = _dot(a, wd_ref[lo:hi, :])
        acc = part if acc is None else acc + part
    return acc


def _store_slabs(slab_ref, val, first_seq):
    n_ch = slab_ref.shape[0]
    rows = slab_ref.reshape(n_ch * SUBLANES, LANES)
    for ct in range(n_ch // SUBLANES):
        for s in range(TILE_SEQS):
            rows[pl.ds(ct * SUBLANES * SUBLANES + first_seq + s, SUBLANES, stride=SUBLANES), :] = (
                val[ct * SUBLANES:(ct + 1) * SUBLANES, s * LANES:(s + 1) * LANES])


def _load_slabs(slab_ref, first_seq):
    n_ch = slab_ref.shape[0]
    rows = slab_ref.reshape(n_ch * SUBLANES, LANES)
    bands = []
    for ct in range(n_ch // SUBLANES):
        bands.append(jnp.concatenate(
            [rows[pl.ds(ct * SUBLANES * SUBLANES + first_seq + s, SUBLANES, stride=SUBLANES), :]
             for s in range(TILE_SEQS)], axis=1))
    return jnp.concatenate(bands, axis=0)


def _ffn1_in_proj_kernel(x_ref, g1_ref, wg_ref, wu_ref, wd_ref, gm_ref, win_t_ref, *rest):
    n_cast = (len(rest) - 3) // 2
    x1_ref, us_ref, zs_ref = rest[n_cast:n_cast + 3]
    first_seq = pl.program_id(1) * TILE_SEQS
    x = x_ref[...].reshape(ROW_TILE, D_MODEL)
    h, r = _rms_split(x, g1_ref[...])
    x1 = x + 0.5 * _swiglu(h, r, wg_ref, wu_ref, wd_ref)
    x1_ref[...] = x1.reshape(TILE_SEQS, CHUNK, D_MODEL)
    h2, r2 = _rms_split(x1, gm_ref[...])
    r2_lanes = jnp.transpose(jnp.broadcast_to(r2, (ROW_TILE, LANES)))[0:1, :]

    def in_proj_t(lo, hi):
        return lax.dot_general(win_t_ref[lo:hi, :], h2, _NT, preferred_element_type=F32) * r2_lanes

    z_t = (in_proj_t(S5_WIDTH, S5_WIDTH + CONV_WIDTH)
           * jax.nn.sigmoid(in_proj_t(S5_WIDTH + CONV_WIDTH, IN_COLS)))
    _store_slabs(zs_ref, z_t, first_seq)
    _store_slabs(us_ref, in_proj_t(0, S5_WIDTH), first_seq)
    for src_ref, dst_ref in zip(rest[:n_cast], rest[n_cast + 3:]):
        dst_ref[...] = src_ref[...].astype(BF16)


def _s5_mix_kernel(us_ref, pow_ref, vec_ref, ys_ref, *scratch, n_chunks, batch):
    for g in range(S5_STEP_GROUPS):
        ch = pl.ds(g * S5_GROUP_CH, S5_GROUP_CH)
        _s5_group(us_ref.at[ch], pow_ref.at[g], vec_ref.at[g], ys_ref.at[ch], *scratch,
                  n_chunks=n_chunks, batch=batch)


def _s5_group(us_ref, pow_ref, vec_ref, ys_ref,
              cb_ref, f_ref, lhs_ref, win_ref, wout_ref, sprev_ref, *, n_chunks, batch):
    swap = lambda v: pltpu.roll(v, S5_STATE, 1)
    lo_half = lax.broadcasted_iota(jnp.int32, (1, 2 * S5_STATE), 1) < S5_STATE
    pa_rev = pow_ref[0:CHUNK, :]
    pb_rev = swap(pa_rev)
    pa_fwd = pow_ref[CHUNK + 1:2 * CHUNK + 1, :]
    pb_fwd = swap(pa_fwd)
    for c in range(S5_GROUP_CH):
        rows = slice(c * CHUNK, (c + 1) * CHUNK)
        vec = lambda i: vec_ref[i, c:c + 1, :]
        win_ref[rows, 0:2 * S5_STATE] = (pa_rev * vec(_V_BR) + pb_rev * vec(_V_BI_NP)).astype(BF16)
        win_ref[rows, 2 * S5_STATE:] = (pb_rev * vec(_V_BR) + pa_rev * vec(_V_BI_PN)).astype(BF16)
        wout_ref[rows, :] = (pa_fwd * vec(_V_CR_PN) + pb_fwd * vec(_V_CI_NN)).astype(BF16)
        lhs_ref[:, rows] = us_ref[c].astype(BF16)
        cb_ref[c * S5_GROUP_CH:(c + 1) * S5_GROUP_CH, :] = (
            vec_ref[_V_C_RE_NIM] * vec(_V_BR) + vec_ref[_V_C_NIM_NRE] * vec(_V_BI_PP))
    f_ref[...] = lax.dot_general(cb_ref[...], pow_ref[CHUNK:2 * CHUNK, :], _NT, precision=lax.Precision.HIGHEST,
                                 preferred_element_type=F32)

    e = _dot(lhs_ref[...], win_ref[...])
    a_chunk = pow_ref[2 * CHUNK:2 * CHUNK + 1, :]
    a_sw = swap(a_chunk)
    a1 = jnp.where(lo_half, a_chunk, a_sw)
    a2 = jnp.where(lo_half, -a_sw, a_chunk)
    s = jnp.zeros((batch, 2 * S5_STATE), F32)
    s_sw = jnp.zeros((batch, 2 * S5_STATE), F32)
    for c in range(n_chunks):
        rows = slice(c * batch, (c + 1) * batch)
        sprev_ref[rows, :] = s
        e_c = e[rows, :]
        s, s_sw = (a1 * s + a2 * s_sw + e_c[:, :2 * S5_STATE],
                   a1 * s_sw - a2 * s + e_c[:, 2 * S5_STATE:])
    sprev = sprev_ref[...].astype(BF16)

    lag_row = lax.broadcasted_iota(jnp.int32, (CHUNK, CHUNK), 0)
    lag_col = lax.broadcasted_iota(jnp.int32, (CHUNK, CHUNK), 1)
    causal = lag_col >= lag_row

    def toeplitz(ci, co):
        f = jnp.broadcast_to(f_ref[pl.ds(ci * S5_GROUP_CH + co, 1), :], (CHUNK, CHUNK))
        return jnp.where(causal, pltpu.roll(f, 0, 1, stride=1, stride_axis=0), 0.0).astype(BF16)

    def out_pair(n, carry):
        co0 = 2 * n
        w_state = wout_ref[pl.ds(pl.multiple_of(co0 * CHUNK, 2 * CHUNK), 2 * CHUNK), :]
        acc = lax.dot_general(sprev, w_state, _NT, preferred_element_type=F32)
        for kt in range(S5_GROUP_CH // 2):
            w = jnp.concatenate(
                [jnp.concatenate([toeplitz(2 * kt + r, co0), toeplitz(2 * kt + r, co0 + 1)], axis=1)
                 for r in range(2)], axis=0)
            acc = acc + _dot(lhs_ref[:, 2 * kt * CHUNK:(2 * kt + 2) * CHUNK], w)
        for r in range(2):
            y = acc[:, r * CHUNK:(r + 1) * CHUNK] + vec_ref[_V_D, pl.ds(co0 + r, 1), :] * us_ref[co0 + r]
            ys_ref[co0 + r] = jax.nn.gelu(y, approximate=True)
        return carry

    lax.fori_loop(0, S5_GROUP_CH // 2, out_pair, 0, unroll=True)


def _conv_mix_kernel(zs_ref, pc_ref, o_ref, *, batch):
    n_rows = zs_ref.shape[1]
    row = lax.broadcasted_iota(jnp.int32, (CHUNK, CHUNK), 0)
    col = lax.broadcasted_iota(jnp.int32, (CHUNK, CHUNK), 1)
    lag = row - col
    band_cur = (lag >= -(CONV_K - 1)) & (lag <= 0)
    band_prev = lag >= CHUNK - (CONV_K - 1)

    def conv_channel(ch, carry):
        z = zs_ref[ch]
        z_prev = jnp.concatenate([jnp.zeros((batch, CHUNK), F32), z[:n_rows - batch]], axis=0)
        lhs = jnp.concatenate([z_prev, z], axis=1).astype(BF16)
        taps = jnp.broadcast_to(pc_ref[ch, _PC_TAPS:_PC_TAPS + 1, :], (CHUNK, CHUNK))
        diag = pltpu.roll(taps, 0, 1, stride=1, stride_axis=0)
        toe = jnp.concatenate([jnp.where(band_prev, diag, 0.0), jnp.where(band_cur, diag, 0.0)],
                              axis=0).astype(BF16)
        o_ref[ch] = _dot(lhs, toe) + pc_ref[ch, _PC_BIAS:_PC_BIAS + 1, :]
        return carry

    lax.fori_loop(0, CONV_HEAD_DIM, conv_channel, 0, unroll=8)

    inv_n = 1.0 / CONV_HEAD_DIM
    for r0 in range(0, n_rows, LN_ROWS):
        rows = pl.ds(r0, LN_ROWS)
        zero = jnp.zeros((LN_ROWS, CHUNK), F32)
        mu = lax.fori_loop(0, CONV_HEAD_DIM, lambda ch, acc: acc + o_ref[ch, rows, :], zero,
                           unroll=4) * inv_n
        var = lax.fori_loop(0, CONV_HEAD_DIM, lambda ch, acc: acc + jnp.square(o_ref[ch, rows, :] - mu),
                            zero, unroll=4) * inv_n
        rstd = lax.rsqrt(var + EPS)

        def norm_channel(ch, carry):
            half = ((o_ref[ch, rows, :] - mu) * rstd * pc_ref[ch, _PC_HALF_GAIN:_PC_HALF_GAIN + 1, :]
                    + pc_ref[ch, _PC_HALF_BETA:_PC_HALF_BETA + 1, :])
            o_ref[ch, rows, :] = half + half * jnp.tanh(half)
            return carry

        lax.fori_loop(0, CONV_HEAD_DIM, norm_channel, 0, unroll=4)


def _out_ffn2_norm_kernel(x1_ref, ys_ref, yc_ref, wglu_t_ref, bglu_ref, wo_ref,
                          g2_ref, wg_ref, wu_ref, wd_ref, gf_ref, o_ref):
    first_seq = pl.program_id(1) * TILE_SEQS
    ys_t = _load_slabs(ys_ref, first_seq)
    yc_t = _load_slabs(yc_ref, first_seq)
    gate_t = jax.nn.sigmoid(_dot(wglu_t_ref[...], ys_t.astype(BF16)) + bglu_ref[...])
    mix = (lax.dot_general((ys_t * gate_t).astype(BF16), wo_ref[0:S5_WIDTH, :], _TN, preferred_element_type=F32)
           + lax.dot_general(yc_t.astype(BF16), wo_ref[S5_WIDTH:, :], _TN, preferred_element_type=F32))
    x2 = x1_ref[...].reshape(ROW_TILE, D_MODEL) + mix
    h, r = _rms_split(x2, g2_ref[...])
    x3 = x2 + 0.5 * _swiglu(h, r, wg_ref, wu_ref, wd_ref)
    o_ref[...] = _rms_norm(x3, gf_ref[...]).reshape(TILE_SEQS, CHUNK, D_MODEL)


def _resident(shape):
    nd = len(shape)
    return pl.BlockSpec(shape, lambda *_: (0,) * nd, pipeline_mode=pl.Buffered(1))


def _tile_spec():
    return pl.BlockSpec((TILE_SEQS, CHUNK, D_MODEL), lambda c, h: (h, c, 0))


def _slab_spec(n_ch):
    return pl.BlockSpec((n_ch, SUBLANES, LANES), lambda c, h: (0, c, 0))


_DENSE_PARAMS = pltpu.CompilerParams(dimension_semantics=("arbitrary", "arbitrary"),
                                     vmem_limit_bytes=VMEM_LIMIT)
_MIX_PARAMS = pltpu.CompilerParams(dimension_semantics=("arbitrary",), vmem_limit_bytes=VMEM_LIMIT)


def _cast_spec(w, n_steps):
    n_blocks = n_steps
    while w.shape[0] % n_blocks or (w.shape[0] // n_blocks) % (2 * SUBLANES):
        n_blocks //= 2
    steps_per_block = n_steps // n_blocks
    return pl.BlockSpec((w.shape[0] // n_blocks, w.shape[1]), lambda c, h: (c // steps_per_block, 0))


def _ffn1_in_proj(x, g1, wg, wu, wd, gm, win_t, later_weights):
    batch, seq, _ = x.shape
    assert batch == SUBLANES and seq % CHUNK == 0
    n_rows = (seq // CHUNK) * batch
    cast_specs = [_cast_spec(w, seq // CHUNK) for w in later_weights]
    return pl.pallas_call(
        _ffn1_in_proj_kernel,
        grid=(seq // CHUNK, batch // TILE_SEQS),
        in_specs=[
            _tile_spec(),
            _resident((1, D_MODEL)),
            _resident((D_MODEL, D_FF)),
            _resident((D_MODEL, D_FF)),
            _resident((D_FF, D_MODEL)),
            _resident((1, D_MODEL)),
            _resident((IN_COLS, D_MODEL)),
        ] + cast_specs,
        out_specs=[_tile_spec(), _slab_spec(S5_WIDTH), _slab_spec(CONV_WIDTH)] + cast_specs,
        out_shape=[
            jax.ShapeDtypeStruct((batch, seq, D_MODEL), F32),
            jax.ShapeDtypeStruct((S5_WIDTH, n_rows, LANES), F32),
            jax.ShapeDtypeStruct((CONV_WIDTH, n_rows, LANES), F32),
        ] + [jax.ShapeDtypeStruct(w.shape, BF16) for w in later_weights],
        compiler_params=_DENSE_PARAMS,
        name="ffn1_in_proj",
    )(x, g1, wg, wu, wd, gm, win_t, *later_weights)


def _s5_mix(us, tables, *, batch):
    n_rows = us.shape[1]
    slab = pl.BlockSpec((S5_STEP_GROUPS * S5_GROUP_CH, n_rows, LANES), lambda g: (g, 0, 0))

    def per_group(t):
        nd = t.ndim - 1
        return pl.BlockSpec((S5_STEP_GROUPS,) + t.shape[1:], lambda g: (g,) + (0,) * nd)

    return pl.pallas_call(
        functools.partial(_s5_mix_kernel, n_chunks=n_rows // batch, batch=batch),
        grid=(S5_GROUPS // S5_STEP_GROUPS,),
        in_specs=[slab] + [per_group(t) for t in tables],
        out_specs=slab,
        out_shape=jax.ShapeDtypeStruct((S5_WIDTH, n_rows, LANES), F32),
        scratch_shapes=[pltpu.VMEM((S5_GROUP_CH * S5_GROUP_CH, 2 * S5_STATE), F32),
                        pltpu.VMEM((S5_GROUP_CH * S5_GROUP_CH, CHUNK), F32),
                        pltpu.VMEM((n_rows, S5_LANES), BF16),
                        pltpu.VMEM((S5_LANES, 4 * S5_STATE), BF16),
                        pltpu.VMEM((S5_LANES, 2 * S5_STATE), BF16),
                        pltpu.VMEM((n_rows, 2 * S5_STATE), F32)],
        compiler_params=_MIX_PARAMS,
        name="s5_mix",
    )(us, *tables)


def _conv_mix(zs, per_channel, *, batch):
    n_rows = zs.shape[1]
    slab = pl.BlockSpec((CONV_HEAD_DIM, n_rows, LANES), lambda h: (h, 0, 0))
    per_ch = pl.BlockSpec((CONV_HEAD_DIM, _PC_COUNT, LANES), lambda h: (h, 0, 0))
    return pl.pallas_call(
        functools.partial(_conv_mix_kernel, batch=batch),
        grid=(CONV_HEADS,),
        in_specs=[slab, per_ch],
        out_specs=slab,
        out_shape=jax.ShapeDtypeStruct((CONV_WIDTH, n_rows, LANES), F32),
        compiler_params=_MIX_PARAMS,
        name="conv_mix",
    )(zs, per_channel)


def _out_ffn2_norm(x1, ys, yc, wglu_t, bglu, wo, g2, wg, wu, wd, gf):
    batch, seq, _ = x1.shape
    return pl.pallas_call(
        _out_ffn2_norm_kernel,
        grid=(seq // CHUNK, batch // TILE_SEQS),
        in_specs=[
            _tile_spec(),
            _slab_spec(S5_WIDTH),
            _slab_spec(CONV_WIDTH),
            _resident((S5_WIDTH, S5_WIDTH)),
            _resident((S5_WIDTH, 1)),
            _resident((S5_WIDTH + CONV_WIDTH, D_MODEL)),
            _resident((1, D_MODEL)),
            _resident((D_MODEL, D_FF)),
            _resident((D_MODEL, D_FF)),
            _resident((D_FF, D_MODEL)),
            _resident((1, D_MODEL)),
        ],
        out_specs=_tile_spec(),
        out_shape=jax.ShapeDtypeStruct((batch, seq, D_MODEL), F32),
        compiler_params=_DENSE_PARAMS,
        name="out_ffn2_norm",
    )(x1, ys, yc, wglu_t, bglu, wo, g2, wg, wu, wd, gf)


def _s5_tables(lam_re, lam_im, log_dt, b_re, b_im, c_re, c_im, d_skip):
    T, C, P, G = CHUNK, S5_GROUP_CH, S5_STATE, S5_GROUPS
    dt = jnp.exp(log_dt)[:, None]
    mag = jnp.exp(lam_re * dt)
    abar_re = mag * jnp.cos(lam_im * dt)
    abar_im = mag * jnp.sin(lam_im * dt)
    den = lam_re * lam_re + lam_im * lam_im
    num_re = abar_re - 1.0
    num_im = abar_im
    f_re = ((num_re * lam_re + num_im * lam_im) / den)[..., None]
    f_im = ((num_im * lam_re - num_re * lam_im) / den)[..., None]
    bb_re = f_re * b_re - f_im * b_im
    bb_im = f_re * b_im + f_im * b_re
    n_q = T // SUBLANES
    expo = jnp.concatenate([jnp.arange(SUBLANES), SUBLANES * jnp.arange(1, n_q + 1)]).astype(F32)[None, :, None]
    bmag = jnp.exp(expo * (lam_re * dt)[:, None, :])
    bang = expo * (lam_im * dt)[:, None, :]
    base_re, base_im = bmag * jnp.cos(bang), bmag * jnp.sin(bang)
    lo_re, lo_im = base_re[:, :SUBLANES], base_im[:, :SUBLANES]
    hi_re = jnp.concatenate([base_re[:, :1], base_re[:, SUBLANES:]], axis=1)
    hi_im = jnp.concatenate([base_im[:, :1], base_im[:, SUBLANES:]], axis=1)

    def outer(q_re, q_im, r_re, r_im):
        re = q_re[:, :, None] * r_re[:, None] - q_im[:, :, None] * r_im[:, None]
        im = q_re[:, :, None] * r_im[:, None] + q_im[:, :, None] * r_re[:, None]
        return jnp.concatenate([re, im], axis=-1).reshape(G, -1, 2 * P)

    cat = lambda a, b: jnp.concatenate([a, b], axis=-1)
    powers = jnp.concatenate([
        outer(hi_re[:, n_q - 1::-1], hi_im[:, n_q - 1::-1], lo_re[:, ::-1], lo_im[:, ::-1]),
        outer(hi_re, hi_im, lo_re, lo_im)], axis=1)
    bt_re = bb_re.transpose(0, 2, 1)
    bt_im = bb_im.transpose(0, 2, 1)
    d_rows = jnp.broadcast_to(d_skip.reshape(G, C, 1), (G, C, LANES))
    vecs = [None] * _V_COUNT
    vecs[_V_BR] = cat(bt_re, bt_re)
    vecs[_V_BI_NP] = cat(-bt_im, bt_im)
    vecs[_V_BI_PN] = cat(bt_im, -bt_im)
    vecs[_V_BI_PP] = cat(bt_im, bt_im)
    vecs[_V_CR_PN] = cat(c_re, -c_re)
    vecs[_V_CI_NN] = cat(-c_im, -c_im)
    vecs[_V_C_RE_NIM] = cat(c_re, -c_im)
    vecs[_V_C_NIM_NRE] = cat(-c_im, -c_re)
    vecs[_V_D] = d_rows
    return powers, jnp.stack(vecs, axis=1)


def _conv_tables(w_dw, b_dw, ln_g, ln_b):
    taps = jnp.pad(w_dw[::-1].T, ((0, 0), (0, LANES - CONV_K)))
    lanes = lambda v: jnp.broadcast_to(v[:, None], (CONV_WIDTH, LANES))
    rows = [None] * _PC_COUNT
    rows[_PC_TAPS], rows[_PC_BIAS] = taps, lanes(b_dw)
    rows[_PC_HALF_GAIN], rows[_PC_HALF_BETA] = lanes(0.5 * ln_g), lanes(0.5 * ln_b)
    return jnp.stack(rows, axis=1)


def kernel(x, ffn1_norm, ffn1_w_gate, ffn1_w_up, ffn1_w_down, mix_norm, w_in, s5_lam_re, s5_lam_im, s5_log_dt, s5_b_re, s5_b_im, s5_c_re, s5_c_im, s5_d, s5_w_glu, s5_b_glu, conv_w_dw, conv_b_dw, conv_ln_g, conv_ln_b, w_out, ffn2_norm, ffn2_w_gate, ffn2_w_up, ffn2_w_down, final_norm):
    batch, seq, d_model = x.shape
    assert ffn1_norm.shape[0] == 1 and d_model == D_MODEL
    l = 0
    row = lambda v: v.reshape(1, -1)

    x1, us, zs, wg2, wu2, wd2, wo = _ffn1_in_proj(
        x, row(ffn1_norm[l]), ffn1_w_gate[l].astype(BF16), ffn1_w_up[l].astype(BF16),
        ffn1_w_down[l].astype(BF16), row(mix_norm[l]), w_in[l].T.astype(BF16),
        (ffn2_w_gate[l], ffn2_w_up[l], ffn2_w_down[l], w_out[l]))

    tables = _s5_tables(
        s5_lam_re[l], s5_lam_im[l], s5_log_dt[l], s5_b_re[l], s5_b_im[l], s5_c_re[l], s5_c_im[l], s5_d[l])
    ys = _s5_mix(us, tables, batch=batch)

    yc = _conv_mix(zs, _conv_tables(conv_w_dw[l], conv_b_dw[l], conv_ln_g[l], conv_ln_b[l]), batch=batch)

    return _out_ffn2_norm(
        x1, ys, yc, s5_w_glu[l].T.astype(BF16), s5_b_glu[l].reshape(-1, 1), wo,
        row(ffn2_norm[l]), wg2, wu2, wd2, row(final_norm))
```

```python
import functools

import jax
import jax.numpy as jnp
from jax import lax
from jax.experimental import pallas as pl
from jax.experimental.pallas import tpu as pltpu

D_MODEL = 1024
D_FF = 2816
S5_WIDTH = 512
S5_GROUP_CH = 16
S5_GROUPS = S5_WIDTH // S5_GROUP_CH
S5_STATE = 64
CONV_WIDTH = 512
CONV_HEAD_DIM = 64
CONV_HEADS = CONV_WIDTH // CONV_HEAD_DIM
CONV_K = 31
IN_COLS = S5_WIDTH + 2 * CONV_WIDTH
EPS = 1e-6

LANES = 128
SUBLANES = 8
CHUNK = LANES
S5_LANES = S5_GROUP_CH * CHUNK
S5_STEP_GROUPS = 2
TILE_SEQS = 4
ROW_TILE = TILE_SEQS * CHUNK
FF_SPLITS = ((0, 1536), (1536, 2816))
LN_ROWS = 64
VMEM_LIMIT = 56 * 1024 * 1024

BF16 = jnp.bfloat16
F32 = jnp.float32

(_V_BR,
 _V_BI_NP,
 _V_BI_PN,
 _V_BI_PP,
 _V_CR_PN,
 _V_CI_NN,
 _V_C_RE_NIM,
 _V_C_NIM_NRE,
 _V_D,
 _V_COUNT) = range(10)

_PC_TAPS, _PC_BIAS, _PC_HALF_GAIN, _PC_HALF_BETA, _PC_COUNT = range(5)

_NT = (((1,), (1,)), ((), ()))
_TN = (((0,), (0,)), ((), ()))


def _rms_norm(x, g):
    return x * lax.rsqrt(jnp.mean(x * x, axis=-1, keepdims=True) + EPS) * g


def _dot(a, b):
    return jnp.dot(a, b, preferred_element_type=F32)


def _rms_split(x, g):
    return (x * g).astype(BF16), lax.rsqrt(jnp.mean(x * x, axis=-1, keepdims=True) + EPS)


def _swiglu(h, row_scale, wg_ref, wu_ref, wd_ref):
    acc = None
    for lo, hi in FF_SPLITS:
        g = _dot(h, wg_ref[:, lo:hi]) * row_scale
        u = _dot(h, wu_ref[:, lo:hi]) * row_scale
        a = (g * jax.nn.sigmoid(g) * u).astype(BF16)
        part = _dot(a, wd_ref[lo:hi, :])
        acc = part if acc is None else acc + part
    return acc


def _store_slabs(slab_ref, val, first_seq):
    n_ch = slab_ref.shape[0]
    rows = slab_ref.reshape(n_ch * SUBLANES, LANES)
    for ct in range(n_ch // SUBLANES):
        for s in range(TILE_SEQS):
            rows[pl.ds(ct * SUBLANES * SUBLANES + first_seq + s, SUBLANES, stride=SUBLANES), :] = (
                val[ct * SUBLANES:(ct + 1) * SUBLANES, s * LANES:(s + 1) * LANES])


def _load_slabs(slab_ref, first_seq):
    n_ch = slab_ref.shape[0]
    rows = slab_ref.reshape(n_ch * SUBLANES, LANES)
    bands = []
    for ct in range(n_ch // SUBLANES):
        bands.append(jnp.concatenate(
            [rows[pl.ds(ct * SUBLANES * SUBLANES + first_seq + s, SUBLANES, stride=SUBLANES), :]
             for s in range(TILE_SEQS)], axis=1))
    return jnp.concatenate(bands, axis=0)


def _ffn1_in_proj_kernel(x_ref, g1_ref, wg_ref, wu_ref, wd_ref, gm_ref, win_t_ref, *rest):
    n_cast = (len(rest) - 3) // 2
    x1_ref, us_ref, zs_ref = rest[n_cast:n_cast + 3]
    first_seq = pl.program_id(1) * TILE_SEQS
    x = x_ref[...].reshape(ROW_TILE, D_MODEL)
    h, r = _rms_split(x, g1_ref[...])
    x1 = x + 0.5 * _swiglu(h, r, wg_ref, wu_ref, wd_ref)
    x1_ref[...] = x1.reshape(TILE_SEQS, CHUNK, D_MODEL)
    h2, r2 = _rms_split(x1, gm_ref[...])
    r2_lanes = jnp.transpose(jnp.broadcast_to(r2, (ROW_TILE, LANES)))[0:1, :]

    def in_proj_t(lo, hi):
        return lax.dot_general(win_t_ref[lo:hi, :], h2, _NT, preferred_element_type=F32) * r2_lanes

    z_t = (in_proj_t(S5_WIDTH, S5_WIDTH + CONV_WIDTH)
           * jax.nn.sigmoid(in_proj_t(S5_WIDTH + CONV_WIDTH, IN_COLS)))
    _store_slabs(zs_ref, z_t, first_seq)
    _store_slabs(us_ref, in_proj_t(0, S5_WIDTH), first_seq)
    for src_ref, dst_ref in zip(rest[:n_cast], rest[n_cast + 3:]):
        dst_ref[...] = src_ref[...].astype(BF16)


def _s5_mix_kernel(us_ref, pow_ref, vec_ref, ys_ref, *scratch, n_chunks, batch):
    for g in range(S5_STEP_GROUPS):
        ch = pl.ds(g * S5_GROUP_CH, S5_GROUP_CH)
        _s5_group(us_ref.at[ch], pow_ref.at[g], vec_ref.at[g], ys_ref.at[ch], *scratch,
                  n_chunks=n_chunks, batch=batch)


def _s5_group(us_ref, pow_ref, vec_ref, ys_ref,
              cb_ref, f_ref, lhs_ref, win_ref, wout_ref, sprev_ref, *, n_chunks, batch):
    swap = lambda v: pltpu.roll(v, S5_STATE, 1)
    lo_half = lax.broadcasted_iota(jnp.int32, (1, 2 * S5_STATE), 1) < S5_STATE
    pa_rev = pow_ref[0:CHUNK, :]
    pb_rev = swap(pa_rev)
    pa_fwd = pow_ref[CHUNK + 1:2 * CHUNK + 1, :]
    pb_fwd = swap(pa_fwd)
    for c in range(S5_GROUP_CH):
        rows = slice(c * CHUNK, (c + 1) * CHUNK)
        vec = lambda i: vec_ref[i, c:c + 1, :]
        win_ref[rows, 0:2 * S5_STATE] = (pa_rev * vec(_V_BR) + pb_rev * vec(_V_BI_NP)).astype(BF16)
        win_ref[rows, 2 * S5_STATE:] = (pb_rev * vec(_V_BR) + pa_rev * vec(_V_BI_PN)).astype(BF16)
        wout_ref[rows, :] = (pa_fwd * vec(_V_CR_PN) + pb_fwd * vec(_V_CI_NN)).astype(BF16)
        lhs_ref[:, rows] = us_ref[c].astype(BF16)
        cb_ref[c * S5_GROUP_CH:(c + 1) * S5_GROUP_CH, :] = (
            vec_ref[_V_C_RE_NIM] * vec(_V_BR) + vec_ref[_V_C_NIM_NRE] * vec(_V_BI_PP))
    f_ref[...] = lax.dot_general(cb_ref[...], pow_ref[CHUNK:2 * CHUNK, :], _NT, precision=lax.Precision.HIGHEST,
                                 preferred_element_type=F32)

    e = _dot(lhs_ref[...], win_ref[...])
    a_chunk = pow_ref[2 * CHUNK:2 * CHUNK + 1, :]
    a_sw = swap(a_chunk)
    a1 = jnp.where(lo_half, a_chunk, a_sw)
    a2 = jnp.where(lo_half, -a_sw, a_chunk)
    s = jnp.zeros((batch, 2 * S5_STATE), F32)
    s_sw = jnp.zeros((batch, 2 * S5_STATE), F32)
    for c in range(n_chunks):
        rows = slice(c * batch, (c + 1) * batch)
        sprev_ref[rows, :] = s
        e_c = e[rows, :]
        s, s_sw = (a1 * s + a2 * s_sw + e_c[:, :2 * S5_STATE],
                   a1 * s_sw - a2 * s + e_c[:, 2 * S5_STATE:])
    sprev = sprev_ref[...].astype(BF16)

    lag_row = lax.broadcasted_iota(jnp.int32, (CHUNK, CHUNK), 0)
    lag_col = lax.broadcasted_iota(jnp.int32, (CHUNK, CHUNK), 1)
    causal = lag_col >= lag_row

    def toeplitz(ci, co):
        f = jnp.broadcast_to(f_ref[pl.ds(ci * S5_GROUP_CH + co, 1), :], (CHUNK, CHUNK))
        return jnp.where(causal, pltpu.roll(f, 0, 1, stride=1, stride_axis=0), 0.0).astype(BF16)

    def out_pair(n, carry):
        co0 = 2 * n
        w_state = wout_ref[pl.ds(pl.multiple_of(co0 * CHUNK, 2 * CHUNK), 2 * CHUNK), :]
        acc = lax.dot_general(sprev, w_state, _NT, preferred_element_type=F32)
        for kt in range(S5_GROUP_CH // 2):
            w = jnp.concatenate(
                [jnp.concatenate([toeplitz(2 * kt + r, co0), toeplitz(2 * kt + r, co0 + 1)], axis=1)
                 for r in range(2)], axis=0)
            acc = acc + _dot(lhs_ref[:, 2 * kt * CHUNK:(2 * kt + 2) * CHUNK], w)
        for r in range(2):
            y = acc[:, r * CHUNK:(r + 1) * CHUNK] + vec_ref[_V_D, pl.ds(co0 + r, 1), :] * us_ref[co0 + r]
            ys_ref[co0 + r] = jax.nn.gelu(y, approximate=True)
        return carry

    lax.fori_loop(0, S5_GROUP_CH // 2, out_pair, 0, unroll=True)


def _conv_mix_kernel(zs_ref, pc_ref, o_ref, *, batch):
    n_rows = zs_ref.shape[1]
    row = lax.broadcasted_iota(jnp.int32, (CHUNK, CHUNK), 0)
    col = lax.broadcasted_iota(jnp.int32, (CHUNK, CHUNK), 1)
    lag = row - col
    band_cur = (lag >= -(CONV_K - 1)) & (lag <= 0)
    band_prev = lag >= CHUNK - (CONV_K - 1)

    def conv_channel(ch, carry):
        z = zs_ref[ch]
        z_prev = jnp.concatenate([jnp.zeros((batch, CHUNK), F32), z[:n_rows - batch]], axis=0)
        lhs = jnp.concatenate([z_prev, z], axis=1).astype(BF16)
        taps = jnp.broadcast_to(pc_ref[ch, _PC_TAPS:_PC_TAPS + 1, :], (CHUNK, CHUNK))
        diag = pltpu.roll(taps, 0, 1, stride=1, stride_axis=0)
        toe = jnp.concatenate([jnp.where(band_prev, diag, 0.0), jnp.where(band_cur, diag, 0.0)],
                              axis=0).astype(BF16)
        o_ref[ch] = _dot(lhs, toe) + pc_ref[ch, _PC_BIAS:_PC_BIAS + 1, :]
        return carry

    lax.fori_loop(0, CONV_HEAD_DIM, conv_channel, 0, unroll=32)

    inv_n = 1.0 / CONV_HEAD_DIM
    for r0 in range(0, n_rows, LN_ROWS):
        rows = pl.ds(r0, LN_ROWS)
        zero = jnp.zeros((LN_ROWS, CHUNK), F32)
        mu = lax.fori_loop(0, CONV_HEAD_DIM, lambda ch, acc: acc + o_ref[ch, rows, :], zero,
                           unroll=4) * inv_n
        var = lax.fori_loop(0, CONV_HEAD_DIM, lambda ch, acc: acc + jnp.square(o_ref[ch, rows, :] - mu),
                            zero, unroll=4) * inv_n
        rstd = lax.rsqrt(var + EPS)

        def norm_channel(ch, carry):
            half = ((o_ref[ch, rows, :] - mu) * rstd * pc_ref[ch, _PC_HALF_GAIN:_PC_HALF_GAIN + 1, :]
                    + pc_ref[ch, _PC_HALF_BETA:_PC_HALF_BETA + 1, :])
            o_ref[ch, rows, :] = half + half * jnp.tanh(half)
            return carry

        lax.fori_loop(0, CONV_HEAD_DIM, norm_channel, 0, unroll=4)


def _out_ffn2_norm_kernel(x1_ref, ys_ref, yc_ref, wglu_t_ref, bglu_ref, wo_ref,
                          g2_ref, wg_ref, wu_ref, wd_ref, gf_ref, o_ref):
    first_seq = pl.program_id(1) * TILE_SEQS
    ys_t = _load_slabs(ys_ref, first_seq)
    yc_t = _load_slabs(yc_ref, first_seq)
    gate_t = jax.nn.sigmoid(_dot(wglu_t_ref[...], ys_t.astype(BF16)) + bglu_ref[...])
    mix = (lax.dot_general((ys_t * gate_t).astype(BF16), wo_ref[0:S5_WIDTH, :], _TN, preferred_element_type=F32)
           + lax.dot_general(yc_t.astype(BF16), wo_ref[S5_WIDTH:, :], _TN, preferred_element_type=F32))
    x2 = x1_ref[...].reshape(ROW_TILE, D_MODEL) + mix
    h, r = _rms_split(x2, g2_ref[...])
    x3 = x2 + 0.5 * _swiglu(h, r, wg_ref, wu_ref, wd_ref)
    o_ref[...] = _rms_norm(x3, gf_ref[...]).reshape(TILE_SEQS, CHUNK, D_MODEL)


def _resident(shape):
    nd = len(shape)
    return pl.BlockSpec(shape, lambda *_: (0,) * nd, pipeline_mode=pl.Buffered(1))


def _tile_spec():
    return pl.BlockSpec((TILE_SEQS, CHUNK, D_MODEL), lambda c, h: (h, c, 0))


def _slab_spec(n_ch):
    return pl.BlockSpec((n_ch, SUBLANES, LANES), lambda c, h: (0, c, 0))


_DENSE_PARAMS = pltpu.CompilerParams(dimension_semantics=("arbitrary", "arbitrary"),
                                     vmem_limit_bytes=VMEM_LIMIT)
_MIX_PARAMS = pltpu.CompilerParams(dimension_semantics=("arbitrary",), vmem_limit_bytes=VMEM_LIMIT)


def _cast_spec(w, n_steps):
    n_blocks = n_steps
    while w.shape[0] % n_blocks or (w.shape[0] // n_blocks) % (2 * SUBLANES):
        n_blocks //= 2
    steps_per_block = n_steps // n_blocks
    return pl.BlockSpec((w.shape[0] // n_blocks, w.shape[1]), lambda c, h: (c // steps_per_block, 0))


def _ffn1_in_proj(x, g1, wg, wu, wd, gm, win_t, later_weights):
    batch, seq, _ = x.shape
    assert batch == SUBLANES and seq % CHUNK == 0
    n_rows = (seq // CHUNK) * batch
    cast_specs = [_cast_spec(w, seq // CHUNK) for w in later_weights]
    return pl.pallas_call(
        _ffn1_in_proj_kernel,
        grid=(seq // CHUNK, batch // TILE_SEQS),
        in_specs=[
            _tile_spec(),
            _resident((1, D_MODEL)),
            _resident((D_MODEL, D_FF)),
            _resident((D_MODEL, D_FF)),
            _resident((D_FF, D_MODEL)),
            _resident((1, D_MODEL)),
            _resident((IN_COLS, D_MODEL)),
        ] + cast_specs,
        out_specs=[_tile_spec(), _slab_spec(S5_WIDTH), _slab_spec(CONV_WIDTH)] + cast_specs,
        out_shape=[
            jax.ShapeDtypeStruct((batch, seq, D_MODEL), F32),
            jax.ShapeDtypeStruct((S5_WIDTH, n_rows, LANES), F32),
            jax.ShapeDtypeStruct((CONV_WIDTH, n_rows, LANES), F32),
        ] + [jax.ShapeDtypeStruct(w.shape, BF16) for w in later_weights],
        compiler_params=_DENSE_PARAMS,
        name="ffn1_in_proj",
    )(x, g1, wg, wu, wd, gm, win_t, *later_weights)


def _s5_mix(us, tables, *, batch):
    n_rows = us.shape[1]
    slab = pl.BlockSpec((S5_STEP_GROUPS * S5_GROUP_CH, n_rows, LANES), lambda g: (g, 0, 0))

    def per_group(t):
        nd = t.ndim - 1
        return pl.BlockSpec((S5_STEP_GROUPS,) + t.shape[1:], lambda g: (g,) + (0,) * nd)

    return pl.pallas_call(
        functools.partial(_s5_mix_kernel, n_chunks=n_rows // batch, batch=batch),
        grid=(S5_GROUPS // S5_STEP_GROUPS,),
        in_specs=[slab] + [per_group(t) for t in tables],
        out_specs=slab,
        out_shape=jax.ShapeDtypeStruct((S5_WIDTH, n_rows, LANES), F32),
        scratch_shapes=[pltpu.VMEM((S5_GROUP_CH * S5_GROUP_CH, 2 * S5_STATE), F32),
                        pltpu.VMEM((S5_GROUP_CH * S5_GROUP_CH, CHUNK), F32),
                        pltpu.VMEM((n_rows, S5_LANES), BF16),
                        pltpu.VMEM((S5_LANES, 4 * S5_STATE), BF16),
                        pltpu.VMEM((S5_LANES, 2 * S5_STATE), BF16),
                        pltpu.VMEM((n_rows, 2 * S5_STATE), F32)],
        compiler_params=_MIX_PARAMS,
        name="s5_mix",
    )(us, *tables)


def _conv_mix(zs, per_channel, *, batch):
    n_rows = zs.shape[1]
    slab = pl.BlockSpec((CONV_HEAD_DIM, n_rows, LANES), lambda h: (h, 0, 0))
    per_ch = pl.BlockSpec((CONV_HEAD_DIM, _PC_COUNT, LANES), lambda h: (h, 0, 0))
    return pl.pallas_call(
        functools.partial(_conv_mix_kernel, batch=batch),
        grid=(CONV_HEADS,),
        in_specs=[slab, per_ch],
        out_specs=slab,
        out_shape=jax.ShapeDtypeStruct((CONV_WIDTH, n_rows, LANES), F32),
        compiler_params=_MIX_PARAMS,
        name="conv_mix",
    )(zs, per_channel)


def _out_ffn2_norm(x1, ys, yc, wglu_t, bglu, wo, g2, wg, wu, wd, gf):
    batch, seq, _ = x1.shape
    return pl.pallas_call(
        _out_ffn2_norm_kernel,
        grid=(seq // CHUNK, batch // TILE_SEQS),
        in_specs=[
            _tile_spec(),
            _slab_spec(S5_WIDTH),
            _slab_spec(CONV_WIDTH),
            _resident((S5_WIDTH, S5_WIDTH)),
            _resident((S5_WIDTH, 1)),
            _resident((S5_WIDTH + CONV_WIDTH, D_MODEL)),
            _resident((1, D_MODEL)),
            _resident((D_MODEL, D_FF)),
            _resident((D_MODEL, D_FF)),
            _resident((D_FF, D_MODEL)),
            _resident((1, D_MODEL)),
        ],
        out_specs=_tile_spec(),
        out_shape=jax.ShapeDtypeStruct((batch, seq, D_MODEL), F32),
        compiler_params=_DENSE_PARAMS,
        name="out_ffn2_norm",
    )(x1, ys, yc, wglu_t, bglu, wo, g2, wg, wu, wd, gf)


def _s5_tables(lam_re, lam_im, log_dt, b_re, b_im, c_re, c_im, d_skip):
    T, C, P, G = CHUNK, S5_GROUP_CH, S5_STATE, S5_GROUPS
    dt = jnp.exp(log_dt)[:, None]
    mag = jnp.exp(lam_re * dt)
    abar_re = mag * jnp.cos(lam_im * dt)
    abar_im = mag * jnp.sin(lam_im * dt)
    den = lam_re * lam_re + lam_im * lam_im
    num_re = abar_re - 1.0
    num_im = abar_im
    f_re = ((num_re * lam_re + num_im * lam_im) / den)[..., None]
    f_im = ((num_im * lam_re - num_re * lam_im) / den)[..., None]
    bb_re = f_re * b_re - f_im * b_im
    bb_im = f_re * b_im + f_im * b_re
    n_q = T // SUBLANES
    expo = jnp.concatenate([jnp.arange(SUBLANES), SUBLANES * jnp.arange(1, n_q + 1)]).astype(F32)[None, :, None]
    bmag = jnp.exp(expo * (lam_re * dt)[:, None, :])
    bang = expo * (lam_im * dt)[:, None, :]
    base_re, base_im = bmag * jnp.cos(bang), bmag * jnp.sin(bang)
    lo_re, lo_im = base_re[:, :SUBLANES], base_im[:, :SUBLANES]
    hi_re = jnp.concatenate([base_re[:, :1], base_re[:, SUBLANES:]], axis=1)
    hi_im = jnp.concatenate([base_im[:, :1], base_im[:, SUBLANES:]], axis=1)

    def outer(q_re, q_im, r_re, r_im):
        re = q_re[:, :, None] * r_re[:, None] - q_im[:, :, None] * r_im[:, None]
        im = q_re[:, :, None] * r_im[:, None] + q_im[:, :, None] * r_re[:, None]
        return jnp.concatenate([re, im], axis=-1).reshape(G, -1, 2 * P)

    cat = lambda a, b: jnp.concatenate([a, b], axis=-1)
    powers = jnp.concatenate([
        outer(hi_re[:, n_q - 1::-1], hi_im[:, n_q - 1::-1], lo_re[:, ::-1], lo_im[:, ::-1]),
        outer(hi_re, hi_im, lo_re, lo_im)], axis=1)
    bt_re = bb_re.transpose(0, 2, 1)
    bt_im = bb_im.transpose(0, 2, 1)
    d_rows = jnp.broadcast_to(d_skip.reshape(G, C, 1), (G, C, LANES))
    vecs = [None] * _V_COUNT
    vecs[_V_BR] = cat(bt_re, bt_re)
    vecs[_V_BI_NP] = cat(-bt_im, bt_im)
    vecs[_V_BI_PN] = cat(bt_im, -bt_im)
    vecs[_V_BI_PP] = cat(bt_im, bt_im)
    vecs[_V_CR_PN] = cat(c_re, -c_re)
    vecs[_V_CI_NN] = cat(-c_im, -c_im)
    vecs[_V_C_RE_NIM] = cat(c_re, -c_im)
    vecs[_V_C_NIM_NRE] = cat(-c_im, -c_re)
    vecs[_V_D] = d_rows
    return powers, jnp.stack(vecs, axis=1)


def _conv_tables(w_dw, b_dw, ln_g, ln_b):
    taps = jnp.pad(w_dw[::-1].T, ((0, 0), (0, LANES - CONV_K)))
    lanes = lambda v: jnp.broadcast_to(v[:, None], (CONV_WIDTH, LANES))
    rows = [None] * _PC_COUNT
    rows[_PC_TAPS], rows[_PC_BIAS] = taps, lanes(b_dw)
    rows[_PC_HALF_GAIN], rows[_PC_HALF_BETA] = lanes(0.5 * ln_g), lanes(0.5 * ln_b)
    return jnp.stack(rows, axis=1)


def kernel(x, ffn1_norm, ffn1_w_gate, ffn1_w_up, ffn1_w_down, mix_norm, w_in, s5_lam_re, s5_lam_im, s5_log_dt, s5_b_re, s5_b_im, s5_c_re, s5_c_im, s5_d, s5_w_glu, s5_b_glu, conv_w_dw, conv_b_dw, conv_ln_g, conv_ln_b, w_out, ffn2_norm, ffn2_w_gate, ffn2_w_up, ffn2_w_down, final_norm):
    batch, seq, d_model = x.shape
    assert ffn1_norm.shape[0] == 1 and d_model == D_MODEL
    l = 0
    row = lambda v: v.reshape(1, -1)

    x1, us, zs, wg2, wu2, wd2, wo = _ffn1_in_proj(
        x, row(ffn1_norm[l]), ffn1_w_gate[l].astype(BF16), ffn1_w_up[l].astype(BF16),
        ffn1_w_down[l].astype(BF16), row(mix_norm[l]), w_in[l].T.astype(BF16),
        (ffn2_w_gate[l], ffn2_w_up[l], ffn2_w_down[l], w_out[l]))

    tables = _s5_tables(
        s5_lam_re[l], s5_lam_im[l], s5_log_dt[l], s5_b_re[l], s5_b_im[l], s5_c_re[l], s5_c_im[l], s5_d[l])
    ys = _s5_mix(us, tables, batch=batch)

    yc = _conv_mix(zs, _conv_tables(conv_w_dw[l], conv_b_dw[l], conv_ln_g[l], conv_ln_b[l]), batch=batch)

    return _out_ffn2_norm(
        x1, ys, yc, s5_w_glu[l].T.astype(BF16), s5_b_glu[l].reshape(-1, 1), wo,
        row(ffn2_norm[l]), wg2, wu2, wd2, row(final_norm))
```

```python
import functools

import jax
import jax.numpy as jnp
from jax import lax
from jax.experimental import pallas as pl
from jax.experimental.pallas import tpu as pltpu

D_MODEL = 1024
D_FF = 2816
S5_WIDTH = 512
S5_GROUP_CH = 16
S5_GROUPS = S5_WIDTH // S5_GROUP_CH
S5_STATE = 64
CONV_WIDTH = 512
CONV_HEAD_DIM = 64
CONV_HEADS = CONV_WIDTH // CONV_HEAD_DIM
CONV_K = 31
IN_COLS = S5_WIDTH + 2 * CONV_WIDTH
EPS = 1e-6

LANES = 128
SUBLANES = 8
CHUNK = LANES
S5_LANES = S5_GROUP_CH * CHUNK
S5_STEP_GROUPS = 2
TILE_SEQS = 8
ROW_TILE = TILE_SEQS * CHUNK
FF_SPLITS = ((0, 1536), (1536, 2816))
LN_ROWS = 64
VMEM_LIMIT = 60 * 1024 * 1024

BF16 = jnp.bfloat16
F32 = jnp.float32

(_V_BR,
 _V_BI_NP,
 _V_BI_PN,
 _V_BI_PP,
 _V_CR_PN,
 _V_CI_NN,
 _V_C_RE_NIM,
 _V_C_NIM_NRE,
 _V_D,
 _V_COUNT) = range(10)

_PC_TAPS, _PC_BIAS, _PC_HALF_GAIN, _PC_HALF_BETA, _PC_COUNT = range(5)

_NT = (((1,), (1,)), ((), ()))
_TN = (((0,), (0,)), ((), ()))


def _rms_norm(x, g):
    return x * lax.rsqrt(jnp.mean(x * x, axis=-1, keepdims=True) + EPS) * g


def _dot(a, b):
    return jnp.dot(a, b, preferred_element_type=F32)


def _rms_split(x, g):
    return (x * g).astype(BF16), lax.rsqrt(jnp.mean(x * x, axis=-1, keepdims=True) + EPS)


def _swiglu(h, row_scale, wg_ref, wu_ref, wd_ref):
    acc = None
    for lo, hi in FF_SPLITS:
        g = _dot(h, wg_ref[:, lo:hi]) * row_scale
        u = _dot(h, wu_ref[:, lo:hi]) * row_scale
        a = (g * jax.nn.sigmoid(g) * u).astype(BF16)
        part = _dot(a, wd_ref[lo:hi, :])
        acc = part if acc is None else acc + part
    return acc


def _store_slabs(slab_ref, val, first_seq):
    n_ch = slab_ref.shape[0]
    rows = slab_ref.reshape(n_ch * SUBLANES, LANES)
    for ct in range(n_ch // SUBLANES):
        for s in range(TILE_SEQS):
            rows[pl.ds(ct * SUBLANES * SUBLANES + first_seq + s, SUBLANES, stride=SUBLANES), :] = (
                val[ct * SUBLANES:(ct + 1) * SUBLANES, s * LANES:(s + 1) * LANES])


def _load_slabs(slab_ref, first_seq):
    n_ch = slab_ref.shape[0]
    rows = slab_ref.reshape(n_ch * SUBLANES, LANES)
    bands = []
    for ct in range(n_ch // SUBLANES):
        bands.append(jnp.concatenate(
            [rows[pl.ds(ct * SUBLANES * SUBLANES + first_seq + s, SUBLANES, stride=SUBLANES), :]
             for s in range(TILE_SEQS)], axis=1))
    return jnp.concatenate(bands, axis=0)


def _ffn1_in_proj_kernel(x_ref, g1_ref, wg_ref, wu_ref, wd_ref, gm_ref, win_t_ref, *rest):
    n_cast = (len(rest) - 3) // 2
    x1_ref, us_ref, zs_ref = rest[n_cast:n_cast + 3]
    first_seq = pl.program_id(1) * TILE_SEQS
    x = x_ref[...].reshape(ROW_TILE, D_MODEL)
    h, r = _rms_split(x, g1_ref[...])
    x1 = x + 0.5 * _swiglu(h, r, wg_ref, wu_ref, wd_ref)
    x1_ref[...] = x1.reshape(TILE_SEQS, CHUNK, D_MODEL)
    h2, r2 = _rms_split(x1, gm_ref[...])
    r2_lanes = jnp.transpose(jnp.broadcast_to(r2, (ROW_TILE, LANES)))[0:1, :]

    def in_proj_t(lo, hi):
        return lax.dot_general(win_t_ref[lo:hi, :], h2, _NT, preferred_element_type=F32) * r2_lanes

    z_t = (in_proj_t(S5_WIDTH, S5_WIDTH + CONV_WIDTH)
           * jax.nn.sigmoid(in_proj_t(S5_WIDTH + CONV_WIDTH, IN_COLS)))
    _store_slabs(zs_ref, z_t, first_seq)
    _store_slabs(us_ref, in_proj_t(0, S5_WIDTH), first_seq)
    for src_ref, dst_ref in zip(rest[:n_cast], rest[n_cast + 3:]):
        dst_ref[...] = src_ref[...].astype(BF16)


def _s5_mix_kernel(us_ref, pow_ref, vec_ref, ys_ref, *scratch, n_chunks, batch):
    for g in range(S5_STEP_GROUPS):
        ch = pl.ds(g * S5_GROUP_CH, S5_GROUP_CH)
        _s5_group(us_ref.at[ch], pow_ref.at[g], vec_ref.at[g], ys_ref.at[ch], *scratch,
                  n_chunks=n_chunks, batch=batch)


def _s5_group(us_ref, pow_ref, vec_ref, ys_ref,
              cb_ref, f_ref, lhs_ref, win_ref, wout_ref, sprev_ref, *, n_chunks, batch):
    swap = lambda v: pltpu.roll(v, S5_STATE, 1)
    lo_half = lax.broadcasted_iota(jnp.int32, (1, 2 * S5_STATE), 1) < S5_STATE
    pa_rev = pow_ref[0:CHUNK, :]
    pb_rev = swap(pa_rev)
    pa_fwd = pow_ref[CHUNK + 1:2 * CHUNK + 1, :]
    pb_fwd = swap(pa_fwd)
    for c in range(S5_GROUP_CH):
        rows = slice(c * CHUNK, (c + 1) * CHUNK)
        vec = lambda i: vec_ref[i, c:c + 1, :]
        win_ref[rows, 0:2 * S5_STATE] = (pa_rev * vec(_V_BR) + pb_rev * vec(_V_BI_NP)).astype(BF16)
        win_ref[rows, 2 * S5_STATE:] = (pb_rev * vec(_V_BR) + pa_rev * vec(_V_BI_PN)).astype(BF16)
        wout_ref[rows, :] = (pa_fwd * vec(_V_CR_PN) + pb_fwd * vec(_V_CI_NN)).astype(BF16)
        lhs_ref[:, rows] = us_ref[c].astype(BF16)
        cb_ref[c * S5_GROUP_CH:(c + 1) * S5_GROUP_CH, :] = (
            vec_ref[_V_C_RE_NIM] * vec(_V_BR) + vec_ref[_V_C_NIM_NRE] * vec(_V_BI_PP))
    f_ref[...] = lax.dot_general(cb_ref[...], pow_ref[CHUNK:2 * CHUNK, :], _NT, precision=lax.Precision.HIGHEST,
                                 preferred_element_type=F32)

    e = _dot(lhs_ref[...], win_ref[...])
    a_chunk = pow_ref[2 * CHUNK:2 * CHUNK + 1, :]
    a_sw = swap(a_chunk)
    a1 = jnp.where(lo_half, a_chunk, a_sw)
    a2 = jnp.where(lo_half, -a_sw, a_chunk)
    s = jnp.zeros((batch, 2 * S5_STATE), F32)
    s_sw = jnp.zeros((batch, 2 * S5_STATE), F32)
    for c in range(n_chunks):
        rows = slice(c * batch, (c + 1) * batch)
        sprev_ref[rows, :] = s
        e_c = e[rows, :]
        s, s_sw = (a1 * s + a2 * s_sw + e_c[:, :2 * S5_STATE],
                   a1 * s_sw - a2 * s + e_c[:, 2 * S5_STATE:])
    sprev = sprev_ref[...].astype(BF16)

    lag_row = lax.broadcasted_iota(jnp.int32, (CHUNK, CHUNK), 0)
    lag_col = lax.broadcasted_iota(jnp.int32, (CHUNK, CHUNK), 1)
    causal = lag_col >= lag_row

    def toeplitz(ci, co):
        f = jnp.broadcast_to(f_ref[pl.ds(ci * S5_GROUP_CH + co, 1), :], (CHUNK, CHUNK))
        return jnp.where(causal, pltpu.roll(f, 0, 1, stride=1, stride_axis=0), 0.0).astype(BF16)

    def out_pair(n, carry):
        co0 = 2 * n
        w_state = wout_ref[pl.ds(pl.multiple_of(co0 * CHUNK, 2 * CHUNK), 2 * CHUNK), :]
        acc = lax.dot_general(sprev, w_state, _NT, preferred_element_type=F32)
        for kt in range(S5_GROUP_CH // 2):
            w = jnp.concatenate(
                [jnp.concatenate([toeplitz(2 * kt + r, co0), toeplitz(2 * kt + r, co0 + 1)], axis=1)
                 for r in range(2)], axis=0)
            acc = acc + _dot(lhs_ref[:, 2 * kt * CHUNK:(2 * kt + 2) * CHUNK], w)
        for r in range(2):
            y = acc[:, r * CHUNK:(r + 1) * CHUNK] + vec_ref[_V_D, pl.ds(co0 + r, 1), :] * us_ref[co0 + r]
            ys_ref[co0 + r] = jax.nn.gelu(y, approximate=True)
        return carry

    lax.fori_loop(0, S5_GROUP_CH // 2, out_pair, 0, unroll=True)


def _conv_mix_kernel(zs_ref, pc_ref, o_ref, *, batch):
    n_rows = zs_ref.shape[1]
    row = lax.broadcasted_iota(jnp.int32, (CHUNK, CHUNK), 0)
    col = lax.broadcasted_iota(jnp.int32, (CHUNK, CHUNK), 1)
    lag = row - col
    band_cur = (lag >= -(CONV_K - 1)) & (lag <= 0)
    band_prev = lag >= CHUNK - (CONV_K - 1)

    def conv_channel(ch, carry):
        z = zs_ref[ch]
        z_prev = jnp.concatenate([jnp.zeros((batch, CHUNK), F32), z[:n_rows - batch]], axis=0)
        lhs = jnp.concatenate([z_prev, z], axis=1).astype(BF16)
        taps = jnp.broadcast_to(pc_ref[ch, _PC_TAPS:_PC_TAPS + 1, :], (CHUNK, CHUNK))
        diag = pltpu.roll(taps, 0, 1, stride=1, stride_axis=0)
        toe = jnp.concatenate([jnp.where(band_prev, diag, 0.0), jnp.where(band_cur, diag, 0.0)],
                              axis=0).astype(BF16)
        o_ref[ch] = _dot(lhs, toe) + pc_ref[ch, _PC_BIAS:_PC_BIAS + 1, :]
        return carry

    lax.fori_loop(0, CONV_HEAD_DIM, conv_channel, 0, unroll=32)

    inv_n = 1.0 / CONV_HEAD_DIM
    for r0 in range(0, n_rows, LN_ROWS):
        rows = pl.ds(r0, LN_ROWS)
        zero = jnp.zeros((LN_ROWS, CHUNK), F32)
        mu = lax.fori_loop(0, CONV_HEAD_DIM, lambda ch, acc: acc + o_ref[ch, rows, :], zero,
                           unroll=4) * inv_n
        var = lax.fori_loop(0, CONV_HEAD_DIM, lambda ch, acc: acc + jnp.square(o_ref[ch, rows, :] - mu),
                            zero, unroll=4) * inv_n
        rstd = lax.rsqrt(var + EPS)

        def norm_channel(ch, carry):
            half = ((o_ref[ch, rows, :] - mu) * rstd * pc_ref[ch, _PC_HALF_GAIN:_PC_HALF_GAIN + 1, :]
                    + pc_ref[ch, _PC_HALF_BETA:_PC_HALF_BETA + 1, :])
            o_ref[ch, rows, :] = half + half * jnp.tanh(half)
            return carry

        lax.fori_loop(0, CONV_HEAD_DIM, norm_channel, 0, unroll=4)


def _out_ffn2_norm_kernel(x1_ref, ys_ref, yc_ref, wglu_t_ref, bglu_ref, wo_ref,
                          g2_ref, wg_ref, wu_ref, wd_ref, gf_ref, o_ref):
    first_seq = pl.program_id(1) * TILE_SEQS
    ys_t = _load_slabs(ys_ref, first_seq)
    yc_t = _load_slabs(yc_ref, first_seq)
    gate_t = jax.nn.sigmoid(_dot(wglu_t_ref[...], ys_t.astype(BF16)) + bglu_ref[...])
    mix = (lax.dot_general((ys_t * gate_t).astype(BF16), wo_ref[0:S5_WIDTH, :], _TN, preferred_element_type=F32)
           + lax.dot_general(yc_t.astype(BF16), wo_ref[S5_WIDTH:, :], _TN, preferred_element_type=F32))
    x2 = x1_ref[...].reshape(ROW_TILE, D_MODEL) + mix
    h, r = _rms_split(x2, g2_ref[...])
    x3 = x2 + 0.5 * _swiglu(h, r, wg_ref, wu_ref, wd_ref)
    o_ref[...] = _rms_norm(x3, gf_ref[...]).reshape(TILE_SEQS, CHUNK, D_MODEL)


def _resident(shape):
    nd = len(shape)
    return pl.BlockSpec(shape, lambda *_: (0,) * nd, pipeline_mode=pl.Buffered(1))


def _tile_spec():
    return pl.BlockSpec((TILE_SEQS, CHUNK, D_MODEL), lambda c, h: (h, c, 0))


def _slab_spec(n_ch):
    return pl.BlockSpec((n_ch, SUBLANES, LANES), lambda c, h: (0, c, 0))


_DENSE_PARAMS = pltpu.CompilerParams(dimension_semantics=("arbitrary", "arbitrary"),
                                     vmem_limit_bytes=VMEM_LIMIT)
_MIX_PARAMS = pltpu.CompilerParams(dimension_semantics=("arbitrary",), vmem_limit_bytes=VMEM_LIMIT)


def _cast_spec(w, n_steps):
    n_blocks = n_steps
    while w.shape[0] % n_blocks or (w.shape[0] // n_blocks) % (2 * SUBLANES):
        n_blocks //= 2
    steps_per_block = n_steps // n_blocks
    return pl.BlockSpec((w.shape[0] // n_blocks, w.shape[1]), lambda c, h: (c // steps_per_block, 0))


def _ffn1_in_proj(x, g1, wg, wu, wd, gm, win_t, later_weights):
    batch, seq, _ = x.shape
    assert batch == SUBLANES and seq % CHUNK == 0
    n_rows = (seq // CHUNK) * batch
    cast_specs = [_cast_spec(w, seq // CHUNK) for w in later_weights]
    return pl.pallas_call(
        _ffn1_in_proj_kernel,
        grid=(seq // CHUNK, batch // TILE_SEQS),
        in_specs=[
            _tile_spec(),
            _resident((1, D_MODEL)),
            _resident((D_MODEL, D_FF)),
            _resident((D_MODEL, D_FF)),
            _resident((D_FF, D_MODEL)),
            _resident((1, D_MODEL)),
            _resident((IN_COLS, D_MODEL)),
        ] + cast_specs,
        out_specs=[_tile_spec(), _slab_spec(S5_WIDTH), _slab_spec(CONV_WIDTH)] + cast_specs,
        out_shape=[
            jax.ShapeDtypeStruct((batch, seq, D_MODEL), F32),
            jax.ShapeDtypeStruct((S5_WIDTH, n_rows, LANES), F32),
            jax.ShapeDtypeStruct((CONV_WIDTH, n_rows, LANES), F32),
        ] + [jax.ShapeDtypeStruct(w.shape, BF16) for w in later_weights],
        compiler_params=_DENSE_PARAMS,
        name="ffn1_in_proj",
    )(x, g1, wg, wu, wd, gm, win_t, *later_weights)


def _s5_mix(us, tables, *, batch):
    n_rows = us.shape[1]
    slab = pl.BlockSpec((S5_STEP_GROUPS * S5_GROUP_CH, n_rows, LANES), lambda g: (g, 0, 0))

    def per_group(t):
        nd = t.ndim - 1
        return pl.BlockSpec((S5_STEP_GROUPS,) + t.shape[1:], lambda g: (g,) + (0,) * nd)

    return pl.pallas_call(
        functools.partial(_s5_mix_kernel, n_chunks=n_rows // batch, batch=batch),
        grid=(S5_GROUPS // S5_STEP_GROUPS,),
        in_specs=[slab] + [per_group(t) for t in tables],
        out_specs=slab,
        out_shape=jax.ShapeDtypeStruct((S5_WIDTH, n_rows, LANES), F32),
        scratch_shapes=[pltpu.VMEM((S5_GROUP_CH * S5_GROUP_CH, 2 * S5_STATE), F32),
                        pltpu.VMEM((S5_GROUP_CH * S5_GROUP_CH, CHUNK), F32),
                        pltpu.VMEM((n_rows, S5_LANES), BF16),
                        pltpu.VMEM((S5_LANES, 4 * S5_STATE), BF16),
                        pltpu.VMEM((S5_LANES, 2 * S5_STATE), BF16),
                        pltpu.VMEM((n_rows, 2 * S5_STATE), F32)],
        compiler_params=_MIX_PARAMS,
        name="s5_mix",
    )(us, *tables)


def _conv_mix(zs, per_channel, *, batch):
    n_rows = zs.shape[1]
    slab = pl.BlockSpec((CONV_HEAD_DIM, n_rows, LANES), lambda h: (h, 0, 0))
    per_ch = pl.BlockSpec((CONV_HEAD_DIM, _PC_COUNT, LANES), lambda h: (h, 0, 0))
    return pl.pallas_call(
        functools.partial(_conv_mix_kernel, batch=batch),
        grid=(CONV_HEADS,),
        in_specs=[slab, per_ch],
        out_specs=slab,
        out_shape=jax.ShapeDtypeStruct((CONV_WIDTH, n_rows, LANES), F32),
        compiler_params=_MIX_PARAMS,
        name="conv_mix",
    )(zs, per_channel)


def _out_ffn2_norm(x1, ys, yc, wglu_t, bglu, wo, g2, wg, wu, wd, gf):
    batch, seq, _ = x1.shape
    return pl.pallas_call(
        _out_ffn2_norm_kernel,
        grid=(seq // CHUNK, batch // TILE_SEQS),
        in_specs=[
            _tile_spec(),
            _slab_spec(S5_WIDTH),
            _slab_spec(CONV_WIDTH),
            _resident((S5_WIDTH, S5_WIDTH)),
            _resident((S5_WIDTH, 1)),
            _resident((S5_WIDTH + CONV_WIDTH, D_MODEL)),
            _resident((1, D_MODEL)),
            _resident((D_MODEL, D_FF)),
            _resident((D_MODEL, D_FF)),
            _resident((D_FF, D_MODEL)),
            _resident((1, D_MODEL)),
        ],
        out_specs=_tile_spec(),
        out_shape=jax.ShapeDtypeStruct((batch, seq, D_MODEL), F32),
        compiler_params=_DENSE_PARAMS,
        name="out_ffn2_norm",
    )(x1, ys, yc, wglu_t, bglu, wo, g2, wg, wu, wd, gf)


def _s5_tables(lam_re, lam_im, log_dt, b_re, b_im, c_re, c_im, d_skip):
    T, C, P, G = CHUNK, S5_GROUP_CH, S5_STATE, S5_GROUPS
    dt = jnp.exp(log_dt)[:, None]
    mag = jnp.exp(lam_re * dt)
    abar_re = mag * jnp.cos(lam_im * dt)
    abar_im = mag * jnp.sin(lam_im * dt)
    den = lam_re * lam_re + lam_im * lam_im
    num_re = abar_re - 1.0
    num_im = abar_im
    f_re = ((num_re * lam_re + num_im * lam_im) / den)[..., None]
    f_im = ((num_im * lam_re - num_re * lam_im) / den)[..., None]
    bb_re = f_re * b_re - f_im * b_im
    bb_im = f_re * b_im + f_im * b_re
    n_q = T // SUBLANES
    expo = jnp.concatenate([jnp.arange(SUBLANES), SUBLANES * jnp.arange(1, n_q + 1)]).astype(F32)[None, :, None]
    bmag = jnp.exp(expo * (lam_re * dt)[:, None, :])
    bang = expo * (lam_im * dt)[:, None, :]
    base_re, base_im = bmag * jnp.cos(bang), bmag * jnp.sin(bang)
    lo_re, lo_im = base_re[:, :SUBLANES], base_im[:, :SUBLANES]
    hi_re = jnp.concatenate([base_re[:, :1], base_re[:, SUBLANES:]], axis=1)
    hi_im = jnp.concatenate([base_im[:, :1], base_im[:, SUBLANES:]], axis=1)

    def outer(q_re, q_im, r_re, r_im):
        re = q_re[:, :, None] * r_re[:, None] - q_im[:, :, None] * r_im[:, None]
        im = q_re[:, :, None] * r_im[:, None] + q_im[:, :, None] * r_re[:, None]
        return jnp.concatenate([re, im], axis=-1).reshape(G, -1, 2 * P)

    cat = lambda a, b: jnp.concatenate([a, b], axis=-1)
    powers = jnp.concatenate([
        outer(hi_re[:, n_q - 1::-1], hi_im[:, n_q - 1::-1], lo_re[:, ::-1], lo_im[:, ::-1]),
        outer(hi_re, hi_im, lo_re, lo_im)], axis=1)
    bt_re = bb_re.transpose(0, 2, 1)
    bt_im = bb_im.transpose(0, 2, 1)
    d_rows = jnp.broadcast_to(d_skip.reshape(G, C, 1), (G, C, LANES))
    vecs = [None] * _V_COUNT
    vecs[_V_BR] = cat(bt_re, bt_re)
    vecs[_V_BI_NP] = cat(-bt_im, bt_im)
    vecs[_V_BI_PN] = cat(bt_im, -bt_im)
    vecs[_V_BI_PP] = cat(bt_im, bt_im)
    vecs[_V_CR_PN] = cat(c_re, -c_re)
    vecs[_V_CI_NN] = cat(-c_im, -c_im)
    vecs[_V_C_RE_NIM] = cat(c_re, -c_im)
    vecs[_V_C_NIM_NRE] = cat(-c_im, -c_re)
    vecs[_V_D] = d_rows
    return powers, jnp.stack(vecs, axis=1)


def _conv_tables(w_dw, b_dw, ln_g, ln_b):
    taps = jnp.pad(w_dw[::-1].T, ((0, 0), (0, LANES - CONV_K)))
    lanes = lambda v: jnp.broadcast_to(v[:, None], (CONV_WIDTH, LANES))
    rows = [None] * _PC_COUNT
    rows[_PC_TAPS], rows[_PC_BIAS] = taps, lanes(b_dw)
    rows[_PC_HALF_GAIN], rows[_PC_HALF_BETA] = lanes(0.5 * ln_g), lanes(0.5 * ln_b)
    return jnp.stack(rows, axis=1)


def kernel(x, ffn1_norm, ffn1_w_gate, ffn1_w_up, ffn1_w_down, mix_norm, w_in, s5_lam_re, s5_lam_im, s5_log_dt, s5_b_re, s5_b_im, s5_c_re, s5_c_im, s5_d, s5_w_glu, s5_b_glu, conv_w_dw, conv_b_dw, conv_ln_g, conv_ln_b, w_out, ffn2_norm, ffn2_w_gate, ffn2_w_up, ffn2_w_down, final_norm):
    batch, seq, d_model = x.shape
    assert ffn1_norm.shape[0] == 1 and d_model == D_MODEL
    l = 0
    row = lambda v: v.reshape(1, -1)

    x1, us, zs, wg2, wu2, wd2, wo = _ffn1_in_proj(
        x, row(ffn1_norm[l]), ffn1_w_gate[l].astype(BF16), ffn1_w_up[l].astype(BF16),
        ffn1_w_down[l].astype(BF16), row(mix_norm[l]), w_in[l].T.astype(BF16),
        (ffn2_w_gate[l], ffn2_w_up[l], ffn2_w_down[l], w_out[l]))

    tables = _s5_tables(
        s5_lam_re[l], s5_lam_im[l], s5_log_dt[l], s5_b_re[l], s5_b_im[l], s5_c_re[l], s5_c_im[l], s5_d[l])
    ys = _s5_mix(us, tables, batch=batch)

    yc = _conv_mix(zs, _conv_tables(conv_w_dw[l], conv_b_dw[l], conv_ln_g[l], conv_ln_b[l]), batch=batch)

    return _out_ffn2_norm(
        x1, ys, yc, s5_w_glu[l].T.astype(BF16), s5_b_glu[l].reshape(-1, 1), wo,
        row(ffn2_norm[l]), wg2, wu2, wd2, row(final_norm))
```

```python
import functools

import jax
import jax.numpy as jnp
from jax import lax
from jax.experimental import pallas as pl
from jax.experimental.pallas import tpu as pltpu

D_MODEL = 1024
D_FF = 2816
S5_WIDTH = 512
S5_GROUP_CH = 16
S5_GROUPS = S5_WIDTH // S5_GROUP_CH
S5_STATE = 64
CONV_WIDTH = 512
CONV_HEAD_DIM = 64
CONV_HEADS = CONV_WIDTH // CONV_HEAD_DIM
CONV_K = 31
IN_COLS = S5_WIDTH + 2 * CONV_WIDTH
EPS = 1e-6

LANES = 128
SUBLANES = 8
CHUNK = LANES
S5_T = 64
S5_HALVES = CHUNK // S5_T
S5_PAIRS = S5_GROUP_CH // 2
S5_LANES = S5_GROUP_CH * S5_T
S5_STEP_GROUPS = 2
TILE_SEQS = 8
ROW_TILE = TILE_SEQS * CHUNK
FF_SPLITS = ((0, 1536), (1536, 2816))
LN_ROWS = 64
VMEM_LIMIT = 60 * 1024 * 1024

BF16 = jnp.bfloat16
F32 = jnp.float32

(_V_BR,
 _V_BI_NP,
 _V_BI_PN,
 _V_BI_PP,
 _V_CR_PN,
 _V_CI_NN,
 _V_C_RE_NIM,
 _V_C_NIM_NRE,
 _V_D,
 _V_COUNT) = range(10)

_PC_TAPS, _PC_BIAS, _PC_HALF_GAIN, _PC_HALF_BETA, _PC_COUNT = range(5)

_NT = (((1,), (1,)), ((), ()))
_TN = (((0,), (0,)), ((), ()))


def _rms_norm(x, g):
    return x * lax.rsqrt(jnp.mean(x * x, axis=-1, keepdims=True) + EPS) * g


def _dot(a, b):
    return jnp.dot(a, b, preferred_element_type=F32)


def _rms_split(x, g):
    return (x * g).astype(BF16), lax.rsqrt(jnp.mean(x * x, axis=-1, keepdims=True) + EPS)


def _swiglu(h, row_scale, wg_ref, wu_ref, wd_ref):
    acc = None
    for lo, hi in FF_SPLITS:
        g = _dot(h, wg_ref[:, lo:hi]) * row_scale
        u = _dot(h, wu_ref[:, lo:hi]) * row_scale
        a = (g * jax.nn.sigmoid(g) * u).astype(BF16)
        part = _dot(a, wd_ref[lo:hi, :])
        acc = part if acc is None else acc + part
    return acc


def _store_slabs(slab_ref, val, first_seq):
    n_ch = slab_ref.shape[0]
    rows = slab_ref.reshape(n_ch * SUBLANES, LANES)
    for ct in range(n_ch // SUBLANES):
        for s in range(TILE_SEQS):
            rows[pl.ds(ct * SUBLANES * SUBLANES + first_seq + s, SUBLANES, stride=SUBLANES), :] = (
                val[ct * SUBLANES:(ct + 1) * SUBLANES, s * LANES:(s + 1) * LANES])


def _load_slabs(slab_ref, first_seq):
    n_ch = slab_ref.shape[0]
    rows = slab_ref.reshape(n_ch * SUBLANES, LANES)
    bands = []
    for ct in range(n_ch // SUBLANES):
        bands.append(jnp.concatenate(
            [rows[pl.ds(ct * SUBLANES * SUBLANES + first_seq + s, SUBLANES, stride=SUBLANES), :]
             for s in range(TILE_SEQS)], axis=1))
    return jnp.concatenate(bands, axis=0)


def _ffn1_in_proj_kernel(x_ref, g1_ref, wg_ref, wu_ref, wd_ref, gm_ref, win_t_ref, *rest):
    n_cast = (len(rest) - 3) // 2
    x1_ref, us_ref, zs_ref = rest[n_cast:n_cast + 3]
    first_seq = pl.program_id(1) * TILE_SEQS
    x = x_ref[...].reshape(ROW_TILE, D_MODEL)
    h, r = _rms_split(x, g1_ref[...])
    x1 = x + 0.5 * _swiglu(h, r, wg_ref, wu_ref, wd_ref)
    x1_ref[...] = x1.reshape(TILE_SEQS, CHUNK, D_MODEL)
    h2, r2 = _rms_split(x1, gm_ref[...])
    r2_lanes = jnp.transpose(jnp.broadcast_to(r2, (ROW_TILE, LANES)))[0:1, :]

    def in_proj_t(lo, hi):
        return lax.dot_general(win_t_ref[lo:hi, :], h2, _NT, preferred_element_type=F32) * r2_lanes

    z_t = (in_proj_t(S5_WIDTH, S5_WIDTH + CONV_WIDTH)
           * jax.nn.sigmoid(in_proj_t(S5_WIDTH + CONV_WIDTH, IN_COLS)))
    _store_slabs(zs_ref, z_t, first_seq)
    _store_slabs(us_ref, in_proj_t(0, S5_WIDTH), first_seq)
    for src_ref, dst_ref in zip(rest[:n_cast], rest[n_cast + 3:]):
        dst_ref[...] = src_ref[...].astype(BF16)


def _s5_mix_kernel(us_ref, pow_ref, vec_ref, ys_ref, *scratch, batch):
    for g in range(S5_STEP_GROUPS):
        ch = pl.ds(g * S5_GROUP_CH, S5_GROUP_CH)
        _s5_group(us_ref.at[ch], pow_ref.at[g], vec_ref.at[g], ys_ref.at[ch], *scratch, batch=batch)


def _s5_group(us_ref, pow_ref, vec_ref, ys_ref,
              cb_ref, f_ref, upk_ref, lhs_ref, win_ref, wout_ref, sprev_ref, *, batch):
    T = S5_T
    n_slab_rows = us_ref.shape[1]
    n_slab_chunks = n_slab_rows // batch
    n_rows = S5_HALVES * n_slab_rows
    lane = lax.broadcasted_iota(jnp.int32, (1, LANES), 1)
    first_state_half = lane < S5_STATE
    first_time_half = lane < T
    swap = lambda v: pltpu.roll(v, S5_STATE, 1)
    pa_rev = pow_ref[0:T, :]
    pb_rev = swap(pa_rev)
    pa_fwd = pow_ref[T + 1:2 * T + 1, :]
    pb_fwd = swap(pa_fwd)
    for c in range(S5_GROUP_CH):
        rows = slice(c * T, (c + 1) * T)
        vec = lambda i: vec_ref[i, c:c + 1, :]
        win_ref[rows, 0:2 * S5_STATE] = (pa_rev * vec(_V_BR) + pb_rev * vec(_V_BI_NP)).astype(BF16)
        win_ref[rows, 2 * S5_STATE:] = (pb_rev * vec(_V_BR) + pa_rev * vec(_V_BI_PN)).astype(BF16)
        wout_ref[rows, :] = (pa_fwd * vec(_V_CR_PN) + pb_fwd * vec(_V_CI_NN)).astype(BF16)
        cb_ref[c * S5_GROUP_CH:(c + 1) * S5_GROUP_CH, :] = (
            vec_ref[_V_C_RE_NIM] * vec(_V_BR) + vec_ref[_V_C_NIM_NRE] * vec(_V_BI_PP))
    f = lax.dot_general(cb_ref[...], pow_ref[T:2 * T, :], _NT, precision=lax.Precision.HIGHEST,
                        preferred_element_type=F32)
    for c in range(S5_GROUP_CH):
        even, odd = c * S5_GROUP_CH, c * S5_GROUP_CH + S5_PAIRS
        f_ref[c * S5_PAIRS:(c + 1) * S5_PAIRS, :] = jnp.concatenate(
            [f[even:even + S5_PAIRS], f[odd:odd + S5_PAIRS]], axis=1)

    for p in range(S5_PAIRS):
        a, b = us_ref[2 * p], us_ref[2 * p + 1]
        halves = (jnp.where(first_time_half, a, pltpu.roll(b, T, 1)),
                  jnp.where(first_time_half, pltpu.roll(a, T, 1), b))
        for hh in range(S5_HALVES):
            upk_ref[:, hh * batch:(hh + 1) * batch, p * LANES:(p + 1) * LANES] = (
                halves[hh].reshape(n_slab_chunks, batch, LANES))
    lhs_ref[...] = upk_ref[...].reshape(n_rows, S5_LANES).astype(BF16)

    e = _dot(lhs_ref[...], win_ref[...])
    a_chunk = pow_ref[2 * T:2 * T + 1, :]
    a_sw = swap(a_chunk)
    a1 = jnp.where(first_state_half, a_chunk, a_sw)
    a2 = jnp.where(first_state_half, -a_sw, a_chunk)
    s = jnp.zeros((batch, 2 * S5_STATE), F32)
    s_sw = jnp.zeros((batch, 2 * S5_STATE), F32)
    for c in range(n_rows // batch):
        rows = slice(c * batch, (c + 1) * batch)
        sprev_ref[rows, :] = s
        e_c = e[rows, :]
        s, s_sw = (a1 * s + a2 * s_sw + e_c[:, :2 * S5_STATE],
                   a1 * s_sw - a2 * s + e_c[:, 2 * S5_STATE:])
    sprev = sprev_ref[...].astype(BF16)

    causal = ((lax.broadcasted_iota(jnp.int32, (T, LANES), 1) & (T - 1))
              >= lax.broadcasted_iota(jnp.int32, (T, LANES), 0))

    def toeplitz(ci, pair):
        f_row = jnp.broadcast_to(f_ref[ci * S5_PAIRS + pair:ci * S5_PAIRS + pair + 1, :], (T, LANES))
        return jnp.where(causal, pltpu.roll(f_row, 0, 1, stride=1, stride_axis=0), 0.0).astype(BF16)

    ci_per_tile = 2 * LANES // T
    for n in range(S5_PAIRS // 2):
        acc = lax.dot_general(sprev, wout_ref[2 * n * LANES:(2 * n + 2) * LANES, :], _NT,
                              preferred_element_type=F32)
        for kt in range(S5_GROUP_CH // ci_per_tile):
            w = jnp.concatenate(
                [jnp.concatenate([toeplitz(ci_per_tile * kt + r, 2 * n), toeplitz(ci_per_tile * kt + r, 2 * n + 1)],
                                 axis=1) for r in range(ci_per_tile)], axis=0)
            acc = acc + _dot(lhs_ref[:, 2 * kt * LANES:(2 * kt + 2) * LANES], w)
        for q in range(2):
            pair = 2 * n + q
            u_pk = upk_ref[:, :, pair * LANES:(pair + 1) * LANES].reshape(n_rows, LANES)
            y = acc[:, q * LANES:(q + 1) * LANES] + vec_ref[_V_D, pair:pair + 1, :] * u_pk
            y = jax.nn.gelu(y, approximate=True).reshape(n_slab_chunks, S5_HALVES * batch, LANES)
            y0 = y[:, 0:batch, :].reshape(n_slab_rows, LANES)
            y1 = y[:, batch:2 * batch, :].reshape(n_slab_rows, LANES)
            ys_ref[2 * pair] = jnp.where(first_time_half, y0, pltpu.roll(y1, T, 1))
            ys_ref[2 * pair + 1] = jnp.where(first_time_half, pltpu.roll(y0, T, 1), y1)


def _conv_mix_kernel(zs_ref, pc_ref, o_ref, *, batch):
    n_rows = zs_ref.shape[1]
    row = lax.broadcasted_iota(jnp.int32, (CHUNK, CHUNK), 0)
    col = lax.broadcasted_iota(jnp.int32, (CHUNK, CHUNK), 1)
    lag = row - col
    band_cur = (lag >= -(CONV_K - 1)) & (lag <= 0)
    band_prev = lag >= CHUNK - (CONV_K - 1)

    def conv_channel(ch, carry):
        z = zs_ref[ch]
        z_prev = jnp.concatenate([jnp.zeros((batch, CHUNK), F32), z[:n_rows - batch]], axis=0)
        lhs = jnp.concatenate([z_prev, z], axis=1).astype(BF16)
        taps = jnp.broadcast_to(pc_ref[ch, _PC_TAPS:_PC_TAPS + 1, :], (CHUNK, CHUNK))
        diag = pltpu.roll(taps, 0, 1, stride=1, stride_axis=0)
        toe = jnp.concatenate([jnp.where(band_prev, diag, 0.0), jnp.where(band_cur, diag, 0.0)],
                              axis=0).astype(BF16)
        o_ref[ch] = _dot(lhs, toe) + pc_ref[ch, _PC_BIAS:_PC_BIAS + 1, :]
        return carry

    lax.fori_loop(0, CONV_HEAD_DIM, conv_channel, 0, unroll=32)

    inv_n = 1.0 / CONV_HEAD_DIM
    for r0 in range(0, n_rows, LN_ROWS):
        rows = pl.ds(r0, LN_ROWS)
        zero = jnp.zeros((LN_ROWS, CHUNK), F32)
        mu = lax.fori_loop(0, CONV_HEAD_DIM, lambda ch, acc: acc + o_ref[ch, rows, :], zero,
                           unroll=4) * inv_n
        var = lax.fori_loop(0, CONV_HEAD_DIM, lambda ch, acc: acc + jnp.square(o_ref[ch, rows, :] - mu),
                            zero, unroll=4) * inv_n
        rstd = lax.rsqrt(var + EPS)

        def norm_channel(ch, carry):
            half = ((o_ref[ch, rows, :] - mu) * rstd * pc_ref[ch, _PC_HALF_GAIN:_PC_HALF_GAIN + 1, :]
                    + pc_ref[ch, _PC_HALF_BETA:_PC_HALF_BETA + 1, :])
            o_ref[ch, rows, :] = half + half * jnp.tanh(half)
            return carry

        lax.fori_loop(0, CONV_HEAD_DIM, norm_channel, 0, unroll=4)


def _out_ffn2_norm_kernel(x1_ref, ys_ref, yc_ref, wglu_t_ref, bglu_ref, wo_ref,
                          g2_ref, wg_ref, wu_ref, wd_ref, gf_ref, o_ref):
    first_seq = pl.program_id(1) * TILE_SEQS
    ys_t = _load_slabs(ys_ref, first_seq)
    yc_t = _load_slabs(yc_ref, first_seq)
    gate_t = jax.nn.sigmoid(_dot(wglu_t_ref[...], ys_t.astype(BF16)) + bglu_ref[...])
    mix = (lax.dot_general((ys_t * gate_t).astype(BF16), wo_ref[0:S5_WIDTH, :], _TN, preferred_element_type=F32)
           + lax.dot_general(yc_t.astype(BF16), wo_ref[S5_WIDTH:, :], _TN, preferred_element_type=F32))
    x2 = x1_ref[...].reshape(ROW_TILE, D_MODEL) + mix
    h, r = _rms_split(x2, g2_ref[...])
    x3 = x2 + 0.5 * _swiglu(h, r, wg_ref, wu_ref, wd_ref)
    o_ref[...] = _rms_norm(x3, gf_ref[...]).reshape(TILE_SEQS, CHUNK, D_MODEL)


def _resident(shape):
    nd = len(shape)
    return pl.BlockSpec(shape, lambda *_: (0,) * nd, pipeline_mode=pl.Buffered(1))


def _tile_spec():
    return pl.BlockSpec((TILE_SEQS, CHUNK, D_MODEL), lambda c, h: (h, c, 0))


def _slab_spec(n_ch):
    return pl.BlockSpec((n_ch, SUBLANES, LANES), lambda c, h: (0, c, 0))


_DENSE_PARAMS = pltpu.CompilerParams(dimension_semantics=("arbitrary", "arbitrary"),
                                     vmem_limit_bytes=VMEM_LIMIT)
_MIX_PARAMS = pltpu.CompilerParams(dimension_semantics=("arbitrary",), vmem_limit_bytes=VMEM_LIMIT)


def _cast_spec(w, n_steps):
    n_blocks = n_steps
    while w.shape[0] % n_blocks or (w.shape[0] // n_blocks) % (2 * SUBLANES):
        n_blocks //= 2
    steps_per_block = n_steps // n_blocks
    return pl.BlockSpec((w.shape[0] // n_blocks, w.shape[1]), lambda c, h: (c // steps_per_block, 0))


def _ffn1_in_proj(x, g1, wg, wu, wd, gm, win_t, later_weights):
    batch, seq, _ = x.shape
    assert batch == SUBLANES and seq % CHUNK == 0
    n_rows = (seq // CHUNK) * batch
    cast_specs = [_cast_spec(w, seq // CHUNK) for w in later_weights]
    return pl.pallas_call(
        _ffn1_in_proj_kernel,
        grid=(seq // CHUNK, batch // TILE_SEQS),
        in_specs=[
            _tile_spec(),
            _resident((1, D_MODEL)),
            _resident((D_MODEL, D_FF)),
            _resident((D_MODEL, D_FF)),
            _resident((D_FF, D_MODEL)),
            _resident((1, D_MODEL)),
            _resident((IN_COLS, D_MODEL)),
        ] + cast_specs,
        out_specs=[_tile_spec(), _slab_spec(S5_WIDTH), _slab_spec(CONV_WIDTH)] + cast_specs,
        out_shape=[
            jax.ShapeDtypeStruct((batch, seq, D_MODEL), F32),
            jax.ShapeDtypeStruct((S5_WIDTH, n_rows, LANES), F32),
            jax.ShapeDtypeStruct((CONV_WIDTH, n_rows, LANES), F32),
        ] + [jax.ShapeDtypeStruct(w.shape, BF16) for w in later_weights],
        compiler_params=_DENSE_PARAMS,
        name="ffn1_in_proj",
    )(x, g1, wg, wu, wd, gm, win_t, *later_weights)


def _s5_mix(us, tables, *, batch):
    n_rows = us.shape[1]
    slab = pl.BlockSpec((S5_STEP_GROUPS * S5_GROUP_CH, n_rows, LANES), lambda g: (g, 0, 0))

    def per_group(t):
        nd = t.ndim - 1
        return pl.BlockSpec((S5_STEP_GROUPS,) + t.shape[1:], lambda g: (g,) + (0,) * nd)

    return pl.pallas_call(
        functools.partial(_s5_mix_kernel, batch=batch),
        grid=(S5_GROUPS // S5_STEP_GROUPS,),
        in_specs=[slab] + [per_group(t) for t in tables],
        out_specs=slab,
        out_shape=jax.ShapeDtypeStruct((S5_WIDTH, n_rows, LANES), F32),
        scratch_shapes=[pltpu.VMEM((S5_GROUP_CH * S5_GROUP_CH, 2 * S5_STATE), F32),
                        pltpu.VMEM((S5_GROUP_CH * S5_PAIRS, LANES), F32),
                        pltpu.VMEM((n_rows // batch, S5_HALVES * batch, S5_LANES), F32),
                        pltpu.VMEM((S5_HALVES * n_rows, S5_LANES), BF16),
                        pltpu.VMEM((S5_LANES, 4 * S5_STATE), BF16),
                        pltpu.VMEM((S5_LANES, 2 * S5_STATE), BF16),
                        pltpu.VMEM((S5_HALVES * n_rows, 2 * S5_STATE), F32)],
        compiler_params=_MIX_PARAMS,
        name="s5_mix",
    )(us, *tables)


def _conv_mix(zs, per_channel, *, batch):
    n_rows = zs.shape[1]
    slab = pl.BlockSpec((CONV_HEAD_DIM, n_rows, LANES), lambda h: (h, 0, 0))
    per_ch = pl.BlockSpec((CONV_HEAD_DIM, _PC_COUNT, LANES), lambda h: (h, 0, 0))
    return pl.pallas_call(
        functools.partial(_conv_mix_kernel, batch=batch),
        grid=(CONV_HEADS,),
        in_specs=[slab, per_ch],
        out_specs=slab,
        out_shape=jax.ShapeDtypeStruct((CONV_WIDTH, n_rows, LANES), F32),
        compiler_params=_MIX_PARAMS,
        name="conv_mix",
    )(zs, per_channel)


def _out_ffn2_norm(x1, ys, yc, wglu_t, bglu, wo, g2, wg, wu, wd, gf):
    batch, seq, _ = x1.shape
    return pl.pallas_call(
        _out_ffn2_norm_kernel,
        grid=(seq // CHUNK, batch // TILE_SEQS),
        in_specs=[
            _tile_spec(),
            _slab_spec(S5_WIDTH),
            _slab_spec(CONV_WIDTH),
            _resident((S5_WIDTH, S5_WIDTH)),
            _resident((S5_WIDTH, 1)),
            _resident((S5_WIDTH + CONV_WIDTH, D_MODEL)),
            _resident((1, D_MODEL)),
            _resident((D_MODEL, D_FF)),
            _resident((D_MODEL, D_FF)),
            _resident((D_FF, D_MODEL)),
            _resident((1, D_MODEL)),
        ],
        out_specs=_tile_spec(),
        out_shape=jax.ShapeDtypeStruct((batch, seq, D_MODEL), F32),
        compiler_params=_DENSE_PARAMS,
        name="out_ffn2_norm",
    )(x1, ys, yc, wglu_t, bglu, wo, g2, wg, wu, wd, gf)


def _s5_tables(lam_re, lam_im, log_dt, b_re, b_im, c_re, c_im, d_skip):
    T, C, P, G = S5_T, S5_GROUP_CH, S5_STATE, S5_GROUPS
    dt = jnp.exp(log_dt)[:, None]
    mag = jnp.exp(lam_re * dt)
    abar_re = mag * jnp.cos(lam_im * dt)
    abar_im = mag * jnp.sin(lam_im * dt)
    den = lam_re * lam_re + lam_im * lam_im
    num_re = abar_re - 1.0
    num_im = abar_im
    f_re = ((num_re * lam_re + num_im * lam_im) / den)[..., None]
    f_im = ((num_im * lam_re - num_re * lam_im) / den)[..., None]
    bb_re = f_re * b_re - f_im * b_im
    bb_im = f_re * b_im + f_im * b_re
    n_q = T // SUBLANES
    expo = jnp.concatenate([jnp.arange(SUBLANES), SUBLANES * jnp.arange(1, n_q + 1)]).astype(F32)[None, :, None]
    bmag = jnp.exp(expo * (lam_re * dt)[:, None, :])
    bang = expo * (lam_im * dt)[:, None, :]
    base_re, base_im = bmag * jnp.cos(bang), bmag * jnp.sin(bang)
    lo_re, lo_im = base_re[:, :SUBLANES], base_im[:, :SUBLANES]
    hi_re = jnp.concatenate([base_re[:, :1], base_re[:, SUBLANES:]], axis=1)
    hi_im = jnp.concatenate([base_im[:, :1], base_im[:, SUBLANES:]], axis=1)

    def outer(q_re, q_im, r_re, r_im):
        re = q_re[:, :, None] * r_re[:, None] - q_im[:, :, None] * r_im[:, None]
        im = q_re[:, :, None] * r_im[:, None] + q_im[:, :, None] * r_re[:, None]
        return jnp.concatenate([re, im], axis=-1).reshape(G, -1, 2 * P)

    cat = lambda a, b: jnp.concatenate([a, b], axis=-1)
    powers = jnp.concatenate([
        outer(hi_re[:, n_q - 1::-1], hi_im[:, n_q - 1::-1], lo_re[:, ::-1], lo_im[:, ::-1]),
        outer(hi_re, hi_im, lo_re, lo_im)], axis=1)
    bt_re = bb_re.transpose(0, 2, 1)
    bt_im = bb_im.transpose(0, 2, 1)
    d_rows = jnp.pad(jnp.repeat(d_skip.reshape(G, C // 2, 2), T, axis=-1), ((0, 0), (0, C // 2), (0, 0)))
    cp_re = jnp.concatenate([c_re[:, 0::2], c_re[:, 1::2]], axis=1)
    cp_im = jnp.concatenate([c_im[:, 0::2], c_im[:, 1::2]], axis=1)
    vecs = [None] * _V_COUNT
    vecs[_V_BR] = cat(bt_re, bt_re)
    vecs[_V_BI_NP] = cat(-bt_im, bt_im)
    vecs[_V_BI_PN] = cat(bt_im, -bt_im)
    vecs[_V_BI_PP] = cat(bt_im, bt_im)
    vecs[_V_CR_PN] = cat(c_re, -c_re)
    vecs[_V_CI_NN] = cat(-c_im, -c_im)
    vecs[_V_C_RE_NIM] = cat(cp_re, -cp_im)
    vecs[_V_C_NIM_NRE] = cat(-cp_im, -cp_re)
    vecs[_V_D] = d_rows
    return powers, jnp.stack(vecs, axis=1)


def _conv_tables(w_dw, b_dw, ln_g, ln_b):
    taps = jnp.pad(w_dw[::-1].T, ((0, 0), (0, LANES - CONV_K)))
    lanes = lambda v: jnp.broadcast_to(v[:, None], (CONV_WIDTH, LANES))
    rows = [None] * _PC_COUNT
    rows[_PC_TAPS], rows[_PC_BIAS] = taps, lanes(b_dw)
    rows[_PC_HALF_GAIN], rows[_PC_HALF_BETA] = lanes(0.5 * ln_g), lanes(0.5 * ln_b)
    return jnp.stack(rows, axis=1)


def kernel(x, ffn1_norm, ffn1_w_gate, ffn1_w_up, ffn1_w_down, mix_norm, w_in, s5_lam_re, s5_lam_im, s5_log_dt, s5_b_re, s5_b_im, s5_c_re, s5_c_im, s5_d, s5_w_glu, s5_b_glu, conv_w_dw, conv_b_dw, conv_ln_g, conv_ln_b, w_out, ffn2_norm, ffn2_w_gate, ffn2_w_up, ffn2_w_down, final_norm):
    batch, seq, d_model = x.shape
    assert ffn1_norm.shape[0] == 1 and d_model == D_MODEL
    l = 0
    row = lambda v: v.reshape(1, -1)

    x1, us, zs, wg2, wu2, wd2, wo = _ffn1_in_proj(
        x, row(ffn1_norm[l]), ffn1_w_gate[l].astype(BF16), ffn1_w_up[l].astype(BF16),
        ffn1_w_down[l].astype(BF16), row(mix_norm[l]), w_in[l].T.astype(BF16),
        (ffn2_w_gate[l], ffn2_w_up[l], ffn2_w_down[l], w_out[l]))

    tables = _s5_tables(
        s5_lam_re[l], s5_lam_im[l], s5_log_dt[l], s5_b_re[l], s5_b_im[l], s5_c_re[l], s5_c_im[l], s5_d[l])
    ys = _s5_mix(us, tables, batch=batch)

    yc = _conv_mix(zs, _conv_tables(conv_w_dw[l], conv_b_dw[l], conv_ln_g[l], conv_ln_b[l]), batch=batch)

    return _out_ffn2_norm(
        x1, ys, yc, s5_w_glu[l].T.astype(BF16), s5_b_glu[l].reshape(-1, 1), wo,
        row(ffn2_norm[l]), wg2, wu2, wd2, row(final_norm))
```

```python
import functools

import jax
import jax.numpy as jnp
from jax import lax
from jax.experimental import pallas as pl
from jax.experimental.pallas import tpu as pltpu

D_MODEL = 1024
D_FF = 2816
S5_WIDTH = 512
S5_GROUP_CH = 16
S5_GROUPS = S5_WIDTH // S5_GROUP_CH
S5_STATE = 64
CONV_WIDTH = 512
CONV_HEAD_DIM = 64
CONV_HEADS = CONV_WIDTH // CONV_HEAD_DIM
CONV_K = 31
IN_COLS = S5_WIDTH + 2 * CONV_WIDTH
EPS = 1e-6

LANES = 128
SUBLANES = 8
CHUNK = LANES
S5_T = 64
S5_HALVES = CHUNK // S5_T
S5_PAIRS = S5_GROUP_CH // 2
S5_LANES = S5_GROUP_CH * S5_T
S5_STEP_GROUPS = 2
TILE_SEQS = 8
ROW_TILE = TILE_SEQS * CHUNK
FF_SPLITS = ((0, 1536), (1536, 2816))
LN_ROWS = 64
VMEM_LIMIT = 60 * 1024 * 1024

BF16 = jnp.bfloat16
F32 = jnp.float32

(_V_BR,
 _V_BI_NP,
 _V_BI_PN,
 _V_BI_PP,
 _V_CR_PN,
 _V_CI_NN,
 _V_C_RE_NIM,
 _V_C_NIM_NRE,
 _V_D,
 _V_COUNT) = range(10)

_PC_TAPS, _PC_BIAS, _PC_HALF_GAIN, _PC_HALF_BETA, _PC_COUNT = range(5)

_NT = (((1,), (1,)), ((), ()))
_TN = (((0,), (0,)), ((), ()))


def _rms_norm(x, g):
    return x * lax.rsqrt(jnp.mean(x * x, axis=-1, keepdims=True) + EPS) * g


def _dot(a, b):
    return jnp.dot(a, b, preferred_element_type=F32)


def _rms_split(x, g):
    return (x * g).astype(BF16), lax.rsqrt(jnp.mean(x * x, axis=-1, keepdims=True) + EPS)


def _swiglu(h, row_scale, wg_ref, wu_ref, wd_ref):
    acc = None
    for lo, hi in FF_SPLITS:
        g = _dot(h, wg_ref[:, lo:hi]) * row_scale
        u = _dot(h, wu_ref[:, lo:hi]) * row_scale
        a = (g * jax.nn.sigmoid(g) * u).astype(BF16)
        part = _dot(a, wd_ref[lo:hi, :])
        acc = part if acc is None else acc + part
    return acc


def _store_slabs(slab_ref, val, first_seq):
    n_ch = slab_ref.shape[0]
    rows = slab_ref.reshape(n_ch * SUBLANES, LANES)
    for ct in range(n_ch // SUBLANES):
        for s in range(TILE_SEQS):
            rows[pl.ds(ct * SUBLANES * SUBLANES + first_seq + s, SUBLANES, stride=SUBLANES), :] = (
                val[ct * SUBLANES:(ct + 1) * SUBLANES, s * LANES:(s + 1) * LANES])


def _load_slabs(slab_ref, first_seq):
    n_ch = slab_ref.shape[0]
    rows = slab_ref.reshape(n_ch * SUBLANES, LANES)
    bands = []
    for ct in range(n_ch // SUBLANES):
        bands.append(jnp.concatenate(
            [rows[pl.ds(ct * SUBLANES * SUBLANES + first_seq + s, SUBLANES, stride=SUBLANES), :]
             for s in range(TILE_SEQS)], axis=1))
    return jnp.concatenate(bands, axis=0)


def _ffn1_in_proj_kernel(x_ref, g1_ref, wg_ref, wu_ref, wd_ref, gm_ref, win_t_ref, *rest):
    n_cast = (len(rest) - 3) // 2
    x1_ref, us_ref, zs_ref = rest[n_cast:n_cast + 3]
    first_seq = pl.program_id(1) * TILE_SEQS
    x = x_ref[...].reshape(ROW_TILE, D_MODEL)
    h, r = _rms_split(x, g1_ref[...])
    x1 = x + 0.5 * _swiglu(h, r, wg_ref, wu_ref, wd_ref)
    x1_ref[...] = x1.reshape(TILE_SEQS, CHUNK, D_MODEL)
    h2, r2 = _rms_split(x1, gm_ref[...])
    r2_lanes = jnp.transpose(jnp.broadcast_to(r2, (ROW_TILE, LANES)))[0:1, :]

    def in_proj_t(lo, hi):
        return lax.dot_general(win_t_ref[lo:hi, :], h2, _NT, preferred_element_type=F32) * r2_lanes

    gate = jax.nn.sigmoid(in_proj_t(S5_WIDTH + CONV_WIDTH, IN_COLS))
    _store_slabs(us_ref, in_proj_t(0, S5_WIDTH), first_seq)
    z_t = in_proj_t(S5_WIDTH, S5_WIDTH + CONV_WIDTH) * gate
    _store_slabs(zs_ref, z_t, first_seq)
    for src_ref, dst_ref in zip(rest[:n_cast], rest[n_cast + 3:]):
        dst_ref[...] = src_ref[...].astype(BF16)


def _s5_mix_kernel(us_ref, pow_ref, vec_ref, ys_ref, *scratch, batch):
    for g in range(S5_STEP_GROUPS):
        ch = pl.ds(g * S5_GROUP_CH, S5_GROUP_CH)
        _s5_group(us_ref.at[ch], pow_ref.at[g], vec_ref.at[g], ys_ref.at[ch], *scratch, batch=batch)


def _s5_group(us_ref, pow_ref, vec_ref, ys_ref,
              cb_ref, f_ref, upk_ref, lhs_ref, win_ref, wout_ref, sprev_ref, *, batch):
    T = S5_T
    n_slab_rows = us_ref.shape[1]
    n_slab_chunks = n_slab_rows // batch
    n_rows = S5_HALVES * n_slab_rows
    lane = lax.broadcasted_iota(jnp.int32, (1, LANES), 1)
    first_state_half = lane < S5_STATE
    first_time_half = lane < T
    swap = lambda v: pltpu.roll(v, S5_STATE, 1)
    pa_rev = pow_ref[0:T, :]
    pb_rev = swap(pa_rev)
    pa_fwd = pow_ref[T + 1:2 * T + 1, :]
    pb_fwd = swap(pa_fwd)
    for c in range(S5_GROUP_CH):
        rows = slice(c * T, (c + 1) * T)
        vec = lambda i: vec_ref[i, c:c + 1, :]
        win_ref[rows, 0:2 * S5_STATE] = (pa_rev * vec(_V_BR) + pb_rev * vec(_V_BI_NP)).astype(BF16)
        win_ref[rows, 2 * S5_STATE:] = (pb_rev * vec(_V_BR) + pa_rev * vec(_V_BI_PN)).astype(BF16)
        wout_ref[rows, :] = (pa_fwd * vec(_V_CR_PN) + pb_fwd * vec(_V_CI_NN)).astype(BF16)
        cb_ref[c * S5_GROUP_CH:(c + 1) * S5_GROUP_CH, :] = (
            vec_ref[_V_C_RE_NIM] * vec(_V_BR) + vec_ref[_V_C_NIM_NRE] * vec(_V_BI_PP))
    f = lax.dot_general(cb_ref[...], pow_ref[T:2 * T, :], _NT, precision=lax.Precision.HIGHEST,
                        preferred_element_type=F32)
    for c in range(S5_GROUP_CH):
        even, odd = c * S5_GROUP_CH, c * S5_GROUP_CH + S5_PAIRS
        f_ref[c * S5_PAIRS:(c + 1) * S5_PAIRS, :] = jnp.concatenate(
            [f[even:even + S5_PAIRS], f[odd:odd + S5_PAIRS]], axis=1)

    for p in range(S5_PAIRS):
        a, b = us_ref[2 * p], us_ref[2 * p + 1]
        halves = (jnp.where(first_time_half, a, pltpu.roll(b, T, 1)),
                  jnp.where(first_time_half, pltpu.roll(a, T, 1), b))
        for hh in range(S5_HALVES):
            upk_ref[:, hh * batch:(hh + 1) * batch, p * LANES:(p + 1) * LANES] = (
                halves[hh].reshape(n_slab_chunks, batch, LANES))
    lhs_ref[...] = upk_ref[...].reshape(n_rows, S5_LANES).astype(BF16)

    e = _dot(lhs_ref[...], win_ref[...])
    a_chunk = pow_ref[2 * T:2 * T + 1, :]
    a_sw = swap(a_chunk)
    a1 = jnp.where(first_state_half, a_chunk, a_sw)
    a2 = jnp.where(first_state_half, -a_sw, a_chunk)
    s = jnp.zeros((batch, 2 * S5_STATE), F32)
    s_sw = jnp.zeros((batch, 2 * S5_STATE), F32)
    for c in range(n_rows // batch):
        rows = slice(c * batch, (c + 1) * batch)
        sprev_ref[rows, :] = s
        e_c = e[rows, :]
        s, s_sw = (a1 * s + a2 * s_sw + e_c[:, :2 * S5_STATE],
                   a1 * s_sw - a2 * s + e_c[:, 2 * S5_STATE:])
    sprev = sprev_ref[...].astype(BF16)

    causal = ((lax.broadcasted_iota(jnp.int32, (T, LANES), 1) & (T - 1))
              >= lax.broadcasted_iota(jnp.int32, (T, LANES), 0))

    def toeplitz(ci, pair):
        f_row = jnp.broadcast_to(f_ref[ci * S5_PAIRS + pair:ci * S5_PAIRS + pair + 1, :], (T, LANES))
        return jnp.where(causal, pltpu.roll(f_row, 0, 1, stride=1, stride_axis=0), 0.0).astype(BF16)

    ci_per_tile = 2 * LANES // T
    for n in range(S5_PAIRS // 2):
        acc = lax.dot_general(sprev, wout_ref[2 * n * LANES:(2 * n + 2) * LANES, :], _NT,
                              preferred_element_type=F32)
        for kt in range(S5_GROUP_CH // ci_per_tile):
            w = jnp.concatenate(
                [jnp.concatenate([toeplitz(ci_per_tile * kt + r, 2 * n), toeplitz(ci_per_tile * kt + r, 2 * n + 1)],
                                 axis=1) for r in range(ci_per_tile)], axis=0)
            acc = acc + _dot(lhs_ref[:, 2 * kt * LANES:(2 * kt + 2) * LANES], w)
        for q in range(2):
            pair = 2 * n + q
            u_pk = upk_ref[:, :, pair * LANES:(pair + 1) * LANES].reshape(n_rows, LANES)
            y = acc[:, q * LANES:(q + 1) * LANES] + vec_ref[_V_D, pair:pair + 1, :] * u_pk
            y = jax.nn.gelu(y, approximate=True).reshape(n_slab_chunks, S5_HALVES * batch, LANES)
            y0 = y[:, 0:batch, :].reshape(n_slab_rows, LANES)
            y1 = y[:, batch:2 * batch, :].reshape(n_slab_rows, LANES)
            ys_ref[2 * pair] = jnp.where(first_time_half, y0, pltpu.roll(y1, T, 1))
            ys_ref[2 * pair + 1] = jnp.where(first_time_half, pltpu.roll(y0, T, 1), y1)


def _conv_mix_kernel(zs_ref, pc_ref, o_ref, *, batch):
    n_rows = zs_ref.shape[1]
    row = lax.broadcasted_iota(jnp.int32, (CHUNK, CHUNK), 0)
    col = lax.broadcasted_iota(jnp.int32, (CHUNK, CHUNK), 1)
    lag = row - col
    band_cur = (lag >= -(CONV_K - 1)) & (lag <= 0)
    band_prev = lag >= CHUNK - (CONV_K - 1)

    def conv_channel(ch, carry):
        z = zs_ref[ch]
        z_prev = jnp.concatenate([jnp.zeros((batch, CHUNK), F32), z[:n_rows - batch]], axis=0)
        lhs = jnp.concatenate([z_prev, z], axis=1).astype(BF16)
        taps = jnp.broadcast_to(pc_ref[ch, _PC_TAPS:_PC_TAPS + 1, :], (CHUNK, CHUNK))
        diag = pltpu.roll(taps, 0, 1, stride=1, stride_axis=0)
        toe = jnp.concatenate([jnp.where(band_prev, diag, 0.0), jnp.where(band_cur, diag, 0.0)],
                              axis=0).astype(BF16)
        o_ref[ch] = _dot(lhs, toe) + pc_ref[ch, _PC_BIAS:_PC_BIAS + 1, :]
        return carry

    lax.fori_loop(0, CONV_HEAD_DIM, conv_channel, 0, unroll=32)

    inv_n = 1.0 / CONV_HEAD_DIM
    for r0 in range(0, n_rows, LN_ROWS):
        rows = pl.ds(r0, LN_ROWS)
        zero = jnp.zeros((LN_ROWS, CHUNK), F32)
        mu = lax.fori_loop(0, CONV_HEAD_DIM, lambda ch, acc: acc + o_ref[ch, rows, :], zero,
                           unroll=4) * inv_n
        var = lax.fori_loop(0, CONV_HEAD_DIM, lambda ch, acc: acc + jnp.square(o_ref[ch, rows, :] - mu),
                            zero, unroll=4) * inv_n
        rstd = lax.rsqrt(var + EPS)

        def norm_channel(ch, carry):
            half = ((o_ref[ch, rows, :] - mu) * rstd * pc_ref[ch, _PC_HALF_GAIN:_PC_HALF_GAIN + 1, :]
                    + pc_ref[ch, _PC_HALF_BETA:_PC_HALF_BETA + 1, :])
            o_ref[ch, rows, :] = half + half * jnp.tanh(half)
            return carry

        lax.fori_loop(0, CONV_HEAD_DIM, norm_channel, 0, unroll=4)


def _out_ffn2_norm_kernel(x1_ref, ys_ref, yc_ref, wglu_t_ref, bglu_ref, wo_ref,
                          g2_ref, wg_ref, wu_ref, wd_ref, gf_ref, o_ref):
    first_seq = pl.program_id(1) * TILE_SEQS
    ys_t = _load_slabs(ys_ref, first_seq)
    yc_t = _load_slabs(yc_ref, first_seq)
    x2 = x1_ref[...].reshape(ROW_TILE, D_MODEL) + lax.dot_general(
        yc_t.astype(BF16), wo_ref[S5_WIDTH:, :], _TN, preferred_element_type=F32)
    gate_t = jax.nn.sigmoid(_dot(wglu_t_ref[...], ys_t.astype(BF16)) + bglu_ref[...])
    x2 = x2 + lax.dot_general((ys_t * gate_t).astype(BF16), wo_ref[0:S5_WIDTH, :], _TN,
                              preferred_element_type=F32)
    h, r = _rms_split(x2, g2_ref[...])
    x3 = x2 + 0.5 * _swiglu(h, r, wg_ref, wu_ref, wd_ref)
    o_ref[...] = _rms_norm(x3, gf_ref[...]).reshape(TILE_SEQS, CHUNK, D_MODEL)


def _resident(shape):
    nd = len(shape)
    return pl.BlockSpec(shape, lambda *_: (0,) * nd, pipeline_mode=pl.Buffered(1))


def _tile_spec():
    return pl.BlockSpec((TILE_SEQS, CHUNK, D_MODEL), lambda c, h: (h, c, 0))


def _slab_spec(n_ch):
    return pl.BlockSpec((n_ch, SUBLANES, LANES), lambda c, h: (0, c, 0))


_DENSE_PARAMS = pltpu.CompilerParams(dimension_semantics=("arbitrary", "arbitrary"),
                                     vmem_limit_bytes=VMEM_LIMIT)
_MIX_PARAMS = pltpu.CompilerParams(dimension_semantics=("arbitrary",), vmem_limit_bytes=VMEM_LIMIT)


def _cast_spec(w, n_steps):
    n_blocks = n_steps
    while w.shape[0] % n_blocks or (w.shape[0] // n_blocks) % (2 * SUBLANES):
        n_blocks //= 2
    steps_per_block = n_steps // n_blocks
    return pl.BlockSpec((w.shape[0] // n_blocks, w.shape[1]), lambda c, h: (c // steps_per_block, 0))


def _ffn1_in_proj(x, g1, wg, wu, wd, gm, win_t, later_weights):
    batch, seq, _ = x.shape
    assert batch == SUBLANES and seq % CHUNK == 0
    n_rows = (seq // CHUNK) * batch
    cast_specs = [_cast_spec(w, seq // CHUNK) for w in later_weights]
    return pl.pallas_call(
        _ffn1_in_proj_kernel,
        grid=(seq // CHUNK, batch // TILE_SEQS),
        in_specs=[
            _tile_spec(),
            _resident((1, D_MODEL)),
            _resident((D_MODEL, D_FF)),
            _resident((D_MODEL, D_FF)),
            _resident((D_FF, D_MODEL)),
            _resident((1, D_MODEL)),
            _resident((IN_COLS, D_MODEL)),
        ] + cast_specs,
        out_specs=[_tile_spec(), _slab_spec(S5_WIDTH), _slab_spec(CONV_WIDTH)] + cast_specs,
        out_shape=[
            jax.ShapeDtypeStruct((batch, seq, D_MODEL), F32),
            jax.ShapeDtypeStruct((S5_WIDTH, n_rows, LANES), F32),
            jax.ShapeDtypeStruct((CONV_WIDTH, n_rows, LANES), F32),
        ] + [jax.ShapeDtypeStruct(w.shape, BF16) for w in later_weights],
        compiler_params=_DENSE_PARAMS,
        name="ffn1_in_proj",
    )(x, g1, wg, wu, wd, gm, win_t, *later_weights)


def _s5_mix(us, tables, *, batch):
    n_rows = us.shape[1]
    slab = pl.BlockSpec((S5_STEP_GROUPS * S5_GROUP_CH, n_rows, LANES), lambda g: (g, 0, 0))

    def per_group(t):
        nd = t.ndim - 1
        return pl.BlockSpec((S5_STEP_GROUPS,) + t.shape[1:], lambda g: (g,) + (0,) * nd)

    return pl.pallas_call(
        functools.partial(_s5_mix_kernel, batch=batch),
        grid=(S5_GROUPS // S5_STEP_GROUPS,),
        in_specs=[slab] + [per_group(t) for t in tables],
        out_specs=slab,
        out_shape=jax.ShapeDtypeStruct((S5_WIDTH, n_rows, LANES), F32),
        scratch_shapes=[pltpu.VMEM((S5_GROUP_CH * S5_GROUP_CH, 2 * S5_STATE), F32),
                        pltpu.VMEM((S5_GROUP_CH * S5_PAIRS, LANES), F32),
                        pltpu.VMEM((n_rows // batch, S5_HALVES * batch, S5_LANES), F32),
                        pltpu.VMEM((S5_HALVES * n_rows, S5_LANES), BF16),
                        pltpu.VMEM((S5_LANES, 4 * S5_STATE), BF16),
                        pltpu.VMEM((S5_LANES, 2 * S5_STATE), BF16),
                        pltpu.VMEM((S5_HALVES * n_rows, 2 * S5_STATE), F32)],
        compiler_params=_MIX_PARAMS,
        name="s5_mix",
    )(us, *tables)


def _conv_mix(zs, per_channel, *, batch):
    n_rows = zs.shape[1]
    slab = pl.BlockSpec((CONV_HEAD_DIM, n_rows, LANES), lambda h: (h, 0, 0))
    per_ch = pl.BlockSpec((CONV_HEAD_DIM, _PC_COUNT, LANES), lambda h: (h, 0, 0))
    return pl.pallas_call(
        functools.partial(_conv_mix_kernel, batch=batch),
        grid=(CONV_HEADS,),
        in_specs=[slab, per_ch],
        out_specs=slab,
        out_shape=jax.ShapeDtypeStruct((CONV_WIDTH, n_rows, LANES), F32),
        compiler_params=_MIX_PARAMS,
        name="conv_mix",
    )(zs, per_channel)


def _out_ffn2_norm(x1, ys, yc, wglu_t, bglu, wo, g2, wg, wu, wd, gf):
    batch, seq, _ = x1.shape
    return pl.pallas_call(
        _out_ffn2_norm_kernel,
        grid=(seq // CHUNK, batch // TILE_SEQS),
        in_specs=[
            _tile_spec(),
            _slab_spec(S5_WIDTH),
            _slab_spec(CONV_WIDTH),
            _resident((S5_WIDTH, S5_WIDTH)),
            _resident((S5_WIDTH, 1)),
            _resident((S5_WIDTH + CONV_WIDTH, D_MODEL)),
            _resident((1, D_MODEL)),
            _resident((D_MODEL, D_FF)),
            _resident((D_MODEL, D_FF)),
            _resident((D_FF, D_MODEL)),
            _resident((1, D_MODEL)),
        ],
        out_specs=_tile_spec(),
        out_shape=jax.ShapeDtypeStruct((batch, seq, D_MODEL), F32),
        compiler_params=_DENSE_PARAMS,
        name="out_ffn2_norm",
    )(x1, ys, yc, wglu_t, bglu, wo, g2, wg, wu, wd, gf)


def _s5_tables(lam_re, lam_im, log_dt, b_re, b_im, c_re, c_im, d_skip):
    T, C, P, G = S5_T, S5_GROUP_CH, S5_STATE, S5_GROUPS
    dt = jnp.exp(log_dt)[:, None]
    mag = jnp.exp(lam_re * dt)
    abar_re = mag * jnp.cos(lam_im * dt)
    abar_im = mag * jnp.sin(lam_im * dt)
    den = lam_re * lam_re + lam_im * lam_im
    num_re = abar_re - 1.0
    num_im = abar_im
    f_re = ((num_re * lam_re + num_im * lam_im) / den)[..., None]
    f_im = ((num_im * lam_re - num_re * lam_im) / den)[..., None]
    bb_re = f_re * b_re - f_im * b_im
    bb_im = f_re * b_im + f_im * b_re
    n_q = T // SUBLANES
    expo = jnp.concatenate([jnp.arange(SUBLANES), SUBLANES * jnp.arange(1, n_q + 1)]).astype(F32)[None, :, None]
    bmag = jnp.exp(expo * (lam_re * dt)[:, None, :])
    bang = expo * (lam_im * dt)[:, None, :]
    base_re, base_im = bmag * jnp.cos(bang), bmag * jnp.sin(bang)
    lo_re, lo_im = base_re[:, :SUBLANES], base_im[:, :SUBLANES]
    hi_re = jnp.concatenate([base_re[:, :1], base_re[:, SUBLANES:]], axis=1)
    hi_im = jnp.concatenate([base_im[:, :1], base_im[:, SUBLANES:]], axis=1)

    def outer(q_re, q_im, r_re, r_im):
        re = q_re[:, :, None] * r_re[:, None] - q_im[:, :, None] * r_im[:, None]
        im = q_re[:, :, None] * r_im[:, None] + q_im[:, :, None] * r_re[:, None]
        return jnp.concatenate([re, im], axis=-1).reshape(G, -1, 2 * P)

    cat = lambda a, b: jnp.concatenate([a, b], axis=-1)
    powers = jnp.concatenate([
        outer(hi_re[:, n_q - 1::-1], hi_im[:, n_q - 1::-1], lo_re[:, ::-1], lo_im[:, ::-1]),
        outer(hi_re, hi_im, lo_re, lo_im)], axis=1)
    bt_re = bb_re.transpose(0, 2, 1)
    bt_im = bb_im.transpose(0, 2, 1)
    d_rows = jnp.pad(jnp.repeat(d_skip.reshape(G, C // 2, 2), T, axis=-1), ((0, 0), (0, C // 2), (0, 0)))
    cp_re = jnp.concatenate([c_re[:, 0::2], c_re[:, 1::2]], axis=1)
    cp_im = jnp.concatenate([c_im[:, 0::2], c_im[:, 1::2]], axis=1)
    vecs = [None] * _V_COUNT
    vecs[_V_BR] = cat(bt_re, bt_re)
    vecs[_V_BI_NP] = cat(-bt_im, bt_im)
    vecs[_V_BI_PN] = cat(bt_im, -bt_im)
    vecs[_V_BI_PP] = cat(bt_im, bt_im)
    vecs[_V_CR_PN] = cat(c_re, -c_re)
    vecs[_V_CI_NN] = cat(-c_im, -c_im)
    vecs[_V_C_RE_NIM] = cat(cp_re, -cp_im)
    vecs[_V_C_NIM_NRE] = cat(-cp_im, -cp_re)
    vecs[_V_D] = d_rows
    return powers, jnp.stack(vecs, axis=1)


def _conv_tables(w_dw, b_dw, ln_g, ln_b):
    taps = jnp.pad(w_dw[::-1].T, ((0, 0), (0, LANES - CONV_K)))
    lanes = lambda v: jnp.broadcast_to(v[:, None], (CONV_WIDTH, LANES))
    rows = [None] * _PC_COUNT
    rows[_PC_TAPS], rows[_PC_BIAS] = taps, lanes(b_dw)
    rows[_PC_HALF_GAIN], rows[_PC_HALF_BETA] = lanes(0.5 * ln_g), lanes(0.5 * ln_b)
    return jnp.stack(rows, axis=1)


def kernel(x, ffn1_norm, ffn1_w_gate, ffn1_w_up, ffn1_w_down, mix_norm, w_in, s5_lam_re, s5_lam_im, s5_log_dt, s5_b_re, s5_b_im, s5_c_re, s5_c_im, s5_d, s5_w_glu, s5_b_glu, conv_w_dw, conv_b_dw, conv_ln_g, conv_ln_b, w_out, ffn2_norm, ffn2_w_gate, ffn2_w_up, ffn2_w_down, final_norm):
    batch, seq, d_model = x.shape
    assert ffn1_norm.shape[0] == 1 and d_model == D_MODEL
    l = 0
    row = lambda v: v.reshape(1, -1)

    x1, us, zs, wg2, wu2, wd2, wo = _ffn1_in_proj(
        x, row(ffn1_norm[l]), ffn1_w_gate[l].astype(BF16), ffn1_w_up[l].astype(BF16),
        ffn1_w_down[l].astype(BF16), row(mix_norm[l]), w_in[l].T.astype(BF16),
        (ffn2_w_gate[l], ffn2_w_up[l], ffn2_w_down[l], w_out[l]))

    tables = _s5_tables(
        s5_lam_re[l], s5_lam_im[l], s5_log_dt[l], s5_b_re[l], s5_b_im[l], s5_c_re[l], s5_c_im[l], s5_d[l])
    ys = _s5_mix(us, tables, batch=batch)

    yc = _conv_mix(zs, _conv_tables(conv_w_dw[l], conv_b_dw[l], conv_ln_g[l], conv_ln_b[l]), batch=batch)

    return _out_ffn2_norm(
        x1, ys, yc, s5_w_glu[l].T.astype(BF16), s5_b_glu[l].reshape(-1, 1), wo,
        row(ffn2_norm[l]), wg2, wu2, wd2, row(final_norm))
```

```python
import functools

import jax
import jax.numpy as jnp
from jax import lax
from jax.experimental import pallas as pl
from jax.experimental.pallas import tpu as pltpu

D_MODEL = 1024
D_FF = 2816
S5_WIDTH = 512
S5_GROUP_CH = 16
S5_GROUPS = S5_WIDTH // S5_GROUP_CH
S5_STATE = 64
CONV_WIDTH = 512
CONV_HEAD_DIM = 64
CONV_HEADS = CONV_WIDTH // CONV_HEAD_DIM
CONV_K = 31
IN_COLS = S5_WIDTH + 2 * CONV_WIDTH
EPS = 1e-6

LANES = 128
SUBLANES = 8
CHUNK = LANES
S5_T = 64
S5_HALVES = CHUNK // S5_T
S5_PAIRS = S5_GROUP_CH // 2
S5_LANES = S5_GROUP_CH * S5_T
S5_STEP_GROUPS = 2
TILE_SEQS = 8
ROW_TILE = TILE_SEQS * CHUNK
FF_SPLITS = ((0, 1536), (1536, 2816))
LN_ROWS = 64
LN_UNROLL = 16
VMEM_LIMIT = 60 * 1024 * 1024

BF16 = jnp.bfloat16
F32 = jnp.float32

(_V_BR,
 _V_BI_NP,
 _V_BI_PN,
 _V_BI_PP,
 _V_CR_PN,
 _V_CI_NN,
 _V_C_RE_NIM,
 _V_C_NIM_NRE,
 _V_D,
 _V_COUNT) = range(10)

_PC_TAPS, _PC_BIAS, _PC_HALF_GAIN, _PC_HALF_BETA, _PC_COUNT = range(5)

_NT = (((1,), (1,)), ((), ()))
_TN = (((0,), (0,)), ((), ()))


def _rms_norm(x, g):
    return x * lax.rsqrt(jnp.mean(x * x, axis=-1, keepdims=True) + EPS) * g


def _dot(a, b):
    return jnp.dot(a, b, preferred_element_type=F32)


def _rms_split(x, g):
    return (x * g).astype(BF16), lax.rsqrt(jnp.mean(x * x, axis=-1, keepdims=True) + EPS)


def _swiglu(h, row_scale, wg_ref, wu_ref, wd_ref):
    acc = None
    for lo, hi in FF_SPLITS:
        g = _dot(h, wg_ref[:, lo:hi]) * row_scale
        u = _dot(h, wu_ref[:, lo:hi]) * row_scale
        a = (g * jax.nn.sigmoid(g) * u).astype(BF16)
        part = _dot(a, wd_ref[lo:hi, :])
        acc = part if acc is None else acc + part
    return acc


def _store_slabs(slab_ref, val, first_seq):
    n_ch = slab_ref.shape[0]
    rows = slab_ref.reshape(n_ch * SUBLANES, LANES)
    for ct in range(n_ch // SUBLANES):
        for s in range(TILE_SEQS):
            rows[pl.ds(ct * SUBLANES * SUBLANES + first_seq + s, SUBLANES, stride=SUBLANES), :] = (
                val[ct * SUBLANES:(ct + 1) * SUBLANES, s * LANES:(s + 1) * LANES])


def _load_slabs(slab_ref, first_seq):
    n_ch = slab_ref.shape[0]
    rows = slab_ref.reshape(n_ch * SUBLANES, LANES)
    bands = []
    for ct in range(n_ch // SUBLANES):
        bands.append(jnp.concatenate(
            [rows[pl.ds(ct * SUBLANES * SUBLANES + first_seq + s, SUBLANES, stride=SUBLANES), :]
             for s in range(TILE_SEQS)], axis=1))
    return jnp.concatenate(bands, axis=0)


def _ffn1_in_proj_kernel(x_ref, g1_ref, wg_ref, wu_ref, wd_ref, gm_ref, win_t_ref, *rest):
    n_cast = (len(rest) - 3) // 2
    x1_ref, us_ref, zs_ref = rest[n_cast:n_cast + 3]
    first_seq = pl.program_id(1) * TILE_SEQS
    x = x_ref[...].reshape(ROW_TILE, D_MODEL)
    h, r = _rms_split(x, g1_ref[...])
    x1 = x + 0.5 * _swiglu(h, r, wg_ref, wu_ref, wd_ref)
    x1_ref[...] = x1.reshape(TILE_SEQS, CHUNK, D_MODEL)
    h2, r2 = _rms_split(x1, gm_ref[...])
    r2_lanes = jnp.transpose(jnp.broadcast_to(r2, (ROW_TILE, LANES)))[0:1, :]

    def in_proj_t(lo, hi):
        return lax.dot_general(win_t_ref[lo:hi, :], h2, _NT, preferred_element_type=F32) * r2_lanes

    gate = jax.nn.sigmoid(in_proj_t(S5_WIDTH + CONV_WIDTH, IN_COLS))
    _store_slabs(us_ref, in_proj_t(0, S5_WIDTH), first_seq)
    z_t = in_proj_t(S5_WIDTH, S5_WIDTH + CONV_WIDTH) * gate
    _store_slabs(zs_ref, z_t, first_seq)
    for src_ref, dst_ref in zip(rest[:n_cast], rest[n_cast + 3:]):
        dst_ref[...] = src_ref[...].astype(BF16)


def _s5_mix_kernel(us_ref, pow_ref, vec_ref, ys_ref, *scratch, batch):
    for g in range(S5_STEP_GROUPS):
        ch = pl.ds(g * S5_GROUP_CH, S5_GROUP_CH)
        _s5_group(us_ref.at[ch], pow_ref.at[g], vec_ref.at[g], ys_ref.at[ch], *scratch, batch=batch)


def _s5_group(us_ref, pow_ref, vec_ref, ys_ref,
              cb_ref, f_ref, upk_ref, lhs_ref, win_ref, wout_ref, sprev_ref, *, batch):
    T = S5_T
    n_slab_rows = us_ref.shape[1]
    n_slab_chunks = n_slab_rows // batch
    n_rows = S5_HALVES * n_slab_rows
    lane = lax.broadcasted_iota(jnp.int32, (1, LANES), 1)
    first_state_half = lane < S5_STATE
    first_time_half = lane < T
    swap = lambda v: pltpu.roll(v, S5_STATE, 1)
    pa_rev = pow_ref[0:T, :]
    pb_rev = swap(pa_rev)
    pa_fwd = pow_ref[T + 1:2 * T + 1, :]
    pb_fwd = swap(pa_fwd)
    for c in range(S5_GROUP_CH):
        rows = slice(c * T, (c + 1) * T)
        vec = lambda i: vec_ref[i, c:c + 1, :]
        win_ref[rows, 0:2 * S5_STATE] = (pa_rev * vec(_V_BR) + pb_rev * vec(_V_BI_NP)).astype(BF16)
        win_ref[rows, 2 * S5_STATE:] = (pb_rev * vec(_V_BR) + pa_rev * vec(_V_BI_PN)).astype(BF16)
        wout_ref[rows, :] = (pa_fwd * vec(_V_CR_PN) + pb_fwd * vec(_V_CI_NN)).astype(BF16)
        cb_ref[c * S5_GROUP_CH:(c + 1) * S5_GROUP_CH, :] = (
            vec_ref[_V_C_RE_NIM] * vec(_V_BR) + vec_ref[_V_C_NIM_NRE] * vec(_V_BI_PP))
    f = lax.dot_general(cb_ref[...], pow_ref[T:2 * T, :], _NT, precision=lax.Precision.HIGHEST,
                        preferred_element_type=F32)
    for c in range(S5_GROUP_CH):
        even, odd = c * S5_GROUP_CH, c * S5_GROUP_CH + S5_PAIRS
        f_ref[c * S5_PAIRS:(c + 1) * S5_PAIRS, :] = jnp.concatenate(
            [f[even:even + S5_PAIRS], f[odd:odd + S5_PAIRS]], axis=1)

    for p in range(S5_PAIRS):
        a, b = us_ref[2 * p], us_ref[2 * p + 1]
        halves = (jnp.where(first_time_half, a, pltpu.roll(b, T, 1)),
                  jnp.where(first_time_half, pltpu.roll(a, T, 1), b))
        for hh in range(S5_HALVES):
            upk_ref[:, hh * batch:(hh + 1) * batch, p * LANES:(p + 1) * LANES] = (
                halves[hh].reshape(n_slab_chunks, batch, LANES))
    lhs_ref[...] = upk_ref[...].reshape(n_rows, S5_LANES).astype(BF16)

    e = _dot(lhs_ref[...], win_ref[...])
    a_chunk = pow_ref[2 * T:2 * T + 1, :]
    a_sw = swap(a_chunk)
    a1 = jnp.where(first_state_half, a_chunk, a_sw)
    a2 = jnp.where(first_state_half, -a_sw, a_chunk)
    s = jnp.zeros((batch, 2 * S5_STATE), F32)
    s_sw = jnp.zeros((batch, 2 * S5_STATE), F32)
    for c in range(n_rows // batch):
        rows = slice(c * batch, (c + 1) * batch)
        sprev_ref[rows, :] = s
        e_c = e[rows, :]
        s, s_sw = (a1 * s + a2 * s_sw + e_c[:, :2 * S5_STATE],
                   a1 * s_sw - a2 * s + e_c[:, 2 * S5_STATE:])
    sprev = sprev_ref[...].astype(BF16)

    causal = ((lax.broadcasted_iota(jnp.int32, (T, LANES), 1) & (T - 1))
              >= lax.broadcasted_iota(jnp.int32, (T, LANES), 0))

    def toeplitz(ci, pair):
        f_row = jnp.broadcast_to(f_ref[ci * S5_PAIRS + pair:ci * S5_PAIRS + pair + 1, :], (T, LANES))
        return jnp.where(causal, pltpu.roll(f_row, 0, 1, stride=1, stride_axis=0), 0.0).astype(BF16)

    ci_per_tile = 2 * LANES // T
    for n in range(S5_PAIRS // 2):
        acc = lax.dot_general(sprev, wout_ref[2 * n * LANES:(2 * n + 2) * LANES, :], _NT,
                              preferred_element_type=F32)
        for kt in range(S5_GROUP_CH // ci_per_tile):
            w = jnp.concatenate(
                [jnp.concatenate([toeplitz(ci_per_tile * kt + r, 2 * n), toeplitz(ci_per_tile * kt + r, 2 * n + 1)],
                                 axis=1) for r in range(ci_per_tile)], axis=0)
            acc = acc + _dot(lhs_ref[:, 2 * kt * LANES:(2 * kt + 2) * LANES], w)
        for q in range(2):
            pair = 2 * n + q
            u_pk = upk_ref[:, :, pair * LANES:(pair + 1) * LANES].reshape(n_rows, LANES)
            y = acc[:, q * LANES:(q + 1) * LANES] + vec_ref[_V_D, pair:pair + 1, :] * u_pk
            y = jax.nn.gelu(y, approximate=True).reshape(n_slab_chunks, S5_HALVES * batch, LANES)
            y0 = y[:, 0:batch, :].reshape(n_slab_rows, LANES)
            y1 = y[:, batch:2 * batch, :].reshape(n_slab_rows, LANES)
            ys_ref[2 * pair] = jnp.where(first_time_half, y0, pltpu.roll(y1, T, 1))
            ys_ref[2 * pair + 1] = jnp.where(first_time_half, pltpu.roll(y0, T, 1), y1)


def _conv_mix_kernel(zs_ref, pc_ref, o_ref, *, batch):
    n_rows = zs_ref.shape[1]
    row = lax.broadcasted_iota(jnp.int32, (CHUNK, CHUNK), 0)
    col = lax.broadcasted_iota(jnp.int32, (CHUNK, CHUNK), 1)
    lag = row - col
    band_cur = (lag >= -(CONV_K - 1)) & (lag <= 0)
    band_prev = lag >= CHUNK - (CONV_K - 1)

    def conv_channel(ch, carry):
        z = zs_ref[ch]
        z_prev = jnp.concatenate([jnp.zeros((batch, CHUNK), F32), z[:n_rows - batch]], axis=0)
        lhs = jnp.concatenate([z_prev, z], axis=1).astype(BF16)
        taps = jnp.broadcast_to(pc_ref[ch, _PC_TAPS:_PC_TAPS + 1, :], (CHUNK, CHUNK))
        diag = pltpu.roll(taps, 0, 1, stride=1, stride_axis=0)
        toe = jnp.concatenate([jnp.where(band_prev, diag, 0.0), jnp.where(band_cur, diag, 0.0)],
                              axis=0).astype(BF16)
        o_ref[ch] = _dot(lhs, toe) + pc_ref[ch, _PC_BIAS:_PC_BIAS + 1, :]
        return carry

    lax.fori_loop(0, CONV_HEAD_DIM, conv_channel, 0, unroll=32)

    inv_n = 1.0 / CONV_HEAD_DIM
    for r0 in range(0, n_rows, LN_ROWS):
        rows = pl.ds(r0, LN_ROWS)
        zero = jnp.zeros((LN_ROWS, CHUNK), F32)
        mu = lax.fori_loop(0, CONV_HEAD_DIM, lambda ch, acc: acc + o_ref[ch, rows, :], zero,
                           unroll=LN_UNROLL) * inv_n
        var = lax.fori_loop(0, CONV_HEAD_DIM, lambda ch, acc: acc + jnp.square(o_ref[ch, rows, :] - mu),
                            zero, unroll=LN_UNROLL) * inv_n
        rstd = lax.rsqrt(var + EPS)

        def norm_channel(ch, carry):
            half = ((o_ref[ch, rows, :] - mu) * rstd * pc_ref[ch, _PC_HALF_GAIN:_PC_HALF_GAIN + 1, :]
                    + pc_ref[ch, _PC_HALF_BETA:_PC_HALF_BETA + 1, :])
            o_ref[ch, rows, :] = half + half * jnp.tanh(half)
            return carry

        lax.fori_loop(0, CONV_HEAD_DIM, norm_channel, 0, unroll=LN_UNROLL)


def _out_ffn2_norm_kernel(x1_ref, ys_ref, yc_ref, wglu_t_ref, bglu_ref, wo_ref,
                          g2_ref, wg_ref, wu_ref, wd_ref, gf_ref, o_ref):
    first_seq = pl.program_id(1) * TILE_SEQS
    ys_t = _load_slabs(ys_ref, first_seq)
    yc_t = _load_slabs(yc_ref, first_seq)
    x2 = x1_ref[...].reshape(ROW_TILE, D_MODEL) + lax.dot_general(
        yc_t.astype(BF16), wo_ref[S5_WIDTH:, :], _TN, preferred_element_type=F32)
    gate_t = jax.nn.sigmoid(_dot(wglu_t_ref[...], ys_t.astype(BF16)) + bglu_ref[...])
    x2 = x2 + lax.dot_general((ys_t * gate_t).astype(BF16), wo_ref[0:S5_WIDTH, :], _TN,
                              preferred_element_type=F32)
    h, r = _rms_split(x2, g2_ref[...])
    x3 = x2 + 0.5 * _swiglu(h, r, wg_ref, wu_ref, wd_ref)
    o_ref[...] = _rms_norm(x3, gf_ref[...]).reshape(TILE_SEQS, CHUNK, D_MODEL)


def _resident(shape):
    nd = len(shape)
    return pl.BlockSpec(shape, lambda *_: (0,) * nd, pipeline_mode=pl.Buffered(1))


def _tile_spec():
    return pl.BlockSpec((TILE_SEQS, CHUNK, D_MODEL), lambda c, h: (h, c, 0))


def _slab_spec(n_ch):
    return pl.BlockSpec((n_ch, SUBLANES, LANES), lambda c, h: (0, c, 0))


_DENSE_PARAMS = pltpu.CompilerParams(dimension_semantics=("arbitrary", "arbitrary"),
                                     vmem_limit_bytes=VMEM_LIMIT)
_MIX_PARAMS = pltpu.CompilerParams(dimension_semantics=("arbitrary",), vmem_limit_bytes=VMEM_LIMIT)


def _cast_spec(w, n_steps):
    n_blocks = n_steps
    while w.shape[0] % n_blocks or (w.shape[0] // n_blocks) % (2 * SUBLANES):
        n_blocks //= 2
    steps_per_block = n_steps // n_blocks
    return pl.BlockSpec((w.shape[0] // n_blocks, w.shape[1]), lambda c, h: (c // steps_per_block, 0))


def _ffn1_in_proj(x, g1, wg, wu, wd, gm, win_t, later_weights):
    batch, seq, _ = x.shape
    assert batch == SUBLANES and seq % CHUNK == 0
    n_rows = (seq // CHUNK) * batch
    cast_specs = [_cast_spec(w, seq // CHUNK) for w in later_weights]
    return pl.pallas_call(
        _ffn1_in_proj_kernel,
        grid=(seq // CHUNK, batch // TILE_SEQS),
        in_specs=[
            _tile_spec(),
            _resident((1, D_MODEL)),
            _resident((D_MODEL, D_FF)),
            _resident((D_MODEL, D_FF)),
            _resident((D_FF, D_MODEL)),
            _resident((1, D_MODEL)),
            _resident((IN_COLS, D_MODEL)),
        ] + cast_specs,
        out_specs=[_tile_spec(), _slab_spec(S5_WIDTH), _slab_spec(CONV_WIDTH)] + cast_specs,
        out_shape=[
            jax.ShapeDtypeStruct((batch, seq, D_MODEL), F32),
            jax.ShapeDtypeStruct((S5_WIDTH, n_rows, LANES), F32),
            jax.ShapeDtypeStruct((CONV_WIDTH, n_rows, LANES), F32),
        ] + [jax.ShapeDtypeStruct(w.shape, BF16) for w in later_weights],
        compiler_params=_DENSE_PARAMS,
        name="ffn1_in_proj",
    )(x, g1, wg, wu, wd, gm, win_t, *later_weights)


def _s5_mix(us, tables, *, batch):
    n_rows = us.shape[1]
    slab = pl.BlockSpec((S5_STEP_GROUPS * S5_GROUP_CH, n_rows, LANES), lambda g: (g, 0, 0))

    def per_group(t):
        nd = t.ndim - 1
        return pl.BlockSpec((S5_STEP_GROUPS,) + t.shape[1:], lambda g: (g,) + (0,) * nd)

    return pl.pallas_call(
        functools.partial(_s5_mix_kernel, batch=batch),
        grid=(S5_GROUPS // S5_STEP_GROUPS,),
        in_specs=[slab] + [per_group(t) for t in tables],
        out_specs=slab,
        out_shape=jax.ShapeDtypeStruct((S5_WIDTH, n_rows, LANES), F32),
        scratch_shapes=[pltpu.VMEM((S5_GROUP_CH * S5_GROUP_CH, 2 * S5_STATE), F32),
                        pltpu.VMEM((S5_GROUP_CH * S5_PAIRS, LANES), F32),
                        pltpu.VMEM((n_rows // batch, S5_HALVES * batch, S5_LANES), F32),
                        pltpu.VMEM((S5_HALVES * n_rows, S5_LANES), BF16),
                        pltpu.VMEM((S5_LANES, 4 * S5_STATE), BF16),
                        pltpu.VMEM((S5_LANES, 2 * S5_STATE), BF16),
                        pltpu.VMEM((S5_HALVES * n_rows, 2 * S5_STATE), F32)],
        compiler_params=_MIX_PARAMS,
        name="s5_mix",
    )(us, *tables)


def _conv_mix(zs, per_channel, *, batch):
    n_rows = zs.shape[1]
    slab = pl.BlockSpec((CONV_HEAD_DIM, n_rows, LANES), lambda h: (h, 0, 0))
    per_ch = pl.BlockSpec((CONV_HEAD_DIM, _PC_COUNT, LANES), lambda h: (h, 0, 0))
    return pl.pallas_call(
        functools.partial(_conv_mix_kernel, batch=batch),
        grid=(CONV_HEADS,),
        in_specs=[slab, per_ch],
        out_specs=slab,
        out_shape=jax.ShapeDtypeStruct((CONV_WIDTH, n_rows, LANES), F32),
        compiler_params=_MIX_PARAMS,
        name="conv_mix",
    )(zs, per_channel)


def _out_ffn2_norm(x1, ys, yc, wglu_t, bglu, wo, g2, wg, wu, wd, gf):
    batch, seq, _ = x1.shape
    return pl.pallas_call(
        _out_ffn2_norm_kernel,
        grid=(seq // CHUNK, batch // TILE_SEQS),
        in_specs=[
            _tile_spec(),
            _slab_spec(S5_WIDTH),
            _slab_spec(CONV_WIDTH),
            _resident((S5_WIDTH, S5_WIDTH)),
            _resident((S5_WIDTH, 1)),
            _resident((S5_WIDTH + CONV_WIDTH, D_MODEL)),
            _resident((1, D_MODEL)),
            _resident((D_MODEL, D_FF)),
            _resident((D_MODEL, D_FF)),
            _resident((D_FF, D_MODEL)),
            _resident((1, D_MODEL)),
        ],
        out_specs=_tile_spec(),
        out_shape=jax.ShapeDtypeStruct((batch, seq, D_MODEL), F32),
        compiler_params=_DENSE_PARAMS,
        name="out_ffn2_norm",
    )(x1, ys, yc, wglu_t, bglu, wo, g2, wg, wu, wd, gf)


def _s5_tables(lam_re, lam_im, log_dt, b_re, b_im, c_re, c_im, d_skip):
    T, C, P, G = S5_T, S5_GROUP_CH, S5_STATE, S5_GROUPS
    dt = jnp.exp(log_dt)[:, None]
    mag = jnp.exp(lam_re * dt)
    abar_re = mag * jnp.cos(lam_im * dt)
    abar_im = mag * jnp.sin(lam_im * dt)
    den = lam_re * lam_re + lam_im * lam_im
    num_re = abar_re - 1.0
    num_im = abar_im
    f_re = ((num_re * lam_re + num_im * lam_im) / den)[..., None]
    f_im = ((num_im * lam_re - num_re * lam_im) / den)[..., None]
    bb_re = f_re * b_re - f_im * b_im
    bb_im = f_re * b_im + f_im * b_re
    n_q = T // SUBLANES
    expo = jnp.concatenate([jnp.arange(SUBLANES), SUBLANES * jnp.arange(1, n_q + 1)]).astype(F32)[None, :, None]
    bmag = jnp.exp(expo * (lam_re * dt)[:, None, :])
    bang = expo * (lam_im * dt)[:, None, :]
    base_re, base_im = bmag * jnp.cos(bang), bmag * jnp.sin(bang)
    lo_re, lo_im = base_re[:, :SUBLANES], base_im[:, :SUBLANES]
    hi_re = jnp.concatenate([base_re[:, :1], base_re[:, SUBLANES:]], axis=1)
    hi_im = jnp.concatenate([base_im[:, :1], base_im[:, SUBLANES:]], axis=1)

    def outer(q_re, q_im, r_re, r_im):
        re = q_re[:, :, None] * r_re[:, None] - q_im[:, :, None] * r_im[:, None]
        im = q_re[:, :, None] * r_im[:, None] + q_im[:, :, None] * r_re[:, None]
        return jnp.concatenate([re, im], axis=-1).reshape(G, -1, 2 * P)

    cat = lambda a, b: jnp.concatenate([a, b], axis=-1)
    powers = jnp.concatenate([
        outer(hi_re[:, n_q - 1::-1], hi_im[:, n_q - 1::-1], lo_re[:, ::-1], lo_im[:, ::-1]),
        outer(hi_re, hi_im, lo_re, lo_im)], axis=1)
    bt_re = bb_re.transpose(0, 2, 1)
    bt_im = bb_im.transpose(0, 2, 1)
    d_rows = jnp.pad(jnp.repeat(d_skip.reshape(G, C // 2, 2), T, axis=-1), ((0, 0), (0, C // 2), (0, 0)))
    cp_re = jnp.concatenate([c_re[:, 0::2], c_re[:, 1::2]], axis=1)
    cp_im = jnp.concatenate([c_im[:, 0::2], c_im[:, 1::2]], axis=1)
    vecs = [None] * _V_COUNT
    vecs[_V_BR] = cat(bt_re, bt_re)
    vecs[_V_BI_NP] = cat(-bt_im, bt_im)
    vecs[_V_BI_PN] = cat(bt_im, -bt_im)
    vecs[_V_BI_PP] = cat(bt_im, bt_im)
    vecs[_V_CR_PN] = cat(c_re, -c_re)
    vecs[_V_CI_NN] = cat(-c_im, -c_im)
    vecs[_V_C_RE_NIM] = cat(cp_re, -cp_im)
    vecs[_V_C_NIM_NRE] = cat(-cp_im, -cp_re)
    vecs[_V_D] = d_rows
    return powers, jnp.stack(vecs, axis=1)


def _conv_tables(w_dw, b_dw, ln_g, ln_b):
    taps = jnp.pad(w_dw[::-1].T, ((0, 0), (0, LANES - CONV_K)))
    lanes = lambda v: jnp.broadcast_to(v[:, None], (CONV_WIDTH, LANES))
    rows = [None] * _PC_COUNT
    rows[_PC_TAPS], rows[_PC_BIAS] = taps, lanes(b_dw)
    rows[_PC_HALF_GAIN], rows[_PC_HALF_BETA] = lanes(0.5 * ln_g), lanes(0.5 * ln_b)
    return jnp.stack(rows, axis=1)


def kernel(x, ffn1_norm, ffn1_w_gate, ffn1_w_up, ffn1_w_down, mix_norm, w_in, s5_lam_re, s5_lam_im, s5_log_dt, s5_b_re, s5_b_im, s5_c_re, s5_c_im, s5_d, s5_w_glu, s5_b_glu, conv_w_dw, conv_b_dw, conv_ln_g, conv_ln_b, w_out, ffn2_norm, ffn2_w_gate, ffn2_w_up, ffn2_w_down, final_norm):
    batch, seq, d_model = x.shape
    assert ffn1_norm.shape[0] == 1 and d_model == D_MODEL
    l = 0
    row = lambda v: v.reshape(1, -1)

    x1, us, zs, wg2, wu2, wd2, wo = _ffn1_in_proj(
        x, row(ffn1_norm[l]), ffn1_w_gate[l].astype(BF16), ffn1_w_up[l].astype(BF16),
        ffn1_w_down[l].astype(BF16), row(mix_norm[l]), w_in[l].T.astype(BF16),
        (ffn2_w_gate[l], ffn2_w_up[l], ffn2_w_down[l], w_out[l]))

    tables = _s5_tables(
        s5_lam_re[l], s5_lam_im[l], s5_log_dt[l], s5_b_re[l], s5_b_im[l], s5_c_re[l], s5_c_im[l], s5_d[l])
    ys = _s5_mix(us, tables, batch=batch)

    yc = _conv_mix(zs, _conv_tables(conv_w_dw[l], conv_b_dw[l], conv_ln_g[l], conv_ln_b[l]), batch=batch)

    return _out_ffn2_norm(
        x1, ys, yc, s5_w_glu[l].T.astype(BF16), s5_b_glu[l].reshape(-1, 1), wo,
        row(ffn2_norm[l]), wg2, wu2, wd2, row(final_norm))
```

```python
import functools

import jax
import jax.numpy as jnp
from jax import lax
from jax.experimental import pallas as pl
from jax.experimental.pallas import tpu as pltpu

D_MODEL = 1024
D_FF = 2816
S5_WIDTH = 512
S5_GROUP_CH = 16
S5_GROUPS = S5_WIDTH // S5_GROUP_CH
S5_STATE = 64
CONV_WIDTH = 512
CONV_HEAD_DIM = 64
CONV_HEADS = CONV_WIDTH // CONV_HEAD_DIM
CONV_K = 31
IN_COLS = S5_WIDTH + 2 * CONV_WIDTH
EPS = 1e-6

LANES = 128
SUBLANES = 8
CHUNK = LANES
S5_T = 64
S5_HALVES = CHUNK // S5_T
S5_PAIRS = S5_GROUP_CH // 2
S5_LANES = S5_GROUP_CH * S5_T
S5_STEP_GROUPS = 4
TILE_SEQS = 8
ROW_TILE = TILE_SEQS * CHUNK
FF_SPLITS = ((0, 1536), (1536, 2816))
LN_ROWS = 64
LN_UNROLL = 16
VMEM_LIMIT = 60 * 1024 * 1024

BF16 = jnp.bfloat16
F32 = jnp.float32

(_V_BR,
 _V_BI_NP,
 _V_BI_PN,
 _V_BI_PP,
 _V_CR_PN,
 _V_CI_NN,
 _V_C_RE_NIM,
 _V_C_NIM_NRE,
 _V_D,
 _V_COUNT) = range(10)

_PC_TAPS, _PC_BIAS, _PC_HALF_GAIN, _PC_HALF_BETA, _PC_COUNT = range(5)

_NT = (((1,), (1,)), ((), ()))
_TN = (((0,), (0,)), ((), ()))


def _rms_norm(x, g):
    return x * lax.rsqrt(jnp.mean(x * x, axis=-1, keepdims=True) + EPS) * g


def _dot(a, b):
    return jnp.dot(a, b, preferred_element_type=F32)


def _rms_split(x, g):
    return (x * g).astype(BF16), lax.rsqrt(jnp.mean(x * x, axis=-1, keepdims=True) + EPS)


def _swiglu(h, row_scale, wg_ref, wu_ref, wd_ref):
    acc = None
    for lo, hi in FF_SPLITS:
        g = _dot(h, wg_ref[:, lo:hi]) * row_scale
        u = _dot(h, wu_ref[:, lo:hi]) * row_scale
        a = (g * jax.nn.sigmoid(g) * u).astype(BF16)
        part = _dot(a, wd_ref[lo:hi, :])
        acc = part if acc is None else acc + part
    return acc


def _store_slabs(slab_ref, val, first_seq):
    n_ch = slab_ref.shape[0]
    rows = slab_ref.reshape(n_ch * SUBLANES, LANES)
    for ct in range(n_ch // SUBLANES):
        for s in range(TILE_SEQS):
            rows[pl.ds(ct * SUBLANES * SUBLANES + first_seq + s, SUBLANES, stride=SUBLANES), :] = (
                val[ct * SUBLANES:(ct + 1) * SUBLANES, s * LANES:(s + 1) * LANES])


def _load_slabs(slab_ref, first_seq):
    n_ch = slab_ref.shape[0]
    rows = slab_ref.reshape(n_ch * SUBLANES, LANES)
    bands = []
    for ct in range(n_ch // SUBLANES):
        bands.append(jnp.concatenate(
            [rows[pl.ds(ct * SUBLANES * SUBLANES + first_seq + s, SUBLANES, stride=SUBLANES), :]
             for s in range(TILE_SEQS)], axis=1))
    return jnp.concatenate(bands, axis=0)


def _ffn1_in_proj_kernel(x_ref, g1_ref, wg_ref, wu_ref, wd_ref, gm_ref, win_t_ref, *rest):
    n_cast = (len(rest) - 3) // 2
    x1_ref, us_ref, zs_ref = rest[n_cast:n_cast + 3]
    first_seq = pl.program_id(1) * TILE_SEQS
    x = x_ref[...].reshape(ROW_TILE, D_MODEL)
    h, r = _rms_split(x, g1_ref[...])
    x1 = x + 0.5 * _swiglu(h, r, wg_ref, wu_ref, wd_ref)
    x1_ref[...] = x1.reshape(TILE_SEQS, CHUNK, D_MODEL)
    h2, r2 = _rms_split(x1, gm_ref[...])
    r2_lanes = jnp.transpose(jnp.broadcast_to(r2, (ROW_TILE, LANES)))[0:1, :]

    def in_proj_t(lo, hi):
        return lax.dot_general(win_t_ref[lo:hi, :], h2, _NT, preferred_element_type=F32) * r2_lanes

    gate = jax.nn.sigmoid(in_proj_t(S5_WIDTH + CONV_WIDTH, IN_COLS))
    _store_slabs(us_ref, in_proj_t(0, S5_WIDTH), first_seq)
    z_t = in_proj_t(S5_WIDTH, S5_WIDTH + CONV_WIDTH) * gate
    _store_slabs(zs_ref, z_t, first_seq)
    for src_ref, dst_ref in zip(rest[:n_cast], rest[n_cast + 3:]):
        dst_ref[...] = src_ref[...].astype(BF16)


def _s5_mix_kernel(us_ref, pow_ref, vec_ref, ys_ref, *scratch, batch):
    for g in range(S5_STEP_GROUPS):
        ch = pl.ds(g * S5_GROUP_CH, S5_GROUP_CH)
        _s5_group(us_ref.at[ch], pow_ref.at[g], vec_ref.at[g], ys_ref.at[ch], *scratch, batch=batch)


def _s5_group(us_ref, pow_ref, vec_ref, ys_ref,
              cb_ref, f_ref, upk_ref, lhs_ref, win_ref, wout_ref, sprev_ref, *, batch):
    T = S5_T
    n_slab_rows = us_ref.shape[1]
    n_slab_chunks = n_slab_rows // batch
    n_rows = S5_HALVES * n_slab_rows
    lane = lax.broadcasted_iota(jnp.int32, (1, LANES), 1)
    first_state_half = lane < S5_STATE
    first_time_half = lane < T
    swap = lambda v: pltpu.roll(v, S5_STATE, 1)
    pa_rev = pow_ref[0:T, :]
    pb_rev = swap(pa_rev)
    pa_fwd = pow_ref[T + 1:2 * T + 1, :]
    pb_fwd = swap(pa_fwd)
    for c in range(S5_GROUP_CH):
        rows = slice(c * T, (c + 1) * T)
        vec = lambda i: vec_ref[i, c:c + 1, :]
        win_ref[rows, 0:2 * S5_STATE] = (pa_rev * vec(_V_BR) + pb_rev * vec(_V_BI_NP)).astype(BF16)
        win_ref[rows, 2 * S5_STATE:] = (pb_rev * vec(_V_BR) + pa_rev * vec(_V_BI_PN)).astype(BF16)
        wout_ref[rows, :] = (pa_fwd * vec(_V_CR_PN) + pb_fwd * vec(_V_CI_NN)).astype(BF16)
        cb_ref[c * S5_GROUP_CH:(c + 1) * S5_GROUP_CH, :] = (
            vec_ref[_V_C_RE_NIM] * vec(_V_BR) + vec_ref[_V_C_NIM_NRE] * vec(_V_BI_PP))
    f = lax.dot_general(cb_ref[...], pow_ref[T:2 * T, :], _NT, precision=lax.Precision.HIGHEST,
                        preferred_element_type=F32)
    for c in range(S5_GROUP_CH):
        even, odd = c * S5_GROUP_CH, c * S5_GROUP_CH + S5_PAIRS
        f_ref[c * S5_PAIRS:(c + 1) * S5_PAIRS, :] = jnp.concatenate(
            [f[even:even + S5_PAIRS], f[odd:odd + S5_PAIRS]], axis=1)

    for p in range(S5_PAIRS):
        a, b = us_ref[2 * p], us_ref[2 * p + 1]
        halves = (jnp.where(first_time_half, a, pltpu.roll(b, T, 1)),
                  jnp.where(first_time_half, pltpu.roll(a, T, 1), b))
        for hh in range(S5_HALVES):
            upk_ref[:, hh * batch:(hh + 1) * batch, p * LANES:(p + 1) * LANES] = (
                halves[hh].reshape(n_slab_chunks, batch, LANES))
    lhs_ref[...] = upk_ref[...].reshape(n_rows, S5_LANES).astype(BF16)

    e = _dot(lhs_ref[...], win_ref[...])
    a_chunk = pow_ref[2 * T:2 * T + 1, :]
    a_sw = swap(a_chunk)
    a1 = jnp.where(first_state_half, a_chunk, a_sw)
    a2 = jnp.where(first_state_half, -a_sw, a_chunk)
    s = jnp.zeros((batch, 2 * S5_STATE), F32)
    s_sw = jnp.zeros((batch, 2 * S5_STATE), F32)
    for c in range(n_rows // batch):
        rows = slice(c * batch, (c + 1) * batch)
        sprev_ref[rows, :] = s
        e_c = e[rows, :]
        s, s_sw = (a1 * s + a2 * s_sw + e_c[:, :2 * S5_STATE],
                   a1 * s_sw - a2 * s + e_c[:, 2 * S5_STATE:])
    sprev = sprev_ref[...].astype(BF16)

    causal = ((lax.broadcasted_iota(jnp.int32, (T, LANES), 1) & (T - 1))
              >= lax.broadcasted_iota(jnp.int32, (T, LANES), 0))

    def toeplitz(ci, pair):
        f_row = jnp.broadcast_to(f_ref[ci * S5_PAIRS + pair:ci * S5_PAIRS + pair + 1, :], (T, LANES))
        return jnp.where(causal, pltpu.roll(f_row, 0, 1, stride=1, stride_axis=0), 0.0).astype(BF16)

    ci_per_tile = 2 * LANES // T
    for n in range(S5_PAIRS // 2):
        acc = lax.dot_general(sprev, wout_ref[2 * n * LANES:(2 * n + 2) * LANES, :], _NT,
                              preferred_element_type=F32)
        for kt in range(S5_GROUP_CH // ci_per_tile):
            w = jnp.concatenate(
                [jnp.concatenate([toeplitz(ci_per_tile * kt + r, 2 * n), toeplitz(ci_per_tile * kt + r, 2 * n + 1)],
                                 axis=1) for r in range(ci_per_tile)], axis=0)
            acc = acc + _dot(lhs_ref[:, 2 * kt * LANES:(2 * kt + 2) * LANES], w)
        for q in range(2):
            pair = 2 * n + q
            u_pk = upk_ref[:, :, pair * LANES:(pair + 1) * LANES].reshape(n_rows, LANES)
            y = acc[:, q * LANES:(q + 1) * LANES] + vec_ref[_V_D, pair:pair + 1, :] * u_pk
            y = jax.nn.gelu(y, approximate=True).reshape(n_slab_chunks, S5_HALVES * batch, LANES)
            y0 = y[:, 0:batch, :].reshape(n_slab_rows, LANES)
            y1 = y[:, batch:2 * batch, :].reshape(n_slab_rows, LANES)
            ys_ref[2 * pair] = jnp.where(first_time_half, y0, pltpu.roll(y1, T, 1))
            ys_ref[2 * pair + 1] = jnp.where(first_time_half, pltpu.roll(y0, T, 1), y1)


def _conv_mix_kernel(zs_ref, pc_ref, o_ref, *, batch):
    n_rows = zs_ref.shape[1]
    row = lax.broadcasted_iota(jnp.int32, (CHUNK, CHUNK), 0)
    col = lax.broadcasted_iota(jnp.int32, (CHUNK, CHUNK), 1)
    lag = row - col
    band_cur = (lag >= -(CONV_K - 1)) & (lag <= 0)
    band_prev = lag >= CHUNK - (CONV_K - 1)

    def conv_channel(ch, carry):
        z = zs_ref[ch]
        z_prev = jnp.concatenate([jnp.zeros((batch, CHUNK), F32), z[:n_rows - batch]], axis=0)
        lhs = jnp.concatenate([z_prev, z], axis=1).astype(BF16)
        taps = jnp.broadcast_to(pc_ref[ch, _PC_TAPS:_PC_TAPS + 1, :], (CHUNK, CHUNK))
        diag = pltpu.roll(taps, 0, 1, stride=1, stride_axis=0)
        toe = jnp.concatenate([jnp.where(band_prev, diag, 0.0), jnp.where(band_cur, diag, 0.0)],
                              axis=0).astype(BF16)
        o_ref[ch] = _dot(lhs, toe) + pc_ref[ch, _PC_BIAS:_PC_BIAS + 1, :]
        return carry

    lax.fori_loop(0, CONV_HEAD_DIM, conv_channel, 0, unroll=32)

    inv_n = 1.0 / CONV_HEAD_DIM
    for r0 in range(0, n_rows, LN_ROWS):
        rows = pl.ds(r0, LN_ROWS)
        zero = jnp.zeros((LN_ROWS, CHUNK), F32)
        mu = lax.fori_loop(0, CONV_HEAD_DIM, lambda ch, acc: acc + o_ref[ch, rows, :], zero,
                           unroll=LN_UNROLL) * inv_n
        var = lax.fori_loop(0, CONV_HEAD_DIM, lambda ch, acc: acc + jnp.square(o_ref[ch, rows, :] - mu),
                            zero, unroll=LN_UNROLL) * inv_n
        rstd = lax.rsqrt(var + EPS)

        def norm_channel(ch, carry):
            half = ((o_ref[ch, rows, :] - mu) * rstd * pc_ref[ch, _PC_HALF_GAIN:_PC_HALF_GAIN + 1, :]
                    + pc_ref[ch, _PC_HALF_BETA:_PC_HALF_BETA + 1, :])
            o_ref[ch, rows, :] = half + half * jnp.tanh(half)
            return carry

        lax.fori_loop(0, CONV_HEAD_DIM, norm_channel, 0, unroll=LN_UNROLL)


def _out_ffn2_norm_kernel(x1_ref, ys_ref, yc_ref, wglu_t_ref, bglu_ref, wo_ref,
                          g2_ref, wg_ref, wu_ref, wd_ref, gf_ref, o_ref):
    first_seq = pl.program_id(1) * TILE_SEQS
    ys_t = _load_slabs(ys_ref, first_seq)
    yc_t = _load_slabs(yc_ref, first_seq)
    x2 = x1_ref[...].reshape(ROW_TILE, D_MODEL) + lax.dot_general(
        yc_t.astype(BF16), wo_ref[S5_WIDTH:, :], _TN, preferred_element_type=F32)
    gate_t = jax.nn.sigmoid(_dot(wglu_t_ref[...], ys_t.astype(BF16)) + bglu_ref[...])
    x2 = x2 + lax.dot_general((ys_t * gate_t).astype(BF16), wo_ref[0:S5_WIDTH, :], _TN,
                              preferred_element_type=F32)
    h, r = _rms_split(x2, g2_ref[...])
    x3 = x2 + 0.5 * _swiglu(h, r, wg_ref, wu_ref, wd_ref)
    o_ref[...] = _rms_norm(x3, gf_ref[...]).reshape(TILE_SEQS, CHUNK, D_MODEL)


def _resident(shape):
    nd = len(shape)
    return pl.BlockSpec(shape, lambda *_: (0,) * nd, pipeline_mode=pl.Buffered(1))


def _tile_spec():
    return pl.BlockSpec((TILE_SEQS, CHUNK, D_MODEL), lambda c, h: (h, c, 0))


def _slab_spec(n_ch):
    return pl.BlockSpec((n_ch, SUBLANES, LANES), lambda c, h: (0, c, 0))


_DENSE_PARAMS = pltpu.CompilerParams(dimension_semantics=("arbitrary", "arbitrary"),
                                     vmem_limit_bytes=VMEM_LIMIT)
_MIX_PARAMS = pltpu.CompilerParams(dimension_semantics=("arbitrary",), vmem_limit_bytes=VMEM_LIMIT)


def _cast_spec(w, n_steps):
    n_blocks = n_steps
    while w.shape[0] % n_blocks or (w.shape[0] // n_blocks) % (2 * SUBLANES):
        n_blocks //= 2
    steps_per_block = n_steps // n_blocks
    return pl.BlockSpec((w.shape[0] // n_blocks, w.shape[1]), lambda c, h: (c // steps_per_block, 0))


def _ffn1_in_proj(x, g1, wg, wu, wd, gm, win_t, later_weights):
    batch, seq, _ = x.shape
    assert batch == SUBLANES and seq % CHUNK == 0
    n_rows = (seq // CHUNK) * batch
    cast_specs = [_cast_spec(w, seq // CHUNK) for w in later_weights]
    return pl.pallas_call(
        _ffn1_in_proj_kernel,
        grid=(seq // CHUNK, batch // TILE_SEQS),
        in_specs=[
            _tile_spec(),
            _resident((1, D_MODEL)),
            _resident((D_MODEL, D_FF)),
            _resident((D_MODEL, D_FF)),
            _resident((D_FF, D_MODEL)),
            _resident((1, D_MODEL)),
            _resident((IN_COLS, D_MODEL)),
        ] + cast_specs,
        out_specs=[_tile_spec(), _slab_spec(S5_WIDTH), _slab_spec(CONV_WIDTH)] + cast_specs,
        out_shape=[
            jax.ShapeDtypeStruct((batch, seq, D_MODEL), F32),
            jax.ShapeDtypeStruct((S5_WIDTH, n_rows, LANES), F32),
            jax.ShapeDtypeStruct((CONV_WIDTH, n_rows, LANES), F32),
        ] + [jax.ShapeDtypeStruct(w.shape, BF16) for w in later_weights],
        compiler_params=_DENSE_PARAMS,
        name="ffn1_in_proj",
    )(x, g1, wg, wu, wd, gm, win_t, *later_weights)


def _s5_mix(us, tables, *, batch):
    n_rows = us.shape[1]
    slab = pl.BlockSpec((S5_STEP_GROUPS * S5_GROUP_CH, n_rows, LANES), lambda g: (g, 0, 0))

    def per_group(t):
        nd = t.ndim - 1
        return pl.BlockSpec((S5_STEP_GROUPS,) + t.shape[1:], lambda g: (g,) + (0,) * nd)

    return pl.pallas_call(
        functools.partial(_s5_mix_kernel, batch=batch),
        grid=(S5_GROUPS // S5_STEP_GROUPS,),
        in_specs=[slab] + [per_group(t) for t in tables],
        out_specs=slab,
        out_shape=jax.ShapeDtypeStruct((S5_WIDTH, n_rows, LANES), F32),
        scratch_shapes=[pltpu.VMEM((S5_GROUP_CH * S5_GROUP_CH, 2 * S5_STATE), F32),
                        pltpu.VMEM((S5_GROUP_CH * S5_PAIRS, LANES), F32),
                        pltpu.VMEM((n_rows // batch, S5_HALVES * batch, S5_LANES), F32),
                        pltpu.VMEM((S5_HALVES * n_rows, S5_LANES), BF16),
                        pltpu.VMEM((S5_LANES, 4 * S5_STATE), BF16),
                        pltpu.VMEM((S5_LANES, 2 * S5_STATE), BF16),
                        pltpu.VMEM((S5_HALVES * n_rows, 2 * S5_STATE), F32)],
        compiler_params=_MIX_PARAMS,
        name="s5_mix",
    )(us, *tables)


def _conv_mix(zs, per_channel, *, batch):
    n_rows = zs.shape[1]
    slab = pl.BlockSpec((CONV_HEAD_DIM, n_rows, LANES), lambda h: (h, 0, 0))
    per_ch = pl.BlockSpec((CONV_HEAD_DIM, _PC_COUNT, LANES), lambda h: (h, 0, 0))
    return pl.pallas_call(
        functools.partial(_conv_mix_kernel, batch=batch),
        grid=(CONV_HEADS,),
        in_specs=[slab, per_ch],
        out_specs=slab,
        out_shape=jax.ShapeDtypeStruct((CONV_WIDTH, n_rows, LANES), F32),
        compiler_params=_MIX_PARAMS,
        name="conv_mix",
    )(zs, per_channel)


def _out_ffn2_norm(x1, ys, yc, wglu_t, bglu, wo, g2, wg, wu, wd, gf):
    batch, seq, _ = x1.shape
    return pl.pallas_call(
        _out_ffn2_norm_kernel,
        grid=(seq // CHUNK, batch // TILE_SEQS),
        in_specs=[
            _tile_spec(),
            _slab_spec(S5_WIDTH),
            _slab_spec(CONV_WIDTH),
            _resident((S5_WIDTH, S5_WIDTH)),
            _resident((S5_WIDTH, 1)),
            _resident((S5_WIDTH + CONV_WIDTH, D_MODEL)),
            _resident((1, D_MODEL)),
            _resident((D_MODEL, D_FF)),
            _resident((D_MODEL, D_FF)),
            _resident((D_FF, D_MODEL)),
            _resident((1, D_MODEL)),
        ],
        out_specs=_tile_spec(),
        out_shape=jax.ShapeDtypeStruct((batch, seq, D_MODEL), F32),
        compiler_params=_DENSE_PARAMS,
        name="out_ffn2_norm",
    )(x1, ys, yc, wglu_t, bglu, wo, g2, wg, wu, wd, gf)


def _s5_tables(lam_re, lam_im, log_dt, b_re, b_im, c_re, c_im, d_skip):
    T, C, P, G = S5_T, S5_GROUP_CH, S5_STATE, S5_GROUPS
    dt = jnp.exp(log_dt)[:, None]
    mag = jnp.exp(lam_re * dt)
    abar_re = mag * jnp.cos(lam_im * dt)
    abar_im = mag * jnp.sin(lam_im * dt)
    den = lam_re * lam_re + lam_im * lam_im
    num_re = abar_re - 1.0
    num_im = abar_im
    f_re = ((num_re * lam_re + num_im * lam_im) / den)[..., None]
    f_im = ((num_im * lam_re - num_re * lam_im) / den)[..., None]
    bb_re = f_re * b_re - f_im * b_im
    bb_im = f_re * b_im + f_im * b_re
    n_q = T // SUBLANES
    expo = jnp.concatenate([jnp.arange(SUBLANES), SUBLANES * jnp.arange(1, n_q + 1)]).astype(F32)[None, :, None]
    bmag = jnp.exp(expo * (lam_re * dt)[:, None, :])
    bang = expo * (lam_im * dt)[:, None, :]
    base_re, base_im = bmag * jnp.cos(bang), bmag * jnp.sin(bang)
    lo_re, lo_im = base_re[:, :SUBLANES], base_im[:, :SUBLANES]
    hi_re = jnp.concatenate([base_re[:, :1], base_re[:, SUBLANES:]], axis=1)
    hi_im = jnp.concatenate([base_im[:, :1], base_im[:, SUBLANES:]], axis=1)

    def outer(q_re, q_im, r_re, r_im):
        re = q_re[:, :, None] * r_re[:, None] - q_im[:, :, None] * r_im[:, None]
        im = q_re[:, :, None] * r_im[:, None] + q_im[:, :, None] * r_re[:, None]
        return jnp.concatenate([re, im], axis=-1).reshape(G, -1, 2 * P)

    cat = lambda a, b: jnp.concatenate([a, b], axis=-1)
    powers = jnp.concatenate([
        outer(hi_re[:, n_q - 1::-1], hi_im[:, n_q - 1::-1], lo_re[:, ::-1], lo_im[:, ::-1]),
        outer(hi_re, hi_im, lo_re, lo_im)], axis=1)
    bt_re = bb_re.transpose(0, 2, 1)
    bt_im = bb_im.transpose(0, 2, 1)
    d_rows = jnp.pad(jnp.repeat(d_skip.reshape(G, C // 2, 2), T, axis=-1), ((0, 0), (0, C // 2), (0, 0)))
    cp_re = jnp.concatenate([c_re[:, 0::2], c_re[:, 1::2]], axis=1)
    cp_im = jnp.concatenate([c_im[:, 0::2], c_im[:, 1::2]], axis=1)
    vecs = [None] * _V_COUNT
    vecs[_V_BR] = cat(bt_re, bt_re)
    vecs[_V_BI_NP] = cat(-bt_im, bt_im)
    vecs[_V_BI_PN] = cat(bt_im, -bt_im)
    vecs[_V_BI_PP] = cat(bt_im, bt_im)
    vecs[_V_CR_PN] = cat(c_re, -c_re)
    vecs[_V_CI_NN] = cat(-c_im, -c_im)
    vecs[_V_C_RE_NIM] = cat(cp_re, -cp_im)
    vecs[_V_C_NIM_NRE] = cat(-cp_im, -cp_re)
    vecs[_V_D] = d_rows
    return powers, jnp.stack(vecs, axis=1)


def _conv_tables(w_dw, b_dw, ln_g, ln_b):
    taps = jnp.pad(w_dw[::-1].T, ((0, 0), (0, LANES - CONV_K)))
    lanes = lambda v: jnp.broadcast_to(v[:, None], (CONV_WIDTH, LANES))
    rows = [None] * _PC_COUNT
    rows[_PC_TAPS], rows[_PC_BIAS] = taps, lanes(b_dw)
    rows[_PC_HALF_GAIN], rows[_PC_HALF_BETA] = lanes(0.5 * ln_g), lanes(0.5 * ln_b)
    return jnp.stack(rows, axis=1)


def kernel(x, ffn1_norm, ffn1_w_gate, ffn1_w_up, ffn1_w_down, mix_norm, w_in, s5_lam_re, s5_lam_im, s5_log_dt, s5_b_re, s5_b_im, s5_c_re, s5_c_im, s5_d, s5_w_glu, s5_b_glu, conv_w_dw, conv_b_dw, conv_ln_g, conv_ln_b, w_out, ffn2_norm, ffn2_w_gate, ffn2_w_up, ffn2_w_down, final_norm):
    batch, seq, d_model = x.shape
    assert ffn1_norm.shape[0] == 1 and d_model == D_MODEL
    l = 0
    row = lambda v: v.reshape(1, -1)

    x1, us, zs, wg2, wu2, wd2, wo = _ffn1_in_proj(
        x, row(ffn1_norm[l]), ffn1_w_gate[l].astype(BF16), ffn1_w_up[l].astype(BF16),
        ffn1_w_down[l].astype(BF16), row(mix_norm[l]), w_in[l].T.astype(BF16),
        (ffn2_w_gate[l], ffn2_w_up[l], ffn2_w_down[l], w_out[l]))

    tables = _s5_tables(
        s5_lam_re[l], s5_lam_im[l], s5_log_dt[l], s5_b_re[l], s5_b_im[l], s5_c_re[l], s5_c_im[l], s5_d[l])
    ys = _s5_mix(us, tables, batch=batch)

    yc = _conv_mix(zs, _conv_tables(conv_w_dw[l], conv_b_dw[l], conv_ln_g[l], conv_ln_b[l]), batch=batch)

    return _out_ffn2_norm(
        x1, ys, yc, s5_w_glu[l].T.astype(BF16), s5_b_glu[l].reshape(-1, 1), wo,
        row(ffn2_norm[l]), wg2, wu2, wd2, row(final_norm))
```

```python
import functools

import jax
import jax.numpy as jnp
from jax import lax
from jax.experimental import pallas as pl
from jax.experimental.pallas import tpu as pltpu

D_MODEL = 1024
D_FF = 2816
S5_WIDTH = 512
S5_GROUP_CH = 16
S5_GROUPS = S5_WIDTH // S5_GROUP_CH
S5_STATE = 64
CONV_WIDTH = 512
CONV_HEAD_DIM = 64
CONV_HEADS = CONV_WIDTH // CONV_HEAD_DIM
CONV_K = 31
IN_COLS = S5_WIDTH + 2 * CONV_WIDTH
EPS = 1e-6

LANES = 128
SUBLANES = 8
CHUNK = LANES
S5_T = 64
S5_HALVES = CHUNK // S5_T
S5_PAIRS = S5_GROUP_CH // 2
S5_LANES = S5_GROUP_CH * S5_T
S5_STEP_GROUPS = 2
TILE_SEQS = 8
ROW_TILE = TILE_SEQS * CHUNK
FF_SPLITS = ((0, 1536), (1536, 2816))
LN_ROWS = 64
LN_UNROLL = 16
VMEM_LIMIT = 60 * 1024 * 1024

BF16 = jnp.bfloat16
F32 = jnp.float32

(_V_BR,
 _V_BI_NP,
 _V_BI_PN,
 _V_BI_PP,
 _V_CR_PN,
 _V_CI_NN,
 _V_C_RE_NIM,
 _V_C_NIM_NRE,
 _V_D,
 _V_COUNT) = range(10)

_PC_TAPS, _PC_BIAS, _PC_HALF_GAIN, _PC_HALF_BETA, _PC_COUNT = range(5)

_NT = (((1,), (1,)), ((), ()))
_TN = (((0,), (0,)), ((), ()))


def _rms_norm(x, g):
    return x * lax.rsqrt(jnp.mean(x * x, axis=-1, keepdims=True) + EPS) * g


def _dot(a, b):
    return jnp.dot(a, b, preferred_element_type=F32)


def _rms_split(x, g):
    return (x * g).astype(BF16), lax.rsqrt(jnp.mean(x * x, axis=-1, keepdims=True) + EPS)


def _swiglu(h, row_scale, wg_ref, wu_ref, wd_ref):
    acc = None
    for lo, hi in FF_SPLITS:
        g = _dot(h, wg_ref[:, lo:hi]) * row_scale
        u = _dot(h, wu_ref[:, lo:hi]) * row_scale
        a = (g * jax.nn.sigmoid(g) * u).astype(BF16)
        part = _dot(a, wd_ref[lo:hi, :])
        acc = part if acc is None else acc + part
    return acc


def _store_slabs(slab_ref, val, first_seq):
    n_ch = slab_ref.shape[0]
    rows = slab_ref.reshape(n_ch * SUBLANES, LANES)
    for ct in range(n_ch // SUBLANES):
        for s in range(TILE_SEQS):
            rows[pl.ds(ct * SUBLANES * SUBLANES + first_seq + s, SUBLANES, stride=SUBLANES), :] = (
                val[ct * SUBLANES:(ct + 1) * SUBLANES, s * LANES:(s + 1) * LANES])


def _load_slabs(slab_ref, first_seq):
    n_ch = slab_ref.shape[0]
    rows = slab_ref.reshape(n_ch * SUBLANES, LANES)
    bands = []
    for ct in range(n_ch // SUBLANES):
        bands.append(jnp.concatenate(
            [rows[pl.ds(ct * SUBLANES * SUBLANES + first_seq + s, SUBLANES, stride=SUBLANES), :]
             for s in range(TILE_SEQS)], axis=1))
    return jnp.concatenate(bands, axis=0)


def _ffn1_in_proj_kernel(x_ref, g1_ref, wg_ref, wu_ref, wd_ref, gm_ref, win_t_ref, *rest):
    n_cast = (len(rest) - 3) // 2
    x1_ref, us_ref, zs_ref = rest[n_cast:n_cast + 3]
    first_seq = pl.program_id(1) * TILE_SEQS
    x = x_ref[...].reshape(ROW_TILE, D_MODEL)
    h, r = _rms_split(x, g1_ref[...])
    x1 = x + 0.5 * _swiglu(h, r, wg_ref, wu_ref, wd_ref)
    x1_ref[...] = x1.reshape(TILE_SEQS, CHUNK, D_MODEL)
    h2, r2 = _rms_split(x1, gm_ref[...])
    r2_lanes = jnp.transpose(jnp.broadcast_to(r2, (ROW_TILE, LANES)))[0:1, :]

    def in_proj_t(lo, hi):
        return lax.dot_general(win_t_ref[lo:hi, :], h2, _NT, preferred_element_type=F32) * r2_lanes

    gate = jax.nn.sigmoid(in_proj_t(S5_WIDTH + CONV_WIDTH, IN_COLS))
    _store_slabs(us_ref, in_proj_t(0, S5_WIDTH), first_seq)
    z_t = in_proj_t(S5_WIDTH, S5_WIDTH + CONV_WIDTH) * gate
    _store_slabs(zs_ref, z_t, first_seq)
    for src_ref, dst_ref in zip(rest[:n_cast], rest[n_cast + 3:]):
        dst_ref[...] = src_ref[...].astype(BF16)


def _regroup_halves(a, b):
    first_half = lax.broadcasted_iota(jnp.int32, (1, LANES), 1) < S5_T
    straight = jnp.where(first_half, a, b)
    crossed = pltpu.roll(jnp.where(first_half, b, a), S5_T, 1)
    return jnp.where(first_half, straight, crossed), jnp.where(first_half, crossed, straight)


def _s5_mix_kernel(us_ref, pow_ref, vec_ref, ys_ref, *scratch, batch):
    for g in range(S5_STEP_GROUPS):
        ch = pl.ds(g * S5_GROUP_CH, S5_GROUP_CH)
        _s5_group(us_ref.at[ch], pow_ref.at[g], vec_ref.at[g], ys_ref.at[ch], *scratch, batch=batch)


def _s5_group(us_ref, pow_ref, vec_ref, ys_ref,
              cb_ref, f_ref, upk_ref, lhs_ref, win_ref, wout_ref, sprev_ref, *, batch):
    T = S5_T
    n_slab_rows = us_ref.shape[1]
    n_slab_chunks = n_slab_rows // batch
    n_rows = S5_HALVES * n_slab_rows
    lane = lax.broadcasted_iota(jnp.int32, (1, LANES), 1)
    first_state_half = lane < S5_STATE
    swap = lambda v: pltpu.roll(v, S5_STATE, 1)
    pa_rev = pow_ref[0:T, :]
    pb_rev = swap(pa_rev)
    pa_fwd = pow_ref[T + 1:2 * T + 1, :]
    pb_fwd = swap(pa_fwd)
    for c in range(S5_GROUP_CH):
        rows = slice(c * T, (c + 1) * T)
        vec = lambda i: vec_ref[i, c:c + 1, :]
        win_ref[rows, 0:2 * S5_STATE] = (pa_rev * vec(_V_BR) + pb_rev * vec(_V_BI_NP)).astype(BF16)
        win_ref[rows, 2 * S5_STATE:] = (pb_rev * vec(_V_BR) + pa_rev * vec(_V_BI_PN)).astype(BF16)
        wout_ref[rows, :] = (pa_fwd * vec(_V_CR_PN) + pb_fwd * vec(_V_CI_NN)).astype(BF16)
        cb_ref[c * S5_GROUP_CH:(c + 1) * S5_GROUP_CH, :] = (
            vec_ref[_V_C_RE_NIM] * vec(_V_BR) + vec_ref[_V_C_NIM_NRE] * vec(_V_BI_PP))
    f = lax.dot_general(cb_ref[...], pow_ref[T:2 * T, :], _NT, precision=lax.Precision.HIGHEST,
                        preferred_element_type=F32)
    for c in range(S5_GROUP_CH):
        even, odd = c * S5_GROUP_CH, c * S5_GROUP_CH + S5_PAIRS
        f_ref[c * S5_PAIRS:(c + 1) * S5_PAIRS, :] = jnp.concatenate(
            [f[even:even + S5_PAIRS], f[odd:odd + S5_PAIRS]], axis=1)

    for p in range(S5_PAIRS):
        halves = _regroup_halves(us_ref[2 * p], us_ref[2 * p + 1])
        for hh in range(S5_HALVES):
            upk_ref[:, hh * batch:(hh + 1) * batch, p * LANES:(p + 1) * LANES] = (
                halves[hh].reshape(n_slab_chunks, batch, LANES))
    lhs_ref[...] = upk_ref[...].reshape(n_rows, S5_LANES).astype(BF16)

    e = _dot(lhs_ref[...], win_ref[...])
    a_chunk = pow_ref[2 * T:2 * T + 1, :]
    a_sw = swap(a_chunk)
    a1 = jnp.where(first_state_half, a_chunk, a_sw)
    a2 = jnp.where(first_state_half, -a_sw, a_chunk)
    s = jnp.zeros((batch, 2 * S5_STATE), F32)
    s_sw = jnp.zeros((batch, 2 * S5_STATE), F32)
    for c in range(n_rows // batch):
        rows = slice(c * batch, (c + 1) * batch)
        sprev_ref[rows, :] = s
        e_c = e[rows, :]
        s, s_sw = (a1 * s + a2 * s_sw + e_c[:, :2 * S5_STATE],
                   a1 * s_sw - a2 * s + e_c[:, 2 * S5_STATE:])
    sprev = sprev_ref[...].astype(BF16)

    causal = ((lax.broadcasted_iota(jnp.int32, (T, LANES), 1) & (T - 1))
              >= lax.broadcasted_iota(jnp.int32, (T, LANES), 0))

    def toeplitz(ci, pair):
        f_row = jnp.broadcast_to(f_ref[ci * S5_PAIRS + pair:ci * S5_PAIRS + pair + 1, :], (T, LANES))
        return jnp.where(causal, pltpu.roll(f_row, 0, 1, stride=1, stride_axis=0), 0.0).astype(BF16)

    ci_per_tile = 2 * LANES // T
    for n in range(S5_PAIRS // 2):
        acc = lax.dot_general(sprev, wout_ref[2 * n * LANES:(2 * n + 2) * LANES, :], _NT,
                              preferred_element_type=F32)
        for kt in range(S5_GROUP_CH // ci_per_tile):
            w = jnp.concatenate(
                [jnp.concatenate([toeplitz(ci_per_tile * kt + r, 2 * n), toeplitz(ci_per_tile * kt + r, 2 * n + 1)],
                                 axis=1) for r in range(ci_per_tile)], axis=0)
            acc = acc + _dot(lhs_ref[:, 2 * kt * LANES:(2 * kt + 2) * LANES], w)
        for q in range(2):
            pair = 2 * n + q
            u_pk = upk_ref[:, :, pair * LANES:(pair + 1) * LANES].reshape(n_rows, LANES)
            y = acc[:, q * LANES:(q + 1) * LANES] + vec_ref[_V_D, pair:pair + 1, :] * u_pk
            y = jax.nn.gelu(y, approximate=True).reshape(n_slab_chunks, S5_HALVES * batch, LANES)
            y0 = y[:, 0:batch, :].reshape(n_slab_rows, LANES)
            y1 = y[:, batch:2 * batch, :].reshape(n_slab_rows, LANES)
            ys_ref[2 * pair], ys_ref[2 * pair + 1] = _regroup_halves(y0, y1)


def _conv_mix_kernel(zs_ref, pc_ref, o_ref, *, batch):
    n_rows = zs_ref.shape[1]
    row = lax.broadcasted_iota(jnp.int32, (CHUNK, CHUNK), 0)
    col = lax.broadcasted_iota(jnp.int32, (CHUNK, CHUNK), 1)
    lag = row - col
    band_cur = (lag >= -(CONV_K - 1)) & (lag <= 0)
    band_prev = lag >= CHUNK - (CONV_K - 1)

    def conv_channel(ch, carry):
        z = zs_ref[ch]
        z_prev = jnp.concatenate([jnp.zeros((batch, CHUNK), F32), z[:n_rows - batch]], axis=0)
        lhs = jnp.concatenate([z_prev, z], axis=1).astype(BF16)
        taps = jnp.broadcast_to(pc_ref[ch, _PC_TAPS:_PC_TAPS + 1, :], (CHUNK, CHUNK))
        diag = pltpu.roll(taps, 0, 1, stride=1, stride_axis=0)
        toe = jnp.concatenate([jnp.where(band_prev, diag, 0.0), jnp.where(band_cur, diag, 0.0)],
                              axis=0).astype(BF16)
        o_ref[ch] = _dot(lhs, toe) + pc_ref[ch, _PC_BIAS:_PC_BIAS + 1, :]
        return carry

    lax.fori_loop(0, CONV_HEAD_DIM, conv_channel, 0, unroll=32)

    inv_n = 1.0 / CONV_HEAD_DIM
    for r0 in range(0, n_rows, LN_ROWS):
        rows = pl.ds(r0, LN_ROWS)
        zero = jnp.zeros((LN_ROWS, CHUNK), F32)
        mu = lax.fori_loop(0, CONV_HEAD_DIM, lambda ch, acc: acc + o_ref[ch, rows, :], zero,
                           unroll=LN_UNROLL) * inv_n
        var = lax.fori_loop(0, CONV_HEAD_DIM, lambda ch, acc: acc + jnp.square(o_ref[ch, rows, :] - mu),
                            zero, unroll=LN_UNROLL) * inv_n
        rstd = lax.rsqrt(var + EPS)

        def norm_channel(ch, carry):
            half = ((o_ref[ch, rows, :] - mu) * rstd * pc_ref[ch, _PC_HALF_GAIN:_PC_HALF_GAIN + 1, :]
                    + pc_ref[ch, _PC_HALF_BETA:_PC_HALF_BETA + 1, :])
            o_ref[ch, rows, :] = half + half * jnp.tanh(half)
            return carry

        lax.fori_loop(0, CONV_HEAD_DIM, norm_channel, 0, unroll=LN_UNROLL)


def _out_ffn2_norm_kernel(x1_ref, ys_ref, yc_ref, wglu_t_ref, bglu_ref, wo_ref,
                          g2_ref, wg_ref, wu_ref, wd_ref, gf_ref, o_ref):
    first_seq = pl.program_id(1) * TILE_SEQS
    ys_t = _load_slabs(ys_ref, first_seq)
    yc_t = _load_slabs(yc_ref, first_seq)
    x2 = x1_ref[...].reshape(ROW_TILE, D_MODEL) + lax.dot_general(
        yc_t.astype(BF16), wo_ref[S5_WIDTH:, :], _TN, preferred_element_type=F32)
    gate_t = jax.nn.sigmoid(_dot(wglu_t_ref[...], ys_t.astype(BF16)) + bglu_ref[...])
    x2 = x2 + lax.dot_general((ys_t * gate_t).astype(BF16), wo_ref[0:S5_WIDTH, :], _TN,
                              preferred_element_type=F32)
    h, r = _rms_split(x2, g2_ref[...])
    x3 = x2 + 0.5 * _swiglu(h, r, wg_ref, wu_ref, wd_ref)
    o_ref[...] = _rms_norm(x3, gf_ref[...]).reshape(TILE_SEQS, CHUNK, D_MODEL)


def _resident(shape):
    nd = len(shape)
    return pl.BlockSpec(shape, lambda *_: (0,) * nd, pipeline_mode=pl.Buffered(1))


def _tile_spec():
    return pl.BlockSpec((TILE_SEQS, CHUNK, D_MODEL), lambda c, h: (h, c, 0))


def _slab_spec(n_ch):
    return pl.BlockSpec((n_ch, SUBLANES, LANES), lambda c, h: (0, c, 0))


_DENSE_PARAMS = pltpu.CompilerParams(dimension_semantics=("arbitrary", "arbitrary"),
                                     vmem_limit_bytes=VMEM_LIMIT)
_MIX_PARAMS = pltpu.CompilerParams(dimension_semantics=("arbitrary",), vmem_limit_bytes=VMEM_LIMIT)


def _cast_spec(w, n_steps):
    n_blocks = n_steps
    while w.shape[0] % n_blocks or (w.shape[0] // n_blocks) % (2 * SUBLANES):
        n_blocks //= 2
    steps_per_block = n_steps // n_blocks
    return pl.BlockSpec((w.shape[0] // n_blocks, w.shape[1]), lambda c, h: (c // steps_per_block, 0))


def _ffn1_in_proj(x, g1, wg, wu, wd, gm, win_t, later_weights):
    batch, seq, _ = x.shape
    assert batch == SUBLANES and seq % CHUNK == 0
    n_rows = (seq // CHUNK) * batch
    cast_specs = [_cast_spec(w, seq // CHUNK) for w in later_weights]
    return pl.pallas_call(
        _ffn1_in_proj_kernel,
        grid=(seq // CHUNK, batch // TILE_SEQS),
        in_specs=[
            _tile_spec(),
            _resident((1, D_MODEL)),
            _resident((D_MODEL, D_FF)),
            _resident((D_MODEL, D_FF)),
            _resident((D_FF, D_MODEL)),
            _resident((1, D_MODEL)),
            _resident((IN_COLS, D_MODEL)),
        ] + cast_specs,
        out_specs=[_tile_spec(), _slab_spec(S5_WIDTH), _slab_spec(CONV_WIDTH)] + cast_specs,
        out_shape=[
            jax.ShapeDtypeStruct((batch, seq, D_MODEL), F32),
            jax.ShapeDtypeStruct((S5_WIDTH, n_rows, LANES), F32),
            jax.ShapeDtypeStruct((CONV_WIDTH, n_rows, LANES), F32),
        ] + [jax.ShapeDtypeStruct(w.shape, BF16) for w in later_weights],
        compiler_params=_DENSE_PARAMS,
        name="ffn1_in_proj",
    )(x, g1, wg, wu, wd, gm, win_t, *later_weights)


def _s5_mix(us, tables, *, batch):
    n_rows = us.shape[1]
    slab = pl.BlockSpec((S5_STEP_GROUPS * S5_GROUP_CH, n_rows, LANES), lambda g: (g, 0, 0))

    def per_group(t):
        nd = t.ndim - 1
        return pl.BlockSpec((S5_STEP_GROUPS,) + t.shape[1:], lambda g: (g,) + (0,) * nd)

    return pl.pallas_call(
        functools.partial(_s5_mix_kernel, batch=batch),
        grid=(S5_GROUPS // S5_STEP_GROUPS,),
        in_specs=[slab] + [per_group(t) for t in tables],
        out_specs=slab,
        out_shape=jax.ShapeDtypeStruct((S5_WIDTH, n_rows, LANES), F32),
        scratch_shapes=[pltpu.VMEM((S5_GROUP_CH * S5_GROUP_CH, 2 * S5_STATE), F32),
                        pltpu.VMEM((S5_GROUP_CH * S5_PAIRS, LANES), F32),
                        pltpu.VMEM((n_rows // batch, S5_HALVES * batch, S5_LANES), F32),
                        pltpu.VMEM((S5_HALVES * n_rows, S5_LANES), BF16),
                        pltpu.VMEM((S5_LANES, 4 * S5_STATE), BF16),
                        pltpu.VMEM((S5_LANES, 2 * S5_STATE), BF16),
                        pltpu.VMEM((S5_HALVES * n_rows, 2 * S5_STATE), F32)],
        compiler_params=_MIX_PARAMS,
        name="s5_mix",
    )(us, *tables)


def _conv_mix(zs, per_channel, *, batch):
    n_rows = zs.shape[1]
    slab = pl.BlockSpec((CONV_HEAD_DIM, n_rows, LANES), lambda h: (h, 0, 0))
    per_ch = pl.BlockSpec((CONV_HEAD_DIM, _PC_COUNT, LANES), lambda h: (h, 0, 0))
    return pl.pallas_call(
        functools.partial(_conv_mix_kernel, batch=batch),
        grid=(CONV_HEADS,),
        in_specs=[slab, per_ch],
        out_specs=slab,
        out_shape=jax.ShapeDtypeStruct((CONV_WIDTH, n_rows, LANES), F32),
        compiler_params=_MIX_PARAMS,
        name="conv_mix",
    )(zs, per_channel)


def _out_ffn2_norm(x1, ys, yc, wglu_t, bglu, wo, g2, wg, wu, wd, gf):
    batch, seq, _ = x1.shape
    return pl.pallas_call(
        _out_ffn2_norm_kernel,
        grid=(seq // CHUNK, batch // TILE_SEQS),
        in_specs=[
            _tile_spec(),
            _slab_spec(S5_WIDTH),
            _slab_spec(CONV_WIDTH),
            _resident((S5_WIDTH, S5_WIDTH)),
            _resident((S5_WIDTH, 1)),
            _resident((S5_WIDTH + CONV_WIDTH, D_MODEL)),
            _resident((1, D_MODEL)),
            _resident((D_MODEL, D_FF)),
            _resident((D_MODEL, D_FF)),
            _resident((D_FF, D_MODEL)),
            _resident((1, D_MODEL)),
        ],
        out_specs=_tile_spec(),
        out_shape=jax.ShapeDtypeStruct((batch, seq, D_MODEL), F32),
        compiler_params=_DENSE_PARAMS,
        name="out_ffn2_norm",
    )(x1, ys, yc, wglu_t, bglu, wo, g2, wg, wu, wd, gf)


def _s5_tables(lam_re, lam_im, log_dt, b_re, b_im, c_re, c_im, d_skip):
    T, C, P, G = S5_T, S5_GROUP_CH, S5_STATE, S5_GROUPS
    dt = jnp.exp(log_dt)[:, None]
    mag = jnp.exp(lam_re * dt)
    abar_re = mag * jnp.cos(lam_im * dt)
    abar_im = mag * jnp.sin(lam_im * dt)
    den = lam_re * lam_re + lam_im * lam_im
    num_re = abar_re - 1.0
    num_im = abar_im
    f_re = ((num_re * lam_re + num_im * lam_im) / den)[..., None]
    f_im = ((num_im * lam_re - num_re * lam_im) / den)[..., None]
    bb_re = f_re * b_re - f_im * b_im
    bb_im = f_re * b_im + f_im * b_re
    n_q = T // SUBLANES
    expo = jnp.concatenate([jnp.arange(SUBLANES), SUBLANES * jnp.arange(1, n_q + 1)]).astype(F32)[None, :, None]
    bmag = jnp.exp(expo * (lam_re * dt)[:, None, :])
    bang = expo * (lam_im * dt)[:, None, :]
    base_re, base_im = bmag * jnp.cos(bang), bmag * jnp.sin(bang)
    lo_re, lo_im = base_re[:, :SUBLANES], base_im[:, :SUBLANES]
    hi_re = jnp.concatenate([base_re[:, :1], base_re[:, SUBLANES:]], axis=1)
    hi_im = jnp.concatenate([base_im[:, :1], base_im[:, SUBLANES:]], axis=1)

    def outer(q_re, q_im, r_re, r_im):
        re = q_re[:, :, None] * r_re[:, None] - q_im[:, :, None] * r_im[:, None]
        im = q_re[:, :, None] * r_im[:, None] + q_im[:, :, None] * r_re[:, None]
        return jnp.concatenate([re, im], axis=-1).reshape(G, -1, 2 * P)

    cat = lambda a, b: jnp.concatenate([a, b], axis=-1)
    powers = jnp.concatenate([
        outer(hi_re[:, n_q - 1::-1], hi_im[:, n_q - 1::-1], lo_re[:, ::-1], lo_im[:, ::-1]),
        outer(hi_re, hi_im, lo_re, lo_im)], axis=1)
    bt_re = bb_re.transpose(0, 2, 1)
    bt_im = bb_im.transpose(0, 2, 1)
    d_rows = jnp.pad(jnp.repeat(d_skip.reshape(G, C // 2, 2), T, axis=-1), ((0, 0), (0, C // 2), (0, 0)))
    cp_re = jnp.concatenate([c_re[:, 0::2], c_re[:, 1::2]], axis=1)
    cp_im = jnp.concatenate([c_im[:, 0::2], c_im[:, 1::2]], axis=1)
    vecs = [None] * _V_COUNT
    vecs[_V_BR] = cat(bt_re, bt_re)
    vecs[_V_BI_NP] = cat(-bt_im, bt_im)
    vecs[_V_BI_PN] = cat(bt_im, -bt_im)
    vecs[_V_BI_PP] = cat(bt_im, bt_im)
    vecs[_V_CR_PN] = cat(c_re, -c_re)
    vecs[_V_CI_NN] = cat(-c_im, -c_im)
    vecs[_V_C_RE_NIM] = cat(cp_re, -cp_im)
    vecs[_V_C_NIM_NRE] = cat(-cp_im, -cp_re)
    vecs[_V_D] = d_rows
    return powers, jnp.stack(vecs, axis=1)


def _conv_tables(w_dw, b_dw, ln_g, ln_b):
    taps = jnp.pad(w_dw[::-1].T, ((0, 0), (0, LANES - CONV_K)))
    lanes = lambda v: jnp.broadcast_to(v[:, None], (CONV_WIDTH, LANES))
    rows = [None] * _PC_COUNT
    rows[_PC_TAPS], rows[_PC_BIAS] = taps, lanes(b_dw)
    rows[_PC_HALF_GAIN], rows[_PC_HALF_BETA] = lanes(0.5 * ln_g), lanes(0.5 * ln_b)
    return jnp.stack(rows, axis=1)


def kernel(x, ffn1_norm, ffn1_w_gate, ffn1_w_up, ffn1_w_down, mix_norm, w_in, s5_lam_re, s5_lam_im, s5_log_dt, s5_b_re, s5_b_im, s5_c_re, s5_c_im, s5_d, s5_w_glu, s5_b_glu, conv_w_dw, conv_b_dw, conv_ln_g, conv_ln_b, w_out, ffn2_norm, ffn2_w_gate, ffn2_w_up, ffn2_w_down, final_norm):
    batch, seq, d_model = x.shape
    assert ffn1_norm.shape[0] == 1 and d_model == D_MODEL
    l = 0
    row = lambda v: v.reshape(1, -1)

    x1, us, zs, wg2, wu2, wd2, wo = _ffn1_in_proj(
        x, row(ffn1_norm[l]), ffn1_w_gate[l].astype(BF16), ffn1_w_up[l].astype(BF16),
        ffn1_w_down[l].astype(BF16), row(mix_norm[l]), w_in[l].T.astype(BF16),
        (ffn2_w_gate[l], ffn2_w_up[l], ffn2_w_down[l], w_out[l]))

    tables = _s5_tables(
        s5_lam_re[l], s5_lam_im[l], s5_log_dt[l], s5_b_re[l], s5_b_im[l], s5_c_re[l], s5_c_im[l], s5_d[l])
    ys = _s5_mix(us, tables, batch=batch)

    yc = _conv_mix(zs, _conv_tables(conv_w_dw[l], conv_b_dw[l], conv_ln_g[l], conv_ln_b[l]), batch=batch)

    return _out_ffn2_norm(
        x1, ys, yc, s5_w_glu[l].T.astype(BF16), s5_b_glu[l].reshape(-1, 1), wo,
        row(ffn2_norm[l]), wg2, wu2, wd2, row(final_norm))
```

```python
import functools

import jax
import jax.numpy as jnp
from jax import lax
from jax.experimental import pallas as pl
from jax.experimental.pallas import tpu as pltpu

D_MODEL = 1024
D_FF = 2816
S5_WIDTH = 512
S5_GROUP_CH = 16
S5_GROUPS = S5_WIDTH // S5_GROUP_CH
S5_STATE = 64
CONV_WIDTH = 512
CONV_HEAD_DIM = 64
CONV_HEADS = CONV_WIDTH // CONV_HEAD_DIM
CONV_K = 31
IN_COLS = S5_WIDTH + 2 * CONV_WIDTH
EPS = 1e-6

LANES = 128
SUBLANES = 8
CHUNK = LANES
S5_T = 64
S5_HALVES = CHUNK // S5_T
S5_PAIRS = S5_GROUP_CH // 2
S5_LANES = S5_GROUP_CH * S5_T
S5_STEP_GROUPS = 2
TILE_SEQS = 8
ROW_TILE = TILE_SEQS * CHUNK
FF_SPLITS = ((0, 1024), (1024, 2048), (2048, 2816))
LN_ROWS = 64
LN_UNROLL = 16
VMEM_LIMIT = 60 * 1024 * 1024

BF16 = jnp.bfloat16
F32 = jnp.float32

(_V_BR,
 _V_BI_NP,
 _V_BI_PN,
 _V_BI_PP,
 _V_CR_PN,
 _V_CI_NN,
 _V_C_RE_NIM,
 _V_C_NIM_NRE,
 _V_D,
 _V_COUNT) = range(10)

_PC_TAPS, _PC_BIAS, _PC_HALF_GAIN, _PC_HALF_BETA, _PC_COUNT = range(5)

_NT = (((1,), (1,)), ((), ()))
_TN = (((0,), (0,)), ((), ()))


def _rms_norm(x, g):
    return x * lax.rsqrt(jnp.mean(x * x, axis=-1, keepdims=True) + EPS) * g


def _dot(a, b):
    return jnp.dot(a, b, preferred_element_type=F32)


def _rms_split(x, g):
    return (x * g).astype(BF16), lax.rsqrt(jnp.mean(x * x, axis=-1, keepdims=True) + EPS)


def _swiglu(h, row_scale, wg_ref, wu_ref, wd_ref):
    acc = None
    for lo, hi in FF_SPLITS:
        g = _dot(h, wg_ref[:, lo:hi]) * row_scale
        u = _dot(h, wu_ref[:, lo:hi]) * row_scale
        a = (g * jax.nn.sigmoid(g) * u).astype(BF16)
        part = _dot(a, wd_ref[lo:hi, :])
        acc = part if acc is None else acc + part
    return acc


def _store_slabs(slab_ref, val, first_seq):
    n_ch = slab_ref.shape[0]
    rows = slab_ref.reshape(n_ch * SUBLANES, LANES)
    for ct in range(n_ch // SUBLANES):
        for s in range(TILE_SEQS):
            rows[pl.ds(ct * SUBLANES * SUBLANES + first_seq + s, SUBLANES, stride=SUBLANES), :] = (
                val[ct * SUBLANES:(ct + 1) * SUBLANES, s * LANES:(s + 1) * LANES])


def _load_slabs(slab_ref, first_seq):
    n_ch = slab_ref.shape[0]
    rows = slab_ref.reshape(n_ch * SUBLANES, LANES)
    bands = []
    for ct in range(n_ch // SUBLANES):
        bands.append(jnp.concatenate(
            [rows[pl.ds(ct * SUBLANES * SUBLANES + first_seq + s, SUBLANES, stride=SUBLANES), :]
             for s in range(TILE_SEQS)], axis=1))
    return jnp.concatenate(bands, axis=0)


def _ffn1_in_proj_kernel(x_ref, g1_ref, wg_ref, wu_ref, wd_ref, gm_ref, win_t_ref, *rest):
    n_cast = (len(rest) - 3) // 2
    x1_ref, us_ref, zs_ref = rest[n_cast:n_cast + 3]
    first_seq = pl.program_id(1) * TILE_SEQS
    x = x_ref[...].reshape(ROW_TILE, D_MODEL)
    h, r = _rms_split(x, g1_ref[...])
    x1 = x + 0.5 * _swiglu(h, r, wg_ref, wu_ref, wd_ref)
    x1_ref[...] = x1.reshape(TILE_SEQS, CHUNK, D_MODEL)
    h2, r2 = _rms_split(x1, gm_ref[...])
    r2_lanes = jnp.transpose(jnp.broadcast_to(r2, (ROW_TILE, LANES)))[0:1, :]

    def in_proj_t(lo, hi):
        return lax.dot_general(win_t_ref[lo:hi, :], h2, _NT, preferred_element_type=F32) * r2_lanes

    gate = jax.nn.sigmoid(in_proj_t(S5_WIDTH + CONV_WIDTH, IN_COLS))
    _store_slabs(us_ref, in_proj_t(0, S5_WIDTH), first_seq)
    z_t = in_proj_t(S5_WIDTH, S5_WIDTH + CONV_WIDTH) * gate
    _store_slabs(zs_ref, z_t, first_seq)
    for src_ref, dst_ref in zip(rest[:n_cast], rest[n_cast + 3:]):
        dst_ref[...] = src_ref[...].astype(BF16)


def _regroup_halves(a, b):
    first_half = lax.broadcasted_iota(jnp.int32, (1, LANES), 1) < S5_T
    straight = jnp.where(first_half, a, b)
    crossed = pltpu.roll(jnp.where(first_half, b, a), S5_T, 1)
    return jnp.where(first_half, straight, crossed), jnp.where(first_half, crossed, straight)


def _s5_mix_kernel(us_ref, pow_ref, vec_ref, ys_ref, *scratch, batch):
    for g in range(S5_STEP_GROUPS):
        ch = pl.ds(g * S5_GROUP_CH, S5_GROUP_CH)
        _s5_group(us_ref.at[ch], pow_ref.at[g], vec_ref.at[g], ys_ref.at[ch], *scratch, batch=batch)


def _s5_group(us_ref, pow_ref, vec_ref, ys_ref,
              cb_ref, f_ref, upk_ref, lhs_ref, win_ref, wout_ref, sprev_ref, *, batch):
    T = S5_T
    n_slab_rows = us_ref.shape[1]
    n_slab_chunks = n_slab_rows // batch
    n_rows = S5_HALVES * n_slab_rows
    lane = lax.broadcasted_iota(jnp.int32, (1, LANES), 1)
    first_state_half = lane < S5_STATE
    swap = lambda v: pltpu.roll(v, S5_STATE, 1)
    pa_rev = pow_ref[0:T, :]
    pb_rev = swap(pa_rev)
    pa_fwd = pow_ref[T + 1:2 * T + 1, :]
    pb_fwd = swap(pa_fwd)
    for c in range(S5_GROUP_CH):
        rows = slice(c * T, (c + 1) * T)
        vec = lambda i: vec_ref[i, c:c + 1, :]
        win_ref[rows, 0:2 * S5_STATE] = (pa_rev * vec(_V_BR) + pb_rev * vec(_V_BI_NP)).astype(BF16)
        win_ref[rows, 2 * S5_STATE:] = (pb_rev * vec(_V_BR) + pa_rev * vec(_V_BI_PN)).astype(BF16)
        wout_ref[rows, :] = (pa_fwd * vec(_V_CR_PN) + pb_fwd * vec(_V_CI_NN)).astype(BF16)
        cb_ref[c * S5_GROUP_CH:(c + 1) * S5_GROUP_CH, :] = (
            vec_ref[_V_C_RE_NIM] * vec(_V_BR) + vec_ref[_V_C_NIM_NRE] * vec(_V_BI_PP))
    f = lax.dot_general(cb_ref[...], pow_ref[T:2 * T, :], _NT, precision=lax.Precision.HIGHEST,
                        preferred_element_type=F32)
    for c in range(S5_GROUP_CH):
        even, odd = c * S5_GROUP_CH, c * S5_GROUP_CH + S5_PAIRS
        f_ref[c * S5_PAIRS:(c + 1) * S5_PAIRS, :] = jnp.concatenate(
            [f[even:even + S5_PAIRS], f[odd:odd + S5_PAIRS]], axis=1)

    for p in range(S5_PAIRS):
        halves = _regroup_halves(us_ref[2 * p], us_ref[2 * p + 1])
        for hh in range(S5_HALVES):
            upk_ref[:, hh * batch:(hh + 1) * batch, p * LANES:(p + 1) * LANES] = (
                halves[hh].reshape(n_slab_chunks, batch, LANES))
    lhs_ref[...] = upk_ref[...].reshape(n_rows, S5_LANES).astype(BF16)

    e = _dot(lhs_ref[...], win_ref[...])
    a_chunk = pow_ref[2 * T:2 * T + 1, :]
    a_sw = swap(a_chunk)
    a1 = jnp.where(first_state_half, a_chunk, a_sw)
    a2 = jnp.where(first_state_half, -a_sw, a_chunk)
    s = jnp.zeros((batch, 2 * S5_STATE), F32)
    s_sw = jnp.zeros((batch, 2 * S5_STATE), F32)
    for c in range(n_rows // batch):
        rows = slice(c * batch, (c + 1) * batch)
        sprev_ref[rows, :] = s
        e_c = e[rows, :]
        s, s_sw = (a1 * s + a2 * s_sw + e_c[:, :2 * S5_STATE],
                   a1 * s_sw - a2 * s + e_c[:, 2 * S5_STATE:])
    sprev = sprev_ref[...].astype(BF16)

    causal = ((lax.broadcasted_iota(jnp.int32, (T, LANES), 1) & (T - 1))
              >= lax.broadcasted_iota(jnp.int32, (T, LANES), 0))

    def toeplitz(ci, pair):
        f_row = jnp.broadcast_to(f_ref[ci * S5_PAIRS + pair:ci * S5_PAIRS + pair + 1, :], (T, LANES))
        return jnp.where(causal, pltpu.roll(f_row, 0, 1, stride=1, stride_axis=0), 0.0).astype(BF16)

    ci_per_tile = 2 * LANES // T
    for n in range(S5_PAIRS // 2):
        acc = lax.dot_general(sprev, wout_ref[2 * n * LANES:(2 * n + 2) * LANES, :], _NT,
                              preferred_element_type=F32)
        for kt in range(S5_GROUP_CH // ci_per_tile):
            w = jnp.concatenate(
                [jnp.concatenate([toeplitz(ci_per_tile * kt + r, 2 * n), toeplitz(ci_per_tile * kt + r, 2 * n + 1)],
                                 axis=1) for r in range(ci_per_tile)], axis=0)
            acc = acc + _dot(lhs_ref[:, 2 * kt * LANES:(2 * kt + 2) * LANES], w)
        for q in range(2):
            pair = 2 * n + q
            u_pk = upk_ref[:, :, pair * LANES:(pair + 1) * LANES].reshape(n_rows, LANES)
            y = acc[:, q * LANES:(q + 1) * LANES] + vec_ref[_V_D, pair:pair + 1, :] * u_pk
            y = jax.nn.gelu(y, approximate=True).reshape(n_slab_chunks, S5_HALVES * batch, LANES)
            y0 = y[:, 0:batch, :].reshape(n_slab_rows, LANES)
            y1 = y[:, batch:2 * batch, :].reshape(n_slab_rows, LANES)
            ys_ref[2 * pair], ys_ref[2 * pair + 1] = _regroup_halves(y0, y1)


def _conv_mix_kernel(zs_ref, pc_ref, o_ref, *, batch):
    n_rows = zs_ref.shape[1]
    row = lax.broadcasted_iota(jnp.int32, (CHUNK, CHUNK), 0)
    col = lax.broadcasted_iota(jnp.int32, (CHUNK, CHUNK), 1)
    lag = row - col
    band_cur = (lag >= -(CONV_K - 1)) & (lag <= 0)
    band_prev = lag >= CHUNK - (CONV_K - 1)

    def conv_channel(ch, carry):
        z = zs_ref[ch]
        z_prev = jnp.concatenate([jnp.zeros((batch, CHUNK), F32), z[:n_rows - batch]], axis=0)
        lhs = jnp.concatenate([z_prev, z], axis=1).astype(BF16)
        taps = jnp.broadcast_to(pc_ref[ch, _PC_TAPS:_PC_TAPS + 1, :], (CHUNK, CHUNK))
        diag = pltpu.roll(taps, 0, 1, stride=1, stride_axis=0)
        toe = jnp.concatenate([jnp.where(band_prev, diag, 0.0), jnp.where(band_cur, diag, 0.0)],
                              axis=0).astype(BF16)
        o_ref[ch] = _dot(lhs, toe) + pc_ref[ch, _PC_BIAS:_PC_BIAS + 1, :]
        return carry

    lax.fori_loop(0, CONV_HEAD_DIM, conv_channel, 0, unroll=32)

    inv_n = 1.0 / CONV_HEAD_DIM
    for r0 in range(0, n_rows, LN_ROWS):
        rows = pl.ds(r0, LN_ROWS)
        zero = jnp.zeros((LN_ROWS, CHUNK), F32)
        mu = lax.fori_loop(0, CONV_HEAD_DIM, lambda ch, acc: acc + o_ref[ch, rows, :], zero,
                           unroll=LN_UNROLL) * inv_n
        var = lax.fori_loop(0, CONV_HEAD_DIM, lambda ch, acc: acc + jnp.square(o_ref[ch, rows, :] - mu),
                            zero, unroll=LN_UNROLL) * inv_n
        rstd = lax.rsqrt(var + EPS)

        def norm_channel(ch, carry):
            half = ((o_ref[ch, rows, :] - mu) * rstd * pc_ref[ch, _PC_HALF_GAIN:_PC_HALF_GAIN + 1, :]
                    + pc_ref[ch, _PC_HALF_BETA:_PC_HALF_BETA + 1, :])
            o_ref[ch, rows, :] = half + half * jnp.tanh(half)
            return carry

        lax.fori_loop(0, CONV_HEAD_DIM, norm_channel, 0, unroll=LN_UNROLL)


def _out_ffn2_norm_kernel(x1_ref, ys_ref, yc_ref, wglu_t_ref, bglu_ref, wo_ref,
                          g2_ref, wg_ref, wu_ref, wd_ref, gf_ref, o_ref):
    first_seq = pl.program_id(1) * TILE_SEQS
    ys_t = _load_slabs(ys_ref, first_seq)
    yc_t = _load_slabs(yc_ref, first_seq)
    x2 = x1_ref[...].reshape(ROW_TILE, D_MODEL) + lax.dot_general(
        yc_t.astype(BF16), wo_ref[S5_WIDTH:, :], _TN, preferred_element_type=F32)
    gate_t = jax.nn.sigmoid(_dot(wglu_t_ref[...], ys_t.astype(BF16)) + bglu_ref[...])
    x2 = x2 + lax.dot_general((ys_t * gate_t).astype(BF16), wo_ref[0:S5_WIDTH, :], _TN,
                              preferred_element_type=F32)
    h, r = _rms_split(x2, g2_ref[...])
    x3 = x2 + 0.5 * _swiglu(h, r, wg_ref, wu_ref, wd_ref)
    o_ref[...] = _rms_norm(x3, gf_ref[...]).reshape(TILE_SEQS, CHUNK, D_MODEL)


def _resident(shape):
    nd = len(shape)
    return pl.BlockSpec(shape, lambda *_: (0,) * nd, pipeline_mode=pl.Buffered(1))


def _tile_spec():
    return pl.BlockSpec((TILE_SEQS, CHUNK, D_MODEL), lambda c, h: (h, c, 0))


def _slab_spec(n_ch):
    return pl.BlockSpec((n_ch, SUBLANES, LANES), lambda c, h: (0, c, 0))


_DENSE_PARAMS = pltpu.CompilerParams(dimension_semantics=("arbitrary", "arbitrary"),
                                     vmem_limit_bytes=VMEM_LIMIT)
_MIX_PARAMS = pltpu.CompilerParams(dimension_semantics=("arbitrary",), vmem_limit_bytes=VMEM_LIMIT)


def _cast_spec(w, n_steps):
    n_blocks = n_steps
    while w.shape[0] % n_blocks or (w.shape[0] // n_blocks) % (2 * SUBLANES):
        n_blocks //= 2
    steps_per_block = n_steps // n_blocks
    return pl.BlockSpec((w.shape[0] // n_blocks, w.shape[1]), lambda c, h: (c // steps_per_block, 0))


def _ffn1_in_proj(x, g1, wg, wu, wd, gm, win_t, later_weights):
    batch, seq, _ = x.shape
    assert batch == SUBLANES and seq % CHUNK == 0
    n_rows = (seq // CHUNK) * batch
    cast_specs = [_cast_spec(w, seq // CHUNK) for w in later_weights]
    return pl.pallas_call(
        _ffn1_in_proj_kernel,
        grid=(seq // CHUNK, batch // TILE_SEQS),
        in_specs=[
            _tile_spec(),
            _resident((1, D_MODEL)),
            _resident((D_MODEL, D_FF)),
            _resident((D_MODEL, D_FF)),
            _resident((D_FF, D_MODEL)),
            _resident((1, D_MODEL)),
            _resident((IN_COLS, D_MODEL)),
        ] + cast_specs,
        out_specs=[_tile_spec(), _slab_spec(S5_WIDTH), _slab_spec(CONV_WIDTH)] + cast_specs,
        out_shape=[
            jax.ShapeDtypeStruct((batch, seq, D_MODEL), F32),
            jax.ShapeDtypeStruct((S5_WIDTH, n_rows, LANES), F32),
            jax.ShapeDtypeStruct((CONV_WIDTH, n_rows, LANES), F32),
        ] + [jax.ShapeDtypeStruct(w.shape, BF16) for w in later_weights],
        compiler_params=_DENSE_PARAMS,
        name="ffn1_in_proj",
    )(x, g1, wg, wu, wd, gm, win_t, *later_weights)


def _s5_mix(us, tables, *, batch):
    n_rows = us.shape[1]
    slab = pl.BlockSpec((S5_STEP_GROUPS * S5_GROUP_CH, n_rows, LANES), lambda g: (g, 0, 0))

    def per_group(t):
        nd = t.ndim - 1
        return pl.BlockSpec((S5_STEP_GROUPS,) + t.shape[1:], lambda g: (g,) + (0,) * nd)

    return pl.pallas_call(
        functools.partial(_s5_mix_kernel, batch=batch),
        grid=(S5_GROUPS // S5_STEP_GROUPS,),
        in_specs=[slab] + [per_group(t) for t in tables],
        out_specs=slab,
        out_shape=jax.ShapeDtypeStruct((S5_WIDTH, n_rows, LANES), F32),
        scratch_shapes=[pltpu.VMEM((S5_GROUP_CH * S5_GROUP_CH, 2 * S5_STATE), F32),
                        pltpu.VMEM((S5_GROUP_CH * S5_PAIRS, LANES), F32),
                        pltpu.VMEM((n_rows // batch, S5_HALVES * batch, S5_LANES), F32),
                        pltpu.VMEM((S5_HALVES * n_rows, S5_LANES), BF16),
                        pltpu.VMEM((S5_LANES, 4 * S5_STATE), BF16),
                        pltpu.VMEM((S5_LANES, 2 * S5_STATE), BF16),
                        pltpu.VMEM((S5_HALVES * n_rows, 2 * S5_STATE), F32)],
        compiler_params=_MIX_PARAMS,
        name="s5_mix",
    )(us, *tables)


def _conv_mix(zs, per_channel, *, batch):
    n_rows = zs.shape[1]
    slab = pl.BlockSpec((CONV_HEAD_DIM, n_rows, LANES), lambda h: (h, 0, 0))
    per_ch = pl.BlockSpec((CONV_HEAD_DIM, _PC_COUNT, LANES), lambda h: (h, 0, 0))
    return pl.pallas_call(
        functools.partial(_conv_mix_kernel, batch=batch),
        grid=(CONV_HEADS,),
        in_specs=[slab, per_ch],
        out_specs=slab,
        out_shape=jax.ShapeDtypeStruct((CONV_WIDTH, n_rows, LANES), F32),
        compiler_params=_MIX_PARAMS,
        name="conv_mix",
    )(zs, per_channel)


def _out_ffn2_norm(x1, ys, yc, wglu_t, bglu, wo, g2, wg, wu, wd, gf):
    batch, seq, _ = x1.shape
    return pl.pallas_call(
        _out_ffn2_norm_kernel,
        grid=(seq // CHUNK, batch // TILE_SEQS),
        in_specs=[
            _tile_spec(),
            _slab_spec(S5_WIDTH),
            _slab_spec(CONV_WIDTH),
            _resident((S5_WIDTH, S5_WIDTH)),
            _resident((S5_WIDTH, 1)),
            _resident((S5_WIDTH + CONV_WIDTH, D_MODEL)),
            _resident((1, D_MODEL)),
            _resident((D_MODEL, D_FF)),
            _resident((D_MODEL, D_FF)),
            _resident((D_FF, D_MODEL)),
            _resident((1, D_MODEL)),
        ],
        out_specs=_tile_spec(),
        out_shape=jax.ShapeDtypeStruct((batch, seq, D_MODEL), F32),
        compiler_params=_DENSE_PARAMS,
        name="out_ffn2_norm",
    )(x1, ys, yc, wglu_t, bglu, wo, g2, wg, wu, wd, gf)


def _s5_tables(lam_re, lam_im, log_dt, b_re, b_im, c_re, c_im, d_skip):
    T, C, P, G = S5_T, S5_GROUP_CH, S5_STATE, S5_GROUPS
    dt = jnp.exp(log_dt)[:, None]
    mag = jnp.exp(lam_re * dt)
    abar_re = mag * jnp.cos(lam_im * dt)
    abar_im = mag * jnp.sin(lam_im * dt)
    den = lam_re * lam_re + lam_im * lam_im
    num_re = abar_re - 1.0
    num_im = abar_im
    f_re = ((num_re * lam_re + num_im * lam_im) / den)[..., None]
    f_im = ((num_im * lam_re - num_re * lam_im) / den)[..., None]
    bb_re = f_re * b_re - f_im * b_im
    bb_im = f_re * b_im + f_im * b_re
    n_q = T // SUBLANES
    expo = jnp.concatenate([jnp.arange(SUBLANES), SUBLANES * jnp.arange(1, n_q + 1)]).astype(F32)[None, :, None]
    bmag = jnp.exp(expo * (lam_re * dt)[:, None, :])
    bang = expo * (lam_im * dt)[:, None, :]
    base_re, base_im = bmag * jnp.cos(bang), bmag * jnp.sin(bang)
    lo_re, lo_im = base_re[:, :SUBLANES], base_im[:, :SUBLANES]
    hi_re = jnp.concatenate([base_re[:, :1], base_re[:, SUBLANES:]], axis=1)
    hi_im = jnp.concatenate([base_im[:, :1], base_im[:, SUBLANES:]], axis=1)

    def outer(q_re, q_im, r_re, r_im):
        re = q_re[:, :, None] * r_re[:, None] - q_im[:, :, None] * r_im[:, None]
        im = q_re[:, :, None] * r_im[:, None] + q_im[:, :, None] * r_re[:, None]
        return jnp.concatenate([re, im], axis=-1).reshape(G, -1, 2 * P)

    cat = lambda a, b: jnp.concatenate([a, b], axis=-1)
    powers = jnp.concatenate([
        outer(hi_re[:, n_q - 1::-1], hi_im[:, n_q - 1::-1], lo_re[:, ::-1], lo_im[:, ::-1]),
        outer(hi_re, hi_im, lo_re, lo_im)], axis=1)
    bt_re = bb_re.transpose(0, 2, 1)
    bt_im = bb_im.transpose(0, 2, 1)
    d_rows = jnp.pad(jnp.repeat(d_skip.reshape(G, C // 2, 2), T, axis=-1), ((0, 0), (0, C // 2), (0, 0)))
    cp_re = jnp.concatenate([c_re[:, 0::2], c_re[:, 1::2]], axis=1)
    cp_im = jnp.concatenate([c_im[:, 0::2], c_im[:, 1::2]], axis=1)
    vecs = [None] * _V_COUNT
    vecs[_V_BR] = cat(bt_re, bt_re)
    vecs[_V_BI_NP] = cat(-bt_im, bt_im)
    vecs[_V_BI_PN] = cat(bt_im, -bt_im)
    vecs[_V_BI_PP] = cat(bt_im, bt_im)
    vecs[_V_CR_PN] = cat(c_re, -c_re)
    vecs[_V_CI_NN] = cat(-c_im, -c_im)
    vecs[_V_C_RE_NIM] = cat(cp_re, -cp_im)
    vecs[_V_C_NIM_NRE] = cat(-cp_im, -cp_re)
    vecs[_V_D] = d_rows
    return powers, jnp.stack(vecs, axis=1)


def _conv_tables(w_dw, b_dw, ln_g, ln_b):
    taps = jnp.pad(w_dw[::-1].T, ((0, 0), (0, LANES - CONV_K)))
    lanes = lambda v: jnp.broadcast_to(v[:, None], (CONV_WIDTH, LANES))
    rows = [None] * _PC_COUNT
    rows[_PC_TAPS], rows[_PC_BIAS] = taps, lanes(b_dw)
    rows[_PC_HALF_GAIN], rows[_PC_HALF_BETA] = lanes(0.5 * ln_g), lanes(0.5 * ln_b)
    return jnp.stack(rows, axis=1)


def kernel(x, ffn1_norm, ffn1_w_gate, ffn1_w_up, ffn1_w_down, mix_norm, w_in, s5_lam_re, s5_lam_im, s5_log_dt, s5_b_re, s5_b_im, s5_c_re, s5_c_im, s5_d, s5_w_glu, s5_b_glu, conv_w_dw, conv_b_dw, conv_ln_g, conv_ln_b, w_out, ffn2_norm, ffn2_w_gate, ffn2_w_up, ffn2_w_down, final_norm):
    batch, seq, d_model = x.shape
    assert ffn1_norm.shape[0] == 1 and d_model == D_MODEL
    l = 0
    row = lambda v: v.reshape(1, -1)

    x1, us, zs, wg2, wu2, wd2, wo = _ffn1_in_proj(
        x, row(ffn1_norm[l]), ffn1_w_gate[l].astype(BF16), ffn1_w_up[l].astype(BF16),
        ffn1_w_down[l].astype(BF16), row(mix_norm[l]), w_in[l].T.astype(BF16),
        (ffn2_w_gate[l], ffn2_w_up[l], ffn2_w_down[l], w_out[l]))

    tables = _s5_tables(
        s5_lam_re[l], s5_lam_im[l], s5_log_dt[l], s5_b_re[l], s5_b_im[l], s5_c_re[l], s5_c_im[l], s5_d[l])
    ys = _s5_mix(us, tables, batch=batch)

    yc = _conv_mix(zs, _conv_tables(conv_w_dw[l], conv_b_dw[l], conv_ln_g[l], conv_ln_b[l]), batch=batch)

    return _out_ffn2_norm(
        x1, ys, yc, s5_w_glu[l].T.astype(BF16), s5_b_glu[l].reshape(-1, 1), wo,
        row(ffn2_norm[l]), wg2, wu2, wd2, row(final_norm))
```

```python
import functools

import jax
import jax.numpy as jnp
from jax import lax
from jax.experimental import pallas as pl
from jax.experimental.pallas import tpu as pltpu

D_MODEL = 1024
D_FF = 2816
S5_WIDTH = 512
S5_GROUP_CH = 16
S5_GROUPS = S5_WIDTH // S5_GROUP_CH
S5_STATE = 64
CONV_WIDTH = 512
CONV_HEAD_DIM = 64
CONV_HEADS = CONV_WIDTH // CONV_HEAD_DIM
CONV_K = 31
IN_COLS = S5_WIDTH + 2 * CONV_WIDTH
EPS = 1e-6

LANES = 128
SUBLANES = 8
CHUNK = LANES
S5_T = 64
S5_HALVES = CHUNK // S5_T
S5_PAIRS = S5_GROUP_CH // 2
S5_LANES = S5_GROUP_CH * S5_T
S5_STEP_GROUPS = 2
TILE_SEQS = 8
ROW_TILE = TILE_SEQS * CHUNK
FF_SPLITS = ((0, 1536), (1536, 2816))
LN_ROWS = 64
LN_UNROLL = 16
VMEM_LIMIT = 60 * 1024 * 1024

BF16 = jnp.bfloat16
F32 = jnp.float32

(_V_BR,
 _V_BI_NP,
 _V_BI_PN,
 _V_BI_PP,
 _V_CR_PN,
 _V_CI_NN,
 _V_C_RE_NIM,
 _V_C_NIM_NRE,
 _V_D,
 _V_COUNT) = range(10)

_PC_TAPS, _PC_BIAS, _PC_HALF_GAIN, _PC_HALF_BETA, _PC_COUNT = range(5)

_NT = (((1,), (1,)), ((), ()))
_TN = (((0,), (0,)), ((), ()))


def _rms_norm(x, g):
    return x * lax.rsqrt(jnp.mean(x * x, axis=-1, keepdims=True) + EPS) * g


def _dot(a, b):
    return jnp.dot(a, b, preferred_element_type=F32)


def _rms_split(x, g):
    return (x * g).astype(BF16), lax.rsqrt(jnp.mean(x * x, axis=-1, keepdims=True) + EPS)


def _swiglu(h, row_scale, wg_ref, wu_ref, wd_ref):
    acc = None
    for lo, hi in FF_SPLITS:
        g = _dot(h, wg_ref[:, lo:hi]) * row_scale
        u = _dot(h, wu_ref[:, lo:hi]) * row_scale
        a = (g * jax.nn.sigmoid(g) * u).astype(BF16)
        part = _dot(a, wd_ref[lo:hi, :])
        acc = part if acc is None else acc + part
    return acc


def _store_slabs(slab_ref, val, first_seq):
    n_ch = slab_ref.shape[0]
    rows = slab_ref.reshape(n_ch * SUBLANES, LANES)
    for ct in range(n_ch // SUBLANES):
        for s in range(TILE_SEQS):
            rows[pl.ds(ct * SUBLANES * SUBLANES + first_seq + s, SUBLANES, stride=SUBLANES), :] = (
                val[ct * SUBLANES:(ct + 1) * SUBLANES, s * LANES:(s + 1) * LANES])


def _load_slabs(slab_ref, first_seq):
    n_ch = slab_ref.shape[0]
    rows = slab_ref.reshape(n_ch * SUBLANES, LANES)
    bands = []
    for ct in range(n_ch // SUBLANES):
        bands.append(jnp.concatenate(
            [rows[pl.ds(ct * SUBLANES * SUBLANES + first_seq + s, SUBLANES, stride=SUBLANES), :]
             for s in range(TILE_SEQS)], axis=1))
    return jnp.concatenate(bands, axis=0)


def _ffn1_in_proj_kernel(x_ref, g1_ref, wg_ref, wu_ref, wd_ref, gm_ref, win_t_ref,
                         x1_ref, us_ref, zs_ref):
    first_seq = pl.program_id(1) * TILE_SEQS
    x = x_ref[...].reshape(ROW_TILE, D_MODEL)
    h, r = _rms_split(x, g1_ref[...])
    x1 = x + 0.5 * _swiglu(h, r, wg_ref, wu_ref, wd_ref)
    x1_ref[...] = x1.reshape(TILE_SEQS, CHUNK, D_MODEL)
    h2, r2 = _rms_split(x1, gm_ref[...])
    r2_lanes = jnp.transpose(jnp.broadcast_to(r2, (ROW_TILE, LANES)))[0:1, :]

    def in_proj_t(lo, hi):
        return lax.dot_general(win_t_ref[lo:hi, :], h2, _NT, preferred_element_type=F32) * r2_lanes

    gate = jax.nn.sigmoid(in_proj_t(S5_WIDTH + CONV_WIDTH, IN_COLS))
    _store_slabs(us_ref, in_proj_t(0, S5_WIDTH), first_seq)
    z_t = in_proj_t(S5_WIDTH, S5_WIDTH + CONV_WIDTH) * gate
    _store_slabs(zs_ref, z_t, first_seq)


def _regroup_halves(a, b):
    first_half = lax.broadcasted_iota(jnp.int32, (1, LANES), 1) < S5_T
    straight = jnp.where(first_half, a, b)
    crossed = pltpu.roll(jnp.where(first_half, b, a), S5_T, 1)
    return jnp.where(first_half, straight, crossed), jnp.where(first_half, crossed, straight)


def _s5_mix_kernel(us_ref, pow_ref, vec_ref, *rest, n_cast, batch):
    ys_ref = rest[n_cast]
    scratch = rest[2 * n_cast + 1:]
    for src_ref, dst_ref in zip(rest[:n_cast], rest[n_cast + 1:2 * n_cast + 1]):
        dst_ref[...] = src_ref[...].astype(BF16)
    for g in range(S5_STEP_GROUPS):
        ch = pl.ds(g * S5_GROUP_CH, S5_GROUP_CH)
        _s5_group(us_ref.at[ch], pow_ref.at[g], vec_ref.at[g], ys_ref.at[ch], *scratch, batch=batch)


def _s5_group(us_ref, pow_ref, vec_ref, ys_ref,
              cb_ref, f_ref, upk_ref, lhs_ref, win_ref, wout_ref, sprev_ref, *, batch):
    T = S5_T
    n_slab_rows = us_ref.shape[1]
    n_slab_chunks = n_slab_rows // batch
    n_rows = S5_HALVES * n_slab_rows
    lane = lax.broadcasted_iota(jnp.int32, (1, LANES), 1)
    first_state_half = lane < S5_STATE
    swap = lambda v: pltpu.roll(v, S5_STATE, 1)
    pa_rev = pow_ref[0:T, :]
    pb_rev = swap(pa_rev)
    pa_fwd = pow_ref[T + 1:2 * T + 1, :]
    pb_fwd = swap(pa_fwd)
    for c in range(S5_GROUP_CH):
        rows = slice(c * T, (c + 1) * T)
        vec = lambda i: vec_ref[i, c:c + 1, :]
        win_ref[rows, 0:2 * S5_STATE] = (pa_rev * vec(_V_BR) + pb_rev * vec(_V_BI_NP)).astype(BF16)
        win_ref[rows, 2 * S5_STATE:] = (pb_rev * vec(_V_BR) + pa_rev * vec(_V_BI_PN)).astype(BF16)
        wout_ref[rows, :] = (pa_fwd * vec(_V_CR_PN) + pb_fwd * vec(_V_CI_NN)).astype(BF16)
        cb_ref[c * S5_GROUP_CH:(c + 1) * S5_GROUP_CH, :] = (
            vec_ref[_V_C_RE_NIM] * vec(_V_BR) + vec_ref[_V_C_NIM_NRE] * vec(_V_BI_PP))
    f = lax.dot_general(cb_ref[...], pow_ref[T:2 * T, :], _NT, precision=lax.Precision.HIGHEST,
                        preferred_element_type=F32)
    for c in range(S5_GROUP_CH):
        even, odd = c * S5_GROUP_CH, c * S5_GROUP_CH + S5_PAIRS
        f_ref[c * S5_PAIRS:(c + 1) * S5_PAIRS, :] = jnp.concatenate(
            [f[even:even + S5_PAIRS], f[odd:odd + S5_PAIRS]], axis=1)

    for p in range(S5_PAIRS):
        halves = _regroup_halves(us_ref[2 * p], us_ref[2 * p + 1])
        for hh in range(S5_HALVES):
            upk_ref[:, hh * batch:(hh + 1) * batch, p * LANES:(p + 1) * LANES] = (
                halves[hh].reshape(n_slab_chunks, batch, LANES))
    lhs_ref[...] = upk_ref[...].reshape(n_rows, S5_LANES).astype(BF16)

    e = _dot(lhs_ref[...], win_ref[...])
    a_chunk = pow_ref[2 * T:2 * T + 1, :]
    a_sw = swap(a_chunk)
    a1 = jnp.where(first_state_half, a_chunk, a_sw)
    a2 = jnp.where(first_state_half, -a_sw, a_chunk)
    s = jnp.zeros((batch, 2 * S5_STATE), F32)
    s_sw = jnp.zeros((batch, 2 * S5_STATE), F32)
    for c in range(n_rows // batch):
        rows = slice(c * batch, (c + 1) * batch)
        sprev_ref[rows, :] = s
        e_c = e[rows, :]
        s, s_sw = (a1 * s + a2 * s_sw + e_c[:, :2 * S5_STATE],
                   a1 * s_sw - a2 * s + e_c[:, 2 * S5_STATE:])
    sprev = sprev_ref[...].astype(BF16)

    causal = ((lax.broadcasted_iota(jnp.int32, (T, LANES), 1) & (T - 1))
              >= lax.broadcasted_iota(jnp.int32, (T, LANES), 0))

    def toeplitz(ci, pair):
        f_row = jnp.broadcast_to(f_ref[ci * S5_PAIRS + pair:ci * S5_PAIRS + pair + 1, :], (T, LANES))
        return jnp.where(causal, pltpu.roll(f_row, 0, 1, stride=1, stride_axis=0), 0.0).astype(BF16)

    ci_per_tile = 2 * LANES // T
    for n in range(S5_PAIRS // 2):
        acc = lax.dot_general(sprev, wout_ref[2 * n * LANES:(2 * n + 2) * LANES, :], _NT,
                              preferred_element_type=F32)
        for kt in range(S5_GROUP_CH // ci_per_tile):
            w = jnp.concatenate(
                [jnp.concatenate([toeplitz(ci_per_tile * kt + r, 2 * n), toeplitz(ci_per_tile * kt + r, 2 * n + 1)],
                                 axis=1) for r in range(ci_per_tile)], axis=0)
            acc = acc + _dot(lhs_ref[:, 2 * kt * LANES:(2 * kt + 2) * LANES], w)
        for q in range(2):
            pair = 2 * n + q
            u_pk = upk_ref[:, :, pair * LANES:(pair + 1) * LANES].reshape(n_rows, LANES)
            y = acc[:, q * LANES:(q + 1) * LANES] + vec_ref[_V_D, pair:pair + 1, :] * u_pk
            y = jax.nn.gelu(y, approximate=True).reshape(n_slab_chunks, S5_HALVES * batch, LANES)
            y0 = y[:, 0:batch, :].reshape(n_slab_rows, LANES)
            y1 = y[:, batch:2 * batch, :].reshape(n_slab_rows, LANES)
            ys_ref[2 * pair], ys_ref[2 * pair + 1] = _regroup_halves(y0, y1)


def _conv_mix_kernel(zs_ref, pc_ref, o_ref, *, batch):
    n_rows = zs_ref.shape[1]
    row = lax.broadcasted_iota(jnp.int32, (CHUNK, CHUNK), 0)
    col = lax.broadcasted_iota(jnp.int32, (CHUNK, CHUNK), 1)
    lag = row - col
    band_cur = (lag >= -(CONV_K - 1)) & (lag <= 0)
    band_prev = lag >= CHUNK - (CONV_K - 1)

    def conv_channel(ch, carry):
        z = zs_ref[ch]
        z_prev = jnp.concatenate([jnp.zeros((batch, CHUNK), F32), z[:n_rows - batch]], axis=0)
        lhs = jnp.concatenate([z_prev, z], axis=1).astype(BF16)
        taps = jnp.broadcast_to(pc_ref[ch, _PC_TAPS:_PC_TAPS + 1, :], (CHUNK, CHUNK))
        diag = pltpu.roll(taps, 0, 1, stride=1, stride_axis=0)
        toe = jnp.concatenate([jnp.where(band_prev, diag, 0.0), jnp.where(band_cur, diag, 0.0)],
                              axis=0).astype(BF16)
        o_ref[ch] = _dot(lhs, toe) + pc_ref[ch, _PC_BIAS:_PC_BIAS + 1, :]
        return carry

    lax.fori_loop(0, CONV_HEAD_DIM, conv_channel, 0, unroll=32)

    inv_n = 1.0 / CONV_HEAD_DIM
    for r0 in range(0, n_rows, LN_ROWS):
        rows = pl.ds(r0, LN_ROWS)
        zero = jnp.zeros((LN_ROWS, CHUNK), F32)
        mu = lax.fori_loop(0, CONV_HEAD_DIM, lambda ch, acc: acc + o_ref[ch, rows, :], zero,
                           unroll=LN_UNROLL) * inv_n
        var = lax.fori_loop(0, CONV_HEAD_DIM, lambda ch, acc: acc + jnp.square(o_ref[ch, rows, :] - mu),
                            zero, unroll=LN_UNROLL) * inv_n
        rstd = lax.rsqrt(var + EPS)

        def norm_channel(ch, carry):
            half = ((o_ref[ch, rows, :] - mu) * rstd * pc_ref[ch, _PC_HALF_GAIN:_PC_HALF_GAIN + 1, :]
                    + pc_ref[ch, _PC_HALF_BETA:_PC_HALF_BETA + 1, :])
            o_ref[ch, rows, :] = half + half * jnp.tanh(half)
            return carry

        lax.fori_loop(0, CONV_HEAD_DIM, norm_channel, 0, unroll=LN_UNROLL)


def _out_ffn2_norm_kernel(x1_ref, ys_ref, yc_ref, wglu_t_ref, bglu_ref, wo_ref,
                          g2_ref, wg_ref, wu_ref, wd_ref, gf_ref, o_ref):
    first_seq = pl.program_id(1) * TILE_SEQS
    ys_t = _load_slabs(ys_ref, first_seq)
    yc_t = _load_slabs(yc_ref, first_seq)
    x2 = x1_ref[...].reshape(ROW_TILE, D_MODEL) + lax.dot_general(
        yc_t.astype(BF16), wo_ref[S5_WIDTH:, :], _TN, preferred_element_type=F32)
    gate_t = jax.nn.sigmoid(_dot(wglu_t_ref[...], ys_t.astype(BF16)) + bglu_ref[...])
    x2 = x2 + lax.dot_general((ys_t * gate_t).astype(BF16), wo_ref[0:S5_WIDTH, :], _TN,
                              preferred_element_type=F32)
    h, r = _rms_split(x2, g2_ref[...])
    x3 = x2 + 0.5 * _swiglu(h, r, wg_ref, wu_ref, wd_ref)
    o_ref[...] = _rms_norm(x3, gf_ref[...]).reshape(TILE_SEQS, CHUNK, D_MODEL)


def _resident(shape):
    nd = len(shape)
    return pl.BlockSpec(shape, lambda *_: (0,) * nd, pipeline_mode=pl.Buffered(1))


def _tile_spec():
    return pl.BlockSpec((TILE_SEQS, CHUNK, D_MODEL), lambda c, h: (h, c, 0))


def _slab_spec(n_ch):
    return pl.BlockSpec((n_ch, SUBLANES, LANES), lambda c, h: (0, c, 0))


_DENSE_PARAMS = pltpu.CompilerParams(dimension_semantics=("arbitrary", "arbitrary"),
                                     vmem_limit_bytes=VMEM_LIMIT)
_MIX_PARAMS = pltpu.CompilerParams(dimension_semantics=("arbitrary",), vmem_limit_bytes=VMEM_LIMIT)


def _cast_spec(w, n_steps):
    n_blocks = n_steps
    while w.shape[0] % n_blocks or (w.shape[0] // n_blocks) % (2 * SUBLANES):
        n_blocks //= 2
    steps_per_block = n_steps // n_blocks
    return pl.BlockSpec((w.shape[0] // n_blocks, w.shape[1]), lambda g: (g // steps_per_block, 0))


def _ffn1_in_proj(x, g1, wg, wu, wd, gm, win_t):
    batch, seq, _ = x.shape
    assert batch == SUBLANES and seq % CHUNK == 0
    n_rows = (seq // CHUNK) * batch
    return pl.pallas_call(
        _ffn1_in_proj_kernel,
        grid=(seq // CHUNK, batch // TILE_SEQS),
        in_specs=[
            _tile_spec(),
            _resident((1, D_MODEL)),
            _resident((D_MODEL, D_FF)),
            _resident((D_MODEL, D_FF)),
            _resident((D_FF, D_MODEL)),
            _resident((1, D_MODEL)),
            _resident((IN_COLS, D_MODEL)),
        ],
        out_specs=[_tile_spec(), _slab_spec(S5_WIDTH), _slab_spec(CONV_WIDTH)],
        out_shape=[
            jax.ShapeDtypeStruct((batch, seq, D_MODEL), F32),
            jax.ShapeDtypeStruct((S5_WIDTH, n_rows, LANES), F32),
            jax.ShapeDtypeStruct((CONV_WIDTH, n_rows, LANES), F32),
        ],
        compiler_params=_DENSE_PARAMS,
        name="ffn1_in_proj",
    )(x, g1, wg, wu, wd, gm, win_t)


def _s5_mix(us, tables, later_weights, *, batch):
    n_rows = us.shape[1]
    n_steps = S5_GROUPS // S5_STEP_GROUPS
    cast_specs = [_cast_spec(w, n_steps) for w in later_weights]
    slab = pl.BlockSpec((S5_STEP_GROUPS * S5_GROUP_CH, n_rows, LANES), lambda g: (g, 0, 0))

    def per_group(t):
        nd = t.ndim - 1
        return pl.BlockSpec((S5_STEP_GROUPS,) + t.shape[1:], lambda g: (g,) + (0,) * nd)

    return pl.pallas_call(
        functools.partial(_s5_mix_kernel, n_cast=len(later_weights), batch=batch),
        grid=(n_steps,),
        in_specs=[slab] + [per_group(t) for t in tables] + cast_specs,
        out_specs=[slab] + cast_specs,
        out_shape=[jax.ShapeDtypeStruct((S5_WIDTH, n_rows, LANES), F32)]
        + [jax.ShapeDtypeStruct(w.shape, BF16) for w in later_weights],
        scratch_shapes=[pltpu.VMEM((S5_GROUP_CH * S5_GROUP_CH, 2 * S5_STATE), F32),
                        pltpu.VMEM((S5_GROUP_CH * S5_PAIRS, LANES), F32),
                        pltpu.VMEM((n_rows // batch, S5_HALVES * batch, S5_LANES), F32),
                        pltpu.VMEM((S5_HALVES * n_rows, S5_LANES), BF16),
                        pltpu.VMEM((S5_LANES, 4 * S5_STATE), BF16),
                        pltpu.VMEM((S5_LANES, 2 * S5_STATE), BF16),
                        pltpu.VMEM((S5_HALVES * n_rows, 2 * S5_STATE), F32)],
        compiler_params=_MIX_PARAMS,
        name="s5_mix",
    )(us, *tables, *later_weights)


def _conv_mix(zs, per_channel, *, batch):
    n_rows = zs.shape[1]
    slab = pl.BlockSpec((CONV_HEAD_DIM, n_rows, LANES), lambda h: (h, 0, 0))
    per_ch = pl.BlockSpec((CONV_HEAD_DIM, _PC_COUNT, LANES), lambda h: (h, 0, 0))
    return pl.pallas_call(
        functools.partial(_conv_mix_kernel, batch=batch),
        grid=(CONV_HEADS,),
        in_specs=[slab, per_ch],
        out_specs=slab,
        out_shape=jax.ShapeDtypeStruct((CONV_WIDTH, n_rows, LANES), F32),
        compiler_params=_MIX_PARAMS,
        name="conv_mix",
    )(zs, per_channel)


def _out_ffn2_norm(x1, ys, yc, wglu_t, bglu, wo, g2, wg, wu, wd, gf):
    batch, seq, _ = x1.shape
    return pl.pallas_call(
        _out_ffn2_norm_kernel,
        grid=(seq // CHUNK, batch // TILE_SEQS),
        in_specs=[
            _tile_spec(),
            _slab_spec(S5_WIDTH),
            _slab_spec(CONV_WIDTH),
            _resident((S5_WIDTH, S5_WIDTH)),
            _resident((S5_WIDTH, 1)),
            _resident((S5_WIDTH + CONV_WIDTH, D_MODEL)),
            _resident((1, D_MODEL)),
            _resident((D_MODEL, D_FF)),
            _resident((D_MODEL, D_FF)),
            _resident((D_FF, D_MODEL)),
            _resident((1, D_MODEL)),
        ],
        out_specs=_tile_spec(),
        out_shape=jax.ShapeDtypeStruct((batch, seq, D_MODEL), F32),
        compiler_params=_DENSE_PARAMS,
        name="out_ffn2_norm",
    )(x1, ys, yc, wglu_t, bglu, wo, g2, wg, wu, wd, gf)


def _s5_tables(lam_re, lam_im, log_dt, b_re, b_im, c_re, c_im, d_skip):
    T, C, P, G = S5_T, S5_GROUP_CH, S5_STATE, S5_GROUPS
    dt = jnp.exp(log_dt)[:, None]
    mag = jnp.exp(lam_re * dt)
    abar_re = mag * jnp.cos(lam_im * dt)
    abar_im = mag * jnp.sin(lam_im * dt)
    den = lam_re * lam_re + lam_im * lam_im
    num_re = abar_re - 1.0
    num_im = abar_im
    f_re = ((num_re * lam_re + num_im * lam_im) / den)[..., None]
    f_im = ((num_im * lam_re - num_re * lam_im) / den)[..., None]
    bb_re = f_re * b_re - f_im * b_im
    bb_im = f_re * b_im + f_im * b_re
    n_q = T // SUBLANES
    expo = jnp.concatenate([jnp.arange(SUBLANES), SUBLANES * jnp.arange(1, n_q + 1)]).astype(F32)[None, :, None]
    bmag = jnp.exp(expo * (lam_re * dt)[:, None, :])
    bang = expo * (lam_im * dt)[:, None, :]
    base_re, base_im = bmag * jnp.cos(bang), bmag * jnp.sin(bang)
    lo_re, lo_im = base_re[:, :SUBLANES], base_im[:, :SUBLANES]
    hi_re = jnp.concatenate([base_re[:, :1], base_re[:, SUBLANES:]], axis=1)
    hi_im = jnp.concatenate([base_im[:, :1], base_im[:, SUBLANES:]], axis=1)

    def outer(q_re, q_im, r_re, r_im):
        re = q_re[:, :, None] * r_re[:, None] - q_im[:, :, None] * r_im[:, None]
        im = q_re[:, :, None] * r_im[:, None] + q_im[:, :, None] * r_re[:, None]
        return jnp.concatenate([re, im], axis=-1).reshape(G, -1, 2 * P)

    cat = lambda a, b: jnp.concatenate([a, b], axis=-1)
    powers = jnp.concatenate([
        outer(hi_re[:, n_q - 1::-1], hi_im[:, n_q - 1::-1], lo_re[:, ::-1], lo_im[:, ::-1]),
        outer(hi_re, hi_im, lo_re, lo_im)], axis=1)
    bt_re = bb_re.transpose(0, 2, 1)
    bt_im = bb_im.transpose(0, 2, 1)
    d_rows = jnp.pad(jnp.repeat(d_skip.reshape(G, C // 2, 2), T, axis=-1), ((0, 0), (0, C // 2), (0, 0)))
    cp_re = jnp.concatenate([c_re[:, 0::2], c_re[:, 1::2]], axis=1)
    cp_im = jnp.concatenate([c_im[:, 0::2], c_im[:, 1::2]], axis=1)
    vecs = [None] * _V_COUNT
    vecs[_V_BR] = cat(bt_re, bt_re)
    vecs[_V_BI_NP] = cat(-bt_im, bt_im)
    vecs[_V_BI_PN] = cat(bt_im, -bt_im)
    vecs[_V_BI_PP] = cat(bt_im, bt_im)
    vecs[_V_CR_PN] = cat(c_re, -c_re)
    vecs[_V_CI_NN] = cat(-c_im, -c_im)
    vecs[_V_C_RE_NIM] = cat(cp_re, -cp_im)
    vecs[_V_C_NIM_NRE] = cat(-cp_im, -cp_re)
    vecs[_V_D] = d_rows
    return powers, jnp.stack(vecs, axis=1)


def _conv_tables(w_dw, b_dw, ln_g, ln_b):
    taps = jnp.pad(w_dw[::-1].T, ((0, 0), (0, LANES - CONV_K)))
    lanes = lambda v: jnp.broadcast_to(v[:, None], (CONV_WIDTH, LANES))
    rows = [None] * _PC_COUNT
    rows[_PC_TAPS], rows[_PC_BIAS] = taps, lanes(b_dw)
    rows[_PC_HALF_GAIN], rows[_PC_HALF_BETA] = lanes(0.5 * ln_g), lanes(0.5 * ln_b)
    return jnp.stack(rows, axis=1)


def kernel(x, ffn1_norm, ffn1_w_gate, ffn1_w_up, ffn1_w_down, mix_norm, w_in, s5_lam_re, s5_lam_im, s5_log_dt, s5_b_re, s5_b_im, s5_c_re, s5_c_im, s5_d, s5_w_glu, s5_b_glu, conv_w_dw, conv_b_dw, conv_ln_g, conv_ln_b, w_out, ffn2_norm, ffn2_w_gate, ffn2_w_up, ffn2_w_down, final_norm):
    batch, seq, d_model = x.shape
    assert ffn1_norm.shape[0] == 1 and d_model == D_MODEL
    l = 0
    row = lambda v: v.reshape(1, -1)

    x1, us, zs = _ffn1_in_proj(
        x, row(ffn1_norm[l]), ffn1_w_gate[l].astype(BF16), ffn1_w_up[l].astype(BF16),
        ffn1_w_down[l].astype(BF16), row(mix_norm[l]), w_in[l].T.astype(BF16))

    tables = _s5_tables(
        s5_lam_re[l], s5_lam_im[l], s5_log_dt[l], s5_b_re[l], s5_b_im[l], s5_c_re[l], s5_c_im[l], s5_d[l])
    ys, wg2, wu2, wd2, wo = _s5_mix(
        us, tables, (ffn2_w_gate[l], ffn2_w_up[l], ffn2_w_down[l], w_out[l]), batch=batch)

    yc = _conv_mix(zs, _conv_tables(conv_w_dw[l], conv_b_dw[l], conv_ln_g[l], conv_ln_b[l]), batch=batch)

    return _out_ffn2_norm(
        x1, ys, yc, s5_w_glu[l].T.astype(BF16), s5_b_glu[l].reshape(-1, 1), wo,
        row(ffn2_norm[l]), wg2, wu2, wd2, row(final_norm))
```

```python
import functools

import jax
import jax.numpy as jnp
from jax import lax
from jax.experimental import pallas as pl
from jax.experimental.pallas import tpu as pltpu

D_MODEL = 1024
D_FF = 2816
S5_WIDTH = 512
S5_GROUP_CH = 16
S5_GROUPS = S5_WIDTH // S5_GROUP_CH
S5_STATE = 64
CONV_WIDTH = 512
CONV_HEAD_DIM = 64
CONV_HEADS = CONV_WIDTH // CONV_HEAD_DIM
CONV_K = 31
IN_COLS = S5_WIDTH + 2 * CONV_WIDTH
EPS = 1e-6

LANES = 128
SUBLANES = 8
CHUNK = LANES
S5_T = 64
S5_HALVES = CHUNK // S5_T
S5_PAIRS = S5_GROUP_CH // 2
S5_LANES = S5_GROUP_CH * S5_T
S5_STEP_GROUPS = 2
TILE_SEQS = 8
ROW_TILE = TILE_SEQS * CHUNK
FF_SPLITS = ((0, 1536), (1536, 2816))
LN_ROWS = 64
LN_UNROLL = 16
VMEM_LIMIT = 60 * 1024 * 1024
CAST_CHUNKS = 16

BF16 = jnp.bfloat16
F32 = jnp.float32

(_V_BR,
 _V_BI_NP,
 _V_BI_PN,
 _V_BI_PP,
 _V_CR_PN,
 _V_CI_NN,
 _V_C_RE_NIM,
 _V_C_NIM_NRE,
 _V_D,
 _V_COUNT) = range(10)

_PC_TAPS, _PC_BIAS, _PC_HALF_GAIN, _PC_HALF_BETA, _PC_COUNT = range(5)

_NT = (((1,), (1,)), ((), ()))
_TN = (((0,), (0,)), ((), ()))


def _rms_norm(x, g):
    return x * lax.rsqrt(jnp.mean(x * x, axis=-1, keepdims=True) + EPS) * g


def _dot(a, b):
    return jnp.dot(a, b, preferred_element_type=F32)


def _rms_split(x, g):
    return (x * g).astype(BF16), lax.rsqrt(jnp.mean(x * x, axis=-1, keepdims=True) + EPS)


def _swiglu(h, row_scale, wg_ref, wu_ref, wd_ref):
    acc = None
    for lo, hi in FF_SPLITS:
        g = _dot(h, wg_ref[:, lo:hi]) * row_scale
        u = _dot(h, wu_ref[:, lo:hi]) * row_scale
        a = (g * jax.nn.sigmoid(g) * u).astype(BF16)
        part = _dot(a, wd_ref[lo:hi, :])
        acc = part if acc is None else acc + part
    return acc


def _store_slabs(slab_ref, val, first_seq):
    n_ch = slab_ref.shape[0]
    rows = slab_ref.reshape(n_ch * SUBLANES, LANES)
    for ct in range(n_ch // SUBLANES):
        for s in range(TILE_SEQS):
            rows[pl.ds(ct * SUBLANES * SUBLANES + first_seq + s, SUBLANES, stride=SUBLANES), :] = (
                val[ct * SUBLANES:(ct + 1) * SUBLANES, s * LANES:(s + 1) * LANES])


def _load_slabs(slab_ref, first_seq):
    n_ch = slab_ref.shape[0]
    rows = slab_ref.reshape(n_ch * SUBLANES, LANES)
    bands = []
    for ct in range(n_ch // SUBLANES):
        bands.append(jnp.concatenate(
            [rows[pl.ds(ct * SUBLANES * SUBLANES + first_seq + s, SUBLANES, stride=SUBLANES), :]
             for s in range(TILE_SEQS)], axis=1))
    return jnp.concatenate(bands, axis=0)


def _cast_weight_in(w_hbm, w_vmem, stage_ref, sem_ref):
    rows = stage_ref.shape[1]
    n_chunks = w_hbm.shape[0] // rows

    def chunk_copy(i):
        return pltpu.make_async_copy(w_hbm.at[pl.ds(i * rows, rows)], stage_ref.at[i % 2], sem_ref.at[i % 2])

    chunk_copy(0).start()
    for i in range(n_chunks):
        if i + 1 < n_chunks:
            chunk_copy(i + 1).start()
        chunk_copy(i).wait()
        w_vmem[i * rows:(i + 1) * rows, :] = stage_ref[i % 2].astype(BF16)


def _ffn1_in_proj_kernel(x_ref, g1_ref, wg_hbm, wu_hbm, wd_hbm, gm_ref, win_t_hbm,
                         x1_ref, us_ref, zs_ref,
                         wg_ref, wu_ref, wd_ref, win_t_ref, stage_ff_ref, stage_d_ref, stage_in_ref,
                         sem_ff, sem_d, sem_in):
    @pl.when(pl.program_id(0) == 0)
    def _():
        _cast_weight_in(wg_hbm, wg_ref, stage_ff_ref, sem_ff)
        _cast_weight_in(wu_hbm, wu_ref, stage_ff_ref, sem_ff)
        _cast_weight_in(wd_hbm, wd_ref, stage_d_ref, sem_d)
        _cast_weight_in(win_t_hbm, win_t_ref, stage_in_ref, sem_in)

    first_seq = pl.program_id(1) * TILE_SEQS
    x = x_ref[...].reshape(ROW_TILE, D_MODEL)
    h, r = _rms_split(x, g1_ref[...])
    x1 = x + 0.5 * _swiglu(h, r, wg_ref, wu_ref, wd_ref)
    x1_ref[...] = x1.reshape(TILE_SEQS, CHUNK, D_MODEL)
    h2, r2 = _rms_split(x1, gm_ref[...])
    r2_lanes = jnp.transpose(jnp.broadcast_to(r2, (ROW_TILE, LANES)))[0:1, :]

    def in_proj_t(lo, hi):
        return lax.dot_general(win_t_ref[lo:hi, :], h2, _NT, preferred_element_type=F32) * r2_lanes

    gate = jax.nn.sigmoid(in_proj_t(S5_WIDTH + CONV_WIDTH, IN_COLS))
    _store_slabs(us_ref, in_proj_t(0, S5_WIDTH), first_seq)
    z_t = in_proj_t(S5_WIDTH, S5_WIDTH + CONV_WIDTH) * gate
    _store_slabs(zs_ref, z_t, first_seq)


def _regroup_halves(a, b):
    first_half = lax.broadcasted_iota(jnp.int32, (1, LANES), 1) < S5_T
    straight = jnp.where(first_half, a, b)
    crossed = pltpu.roll(jnp.where(first_half, b, a), S5_T, 1)
    return jnp.where(first_half, straight, crossed), jnp.where(first_half, crossed, straight)


def _s5_mix_kernel(us_ref, pow_ref, vec_ref, *rest, n_cast, batch):
    ys_ref = rest[n_cast]
    scratch = rest[2 * n_cast + 1:]
    for src_ref, dst_ref in zip(rest[:n_cast], rest[n_cast + 1:2 * n_cast + 1]):
        dst_ref[...] = src_ref[...].astype(BF16)
    for g in range(S5_STEP_GROUPS):
        ch = pl.ds(g * S5_GROUP_CH, S5_GROUP_CH)
        _s5_group(us_ref.at[ch], pow_ref.at[g], vec_ref.at[g], ys_ref.at[ch], *scratch, batch=batch)


def _s5_group(us_ref, pow_ref, vec_ref, ys_ref,
              cb_ref, f_ref, upk_ref, lhs_ref, win_ref, wout_ref, sprev_ref, *, batch):
    T = S5_T
    n_slab_rows = us_ref.shape[1]
    n_slab_chunks = n_slab_rows // batch
    n_rows = S5_HALVES * n_slab_rows
    lane = lax.broadcasted_iota(jnp.int32, (1, LANES), 1)
    first_state_half = lane < S5_STATE
    swap = lambda v: pltpu.roll(v, S5_STATE, 1)
    pa_rev = pow_ref[0:T, :]
    pb_rev = swap(pa_rev)
    pa_fwd = pow_ref[T + 1:2 * T + 1, :]
    pb_fwd = swap(pa_fwd)
    for c in range(S5_GROUP_CH):
        rows = slice(c * T, (c + 1) * T)
        vec = lambda i: vec_ref[i, c:c + 1, :]
        win_ref[rows, 0:2 * S5_STATE] = (pa_rev * vec(_V_BR) + pb_rev * vec(_V_BI_NP)).astype(BF16)
        win_ref[rows, 2 * S5_STATE:] = (pb_rev * vec(_V_BR) + pa_rev * vec(_V_BI_PN)).astype(BF16)
        wout_ref[rows, :] = (pa_fwd * vec(_V_CR_PN) + pb_fwd * vec(_V_CI_NN)).astype(BF16)
        cb_ref[c * S5_GROUP_CH:(c + 1) * S5_GROUP_CH, :] = (
            vec_ref[_V_C_RE_NIM] * vec(_V_BR) + vec_ref[_V_C_NIM_NRE] * vec(_V_BI_PP))
    f = lax.dot_general(cb_ref[...], pow_ref[T:2 * T, :], _NT, precision=lax.Precision.HIGHEST,
                        preferred_element_type=F32)
    for c in range(S5_GROUP_CH):
        even, odd = c * S5_GROUP_CH, c * S5_GROUP_CH + S5_PAIRS
        f_ref[c * S5_PAIRS:(c + 1) * S5_PAIRS, :] = jnp.concatenate(
            [f[even:even + S5_PAIRS], f[odd:odd + S5_PAIRS]], axis=1)

    for p in range(S5_PAIRS):
        halves = _regroup_halves(us_ref[2 * p], us_ref[2 * p + 1])
        for hh in range(S5_HALVES):
            upk_ref[:, hh * batch:(hh + 1) * batch, p * LANES:(p + 1) * LANES] = (
                halves[hh].reshape(n_slab_chunks, batch, LANES))
    lhs_ref[...] = upk_ref[...].reshape(n_rows, S5_LANES).astype(BF16)

    e = _dot(lhs_ref[...], win_ref[...])
    a_chunk = pow_ref[2 * T:2 * T + 1, :]
    a_sw = swap(a_chunk)
    a1 = jnp.where(first_state_half, a_chunk, a_sw)
    a2 = jnp.where(first_state_half, -a_sw, a_chunk)
    s = jnp.zeros((batch, 2 * S5_STATE), F32)
    s_sw = jnp.zeros((batch, 2 * S5_STATE), F32)
    for c in range(n_rows // batch):
        rows = slice(c * batch, (c + 1) * batch)
        sprev_ref[rows, :] = s
        e_c = e[rows, :]
        s, s_sw = (a1 * s + a2 * s_sw + e_c[:, :2 * S5_STATE],
                   a1 * s_sw - a2 * s + e_c[:, 2 * S5_STATE:])
    sprev = sprev_ref[...].astype(BF16)

    causal = ((lax.broadcasted_iota(jnp.int32, (T, LANES), 1) & (T - 1))
              >= lax.broadcasted_iota(jnp.int32, (T, LANES), 0))

    def toeplitz(ci, pair):
        f_row = jnp.broadcast_to(f_ref[ci * S5_PAIRS + pair:ci * S5_PAIRS + pair + 1, :], (T, LANES))
        return jnp.where(causal, pltpu.roll(f_row, 0, 1, stride=1, stride_axis=0), 0.0).astype(BF16)

    ci_per_tile = 2 * LANES // T
    for n in range(S5_PAIRS // 2):
        acc = lax.dot_general(sprev, wout_ref[2 * n * LANES:(2 * n + 2) * LANES, :], _NT,
                              preferred_element_type=F32)
        for kt in range(S5_GROUP_CH // ci_per_tile):
            w = jnp.concatenate(
                [jnp.concatenate([toeplitz(ci_per_tile * kt + r, 2 * n), toeplitz(ci_per_tile * kt + r, 2 * n + 1)],
                                 axis=1) for r in range(ci_per_tile)], axis=0)
            acc = acc + _dot(lhs_ref[:, 2 * kt * LANES:(2 * kt + 2) * LANES], w)
        for q in range(2):
            pair = 2 * n + q
            u_pk = upk_ref[:, :, pair * LANES:(pair + 1) * LANES].reshape(n_rows, LANES)
            y = acc[:, q * LANES:(q + 1) * LANES] + vec_ref[_V_D, pair:pair + 1, :] * u_pk
            y = jax.nn.gelu(y, approximate=True).reshape(n_slab_chunks, S5_HALVES * batch, LANES)
            y0 = y[:, 0:batch, :].reshape(n_slab_rows, LANES)
            y1 = y[:, batch:2 * batch, :].reshape(n_slab_rows, LANES)
            ys_ref[2 * pair], ys_ref[2 * pair + 1] = _regroup_halves(y0, y1)


def _conv_mix_kernel(zs_ref, pc_ref, o_ref, *, batch):
    n_rows = zs_ref.shape[1]
    row = lax.broadcasted_iota(jnp.int32, (CHUNK, CHUNK), 0)
    col = lax.broadcasted_iota(jnp.int32, (CHUNK, CHUNK), 1)
    lag = row - col
    band_cur = (lag >= -(CONV_K - 1)) & (lag <= 0)
    band_prev = lag >= CHUNK - (CONV_K - 1)

    def conv_channel(ch, carry):
        z = zs_ref[ch]
        z_prev = jnp.concatenate([jnp.zeros((batch, CHUNK), F32), z[:n_rows - batch]], axis=0)
        lhs = jnp.concatenate([z_prev, z], axis=1).astype(BF16)
        taps = jnp.broadcast_to(pc_ref[ch, _PC_TAPS:_PC_TAPS + 1, :], (CHUNK, CHUNK))
        diag = pltpu.roll(taps, 0, 1, stride=1, stride_axis=0)
        toe = jnp.concatenate([jnp.where(band_prev, diag, 0.0), jnp.where(band_cur, diag, 0.0)],
                              axis=0).astype(BF16)
        o_ref[ch] = _dot(lhs, toe) + pc_ref[ch, _PC_BIAS:_PC_BIAS + 1, :]
        return carry

    lax.fori_loop(0, CONV_HEAD_DIM, conv_channel, 0, unroll=32)

    inv_n = 1.0 / CONV_HEAD_DIM
    for r0 in range(0, n_rows, LN_ROWS):
        rows = pl.ds(r0, LN_ROWS)
        zero = jnp.zeros((LN_ROWS, CHUNK), F32)
        mu = lax.fori_loop(0, CONV_HEAD_DIM, lambda ch, acc: acc + o_ref[ch, rows, :], zero,
                           unroll=LN_UNROLL) * inv_n
        var = lax.fori_loop(0, CONV_HEAD_DIM, lambda ch, acc: acc + jnp.square(o_ref[ch, rows, :] - mu),
                            zero, unroll=LN_UNROLL) * inv_n
        rstd = lax.rsqrt(var + EPS)

        def norm_channel(ch, carry):
            half = ((o_ref[ch, rows, :] - mu) * rstd * pc_ref[ch, _PC_HALF_GAIN:_PC_HALF_GAIN + 1, :]
                    + pc_ref[ch, _PC_HALF_BETA:_PC_HALF_BETA + 1, :])
            o_ref[ch, rows, :] = half + half * jnp.tanh(half)
            return carry

        lax.fori_loop(0, CONV_HEAD_DIM, norm_channel, 0, unroll=LN_UNROLL)


def _out_ffn2_norm_kernel(x1_ref, ys_ref, yc_ref, wglu_t_ref, bglu_ref, wo_ref,
                          g2_ref, wg_ref, wu_ref, wd_ref, gf_ref, o_ref):
    first_seq = pl.program_id(1) * TILE_SEQS
    ys_t = _load_slabs(ys_ref, first_seq)
    yc_t = _load_slabs(yc_ref, first_seq)
    x2 = x1_ref[...].reshape(ROW_TILE, D_MODEL) + lax.dot_general(
        yc_t.astype(BF16), wo_ref[S5_WIDTH:, :], _TN, preferred_element_type=F32)
    gate_t = jax.nn.sigmoid(_dot(wglu_t_ref[...], ys_t.astype(BF16)) + bglu_ref[...])
    x2 = x2 + lax.dot_general((ys_t * gate_t).astype(BF16), wo_ref[0:S5_WIDTH, :], _TN,
                              preferred_element_type=F32)
    h, r = _rms_split(x2, g2_ref[...])
    x3 = x2 + 0.5 * _swiglu(h, r, wg_ref, wu_ref, wd_ref)
    o_ref[...] = _rms_norm(x3, gf_ref[...]).reshape(TILE_SEQS, CHUNK, D_MODEL)


def _resident(shape):
    nd = len(shape)
    return pl.BlockSpec(shape, lambda *_: (0,) * nd, pipeline_mode=pl.Buffered(1))


def _tile_spec():
    return pl.BlockSpec((TILE_SEQS, CHUNK, D_MODEL), lambda c, h: (h, c, 0))


def _slab_spec(n_ch):
    return pl.BlockSpec((n_ch, SUBLANES, LANES), lambda c, h: (0, c, 0))


_DENSE_PARAMS = pltpu.CompilerParams(dimension_semantics=("arbitrary", "arbitrary"),
                                     vmem_limit_bytes=VMEM_LIMIT)
_MIX_PARAMS = pltpu.CompilerParams(dimension_semantics=("arbitrary",), vmem_limit_bytes=VMEM_LIMIT)


def _cast_spec(w, n_steps):
    n_blocks = n_steps
    while w.shape[0] % n_blocks or (w.shape[0] // n_blocks) % (2 * SUBLANES):
        n_blocks //= 2
    steps_per_block = n_steps // n_blocks
    return pl.BlockSpec((w.shape[0] // n_blocks, w.shape[1]), lambda g: (g // steps_per_block, 0))


def _ffn1_in_proj(x, g1, wg, wu, wd, gm, win_t):
    batch, seq, _ = x.shape
    assert batch == SUBLANES and seq % CHUNK == 0
    n_rows = (seq // CHUNK) * batch
    return pl.pallas_call(
        _ffn1_in_proj_kernel,
        grid=(seq // CHUNK, batch // TILE_SEQS),
        in_specs=[
            _tile_spec(),
            _resident((1, D_MODEL)),
            pl.BlockSpec(memory_space=pl.ANY),
            pl.BlockSpec(memory_space=pl.ANY),
            pl.BlockSpec(memory_space=pl.ANY),
            _resident((1, D_MODEL)),
            pl.BlockSpec(memory_space=pl.ANY),
        ],
        out_specs=[_tile_spec(), _slab_spec(S5_WIDTH), _slab_spec(CONV_WIDTH)],
        out_shape=[
            jax.ShapeDtypeStruct((batch, seq, D_MODEL), F32),
            jax.ShapeDtypeStruct((S5_WIDTH, n_rows, LANES), F32),
            jax.ShapeDtypeStruct((CONV_WIDTH, n_rows, LANES), F32),
        ],
        scratch_shapes=[
            pltpu.VMEM((D_MODEL, D_FF), BF16), pltpu.VMEM((D_MODEL, D_FF), BF16),
            pltpu.VMEM((D_FF, D_MODEL), BF16), pltpu.VMEM((IN_COLS, D_MODEL), BF16),
            pltpu.VMEM((2, D_MODEL // CAST_CHUNKS, D_FF), F32),
            pltpu.VMEM((2, D_FF // CAST_CHUNKS, D_MODEL), F32),
            pltpu.VMEM((2, IN_COLS // CAST_CHUNKS, D_MODEL), F32),
            pltpu.SemaphoreType.DMA((2,)), pltpu.SemaphoreType.DMA((2,)), pltpu.SemaphoreType.DMA((2,)),
        ],
        compiler_params=_DENSE_PARAMS,
        name="ffn1_in_proj",
    )(x, g1, wg, wu, wd, gm, win_t)


def _s5_mix(us, tables, later_weights, *, batch):
    n_rows = us.shape[1]
    n_steps = S5_GROUPS // S5_STEP_GROUPS
    cast_specs = [_cast_spec(w, n_steps) for w in later_weights]
    slab = pl.BlockSpec((S5_STEP_GROUPS * S5_GROUP_CH, n_rows, LANES), lambda g: (g, 0, 0))

    def per_group(t):
        nd = t.ndim - 1
        return pl.BlockSpec((S5_STEP_GROUPS,) + t.shape[1:], lambda g: (g,) + (0,) * nd)

    return pl.pallas_call(
        functools.partial(_s5_mix_kernel, n_cast=len(later_weights), batch=batch),
        grid=(n_steps,),
        in_specs=[slab] + [per_group(t) for t in tables] + cast_specs,
        out_specs=[slab] + cast_specs,
        out_shape=[jax.ShapeDtypeStruct((S5_WIDTH, n_rows, LANES), F32)]
        + [jax.ShapeDtypeStruct(w.shape, BF16) for w in later_weights],
        scratch_shapes=[pltpu.VMEM((S5_GROUP_CH * S5_GROUP_CH, 2 * S5_STATE), F32),
                        pltpu.VMEM((S5_GROUP_CH * S5_PAIRS, LANES), F32),
                        pltpu.VMEM((n_rows // batch, S5_HALVES * batch, S5_LANES), F32),
                        pltpu.VMEM((S5_HALVES * n_rows, S5_LANES), BF16),
                        pltpu.VMEM((S5_LANES, 4 * S5_STATE), BF16),
                        pltpu.VMEM((S5_LANES, 2 * S5_STATE), BF16),
                        pltpu.VMEM((S5_HALVES * n_rows, 2 * S5_STATE), F32)],
        compiler_params=_MIX_PARAMS,
        name="s5_mix",
    )(us, *tables, *later_weights)


def _conv_mix(zs, per_channel, *, batch):
    n_rows = zs.shape[1]
    slab = pl.BlockSpec((CONV_HEAD_DIM, n_rows, LANES), lambda h: (h, 0, 0))
    per_ch = pl.BlockSpec((CONV_HEAD_DIM, _PC_COUNT, LANES), lambda h: (h, 0, 0))
    return pl.pallas_call(
        functools.partial(_conv_mix_kernel, batch=batch),
        grid=(CONV_HEADS,),
        in_specs=[slab, per_ch],
        out_specs=slab,
        out_shape=jax.ShapeDtypeStruct((CONV_WIDTH, n_rows, LANES), F32),
        compiler_params=_MIX_PARAMS,
        name="conv_mix",
    )(zs, per_channel)


def _out_ffn2_norm(x1, ys, yc, wglu_t, bglu, wo, g2, wg, wu, wd, gf):
    batch, seq, _ = x1.shape
    return pl.pallas_call(
        _out_ffn2_norm_kernel,
        grid=(seq // CHUNK, batch // TILE_SEQS),
        in_specs=[
            _tile_spec(),
            _slab_spec(S5_WIDTH),
            _slab_spec(CONV_WIDTH),
            _resident((S5_WIDTH, S5_WIDTH)),
            _resident((S5_WIDTH, 1)),
            _resident((S5_WIDTH + CONV_WIDTH, D_MODEL)),
            _resident((1, D_MODEL)),
            _resident((D_MODEL, D_FF)),
            _resident((D_MODEL, D_FF)),
            _resident((D_FF, D_MODEL)),
            _resident((1, D_MODEL)),
        ],
        out_specs=_tile_spec(),
        out_shape=jax.ShapeDtypeStruct((batch, seq, D_MODEL), F32),
        compiler_params=_DENSE_PARAMS,
        name="out_ffn2_norm",
    )(x1, ys, yc, wglu_t, bglu, wo, g2, wg, wu, wd, gf)


def _s5_tables(lam_re, lam_im, log_dt, b_re, b_im, c_re, c_im, d_skip):
    T, C, P, G = S5_T, S5_GROUP_CH, S5_STATE, S5_GROUPS
    dt = jnp.exp(log_dt)[:, None]
    mag = jnp.exp(lam_re * dt)
    abar_re = mag * jnp.cos(lam_im * dt)
    abar_im = mag * jnp.sin(lam_im * dt)
    den = lam_re * lam_re + lam_im * lam_im
    num_re = abar_re - 1.0
    num_im = abar_im
    f_re = ((num_re * lam_re + num_im * lam_im) / den)[..., None]
    f_im = ((num_im * lam_re - num_re * lam_im) / den)[..., None]
    bb_re = f_re * b_re - f_im * b_im
    bb_im = f_re * b_im + f_im * b_re
    n_q = T // SUBLANES
    expo = jnp.concatenate([jnp.arange(SUBLANES), SUBLANES * jnp.arange(1, n_q + 1)]).astype(F32)[None, :, None]
    bmag = jnp.exp(expo * (lam_re * dt)[:, None, :])
    bang = expo * (lam_im * dt)[:, None, :]
    base_re, base_im = bmag * jnp.cos(bang), bmag * jnp.sin(bang)
    lo_re, lo_im = base_re[:, :SUBLANES], base_im[:, :SUBLANES]
    hi_re = jnp.concatenate([base_re[:, :1], base_re[:, SUBLANES:]], axis=1)
    hi_im = jnp.concatenate([base_im[:, :1], base_im[:, SUBLANES:]], axis=1)

    def outer(q_re, q_im, r_re, r_im):
        re = q_re[:, :, None] * r_re[:, None] - q_im[:, :, None] * r_im[:, None]
        im = q_re[:, :, None] * r_im[:, None] + q_im[:, :, None] * r_re[:, None]
        return jnp.concatenate([re, im], axis=-1).reshape(G, -1, 2 * P)

    cat = lambda a, b: jnp.concatenate([a, b], axis=-1)
    powers = jnp.concatenate([
        outer(hi_re[:, n_q - 1::-1], hi_im[:, n_q - 1::-1], lo_re[:, ::-1], lo_im[:, ::-1]),
        outer(hi_re, hi_im, lo_re, lo_im)], axis=1)
    bt_re = bb_re.transpose(0, 2, 1)
    bt_im = bb_im.transpose(0, 2, 1)
    d_rows = jnp.pad(jnp.repeat(d_skip.reshape(G, C // 2, 2), T, axis=-1), ((0, 0), (0, C // 2), (0, 0)))
    cp_re = jnp.concatenate([c_re[:, 0::2], c_re[:, 1::2]], axis=1)
    cp_im = jnp.concatenate([c_im[:, 0::2], c_im[:, 1::2]], axis=1)
    vecs = [None] * _V_COUNT
    vecs[_V_BR] = cat(bt_re, bt_re)
    vecs[_V_BI_NP] = cat(-bt_im, bt_im)
    vecs[_V_BI_PN] = cat(bt_im, -bt_im)
    vecs[_V_BI_PP] = cat(bt_im, bt_im)
    vecs[_V_CR_PN] = cat(c_re, -c_re)
    vecs[_V_CI_NN] = cat(-c_im, -c_im)
    vecs[_V_C_RE_NIM] = cat(cp_re, -cp_im)
    vecs[_V_C_NIM_NRE] = cat(-cp_im, -cp_re)
    vecs[_V_D] = d_rows
    return powers, jnp.stack(vecs, axis=1)


def _conv_tables(w_dw, b_dw, ln_g, ln_b):
    taps = jnp.pad(w_dw[::-1].T, ((0, 0), (0, LANES - CONV_K)))
    lanes = lambda v: jnp.broadcast_to(v[:, None], (CONV_WIDTH, LANES))
    rows = [None] * _PC_COUNT
    rows[_PC_TAPS], rows[_PC_BIAS] = taps, lanes(b_dw)
    rows[_PC_HALF_GAIN], rows[_PC_HALF_BETA] = lanes(0.5 * ln_g), lanes(0.5 * ln_b)
    return jnp.stack(rows, axis=1)


def kernel(x, ffn1_norm, ffn1_w_gate, ffn1_w_up, ffn1_w_down, mix_norm, w_in, s5_lam_re, s5_lam_im, s5_log_dt, s5_b_re, s5_b_im, s5_c_re, s5_c_im, s5_d, s5_w_glu, s5_b_glu, conv_w_dw, conv_b_dw, conv_ln_g, conv_ln_b, w_out, ffn2_norm, ffn2_w_gate, ffn2_w_up, ffn2_w_down, final_norm):
    batch, seq, d_model = x.shape
    assert ffn1_norm.shape[0] == 1 and d_model == D_MODEL
    l = 0
    row = lambda v: v.reshape(1, -1)

    x1, us, zs = _ffn1_in_proj(
        x, row(ffn1_norm[l]), ffn1_w_gate[l], ffn1_w_up[l], ffn1_w_down[l], row(mix_norm[l]), w_in[l].T)

    tables = _s5_tables(
        s5_lam_re[l], s5_lam_im[l], s5_log_dt[l], s5_b_re[l], s5_b_im[l], s5_c_re[l], s5_c_im[l], s5_d[l])
    ys, wg2, wu2, wd2, wo = _s5_mix(
        us, tables, (ffn2_w_gate[l], ffn2_w_up[l], ffn2_w_down[l], w_out[l]), batch=batch)

    yc = _conv_mix(zs, _conv_tables(conv_w_dw[l], conv_b_dw[l], conv_ln_g[l], conv_ln_b[l]), batch=batch)

    return _out_ffn2_norm(
        x1, ys, yc, s5_w_glu[l].T.astype(BF16), s5_b_glu[l].reshape(-1, 1), wo,
        row(ffn2_norm[l]), wg2, wu2, wd2, row(final_norm))
```

```python
import functools

import jax
import jax.numpy as jnp
from jax import lax
from jax.experimental import pallas as pl
from jax.experimental.pallas import tpu as pltpu

D_MODEL = 1024
D_FF = 2816
S5_WIDTH = 512
S5_GROUP_CH = 16
S5_GROUPS = S5_WIDTH // S5_GROUP_CH
S5_STATE = 64
CONV_WIDTH = 512
CONV_HEAD_DIM = 64
CONV_HEADS = CONV_WIDTH // CONV_HEAD_DIM
CONV_K = 31
IN_COLS = S5_WIDTH + 2 * CONV_WIDTH
EPS = 1e-6

LANES = 128
SUBLANES = 8
CHUNK = LANES
S5_T = 64
S5_HALVES = CHUNK // S5_T
S5_PAIRS = S5_GROUP_CH // 2
S5_LANES = S5_GROUP_CH * S5_T
S5_STEP_GROUPS = 2
TILE_SEQS = 8
ROW_TILE = TILE_SEQS * CHUNK
FF_SPLITS = ((0, 1536), (1536, 2816))
LN_ROWS = 64
LN_UNROLL = 16
VMEM_LIMIT = 60 * 1024 * 1024
CAST_CHUNKS = 16

BF16 = jnp.bfloat16
F32 = jnp.float32

(_V_BR,
 _V_BI_NP,
 _V_BI_PN,
 _V_BI_PP,
 _V_CR_PN,
 _V_CI_NN,
 _V_C_RE_NIM,
 _V_C_NIM_NRE,
 _V_D,
 _V_COUNT) = range(10)

_PC_TAPS, _PC_BIAS, _PC_HALF_GAIN, _PC_HALF_BETA, _PC_COUNT = range(5)

_NT = (((1,), (1,)), ((), ()))
_TN = (((0,), (0,)), ((), ()))


def _rms_norm(x, g):
    return x * lax.rsqrt(jnp.mean(x * x, axis=-1, keepdims=True) + EPS) * g


def _dot(a, b):
    return jnp.dot(a, b, preferred_element_type=F32)


def _rms_split(x, g):
    return (x * g).astype(BF16), lax.rsqrt(jnp.mean(x * x, axis=-1, keepdims=True) + EPS)


def _swiglu(h, row_scale, wg_ref, wu_ref, wd_ref):
    acc = None
    for lo, hi in FF_SPLITS:
        g = _dot(h, wg_ref[:, lo:hi]) * row_scale
        u = _dot(h, wu_ref[:, lo:hi]) * row_scale
        a = (g * jax.nn.sigmoid(g) * u).astype(BF16)
        part = _dot(a, wd_ref[lo:hi, :])
        acc = part if acc is None else acc + part
    return acc


def _store_slabs(slab_ref, val, first_seq):
    n_ch = slab_ref.shape[0]
    rows = slab_ref.reshape(n_ch * SUBLANES, LANES)
    for ct in range(n_ch // SUBLANES):
        for s in range(TILE_SEQS):
            rows[pl.ds(ct * SUBLANES * SUBLANES + first_seq + s, SUBLANES, stride=SUBLANES), :] = (
                val[ct * SUBLANES:(ct + 1) * SUBLANES, s * LANES:(s + 1) * LANES])


def _load_slabs(slab_ref, first_seq):
    n_ch = slab_ref.shape[0]
    rows = slab_ref.reshape(n_ch * SUBLANES, LANES)
    bands = []
    for ct in range(n_ch // SUBLANES):
        bands.append(jnp.concatenate(
            [rows[pl.ds(ct * SUBLANES * SUBLANES + first_seq + s, SUBLANES, stride=SUBLANES), :]
             for s in range(TILE_SEQS)], axis=1))
    return jnp.concatenate(bands, axis=0)


class _CastStream:
    def __init__(self, pairs, stage_ref, sem_ref):
        self.stage_ref, self.sem_ref = stage_ref, sem_ref
        rows = stage_ref.shape[1]
        per_weight = pairs[0][0].shape[0] // rows
        self.chunks = [(w_hbm, w_vmem, i * rows, rows) for i in range(per_weight) for w_hbm, w_vmem in pairs]

    def _copy(self, k):
        w_hbm, _, row0, rows = self.chunks[k]
        return pltpu.make_async_copy(w_hbm.at[pl.ds(row0, rows)], self.stage_ref.at[k % 2], self.sem_ref.at[k % 2])

    def start(self, k):
        if k < len(self.chunks):
            self._copy(k).start()

    def finish(self, k):
        if k < len(self.chunks):
            self._copy(k).wait()
            _, w_vmem, row0, rows = self.chunks[k]
            w_vmem[row0:row0 + rows, :] = self.stage_ref[k % 2].astype(BF16)
            self.start(k + 2)


def _cast_weights_in(streams):
    for st in streams:
        st.start(0)
        st.start(1)
    for k in range(max(len(st.chunks) for st in streams)):
        for st in streams:
            st.finish(k)


def _ffn1_in_proj_kernel(x_ref, g1_ref, wg_hbm, wu_hbm, wd_hbm, gm_ref, win_t_hbm,
                         x1_ref, us_ref, zs_ref,
                         wg_ref, wu_ref, wd_ref, win_t_ref, stage_ff_ref, stage_d_ref, stage_in_ref,
                         sem_ff, sem_d, sem_in):
    @pl.when(pl.program_id(0) == 0)
    def _():
        _cast_weights_in([
            _CastStream([(wg_hbm, wg_ref), (wu_hbm, wu_ref)], stage_ff_ref, sem_ff),
            _CastStream([(wd_hbm, wd_ref)], stage_d_ref, sem_d),
            _CastStream([(win_t_hbm, win_t_ref)], stage_in_ref, sem_in)])

    first_seq = pl.program_id(1) * TILE_SEQS
    x = x_ref[...].reshape(ROW_TILE, D_MODEL)
    h, r = _rms_split(x, g1_ref[...])
    x1 = x + 0.5 * _swiglu(h, r, wg_ref, wu_ref, wd_ref)
    x1_ref[...] = x1.reshape(TILE_SEQS, CHUNK, D_MODEL)
    h2, r2 = _rms_split(x1, gm_ref[...])
    r2_lanes = jnp.transpose(jnp.broadcast_to(r2, (ROW_TILE, LANES)))[0:1, :]

    def in_proj_t(lo, hi):
        return lax.dot_general(win_t_ref[lo:hi, :], h2, _NT, preferred_element_type=F32) * r2_lanes

    gate = jax.nn.sigmoid(in_proj_t(S5_WIDTH + CONV_WIDTH, IN_COLS))
    _store_slabs(us_ref, in_proj_t(0, S5_WIDTH), first_seq)
    z_t = in_proj_t(S5_WIDTH, S5_WIDTH + CONV_WIDTH) * gate
    _store_slabs(zs_ref, z_t, first_seq)


def _regroup_halves(a, b):
    first_half = lax.broadcasted_iota(jnp.int32, (1, LANES), 1) < S5_T
    straight = jnp.where(first_half, a, b)
    crossed = pltpu.roll(jnp.where(first_half, b, a), S5_T, 1)
    return jnp.where(first_half, straight, crossed), jnp.where(first_half, crossed, straight)


def _s5_mix_kernel(us_ref, pow_ref, vec_ref, *rest, n_cast, batch):
    ys_ref = rest[n_cast]
    scratch = rest[2 * n_cast + 1:]
    for src_ref, dst_ref in zip(rest[:n_cast], rest[n_cast + 1:2 * n_cast + 1]):
        dst_ref[...] = src_ref[...].astype(BF16)
    for g in range(S5_STEP_GROUPS):
        ch = pl.ds(g * S5_GROUP_CH, S5_GROUP_CH)
        _s5_group(us_ref.at[ch], pow_ref.at[g], vec_ref.at[g], ys_ref.at[ch], *scratch, batch=batch)


def _s5_group(us_ref, pow_ref, vec_ref, ys_ref,
              cb_ref, f_ref, upk_ref, lhs_ref, win_ref, wout_ref, sprev_ref, *, batch):
    T = S5_T
    n_slab_rows = us_ref.shape[1]
    n_slab_chunks = n_slab_rows // batch
    n_rows = S5_HALVES * n_slab_rows
    lane = lax.broadcasted_iota(jnp.int32, (1, LANES), 1)
    first_state_half = lane < S5_STATE
    swap = lambda v: pltpu.roll(v, S5_STATE, 1)
    pa_rev = pow_ref[0:T, :]
    pb_rev = swap(pa_rev)
    pa_fwd = pow_ref[T + 1:2 * T + 1, :]
    pb_fwd = swap(pa_fwd)
    for c in range(S5_GROUP_CH):
        rows = slice(c * T, (c + 1) * T)
        vec = lambda i: vec_ref[i, c:c + 1, :]
        win_ref[rows, 0:2 * S5_STATE] = (pa_rev * vec(_V_BR) + pb_rev * vec(_V_BI_NP)).astype(BF16)
        win_ref[rows, 2 * S5_STATE:] = (pb_rev * vec(_V_BR) + pa_rev * vec(_V_BI_PN)).astype(BF16)
        wout_ref[rows, :] = (pa_fwd * vec(_V_CR_PN) + pb_fwd * vec(_V_CI_NN)).astype(BF16)
        cb_ref[c * S5_GROUP_CH:(c + 1) * S5_GROUP_CH, :] = (
            vec_ref[_V_C_RE_NIM] * vec(_V_BR) + vec_ref[_V_C_NIM_NRE] * vec(_V_BI_PP))
    f = lax.dot_general(cb_ref[...], pow_ref[T:2 * T, :], _NT, precision=lax.Precision.HIGHEST,
                        preferred_element_type=F32)
    for c in range(S5_GROUP_CH):
        even, odd = c * S5_GROUP_CH, c * S5_GROUP_CH + S5_PAIRS
        f_ref[c * S5_PAIRS:(c + 1) * S5_PAIRS, :] = jnp.concatenate(
            [f[even:even + S5_PAIRS], f[odd:odd + S5_PAIRS]], axis=1)

    for p in range(S5_PAIRS):
        halves = _regroup_halves(us_ref[2 * p], us_ref[2 * p + 1])
        for hh in range(S5_HALVES):
            upk_ref[:, hh * batch:(hh + 1) * batch, p * LANES:(p + 1) * LANES] = (
                halves[hh].reshape(n_slab_chunks, batch, LANES))
    lhs_ref[...] = upk_ref[...].reshape(n_rows, S5_LANES).astype(BF16)

    e = _dot(lhs_ref[...], win_ref[...])
    a_chunk = pow_ref[2 * T:2 * T + 1, :]
    a_sw = swap(a_chunk)
    a1 = jnp.where(first_state_half, a_chunk, a_sw)
    a2 = jnp.where(first_state_half, -a_sw, a_chunk)
    s = jnp.zeros((batch, 2 * S5_STATE), F32)
    s_sw = jnp.zeros((batch, 2 * S5_STATE), F32)
    for c in range(n_rows // batch):
        rows = slice(c * batch, (c + 1) * batch)
        sprev_ref[rows, :] = s
        e_c = e[rows, :]
        s, s_sw = (a1 * s + a2 * s_sw + e_c[:, :2 * S5_STATE],
                   a1 * s_sw - a2 * s + e_c[:, 2 * S5_STATE:])
    sprev = sprev_ref[...].astype(BF16)

    causal = ((lax.broadcasted_iota(jnp.int32, (T, LANES), 1) & (T - 1))
              >= lax.broadcasted_iota(jnp.int32, (T, LANES), 0))

    def toeplitz(ci, pair):
        f_row = jnp.broadcast_to(f_ref[ci * S5_PAIRS + pair:ci * S5_PAIRS + pair + 1, :], (T, LANES))
        return jnp.where(causal, pltpu.roll(f_row, 0, 1, stride=1, stride_axis=0), 0.0).astype(BF16)

    ci_per_tile = 2 * LANES // T
    for n in range(S5_PAIRS // 2):
        acc = lax.dot_general(sprev, wout_ref[2 * n * LANES:(2 * n + 2) * LANES, :], _NT,
                              preferred_element_type=F32)
        for kt in range(S5_GROUP_CH // ci_per_tile):
            w = jnp.concatenate(
                [jnp.concatenate([toeplitz(ci_per_tile * kt + r, 2 * n), toeplitz(ci_per_tile * kt + r, 2 * n + 1)],
                                 axis=1) for r in range(ci_per_tile)], axis=0)
            acc = acc + _dot(lhs_ref[:, 2 * kt * LANES:(2 * kt + 2) * LANES], w)
        for q in range(2):
            pair = 2 * n + q
            u_pk = upk_ref[:, :, pair * LANES:(pair + 1) * LANES].reshape(n_rows, LANES)
            y = acc[:, q * LANES:(q + 1) * LANES] + vec_ref[_V_D, pair:pair + 1, :] * u_pk
            y = jax.nn.gelu(y, approximate=True).reshape(n_slab_chunks, S5_HALVES * batch, LANES)
            y0 = y[:, 0:batch, :].reshape(n_slab_rows, LANES)
            y1 = y[:, batch:2 * batch, :].reshape(n_slab_rows, LANES)
            ys_ref[2 * pair], ys_ref[2 * pair + 1] = _regroup_halves(y0, y1)


def _conv_mix_kernel(zs_ref, pc_ref, o_ref, *, batch):
    n_rows = zs_ref.shape[1]
    row = lax.broadcasted_iota(jnp.int32, (CHUNK, CHUNK), 0)
    col = lax.broadcasted_iota(jnp.int32, (CHUNK, CHUNK), 1)
    lag = row - col
    band_cur = (lag >= -(CONV_K - 1)) & (lag <= 0)
    band_prev = lag >= CHUNK - (CONV_K - 1)

    def conv_channel(ch, carry):
        z = zs_ref[ch]
        z_prev = jnp.concatenate([jnp.zeros((batch, CHUNK), F32), z[:n_rows - batch]], axis=0)
        lhs = jnp.concatenate([z_prev, z], axis=1).astype(BF16)
        taps = jnp.broadcast_to(pc_ref[ch, _PC_TAPS:_PC_TAPS + 1, :], (CHUNK, CHUNK))
        diag = pltpu.roll(taps, 0, 1, stride=1, stride_axis=0)
        toe = jnp.concatenate([jnp.where(band_prev, diag, 0.0), jnp.where(band_cur, diag, 0.0)],
                              axis=0).astype(BF16)
        o_ref[ch] = _dot(lhs, toe) + pc_ref[ch, _PC_BIAS:_PC_BIAS + 1, :]
        return carry

    lax.fori_loop(0, CONV_HEAD_DIM, conv_channel, 0, unroll=32)

    inv_n = 1.0 / CONV_HEAD_DIM
    for r0 in range(0, n_rows, LN_ROWS):
        rows = pl.ds(r0, LN_ROWS)
        zero = jnp.zeros((LN_ROWS, CHUNK), F32)
        mu = lax.fori_loop(0, CONV_HEAD_DIM, lambda ch, acc: acc + o_ref[ch, rows, :], zero,
                           unroll=LN_UNROLL) * inv_n
        var = lax.fori_loop(0, CONV_HEAD_DIM, lambda ch, acc: acc + jnp.square(o_ref[ch, rows, :] - mu),
                            zero, unroll=LN_UNROLL) * inv_n
        rstd = lax.rsqrt(var + EPS)

        def norm_channel(ch, carry):
            half = ((o_ref[ch, rows, :] - mu) * rstd * pc_ref[ch, _PC_HALF_GAIN:_PC_HALF_GAIN + 1, :]
                    + pc_ref[ch, _PC_HALF_BETA:_PC_HALF_BETA + 1, :])
            o_ref[ch, rows, :] = half + half * jnp.tanh(half)
            return carry

        lax.fori_loop(0, CONV_HEAD_DIM, norm_channel, 0, unroll=LN_UNROLL)


def _out_ffn2_norm_kernel(x1_ref, ys_ref, yc_ref, wglu_t_ref, bglu_ref, wo_ref,
                          g2_ref, wg_ref, wu_ref, wd_ref, gf_ref, o_ref):
    first_seq = pl.program_id(1) * TILE_SEQS
    ys_t = _load_slabs(ys_ref, first_seq)
    yc_t = _load_slabs(yc_ref, first_seq)
    x2 = x1_ref[...].reshape(ROW_TILE, D_MODEL) + lax.dot_general(
        yc_t.astype(BF16), wo_ref[S5_WIDTH:, :], _TN, preferred_element_type=F32)
    gate_t = jax.nn.sigmoid(_dot(wglu_t_ref[...], ys_t.astype(BF16)) + bglu_ref[...])
    x2 = x2 + lax.dot_general((ys_t * gate_t).astype(BF16), wo_ref[0:S5_WIDTH, :], _TN,
                              preferred_element_type=F32)
    h, r = _rms_split(x2, g2_ref[...])
    x3 = x2 + 0.5 * _swiglu(h, r, wg_ref, wu_ref, wd_ref)
    o_ref[...] = _rms_norm(x3, gf_ref[...]).reshape(TILE_SEQS, CHUNK, D_MODEL)


def _resident(shape):
    nd = len(shape)
    return pl.BlockSpec(shape, lambda *_: (0,) * nd, pipeline_mode=pl.Buffered(1))


def _tile_spec():
    return pl.BlockSpec((TILE_SEQS, CHUNK, D_MODEL), lambda c, h: (h, c, 0))


def _slab_spec(n_ch):
    return pl.BlockSpec((n_ch, SUBLANES, LANES), lambda c, h: (0, c, 0))


_DENSE_PARAMS = pltpu.CompilerParams(dimension_semantics=("arbitrary", "arbitrary"),
                                     vmem_limit_bytes=VMEM_LIMIT)
_MIX_PARAMS = pltpu.CompilerParams(dimension_semantics=("arbitrary",), vmem_limit_bytes=VMEM_LIMIT)


def _cast_spec(w, n_steps):
    n_blocks = n_steps
    while w.shape[0] % n_blocks or (w.shape[0] // n_blocks) % (2 * SUBLANES):
        n_blocks //= 2
    steps_per_block = n_steps // n_blocks
    return pl.BlockSpec((w.shape[0] // n_blocks, w.shape[1]), lambda g: (g // steps_per_block, 0))


def _ffn1_in_proj(x, g1, wg, wu, wd, gm, win_t):
    batch, seq, _ = x.shape
    assert batch == SUBLANES and seq % CHUNK == 0
    n_rows = (seq // CHUNK) * batch
    return pl.pallas_call(
        _ffn1_in_proj_kernel,
        grid=(seq // CHUNK, batch // TILE_SEQS),
        in_specs=[
            _tile_spec(),
            _resident((1, D_MODEL)),
            pl.BlockSpec(memory_space=pl.ANY),
            pl.BlockSpec(memory_space=pl.ANY),
            pl.BlockSpec(memory_space=pl.ANY),
            _resident((1, D_MODEL)),
            pl.BlockSpec(memory_space=pl.ANY),
        ],
        out_specs=[_tile_spec(), _slab_spec(S5_WIDTH), _slab_spec(CONV_WIDTH)],
        out_shape=[
            jax.ShapeDtypeStruct((batch, seq, D_MODEL), F32),
            jax.ShapeDtypeStruct((S5_WIDTH, n_rows, LANES), F32),
            jax.ShapeDtypeStruct((CONV_WIDTH, n_rows, LANES), F32),
        ],
        scratch_shapes=[
            pltpu.VMEM((D_MODEL, D_FF), BF16), pltpu.VMEM((D_MODEL, D_FF), BF16),
            pltpu.VMEM((D_FF, D_MODEL), BF16), pltpu.VMEM((IN_COLS, D_MODEL), BF16),
            pltpu.VMEM((2, D_MODEL // CAST_CHUNKS, D_FF), F32),
            pltpu.VMEM((2, D_FF // CAST_CHUNKS, D_MODEL), F32),
            pltpu.VMEM((2, IN_COLS // CAST_CHUNKS, D_MODEL), F32),
            pltpu.SemaphoreType.DMA((2,)), pltpu.SemaphoreType.DMA((2,)), pltpu.SemaphoreType.DMA((2,)),
        ],
        compiler_params=_DENSE_PARAMS,
        name="ffn1_in_proj",
    )(x, g1, wg, wu, wd, gm, win_t)


def _s5_mix(us, tables, later_weights, *, batch):
    n_rows = us.shape[1]
    n_steps = S5_GROUPS // S5_STEP_GROUPS
    cast_specs = [_cast_spec(w, n_steps) for w in later_weights]
    slab = pl.BlockSpec((S5_STEP_GROUPS * S5_GROUP_CH, n_rows, LANES), lambda g: (g, 0, 0))

    def per_group(t):
        nd = t.ndim - 1
        return pl.BlockSpec((S5_STEP_GROUPS,) + t.shape[1:], lambda g: (g,) + (0,) * nd)

    return pl.pallas_call(
        functools.partial(_s5_mix_kernel, n_cast=len(later_weights), batch=batch),
        grid=(n_steps,),
        in_specs=[slab] + [per_group(t) for t in tables] + cast_specs,
        out_specs=[slab] + cast_specs,
        out_shape=[jax.ShapeDtypeStruct((S5_WIDTH, n_rows, LANES), F32)]
        + [jax.ShapeDtypeStruct(w.shape, BF16) for w in later_weights],
        scratch_shapes=[pltpu.VMEM((S5_GROUP_CH * S5_GROUP_CH, 2 * S5_STATE), F32),
                        pltpu.VMEM((S5_GROUP_CH * S5_PAIRS, LANES), F32),
                        pltpu.VMEM((n_rows // batch, S5_HALVES * batch, S5_LANES), F32),
                        pltpu.VMEM((S5_HALVES * n_rows, S5_LANES), BF16),
                        pltpu.VMEM((S5_LANES, 4 * S5_STATE), BF16),
                        pltpu.VMEM((S5_LANES, 2 * S5_STATE), BF16),
                        pltpu.VMEM((S5_HALVES * n_rows, 2 * S5_STATE), F32)],
        compiler_params=_MIX_PARAMS,
        name="s5_mix",
    )(us, *tables, *later_weights)


def _conv_mix(zs, per_channel, *, batch):
    n_rows = zs.shape[1]
    slab = pl.BlockSpec((CONV_HEAD_DIM, n_rows, LANES), lambda h: (h, 0, 0))
    per_ch = pl.BlockSpec((CONV_HEAD_DIM, _PC_COUNT, LANES), lambda h: (h, 0, 0))
    return pl.pallas_call(
        functools.partial(_conv_mix_kernel, batch=batch),
        grid=(CONV_HEADS,),
        in_specs=[slab, per_ch],
        out_specs=slab,
        out_shape=jax.ShapeDtypeStruct((CONV_WIDTH, n_rows, LANES), F32),
        compiler_params=_MIX_PARAMS,
        name="conv_mix",
    )(zs, per_channel)


def _out_ffn2_norm(x1, ys, yc, wglu_t, bglu, wo, g2, wg, wu, wd, gf):
    batch, seq, _ = x1.shape
    return pl.pallas_call(
        _out_ffn2_norm_kernel,
        grid=(seq // CHUNK, batch // TILE_SEQS),
        in_specs=[
            _tile_spec(),
            _slab_spec(S5_WIDTH),
            _slab_spec(CONV_WIDTH),
            _resident((S5_WIDTH, S5_WIDTH)),
            _resident((S5_WIDTH, 1)),
            _resident((S5_WIDTH + CONV_WIDTH, D_MODEL)),
            _resident((1, D_MODEL)),
            _resident((D_MODEL, D_FF)),
            _resident((D_MODEL, D_FF)),
            _resident((D_FF, D_MODEL)),
            _resident((1, D_MODEL)),
        ],
        out_specs=_tile_spec(),
        out_shape=jax.ShapeDtypeStruct((batch, seq, D_MODEL), F32),
        compiler_params=_DENSE_PARAMS,
        name="out_ffn2_norm",
    )(x1, ys, yc, wglu_t, bglu, wo, g2, wg, wu, wd, gf)


def _s5_tables(lam_re, lam_im, log_dt, b_re, b_im, c_re, c_im, d_skip):
    T, C, P, G = S5_T, S5_GROUP_CH, S5_STATE, S5_GROUPS
    dt = jnp.exp(log_dt)[:, None]
    mag = jnp.exp(lam_re * dt)
    abar_re = mag * jnp.cos(lam_im * dt)
    abar_im = mag * jnp.sin(lam_im * dt)
    den = lam_re * lam_re + lam_im * lam_im
    num_re = abar_re - 1.0
    num_im = abar_im
    f_re = ((num_re * lam_re + num_im * lam_im) / den)[..., None]
    f_im = ((num_im * lam_re - num_re * lam_im) / den)[..., None]
    bb_re = f_re * b_re - f_im * b_im
    bb_im = f_re * b_im + f_im * b_re
    n_q = T // SUBLANES
    expo = jnp.concatenate([jnp.arange(SUBLANES), SUBLANES * jnp.arange(1, n_q + 1)]).astype(F32)[None, :, None]
    bmag = jnp.exp(expo * (lam_re * dt)[:, None, :])
    bang = expo * (lam_im * dt)[:, None, :]
    base_re, base_im = bmag * jnp.cos(bang), bmag * jnp.sin(bang)
    lo_re, lo_im = base_re[:, :SUBLANES], base_im[:, :SUBLANES]
    hi_re = jnp.concatenate([base_re[:, :1], base_re[:, SUBLANES:]], axis=1)
    hi_im = jnp.concatenate([base_im[:, :1], base_im[:, SUBLANES:]], axis=1)

    def outer(q_re, q_im, r_re, r_im):
        re = q_re[:, :, None] * r_re[:, None] - q_im[:, :, None] * r_im[:, None]
        im = q_re[:, :, None] * r_im[:, None] + q_im[:, :, None] * r_re[:, None]
        return jnp.concatenate([re, im], axis=-1).reshape(G, -1, 2 * P)

    cat = lambda a, b: jnp.concatenate([a, b], axis=-1)
    powers = jnp.concatenate([
        outer(hi_re[:, n_q - 1::-1], hi_im[:, n_q - 1::-1], lo_re[:, ::-1], lo_im[:, ::-1]),
        outer(hi_re, hi_im, lo_re, lo_im)], axis=1)
    bt_re = bb_re.transpose(0, 2, 1)
    bt_im = bb_im.transpose(0, 2, 1)
    d_rows = jnp.pad(jnp.repeat(d_skip.reshape(G, C // 2, 2), T, axis=-1), ((0, 0), (0, C // 2), (0, 0)))
    cp_re = jnp.concatenate([c_re[:, 0::2], c_re[:, 1::2]], axis=1)
    cp_im = jnp.concatenate([c_im[:, 0::2], c_im[:, 1::2]], axis=1)
    vecs = [None] * _V_COUNT
    vecs[_V_BR] = cat(bt_re, bt_re)
    vecs[_V_BI_NP] = cat(-bt_im, bt_im)
    vecs[_V_BI_PN] = cat(bt_im, -bt_im)
    vecs[_V_BI_PP] = cat(bt_im, bt_im)
    vecs[_V_CR_PN] = cat(c_re, -c_re)
    vecs[_V_CI_NN] = cat(-c_im, -c_im)
    vecs[_V_C_RE_NIM] = cat(cp_re, -cp_im)
    vecs[_V_C_NIM_NRE] = cat(-cp_im, -cp_re)
    vecs[_V_D] = d_rows
    return powers, jnp.stack(vecs, axis=1)


def _conv_tables(w_dw, b_dw, ln_g, ln_b):
    taps = jnp.pad(w_dw[::-1].T, ((0, 0), (0, LANES - CONV_K)))
    lanes = lambda v: jnp.broadcast_to(v[:, None], (CONV_WIDTH, LANES))
    rows = [None] * _PC_COUNT
    rows[_PC_TAPS], rows[_PC_BIAS] = taps, lanes(b_dw)
    rows[_PC_HALF_GAIN], rows[_PC_HALF_BETA] = lanes(0.5 * ln_g), lanes(0.5 * ln_b)
    return jnp.stack(rows, axis=1)


def kernel(x, ffn1_norm, ffn1_w_gate, ffn1_w_up, ffn1_w_down, mix_norm, w_in, s5_lam_re, s5_lam_im, s5_log_dt, s5_b_re, s5_b_im, s5_c_re, s5_c_im, s5_d, s5_w_glu, s5_b_glu, conv_w_dw, conv_b_dw, conv_ln_g, conv_ln_b, w_out, ffn2_norm, ffn2_w_gate, ffn2_w_up, ffn2_w_down, final_norm):
    batch, seq, d_model = x.shape
    assert ffn1_norm.shape[0] == 1 and d_model == D_MODEL
    l = 0
    row = lambda v: v.reshape(1, -1)

    x1, us, zs = _ffn1_in_proj(
        x, row(ffn1_norm[l]), ffn1_w_gate[l], ffn1_w_up[l], ffn1_w_down[l], row(mix_norm[l]), w_in[l].T)

    tables = _s5_tables(
        s5_lam_re[l], s5_lam_im[l], s5_log_dt[l], s5_b_re[l], s5_b_im[l], s5_c_re[l], s5_c_im[l], s5_d[l])
    ys, wg2, wu2, wd2, wo = _s5_mix(
        us, tables, (ffn2_w_gate[l], ffn2_w_up[l], ffn2_w_down[l], w_out[l]), batch=batch)

    yc = _conv_mix(zs, _conv_tables(conv_w_dw[l], conv_b_dw[l], conv_ln_g[l], conv_ln_b[l]), batch=batch)

    return _out_ffn2_norm(
        x1, ys, yc, s5_w_glu[l].T.astype(BF16), s5_b_glu[l].reshape(-1, 1), wo,
        row(ffn2_norm[l]), wg2, wu2, wd2, row(final_norm))
```

```python
import functools

import jax
import jax.numpy as jnp
from jax import lax
from jax.experimental import pallas as pl
from jax.experimental.pallas import tpu as pltpu

D_MODEL = 1024
D_FF = 2816
S5_WIDTH = 512
S5_GROUP_CH = 16
S5_GROUPS = S5_WIDTH // S5_GROUP_CH
S5_STATE = 64
CONV_WIDTH = 512
CONV_HEAD_DIM = 64
CONV_HEADS = CONV_WIDTH // CONV_HEAD_DIM
CONV_K = 31
IN_COLS = S5_WIDTH + 2 * CONV_WIDTH
EPS = 1e-6

LANES = 128
SUBLANES = 8
CHUNK = LANES
S5_T = 64
S5_HALVES = CHUNK // S5_T
S5_PAIRS = S5_GROUP_CH // 2
S5_LANES = S5_GROUP_CH * S5_T
S5_STEP_GROUPS = 2
TILE_SEQS = 8
ROW_TILE = TILE_SEQS * CHUNK
FF_SPLITS = ((0, 1536), (1536, 2816))
LN_ROWS = 64
LN_UNROLL = 16
VMEM_LIMIT = 60 * 1024 * 1024

BF16 = jnp.bfloat16
F32 = jnp.float32

(_V_BR,
 _V_BI_NP,
 _V_BI_PN,
 _V_BI_PP,
 _V_CR_PN,
 _V_CI_NN,
 _V_C_RE_NIM,
 _V_C_NIM_NRE,
 _V_D,
 _V_COUNT) = range(10)

_PC_TAPS, _PC_BIAS, _PC_HALF_GAIN, _PC_HALF_BETA, _PC_COUNT = range(5)

_NT = (((1,), (1,)), ((), ()))
_TN = (((0,), (0,)), ((), ()))


def _rms_norm(x, g):
    return x * lax.rsqrt(jnp.mean(x * x, axis=-1, keepdims=True) + EPS) * g


def _dot(a, b):
    return jnp.dot(a, b, preferred_element_type=F32)


def _rms_split(x, g):
    return (x * g).astype(BF16), lax.rsqrt(jnp.mean(x * x, axis=-1, keepdims=True) + EPS)


def _swiglu(h, row_scale, wg_ref, wu_ref, wd_ref):
    acc = None
    for lo, hi in FF_SPLITS:
        g = (_dot(h, wg_ref[:, lo:hi]) * row_scale).astype(BF16)
        u = (_dot(h, wu_ref[:, lo:hi]) * row_scale).astype(BF16)
        a = g * jax.nn.sigmoid(g) * u
        part = _dot(a, wd_ref[lo:hi, :])
        acc = part if acc is None else acc + part
    return acc


def _store_slabs(slab_ref, val, first_seq):
    n_ch = slab_ref.shape[0]
    rows = slab_ref.reshape(n_ch * SUBLANES, LANES)
    for ct in range(n_ch // SUBLANES):
        for s in range(TILE_SEQS):
            rows[pl.ds(ct * SUBLANES * SUBLANES + first_seq + s, SUBLANES, stride=SUBLANES), :] = (
                val[ct * SUBLANES:(ct + 1) * SUBLANES, s * LANES:(s + 1) * LANES])


def _load_slabs(slab_ref, first_seq):
    n_ch = slab_ref.shape[0]
    rows = slab_ref.reshape(n_ch * SUBLANES, LANES)
    bands = []
    for ct in range(n_ch // SUBLANES):
        bands.append(jnp.concatenate(
            [rows[pl.ds(ct * SUBLANES * SUBLANES + first_seq + s, SUBLANES, stride=SUBLANES), :]
             for s in range(TILE_SEQS)], axis=1))
    return jnp.concatenate(bands, axis=0)


def _ffn1_in_proj_kernel(x_ref, g1_ref, wg_ref, wu_ref, wd_ref, gm_ref, win_t_ref, *rest):
    n_cast = (len(rest) - 3) // 2
    x1_ref, us_ref, zs_ref = rest[n_cast:n_cast + 3]
    first_seq = pl.program_id(1) * TILE_SEQS
    x = x_ref[...].reshape(ROW_TILE, D_MODEL)
    h, r = _rms_split(x, g1_ref[...])
    x1 = x + 0.5 * _swiglu(h, r, wg_ref, wu_ref, wd_ref)
    x1_ref[...] = x1.reshape(TILE_SEQS, CHUNK, D_MODEL)
    h2, r2 = _rms_split(x1, gm_ref[...])
    r2_lanes = jnp.transpose(jnp.broadcast_to(r2, (ROW_TILE, LANES)))[0:1, :]

    def in_proj_t(lo, hi):
        return lax.dot_general(win_t_ref[lo:hi, :], h2, _NT, preferred_element_type=F32) * r2_lanes

    gate = jax.nn.sigmoid(in_proj_t(S5_WIDTH + CONV_WIDTH, IN_COLS))
    _store_slabs(us_ref, in_proj_t(0, S5_WIDTH), first_seq)
    z_t = in_proj_t(S5_WIDTH, S5_WIDTH + CONV_WIDTH) * gate
    _store_slabs(zs_ref, z_t, first_seq)
    for src_ref, dst_ref in zip(rest[:n_cast], rest[n_cast + 3:]):
        dst_ref[...] = src_ref[...].astype(BF16)


def _regroup_halves(a, b):
    first_half = lax.broadcasted_iota(jnp.int32, (1, LANES), 1) < S5_T
    straight = jnp.where(first_half, a, b)
    crossed = pltpu.roll(jnp.where(first_half, b, a), S5_T, 1)
    return jnp.where(first_half, straight, crossed), jnp.where(first_half, crossed, straight)


def _s5_mix_kernel(us_ref, pow_ref, vec_ref, ys_ref, *scratch, batch):
    for g in range(S5_STEP_GROUPS):
        ch = pl.ds(g * S5_GROUP_CH, S5_GROUP_CH)
        _s5_group(us_ref.at[ch], pow_ref.at[g], vec_ref.at[g], ys_ref.at[ch], *scratch, batch=batch)


def _s5_group(us_ref, pow_ref, vec_ref, ys_ref,
              cb_ref, f_ref, upk_ref, lhs_ref, win_ref, wout_ref, sprev_ref, *, batch):
    T = S5_T
    n_slab_rows = us_ref.shape[1]
    n_slab_chunks = n_slab_rows // batch
    n_rows = S5_HALVES * n_slab_rows
    lane = lax.broadcasted_iota(jnp.int32, (1, LANES), 1)
    first_state_half = lane < S5_STATE
    swap = lambda v: pltpu.roll(v, S5_STATE, 1)
    pa_rev = pow_ref[0:T, :]
    pb_rev = swap(pa_rev)
    pa_fwd = pow_ref[T + 1:2 * T + 1, :]
    pb_fwd = swap(pa_fwd)
    for c in range(S5_GROUP_CH):
        rows = slice(c * T, (c + 1) * T)
        vec = lambda i: vec_ref[i, c:c + 1, :]
        win_ref[rows, 0:2 * S5_STATE] = (pa_rev * vec(_V_BR) + pb_rev * vec(_V_BI_NP)).astype(BF16)
        win_ref[rows, 2 * S5_STATE:] = (pb_rev * vec(_V_BR) + pa_rev * vec(_V_BI_PN)).astype(BF16)
        wout_ref[rows, :] = (pa_fwd * vec(_V_CR_PN) + pb_fwd * vec(_V_CI_NN)).astype(BF16)
        cb_ref[c * S5_GROUP_CH:(c + 1) * S5_GROUP_CH, :] = (
            vec_ref[_V_C_RE_NIM] * vec(_V_BR) + vec_ref[_V_C_NIM_NRE] * vec(_V_BI_PP))
    f = lax.dot_general(cb_ref[...], pow_ref[T:2 * T, :], _NT, precision=lax.Precision.HIGHEST,
                        preferred_element_type=F32)
    for c in range(S5_GROUP_CH):
        even, odd = c * S5_GROUP_CH, c * S5_GROUP_CH + S5_PAIRS
        f_ref[c * S5_PAIRS:(c + 1) * S5_PAIRS, :] = jnp.concatenate(
            [f[even:even + S5_PAIRS], f[odd:odd + S5_PAIRS]], axis=1)

    for p in range(S5_PAIRS):
        halves = _regroup_halves(us_ref[2 * p], us_ref[2 * p + 1])
        for hh in range(S5_HALVES):
            upk_ref[:, hh * batch:(hh + 1) * batch, p * LANES:(p + 1) * LANES] = (
                halves[hh].reshape(n_slab_chunks, batch, LANES))
    lhs_ref[...] = upk_ref[...].reshape(n_rows, S5_LANES).astype(BF16)

    e = _dot(lhs_ref[...], win_ref[...])
    a_chunk = pow_ref[2 * T:2 * T + 1, :]
    a_sw = swap(a_chunk)
    a1 = jnp.where(first_state_half, a_chunk, a_sw)
    a2 = jnp.where(first_state_half, -a_sw, a_chunk)
    s = jnp.zeros((batch, 2 * S5_STATE), F32)
    s_sw = jnp.zeros((batch, 2 * S5_STATE), F32)
    for c in range(n_rows // batch):
        rows = slice(c * batch, (c + 1) * batch)
        sprev_ref[rows, :] = s
        e_c = e[rows, :]
        s, s_sw = (a1 * s + a2 * s_sw + e_c[:, :2 * S5_STATE],
                   a1 * s_sw - a2 * s + e_c[:, 2 * S5_STATE:])
    sprev = sprev_ref[...].astype(BF16)

    causal = ((lax.broadcasted_iota(jnp.int32, (T, LANES), 1) & (T - 1))
              >= lax.broadcasted_iota(jnp.int32, (T, LANES), 0))

    def toeplitz(ci, pair):
        f_row = jnp.broadcast_to(f_ref[ci * S5_PAIRS + pair:ci * S5_PAIRS + pair + 1, :], (T, LANES))
        return jnp.where(causal, pltpu.roll(f_row, 0, 1, stride=1, stride_axis=0), 0.0).astype(BF16)

    ci_per_tile = 2 * LANES // T
    for n in range(S5_PAIRS // 2):
        acc = lax.dot_general(sprev, wout_ref[2 * n * LANES:(2 * n + 2) * LANES, :], _NT,
                              preferred_element_type=F32)
        for kt in range(S5_GROUP_CH // ci_per_tile):
            w = jnp.concatenate(
                [jnp.concatenate([toeplitz(ci_per_tile * kt + r, 2 * n), toeplitz(ci_per_tile * kt + r, 2 * n + 1)],
                                 axis=1) for r in range(ci_per_tile)], axis=0)
            acc = acc + _dot(lhs_ref[:, 2 * kt * LANES:(2 * kt + 2) * LANES], w)
        for q in range(2):
            pair = 2 * n + q
            u_pk = upk_ref[:, :, pair * LANES:(pair + 1) * LANES].reshape(n_rows, LANES)
            y = acc[:, q * LANES:(q + 1) * LANES] + vec_ref[_V_D, pair:pair + 1, :] * u_pk
            y = jax.nn.gelu(y, approximate=True).reshape(n_slab_chunks, S5_HALVES * batch, LANES)
            y0 = y[:, 0:batch, :].reshape(n_slab_rows, LANES)
            y1 = y[:, batch:2 * batch, :].reshape(n_slab_rows, LANES)
            ys_ref[2 * pair], ys_ref[2 * pair + 1] = _regroup_halves(y0, y1)


def _conv_mix_kernel(zs_ref, pc_ref, o_ref, *, batch):
    n_rows = zs_ref.shape[1]
    row = lax.broadcasted_iota(jnp.int32, (CHUNK, CHUNK), 0)
    col = lax.broadcasted_iota(jnp.int32, (CHUNK, CHUNK), 1)
    lag = row - col
    band_cur = (lag >= -(CONV_K - 1)) & (lag <= 0)
    band_prev = lag >= CHUNK - (CONV_K - 1)

    def conv_channel(ch, carry):
        z = zs_ref[ch]
        z_prev = jnp.concatenate([jnp.zeros((batch, CHUNK), F32), z[:n_rows - batch]], axis=0)
        lhs = jnp.concatenate([z_prev, z], axis=1).astype(BF16)
        taps = jnp.broadcast_to(pc_ref[ch, _PC_TAPS:_PC_TAPS + 1, :], (CHUNK, CHUNK))
        diag = pltpu.roll(taps, 0, 1, stride=1, stride_axis=0)
        toe = jnp.concatenate([jnp.where(band_prev, diag, 0.0), jnp.where(band_cur, diag, 0.0)],
                              axis=0).astype(BF16)
        o_ref[ch] = _dot(lhs, toe) + pc_ref[ch, _PC_BIAS:_PC_BIAS + 1, :]
        return carry

    lax.fori_loop(0, CONV_HEAD_DIM, conv_channel, 0, unroll=32)

    inv_n = 1.0 / CONV_HEAD_DIM
    for r0 in range(0, n_rows, LN_ROWS):
        rows = pl.ds(r0, LN_ROWS)
        zero = jnp.zeros((LN_ROWS, CHUNK), F32)
        mu = lax.fori_loop(0, CONV_HEAD_DIM, lambda ch, acc: acc + o_ref[ch, rows, :], zero,
                           unroll=LN_UNROLL) * inv_n
        var = lax.fori_loop(0, CONV_HEAD_DIM, lambda ch, acc: acc + jnp.square(o_ref[ch, rows, :] - mu),
                            zero, unroll=LN_UNROLL) * inv_n
        rstd = lax.rsqrt(var + EPS)

        def norm_channel(ch, carry):
            half = ((o_ref[ch, rows, :] - mu) * rstd * pc_ref[ch, _PC_HALF_GAIN:_PC_HALF_GAIN + 1, :]
                    + pc_ref[ch, _PC_HALF_BETA:_PC_HALF_BETA + 1, :])
            o_ref[ch, rows, :] = half + half * jnp.tanh(half)
            return carry

        lax.fori_loop(0, CONV_HEAD_DIM, norm_channel, 0, unroll=LN_UNROLL)


def _out_ffn2_norm_kernel(x1_ref, ys_ref, yc_ref, wglu_t_ref, bglu_ref, wo_ref,
                          g2_ref, wg_ref, wu_ref, wd_ref, gf_ref, o_ref):
    first_seq = pl.program_id(1) * TILE_SEQS
    ys_t = _load_slabs(ys_ref, first_seq)
    yc_t = _load_slabs(yc_ref, first_seq)
    x2 = x1_ref[...].reshape(ROW_TILE, D_MODEL) + lax.dot_general(
        yc_t.astype(BF16), wo_ref[S5_WIDTH:, :], _TN, preferred_element_type=F32)
    gate_t = jax.nn.sigmoid(_dot(wglu_t_ref[...], ys_t.astype(BF16)) + bglu_ref[...])
    x2 = x2 + lax.dot_general((ys_t * gate_t).astype(BF16), wo_ref[0:S5_WIDTH, :], _TN,
                              preferred_element_type=F32)
    h, r = _rms_split(x2, g2_ref[...])
    x3 = x2 + 0.5 * _swiglu(h, r, wg_ref, wu_ref, wd_ref)
    o_ref[...] = _rms_norm(x3, gf_ref[...]).reshape(TILE_SEQS, CHUNK, D_MODEL)


def _resident(shape):
    nd = len(shape)
    return pl.BlockSpec(shape, lambda *_: (0,) * nd, pipeline_mode=pl.Buffered(1))


def _tile_spec():
    return pl.BlockSpec((TILE_SEQS, CHUNK, D_MODEL), lambda c, h: (h, c, 0))


def _slab_spec(n_ch):
    return pl.BlockSpec((n_ch, SUBLANES, LANES), lambda c, h: (0, c, 0))


_DENSE_PARAMS = pltpu.CompilerParams(dimension_semantics=("arbitrary", "arbitrary"),
                                     vmem_limit_bytes=VMEM_LIMIT)
_MIX_PARAMS = pltpu.CompilerParams(dimension_semantics=("arbitrary",), vmem_limit_bytes=VMEM_LIMIT)


def _cast_spec(w, n_steps):
    n_blocks = n_steps
    while w.shape[0] % n_blocks or (w.shape[0] // n_blocks) % (2 * SUBLANES):
        n_blocks //= 2
    steps_per_block = n_steps // n_blocks
    return pl.BlockSpec((w.shape[0] // n_blocks, w.shape[1]), lambda c, h: (c // steps_per_block, 0))


def _ffn1_in_proj(x, g1, wg, wu, wd, gm, win_t, later_weights):
    batch, seq, _ = x.shape
    assert batch == SUBLANES and seq % CHUNK == 0
    n_rows = (seq // CHUNK) * batch
    cast_specs = [_cast_spec(w, seq // CHUNK) for w in later_weights]
    return pl.pallas_call(
        _ffn1_in_proj_kernel,
        grid=(seq // CHUNK, batch // TILE_SEQS),
        in_specs=[
            _tile_spec(),
            _resident((1, D_MODEL)),
            _resident((D_MODEL, D_FF)),
            _resident((D_MODEL, D_FF)),
            _resident((D_FF, D_MODEL)),
            _resident((1, D_MODEL)),
            _resident((IN_COLS, D_MODEL)),
        ] + cast_specs,
        out_specs=[_tile_spec(), _slab_spec(S5_WIDTH), _slab_spec(CONV_WIDTH)] + cast_specs,
        out_shape=[
            jax.ShapeDtypeStruct((batch, seq, D_MODEL), F32),
            jax.ShapeDtypeStruct((S5_WIDTH, n_rows, LANES), F32),
            jax.ShapeDtypeStruct((CONV_WIDTH, n_rows, LANES), F32),
        ] + [jax.ShapeDtypeStruct(w.shape, BF16) for w in later_weights],
        compiler_params=_DENSE_PARAMS,
        name="ffn1_in_proj",
    )(x, g1, wg, wu, wd, gm, win_t, *later_weights)


def _s5_mix(us, tables, *, batch):
    n_rows = us.shape[1]
    slab = pl.BlockSpec((S5_STEP_GROUPS * S5_GROUP_CH, n_rows, LANES), lambda g: (g, 0, 0))

    def per_group(t):
        nd = t.ndim - 1
        return pl.BlockSpec((S5_STEP_GROUPS,) + t.shape[1:], lambda g: (g,) + (0,) * nd)

    return pl.pallas_call(
        functools.partial(_s5_mix_kernel, batch=batch),
        grid=(S5_GROUPS // S5_STEP_GROUPS,),
        in_specs=[slab] + [per_group(t) for t in tables],
        out_specs=slab,
        out_shape=jax.ShapeDtypeStruct((S5_WIDTH, n_rows, LANES), F32),
        scratch_shapes=[pltpu.VMEM((S5_GROUP_CH * S5_GROUP_CH, 2 * S5_STATE), F32),
                        pltpu.VMEM((S5_GROUP_CH * S5_PAIRS, LANES), F32),
                        pltpu.VMEM((n_rows // batch, S5_HALVES * batch, S5_LANES), F32),
                        pltpu.VMEM((S5_HALVES * n_rows, S5_LANES), BF16),
                        pltpu.VMEM((S5_LANES, 4 * S5_STATE), BF16),
                        pltpu.VMEM((S5_LANES, 2 * S5_STATE), BF16),
                        pltpu.VMEM((S5_HALVES * n_rows, 2 * S5_STATE), F32)],
        compiler_params=_MIX_PARAMS,
        name="s5_mix",
    )(us, *tables)


def _conv_mix(zs, per_channel, *, batch):
    n_rows = zs.shape[1]
    slab = pl.BlockSpec((CONV_HEAD_DIM, n_rows, LANES), lambda h: (h, 0, 0))
    per_ch = pl.BlockSpec((CONV_HEAD_DIM, _PC_COUNT, LANES), lambda h: (h, 0, 0))
    return pl.pallas_call(
        functools.partial(_conv_mix_kernel, batch=batch),
        grid=(CONV_HEADS,),
        in_specs=[slab, per_ch],
        out_specs=slab,
        out_shape=jax.ShapeDtypeStruct((CONV_WIDTH, n_rows, LANES), F32),
        compiler_params=_MIX_PARAMS,
        name="conv_mix",
    )(zs, per_channel)


def _out_ffn2_norm(x1, ys, yc, wglu_t, bglu, wo, g2, wg, wu, wd, gf):
    batch, seq, _ = x1.shape
    return pl.pallas_call(
        _out_ffn2_norm_kernel,
        grid=(seq // CHUNK, batch // TILE_SEQS),
        in_specs=[
            _tile_spec(),
            _slab_spec(S5_WIDTH),
            _slab_spec(CONV_WIDTH),
            _resident((S5_WIDTH, S5_WIDTH)),
            _resident((S5_WIDTH, 1)),
            _resident((S5_WIDTH + CONV_WIDTH, D_MODEL)),
            _resident((1, D_MODEL)),
            _resident((D_MODEL, D_FF)),
            _resident((D_MODEL, D_FF)),
            _resident((D_FF, D_MODEL)),
            _resident((1, D_MODEL)),
        ],
        out_specs=_tile_spec(),
        out_shape=jax.ShapeDtypeStruct((batch, seq, D_MODEL), F32),
        compiler_params=_DENSE_PARAMS,
        name="out_ffn2_norm",
    )(x1, ys, yc, wglu_t, bglu, wo, g2, wg, wu, wd, gf)


def _s5_tables(lam_re, lam_im, log_dt, b_re, b_im, c_re, c_im, d_skip):
    T, C, P, G = S5_T, S5_GROUP_CH, S5_STATE, S5_GROUPS
    dt = jnp.exp(log_dt)[:, None]
    mag = jnp.exp(lam_re * dt)
    abar_re = mag * jnp.cos(lam_im * dt)
    abar_im = mag * jnp.sin(lam_im * dt)
    den = lam_re * lam_re + lam_im * lam_im
    num_re = abar_re - 1.0
    num_im = abar_im
    f_re = ((num_re * lam_re + num_im * lam_im) / den)[..., None]
    f_im = ((num_im * lam_re - num_re * lam_im) / den)[..., None]
    bb_re = f_re * b_re - f_im * b_im
    bb_im = f_re * b_im + f_im * b_re
    n_q = T // SUBLANES
    expo = jnp.concatenate([jnp.arange(SUBLANES), SUBLANES * jnp.arange(1, n_q + 1)]).astype(F32)[None, :, None]
    bmag = jnp.exp(expo * (lam_re * dt)[:, None, :])
    bang = expo * (lam_im * dt)[:, None, :]
    base_re, base_im = bmag * jnp.cos(bang), bmag * jnp.sin(bang)
    lo_re, lo_im = base_re[:, :SUBLANES], base_im[:, :SUBLANES]
    hi_re = jnp.concatenate([base_re[:, :1], base_re[:, SUBLANES:]], axis=1)
    hi_im = jnp.concatenate([base_im[:, :1], base_im[:, SUBLANES:]], axis=1)

    def outer(q_re, q_im, r_re, r_im):
        re = q_re[:, :, None] * r_re[:, None] - q_im[:, :, None] * r_im[:, None]
        im = q_re[:, :, None] * r_im[:, None] + q_im[:, :, None] * r_re[:, None]
        return jnp.concatenate([re, im], axis=-1).reshape(G, -1, 2 * P)

    cat = lambda a, b: jnp.concatenate([a, b], axis=-1)
    powers = jnp.concatenate([
        outer(hi_re[:, n_q - 1::-1], hi_im[:, n_q - 1::-1], lo_re[:, ::-1], lo_im[:, ::-1]),
        outer(hi_re, hi_im, lo_re, lo_im)], axis=1)
    bt_re = bb_re.transpose(0, 2, 1)
    bt_im = bb_im.transpose(0, 2, 1)
    d_rows = jnp.pad(jnp.repeat(d_skip.reshape(G, C // 2, 2), T, axis=-1), ((0, 0), (0, C // 2), (0, 0)))
    cp_re = jnp.concatenate([c_re[:, 0::2], c_re[:, 1::2]], axis=1)
    cp_im = jnp.concatenate([c_im[:, 0::2], c_im[:, 1::2]], axis=1)
    vecs = [None] * _V_COUNT
    vecs[_V_BR] = cat(bt_re, bt_re)
    vecs[_V_BI_NP] = cat(-bt_im, bt_im)
    vecs[_V_BI_PN] = cat(bt_im, -bt_im)
    vecs[_V_BI_PP] = cat(bt_im, bt_im)
    vecs[_V_CR_PN] = cat(c_re, -c_re)
    vecs[_V_CI_NN] = cat(-c_im, -c_im)
    vecs[_V_C_RE_NIM] = cat(cp_re, -cp_im)
    vecs[_V_C_NIM_NRE] = cat(-cp_im, -cp_re)
    vecs[_V_D] = d_rows
    return powers, jnp.stack(vecs, axis=1)


def _conv_tables(w_dw, b_dw, ln_g, ln_b):
    taps = jnp.pad(w_dw[::-1].T, ((0, 0), (0, LANES - CONV_K)))
    lanes = lambda v: jnp.broadcast_to(v[:, None], (CONV_WIDTH, LANES))
    rows = [None] * _PC_COUNT
    rows[_PC_TAPS], rows[_PC_BIAS] = taps, lanes(b_dw)
    rows[_PC_HALF_GAIN], rows[_PC_HALF_BETA] = lanes(0.5 * ln_g), lanes(0.5 * ln_b)
    return jnp.stack(rows, axis=1)


def kernel(x, ffn1_norm, ffn1_w_gate, ffn1_w_up, ffn1_w_down, mix_norm, w_in, s5_lam_re, s5_lam_im, s5_log_dt, s5_b_re, s5_b_im, s5_c_re, s5_c_im, s5_d, s5_w_glu, s5_b_glu, conv_w_dw, conv_b_dw, conv_ln_g, conv_ln_b, w_out, ffn2_norm, ffn2_w_gate, ffn2_w_up, ffn2_w_down, final_norm):
    batch, seq, d_model = x.shape
    assert ffn1_norm.shape[0] == 1 and d_model == D_MODEL
    l = 0
    row = lambda v: v.reshape(1, -1)

    x1, us, zs, wg2, wu2, wd2, wo = _ffn1_in_proj(
        x, row(ffn1_norm[l]), ffn1_w_gate[l].astype(BF16), ffn1_w_up[l].astype(BF16),
        ffn1_w_down[l].astype(BF16), row(mix_norm[l]), w_in[l].T.astype(BF16),
        (ffn2_w_gate[l], ffn2_w_up[l], ffn2_w_down[l], w_out[l]))

    tables = _s5_tables(
        s5_lam_re[l], s5_lam_im[l], s5_log_dt[l], s5_b_re[l], s5_b_im[l], s5_c_re[l], s5_c_im[l], s5_d[l])
    ys = _s5_mix(us, tables, batch=batch)

    yc = _conv_mix(zs, _conv_tables(conv_w_dw[l], conv_b_dw[l], conv_ln_g[l], conv_ln_b[l]), batch=batch)

    return _out_ffn2_norm(
        x1, ys, yc, s5_w_glu[l].T.astype(BF16), s5_b_glu[l].reshape(-1, 1), wo,
        row(ffn2_norm[l]), wg2, wu2, wd2, row(final_norm))
```

```python
import functools

import jax
import jax.numpy as jnp
from jax import lax
from jax.experimental import pallas as pl
from jax.experimental.pallas import tpu as pltpu

D_MODEL = 1024
D_FF = 2816
S5_WIDTH = 512
S5_GROUP_CH = 16
S5_GROUPS = S5_WIDTH // S5_GROUP_CH
S5_STATE = 64
CONV_WIDTH = 512
CONV_HEAD_DIM = 64
CONV_HEADS = CONV_WIDTH // CONV_HEAD_DIM
CONV_K = 31
IN_COLS = S5_WIDTH + 2 * CONV_WIDTH
EPS = 1e-6

LANES = 128
SUBLANES = 8
CHUNK = LANES
S5_T = 64
S5_HALVES = CHUNK // S5_T
S5_PAIRS = S5_GROUP_CH // 2
S5_LANES = S5_GROUP_CH * S5_T
S5_STEP_GROUPS = 2
TILE_SEQS = 8
ROW_TILE = TILE_SEQS * CHUNK
FF_SPLITS = ((0, 1536), (1536, 2816))
LN_ROWS = 64
LN_UNROLL = 16
VMEM_LIMIT = 60 * 1024 * 1024

BF16 = jnp.bfloat16
F32 = jnp.float32

(_V_BR,
 _V_BI_NP,
 _V_BI_PN,
 _V_BI_PP,
 _V_CR_PN,
 _V_CI_NN,
 _V_C_RE_NIM,
 _V_C_NIM_NRE,
 _V_D,
 _V_COUNT) = range(10)

_PC_TAPS, _PC_BIAS, _PC_HALF_GAIN, _PC_HALF_BETA, _PC_COUNT = range(5)

_NT = (((1,), (1,)), ((), ()))
_TN = (((0,), (0,)), ((), ()))


def _rms_norm(x, g):
    return x * lax.rsqrt(jnp.mean(x * x, axis=-1, keepdims=True) + EPS) * g


def _dot(a, b):
    return jnp.dot(a, b, preferred_element_type=F32)


def _rms_split(x, g):
    return (x * g).astype(BF16), lax.rsqrt(jnp.mean(x * x, axis=-1, keepdims=True) + EPS)


def _swiglu(h, row_scale, wg_ref, wu_ref, wd_ref):
    acc = None
    for lo, hi in FF_SPLITS:
        g = _dot(h, wg_ref[:, lo:hi]) * row_scale
        u = _dot(h, wu_ref[:, lo:hi]) * row_scale
        a = (g * jax.nn.sigmoid(g) * u).astype(BF16)
        part = _dot(a, wd_ref[lo:hi, :])
        acc = part if acc is None else acc + part
    return acc


def _store_slabs(slab_ref, val, first_seq):
    n_ch = slab_ref.shape[0]
    rows = slab_ref.reshape(n_ch * SUBLANES, LANES)
    for ct in range(n_ch // SUBLANES):
        for s in range(TILE_SEQS):
            rows[pl.ds(ct * SUBLANES * SUBLANES + first_seq + s, SUBLANES, stride=SUBLANES), :] = (
                val[ct * SUBLANES:(ct + 1) * SUBLANES, s * LANES:(s + 1) * LANES])


def _load_slabs(slab_ref, first_seq):
    n_ch = slab_ref.shape[0]
    rows = slab_ref.reshape(n_ch * SUBLANES, LANES)
    bands = []
    for ct in range(n_ch // SUBLANES):
        bands.append(jnp.concatenate(
            [rows[pl.ds(ct * SUBLANES * SUBLANES + first_seq + s, SUBLANES, stride=SUBLANES), :]
             for s in range(TILE_SEQS)], axis=1))
    return jnp.concatenate(bands, axis=0)


def _ffn1_in_proj_kernel(x_ref, g1_ref, wg_ref, wu_ref, wd_ref, gm_ref, win_t_ref, *rest):
    n_cast = (len(rest) - 3) // 2
    x1_ref, us_ref, zs_ref = rest[n_cast:n_cast + 3]
    first_seq = pl.program_id(1) * TILE_SEQS
    x = x_ref[...].reshape(ROW_TILE, D_MODEL)
    h, r = _rms_split(x, g1_ref[...])
    x1 = x + 0.5 * _swiglu(h, r, wg_ref, wu_ref, wd_ref)
    x1_ref[...] = x1.reshape(TILE_SEQS, CHUNK, D_MODEL)
    h2, r2 = _rms_split(x1, gm_ref[...])
    r2_lanes = jnp.transpose(jnp.broadcast_to(r2, (ROW_TILE, LANES)))[0:1, :]

    def in_proj_t(lo, hi):
        return lax.dot_general(win_t_ref[lo:hi, :], h2, _NT, preferred_element_type=F32) * r2_lanes

    gate = jax.nn.sigmoid(in_proj_t(S5_WIDTH + CONV_WIDTH, IN_COLS))
    _store_slabs(us_ref, in_proj_t(0, S5_WIDTH), first_seq)
    z_t = in_proj_t(S5_WIDTH, S5_WIDTH + CONV_WIDTH) * gate
    _store_slabs(zs_ref, z_t, first_seq)
    for src_ref, dst_ref in zip(rest[:n_cast], rest[n_cast + 3:]):
        dst_ref[...] = src_ref[...].astype(BF16)


def _regroup_halves(a, b):
    first_half = lax.broadcasted_iota(jnp.int32, (1, LANES), 1) < S5_T
    straight = jnp.where(first_half, a, b)
    crossed = pltpu.roll(jnp.where(first_half, b, a), S5_T, 1)
    return jnp.where(first_half, straight, crossed), jnp.where(first_half, crossed, straight)


def _s5_mix_kernel(us_ref, pow_ref, vec_ref, ys_ref, *scratch, batch):
    for g in range(S5_STEP_GROUPS):
        ch = pl.ds(g * S5_GROUP_CH, S5_GROUP_CH)
        _s5_group(us_ref.at[ch], pow_ref.at[g], vec_ref.at[g], ys_ref.at[ch], *scratch, batch=batch)


def _s5_group(us_ref, pow_ref, vec_ref, ys_ref,
              cb_ref, f_ref, upk_ref, lhs_ref, win_ref, wout_ref, sprev_ref, *, batch):
    T = S5_T
    n_slab_rows = us_ref.shape[1]
    n_slab_chunks = n_slab_rows // batch
    n_rows = S5_HALVES * n_slab_rows
    lane = lax.broadcasted_iota(jnp.int32, (1, LANES), 1)
    first_state_half = lane < S5_STATE
    swap = lambda v: pltpu.roll(v, S5_STATE, 1)
    pa_rev = pow_ref[0:T, :]
    pb_rev = swap(pa_rev)
    pa_fwd = pow_ref[T + 1:2 * T + 1, :]
    pb_fwd = swap(pa_fwd)
    for c in range(S5_GROUP_CH):
        rows = slice(c * T, (c + 1) * T)
        vec = lambda i: vec_ref[i, c:c + 1, :]
        win_ref[rows, 0:2 * S5_STATE] = (pa_rev * vec(_V_BR) + pb_rev * vec(_V_BI_NP)).astype(BF16)
        win_ref[rows, 2 * S5_STATE:] = (pb_rev * vec(_V_BR) + pa_rev * vec(_V_BI_PN)).astype(BF16)
        wout_ref[rows, :] = (pa_fwd * vec(_V_CR_PN) + pb_fwd * vec(_V_CI_NN)).astype(BF16)
        cb_ref[c * S5_GROUP_CH:(c + 1) * S5_GROUP_CH, :] = (
            vec_ref[_V_C_RE_NIM] * vec(_V_BR) + vec_ref[_V_C_NIM_NRE] * vec(_V_BI_PP))
    f = lax.dot_general(cb_ref[...], pow_ref[T:2 * T, :], _NT, precision=lax.Precision.HIGHEST,
                        preferred_element_type=F32)
    for c in range(S5_GROUP_CH):
        even, odd = c * S5_GROUP_CH, c * S5_GROUP_CH + S5_PAIRS
        f_ref[c * S5_PAIRS:(c + 1) * S5_PAIRS, :] = jnp.concatenate(
            [f[even:even + S5_PAIRS], f[odd:odd + S5_PAIRS]], axis=1)

    for p in range(S5_PAIRS):
        halves = _regroup_halves(us_ref[2 * p], us_ref[2 * p + 1])
        for hh in range(S5_HALVES):
            upk_ref[:, hh * batch:(hh + 1) * batch, p * LANES:(p + 1) * LANES] = (
                halves[hh].reshape(n_slab_chunks, batch, LANES))
    lhs_ref[...] = upk_ref[...].reshape(n_rows, S5_LANES).astype(BF16)

    e = _dot(lhs_ref[...], win_ref[...])
    a_chunk = pow_ref[2 * T:2 * T + 1, :]
    a_sw = swap(a_chunk)
    a1 = jnp.where(first_state_half, a_chunk, a_sw)
    a2 = jnp.where(first_state_half, -a_sw, a_chunk)
    s = jnp.zeros((batch, 2 * S5_STATE), F32)
    s_sw = jnp.zeros((batch, 2 * S5_STATE), F32)
    for c in range(n_rows // batch):
        rows = slice(c * batch, (c + 1) * batch)
        sprev_ref[rows, :] = s
        e_c = e[rows, :]
        s, s_sw = (a1 * s + a2 * s_sw + e_c[:, :2 * S5_STATE],
                   a1 * s_sw - a2 * s + e_c[:, 2 * S5_STATE:])
    sprev = sprev_ref[...].astype(BF16)

    causal = ((lax.broadcasted_iota(jnp.int32, (T, LANES), 1) & (T - 1))
              >= lax.broadcasted_iota(jnp.int32, (T, LANES), 0))

    def toeplitz(ci, pair):
        f_row = jnp.broadcast_to(f_ref[ci * S5_PAIRS + pair:ci * S5_PAIRS + pair + 1, :], (T, LANES))
        return jnp.where(causal, pltpu.roll(f_row, 0, 1, stride=1, stride_axis=0), 0.0).astype(BF16)

    ci_per_tile = 2 * LANES // T
    for n in range(S5_PAIRS // 2):
        acc = lax.dot_general(sprev, wout_ref[2 * n * LANES:(2 * n + 2) * LANES, :], _NT,
                              preferred_element_type=F32)
        for kt in range(S5_GROUP_CH // ci_per_tile):
            w = jnp.concatenate(
                [jnp.concatenate([toeplitz(ci_per_tile * kt + r, 2 * n), toeplitz(ci_per_tile * kt + r, 2 * n + 1)],
                                 axis=1) for r in range(ci_per_tile)], axis=0)
            acc = acc + _dot(lhs_ref[:, 2 * kt * LANES:(2 * kt + 2) * LANES], w)
        for q in range(2):
            pair = 2 * n + q
            u_pk = upk_ref[:, :, pair * LANES:(pair + 1) * LANES].reshape(n_rows, LANES)
            y = acc[:, q * LANES:(q + 1) * LANES] + vec_ref[_V_D, pair:pair + 1, :] * u_pk
            y = jax.nn.gelu(y, approximate=True).reshape(n_slab_chunks, S5_HALVES * batch, LANES)
            y0 = y[:, 0:batch, :].reshape(n_slab_rows, LANES)
            y1 = y[:, batch:2 * batch, :].reshape(n_slab_rows, LANES)
            ys_ref[2 * pair], ys_ref[2 * pair + 1] = _regroup_halves(y0, y1)


def _conv_mix_kernel(zs_ref, pc_ref, o_ref, *, batch):
    n_rows = zs_ref.shape[1]
    row = lax.broadcasted_iota(jnp.int32, (CHUNK, CHUNK), 0)
    col = lax.broadcasted_iota(jnp.int32, (CHUNK, CHUNK), 1)
    lag = row - col
    band_cur = (lag >= -(CONV_K - 1)) & (lag <= 0)
    band_prev = lag >= CHUNK - (CONV_K - 1)

    def conv_channel(ch, carry):
        z = zs_ref[ch]
        z_prev = jnp.concatenate([jnp.zeros((batch, CHUNK), F32), z[:n_rows - batch]], axis=0)
        lhs = jnp.concatenate([z_prev, z], axis=1).astype(BF16)
        taps = jnp.broadcast_to(pc_ref[ch, _PC_TAPS:_PC_TAPS + 1, :], (CHUNK, CHUNK))
        diag = pltpu.roll(taps, 0, 1, stride=1, stride_axis=0)
        toe = jnp.concatenate([jnp.where(band_prev, diag, 0.0), jnp.where(band_cur, diag, 0.0)],
                              axis=0).astype(BF16)
        o_ref[ch] = _dot(lhs, toe) + pc_ref[ch, _PC_BIAS:_PC_BIAS + 1, :]
        return carry

    lax.fori_loop(0, CONV_HEAD_DIM, conv_channel, 0, unroll=32)

    inv_n = 1.0 / CONV_HEAD_DIM
    for r0 in range(0, n_rows, LN_ROWS):
        rows = pl.ds(r0, LN_ROWS)
        zero = jnp.zeros((LN_ROWS, CHUNK), F32)
        mu = lax.fori_loop(0, CONV_HEAD_DIM, lambda ch, acc: acc + o_ref[ch, rows, :], zero,
                           unroll=LN_UNROLL) * inv_n
        var = lax.fori_loop(0, CONV_HEAD_DIM, lambda ch, acc: acc + jnp.square(o_ref[ch, rows, :] - mu),
                            zero, unroll=LN_UNROLL) * inv_n
        rstd = lax.rsqrt(var + EPS)

        def norm_channel(ch, carry):
            half = ((o_ref[ch, rows, :] - mu) * rstd * pc_ref[ch, _PC_HALF_GAIN:_PC_HALF_GAIN + 1, :]
                    + pc_ref[ch, _PC_HALF_BETA:_PC_HALF_BETA + 1, :])
            o_ref[ch, rows, :] = half + half * jnp.tanh(half)
            return carry

        lax.fori_loop(0, CONV_HEAD_DIM, norm_channel, 0, unroll=LN_UNROLL)


def _out_ffn2_norm_kernel(x1_ref, ys_ref, yc_ref, wglu_t_ref, bglu_ref, wo_ref,
                          g2_ref, wg_ref, wu_ref, wd_ref, gf_ref, o_ref):
    first_seq = pl.program_id(1) * TILE_SEQS
    ys_t = _load_slabs(ys_ref, first_seq)
    yc_t = _load_slabs(yc_ref, first_seq)
    x2 = x1_ref[...].reshape(ROW_TILE, D_MODEL) + lax.dot_general(
        yc_t.astype(BF16), wo_ref[S5_WIDTH:, :], _TN, preferred_element_type=F32)
    gate_t = jax.nn.sigmoid(_dot(wglu_t_ref[...], ys_t.astype(BF16)) + bglu_ref[...])
    x2 = x2 + lax.dot_general((ys_t * gate_t).astype(BF16), wo_ref[0:S5_WIDTH, :], _TN,
                              preferred_element_type=F32)
    h, r = _rms_split(x2, g2_ref[...])
    x3 = x2 + 0.5 * _swiglu(h, r, wg_ref, wu_ref, wd_ref)
    o_ref[...] = _rms_norm(x3, gf_ref[...]).reshape(TILE_SEQS, CHUNK, D_MODEL)


def _resident(shape):
    nd = len(shape)
    return pl.BlockSpec(shape, lambda *_: (0,) * nd, pipeline_mode=pl.Buffered(1))


def _tile_spec():
    return pl.BlockSpec((TILE_SEQS, CHUNK, D_MODEL), lambda c, h: (h, c, 0))


def _slab_spec(n_ch):
    return pl.BlockSpec((n_ch, SUBLANES, LANES), lambda c, h: (0, c, 0))


_DENSE_PARAMS = pltpu.CompilerParams(dimension_semantics=("parallel", "parallel"),
                                     vmem_limit_bytes=VMEM_LIMIT)
_MIX_PARAMS = pltpu.CompilerParams(dimension_semantics=("parallel",), vmem_limit_bytes=VMEM_LIMIT)


def _cast_spec(w, n_steps):
    n_blocks = n_steps
    while w.shape[0] % n_blocks or (w.shape[0] // n_blocks) % (2 * SUBLANES):
        n_blocks //= 2
    steps_per_block = n_steps // n_blocks
    return pl.BlockSpec((w.shape[0] // n_blocks, w.shape[1]), lambda c, h: (c // steps_per_block, 0))


def _ffn1_in_proj(x, g1, wg, wu, wd, gm, win_t, later_weights):
    batch, seq, _ = x.shape
    assert batch == SUBLANES and seq % CHUNK == 0
    n_rows = (seq // CHUNK) * batch
    cast_specs = [_cast_spec(w, seq // CHUNK) for w in later_weights]
    return pl.pallas_call(
        _ffn1_in_proj_kernel,
        grid=(seq // CHUNK, batch // TILE_SEQS),
        in_specs=[
            _tile_spec(),
            _resident((1, D_MODEL)),
            _resident((D_MODEL, D_FF)),
            _resident((D_MODEL, D_FF)),
            _resident((D_FF, D_MODEL)),
            _resident((1, D_MODEL)),
            _resident((IN_COLS, D_MODEL)),
        ] + cast_specs,
        out_specs=[_tile_spec(), _slab_spec(S5_WIDTH), _slab_spec(CONV_WIDTH)] + cast_specs,
        out_shape=[
            jax.ShapeDtypeStruct((batch, seq, D_MODEL), F32),
            jax.ShapeDtypeStruct((S5_WIDTH, n_rows, LANES), F32),
            jax.ShapeDtypeStruct((CONV_WIDTH, n_rows, LANES), F32),
        ] + [jax.ShapeDtypeStruct(w.shape, BF16) for w in later_weights],
        compiler_params=_DENSE_PARAMS,
        name="ffn1_in_proj",
    )(x, g1, wg, wu, wd, gm, win_t, *later_weights)


def _s5_mix(us, tables, *, batch):
    n_rows = us.shape[1]
    slab = pl.BlockSpec((S5_STEP_GROUPS * S5_GROUP_CH, n_rows, LANES), lambda g: (g, 0, 0))

    def per_group(t):
        nd = t.ndim - 1
        return pl.BlockSpec((S5_STEP_GROUPS,) + t.shape[1:], lambda g: (g,) + (0,) * nd)

    return pl.pallas_call(
        functools.partial(_s5_mix_kernel, batch=batch),
        grid=(S5_GROUPS // S5_STEP_GROUPS,),
        in_specs=[slab] + [per_group(t) for t in tables],
        out_specs=slab,
        out_shape=jax.ShapeDtypeStruct((S5_WIDTH, n_rows, LANES), F32),
        scratch_shapes=[pltpu.VMEM((S5_GROUP_CH * S5_GROUP_CH, 2 * S5_STATE), F32),
                        pltpu.VMEM((S5_GROUP_CH * S5_PAIRS, LANES), F32),
                        pltpu.VMEM((n_rows // batch, S5_HALVES * batch, S5_LANES), F32),
                        pltpu.VMEM((S5_HALVES * n_rows, S5_LANES), BF16),
                        pltpu.VMEM((S5_LANES, 4 * S5_STATE), BF16),
                        pltpu.VMEM((S5_LANES, 2 * S5_STATE), BF16),
                        pltpu.VMEM((S5_HALVES * n_rows, 2 * S5_STATE), F32)],
        compiler_params=_MIX_PARAMS,
        name="s5_mix",
    )(us, *tables)


def _conv_mix(zs, per_channel, *, batch):
    n_rows = zs.shape[1]
    slab = pl.BlockSpec((CONV_HEAD_DIM, n_rows, LANES), lambda h: (h, 0, 0))
    per_ch = pl.BlockSpec((CONV_HEAD_DIM, _PC_COUNT, LANES), lambda h: (h, 0, 0))
    return pl.pallas_call(
        functools.partial(_conv_mix_kernel, batch=batch),
        grid=(CONV_HEADS,),
        in_specs=[slab, per_ch],
        out_specs=slab,
        out_shape=jax.ShapeDtypeStruct((CONV_WIDTH, n_rows, LANES), F32),
        compiler_params=_MIX_PARAMS,
        name="conv_mix",
    )(zs, per_channel)


def _out_ffn2_norm(x1, ys, yc, wglu_t, bglu, wo, g2, wg, wu, wd, gf):
    batch, seq, _ = x1.shape
    return pl.pallas_call(
        _out_ffn2_norm_kernel,
        grid=(seq // CHUNK, batch // TILE_SEQS),
        in_specs=[
            _tile_spec(),
            _slab_spec(S5_WIDTH),
            _slab_spec(CONV_WIDTH),
            _resident((S5_WIDTH, S5_WIDTH)),
            _resident((S5_WIDTH, 1)),
            _resident((S5_WIDTH + CONV_WIDTH, D_MODEL)),
            _resident((1, D_MODEL)),
            _resident((D_MODEL, D_FF)),
            _resident((D_MODEL, D_FF)),
            _resident((D_FF, D_MODEL)),
            _resident((1, D_MODEL)),
        ],
        out_specs=_tile_spec(),
        out_shape=jax.ShapeDtypeStruct((batch, seq, D_MODEL), F32),
        compiler_params=_DENSE_PARAMS,
        name="out_ffn2_norm",
    )(x1, ys, yc, wglu_t, bglu, wo, g2, wg, wu, wd, gf)


def _s5_tables(lam_re, lam_im, log_dt, b_re, b_im, c_re, c_im, d_skip):
    T, C, P, G = S5_T, S5_GROUP_CH, S5_STATE, S5_GROUPS
    dt = jnp.exp(log_dt)[:, None]
    mag = jnp.exp(lam_re * dt)
    abar_re = mag * jnp.cos(lam_im * dt)
    abar_im = mag * jnp.sin(lam_im * dt)
    den = lam_re * lam_re + lam_im * lam_im
    num_re = abar_re - 1.0
    num_im = abar_im
    f_re = ((num_re * lam_re + num_im * lam_im) / den)[..., None]
    f_im = ((num_im * lam_re - num_re * lam_im) / den)[..., None]
    bb_re = f_re * b_re - f_im * b_im
    bb_im = f_re * b_im + f_im * b_re
    n_q = T // SUBLANES
    expo = jnp.concatenate([jnp.arange(SUBLANES), SUBLANES * jnp.arange(1, n_q + 1)]).astype(F32)[None, :, None]
    bmag = jnp.exp(expo * (lam_re * dt)[:, None, :])
    bang = expo * (lam_im * dt)[:, None, :]
    base_re, base_im = bmag * jnp.cos(bang), bmag * jnp.sin(bang)
    lo_re, lo_im = base_re[:, :SUBLANES], base_im[:, :SUBLANES]
    hi_re = jnp.concatenate([base_re[:, :1], base_re[:, SUBLANES:]], axis=1)
    hi_im = jnp.concatenate([base_im[:, :1], base_im[:, SUBLANES:]], axis=1)

    def outer(q_re, q_im, r_re, r_im):
        re = q_re[:, :, None] * r_re[:, None] - q_im[:, :, None] * r_im[:, None]
        im = q_re[:, :, None] * r_im[:, None] + q_im[:, :, None] * r_re[:, None]
        return jnp.concatenate([re, im], axis=-1).reshape(G, -1, 2 * P)

    cat = lambda a, b: jnp.concatenate([a, b], axis=-1)
    powers = jnp.concatenate([
        outer(hi_re[:, n_q - 1::-1], hi_im[:, n_q - 1::-1], lo_re[:, ::-1], lo_im[:, ::-1]),
        outer(hi_re, hi_im, lo_re, lo_im)], axis=1)
    bt_re = bb_re.transpose(0, 2, 1)
    bt_im = bb_im.transpose(0, 2, 1)
    d_rows = jnp.pad(jnp.repeat(d_skip.reshape(G, C // 2, 2), T, axis=-1), ((0, 0), (0, C // 2), (0, 0)))
    cp_re = jnp.concatenate([c_re[:, 0::2], c_re[:, 1::2]], axis=1)
    cp_im = jnp.concatenate([c_im[:, 0::2], c_im[:, 1::2]], axis=1)
    vecs = [None] * _V_COUNT
    vecs[_V_BR] = cat(bt_re, bt_re)
    vecs[_V_BI_NP] = cat(-bt_im, bt_im)
    vecs[_V_BI_PN] = cat(bt_im, -bt_im)
    vecs[_V_BI_PP] = cat(bt_im, bt_im)
    vecs[_V_CR_PN] = cat(c_re, -c_re)
    vecs[_V_CI_NN] = cat(-c_im, -c_im)
    vecs[_V_C_RE_NIM] = cat(cp_re, -cp_im)
    vecs[_V_C_NIM_NRE] = cat(-cp_im, -cp_re)
    vecs[_V_D] = d_rows
    return powers, jnp.stack(vecs, axis=1)


def _conv_tables(w_dw, b_dw, ln_g, ln_b):
    taps = jnp.pad(w_dw[::-1].T, ((0, 0), (0, LANES - CONV_K)))
    lanes = lambda v: jnp.broadcast_to(v[:, None], (CONV_WIDTH, LANES))
    rows = [None] * _PC_COUNT
    rows[_PC_TAPS], rows[_PC_BIAS] = taps, lanes(b_dw)
    rows[_PC_HALF_GAIN], rows[_PC_HALF_BETA] = lanes(0.5 * ln_g), lanes(0.5 * ln_b)
    return jnp.stack(rows, axis=1)


def kernel(x, ffn1_norm, ffn1_w_gate, ffn1_w_up, ffn1_w_down, mix_norm, w_in, s5_lam_re, s5_lam_im, s5_log_dt, s5_b_re, s5_b_im, s5_c_re, s5_c_im, s5_d, s5_w_glu, s5_b_glu, conv_w_dw, conv_b_dw, conv_ln_g, conv_ln_b, w_out, ffn2_norm, ffn2_w_gate, ffn2_w_up, ffn2_w_down, final_norm):
    batch, seq, d_model = x.shape
    assert ffn1_norm.shape[0] == 1 and d_model == D_MODEL
    l = 0
    row = lambda v: v.reshape(1, -1)

    x1, us, zs, wg2, wu2, wd2, wo = _ffn1_in_proj(
        x, row(ffn1_norm[l]), ffn1_w_gate[l].astype(BF16), ffn1_w_up[l].astype(BF16),
        ffn1_w_down[l].astype(BF16), row(mix_norm[l]), w_in[l].T.astype(BF16),
        (ffn2_w_gate[l], ffn2_w_up[l], ffn2_w_down[l], w_out[l]))

    tables = _s5_tables(
        s5_lam_re[l], s5_lam_im[l], s5_log_dt[l], s5_b_re[l], s5_b_im[l], s5_c_re[l], s5_c_im[l], s5_d[l])
    ys = _s5_mix(us, tables, batch=batch)

    yc = _conv_mix(zs, _conv_tables(conv_w_dw[l], conv_b_dw[l], conv_ln_g[l], conv_ln_b[l]), batch=batch)

    return _out_ffn2_norm(
        x1, ys, yc, s5_w_glu[l].T.astype(BF16), s5_b_glu[l].reshape(-1, 1), wo,
        row(ffn2_norm[l]), wg2, wu2, wd2, row(final_norm))
```

```python
import functools

import jax
import jax.numpy as jnp
from jax import lax
from jax.experimental import pallas as pl
from jax.experimental.pallas import tpu as pltpu

D_MODEL = 1024
D_FF = 2816
S5_WIDTH = 512
S5_GROUP_CH = 16
S5_GROUPS = S5_WIDTH // S5_GROUP_CH
S5_STATE = 64
CONV_WIDTH = 512
CONV_HEAD_DIM = 64
CONV_HEADS = CONV_WIDTH // CONV_HEAD_DIM
CONV_K = 31
IN_COLS = S5_WIDTH + 2 * CONV_WIDTH
EPS = 1e-6

LANES = 128
SUBLANES = 8
CHUNK = LANES
S5_T = 64
S5_HALVES = CHUNK // S5_T
S5_PAIRS = S5_GROUP_CH // 2
S5_LANES = S5_GROUP_CH * S5_T
S5_STEP_GROUPS = 2
TILE_SEQS = 8
ROW_TILE = TILE_SEQS * CHUNK
FF_SPLITS = ((0, 1536), (1536, 2816))
LN_ROWS = 64
LN_UNROLL = 32
VMEM_LIMIT = 60 * 1024 * 1024

BF16 = jnp.bfloat16
F32 = jnp.float32

(_V_BR,
 _V_BI_NP,
 _V_BI_PN,
 _V_BI_PP,
 _V_CR_PN,
 _V_CI_NN,
 _V_C_RE_NIM,
 _V_C_NIM_NRE,
 _V_D,
 _V_COUNT) = range(10)

_PC_TAPS, _PC_BIAS, _PC_HALF_GAIN, _PC_HALF_BETA, _PC_COUNT = range(5)

_NT = (((1,), (1,)), ((), ()))
_TN = (((0,), (0,)), ((), ()))


def _rms_norm(x, g):
    return x * lax.rsqrt(jnp.mean(x * x, axis=-1, keepdims=True) + EPS) * g


def _dot(a, b):
    return jnp.dot(a, b, preferred_element_type=F32)


def _rms_split(x, g):
    return (x * g).astype(BF16), lax.rsqrt(jnp.mean(x * x, axis=-1, keepdims=True) + EPS)


def _swiglu(h, row_scale, wg_ref, wu_ref, wd_ref):
    acc = None
    for lo, hi in FF_SPLITS:
        g = _dot(h, wg_ref[:, lo:hi]) * row_scale
        u = _dot(h, wu_ref[:, lo:hi]) * row_scale
        a = (g * jax.nn.sigmoid(g) * u).astype(BF16)
        part = _dot(a, wd_ref[lo:hi, :])
        acc = part if acc is None else acc + part
    return acc


def _store_slabs(slab_ref, val, first_seq):
    n_ch = slab_ref.shape[0]
    rows = slab_ref.reshape(n_ch * SUBLANES, LANES)
    for ct in range(n_ch // SUBLANES):
        for s in range(TILE_SEQS):
            rows[pl.ds(ct * SUBLANES * SUBLANES + first_seq + s, SUBLANES, stride=SUBLANES), :] = (
                val[ct * SUBLANES:(ct + 1) * SUBLANES, s * LANES:(s + 1) * LANES])


def _load_slabs(slab_ref, first_seq):
    n_ch = slab_ref.shape[0]
    rows = slab_ref.reshape(n_ch * SUBLANES, LANES)
    bands = []
    for ct in range(n_ch // SUBLANES):
        bands.append(jnp.concatenate(
            [rows[pl.ds(ct * SUBLANES * SUBLANES + first_seq + s, SUBLANES, stride=SUBLANES), :]
             for s in range(TILE_SEQS)], axis=1))
    return jnp.concatenate(bands, axis=0)


def _ffn1_in_proj_kernel(x_ref, g1_ref, wg_ref, wu_ref, wd_ref, gm_ref, win_t_ref, *rest):
    n_cast = (len(rest) - 3) // 2
    x1_ref, us_ref, zs_ref = rest[n_cast:n_cast + 3]
    first_seq = pl.program_id(1) * TILE_SEQS
    x = x_ref[...].reshape(ROW_TILE, D_MODEL)
    h, r = _rms_split(x, g1_ref[...])
    x1 = x + 0.5 * _swiglu(h, r, wg_ref, wu_ref, wd_ref)
    x1_ref[...] = x1.reshape(TILE_SEQS, CHUNK, D_MODEL)
    h2, r2 = _rms_split(x1, gm_ref[...])
    r2_lanes = jnp.transpose(jnp.broadcast_to(r2, (ROW_TILE, LANES)))[0:1, :]

    def in_proj_t(lo, hi):
        return lax.dot_general(win_t_ref[lo:hi, :], h2, _NT, preferred_element_type=F32) * r2_lanes

    gate = jax.nn.sigmoid(in_proj_t(S5_WIDTH + CONV_WIDTH, IN_COLS))
    _store_slabs(us_ref, in_proj_t(0, S5_WIDTH), first_seq)
    z_t = in_proj_t(S5_WIDTH, S5_WIDTH + CONV_WIDTH) * gate
    _store_slabs(zs_ref, z_t, first_seq)
    for src_ref, dst_ref in zip(rest[:n_cast], rest[n_cast + 3:]):
        dst_ref[...] = src_ref[...].astype(BF16)


def _regroup_halves(a, b):
    first_half = lax.broadcasted_iota(jnp.int32, (1, LANES), 1) < S5_T
    straight = jnp.where(first_half, a, b)
    crossed = pltpu.roll(jnp.where(first_half, b, a), S5_T, 1)
    return jnp.where(first_half, straight, crossed), jnp.where(first_half, crossed, straight)


def _s5_mix_kernel(us_ref, pow_ref, vec_ref, ys_ref, *scratch, batch):
    for g in range(S5_STEP_GROUPS):
        ch = pl.ds(g * S5_GROUP_CH, S5_GROUP_CH)
        _s5_group(us_ref.at[ch], pow_ref.at[g], vec_ref.at[g], ys_ref.at[ch], *scratch, batch=batch)


def _s5_group(us_ref, pow_ref, vec_ref, ys_ref,
              cb_ref, f_ref, upk_ref, lhs_ref, win_ref, wout_ref, sprev_ref, *, batch):
    T = S5_T
    n_slab_rows = us_ref.shape[1]
    n_slab_chunks = n_slab_rows // batch
    n_rows = S5_HALVES * n_slab_rows
    lane = lax.broadcasted_iota(jnp.int32, (1, LANES), 1)
    first_state_half = lane < S5_STATE
    swap = lambda v: pltpu.roll(v, S5_STATE, 1)
    pa_rev = pow_ref[0:T, :]
    pb_rev = swap(pa_rev)
    pa_fwd = pow_ref[T + 1:2 * T + 1, :]
    pb_fwd = swap(pa_fwd)
    for c in range(S5_GROUP_CH):
        rows = slice(c * T, (c + 1) * T)
        vec = lambda i: vec_ref[i, c:c + 1, :]
        win_ref[rows, 0:2 * S5_STATE] = (pa_rev * vec(_V_BR) + pb_rev * vec(_V_BI_NP)).astype(BF16)
        win_ref[rows, 2 * S5_STATE:] = (pb_rev * vec(_V_BR) + pa_rev * vec(_V_BI_PN)).astype(BF16)
        wout_ref[rows, :] = (pa_fwd * vec(_V_CR_PN) + pb_fwd * vec(_V_CI_NN)).astype(BF16)
        cb_ref[c * S5_GROUP_CH:(c + 1) * S5_GROUP_CH, :] = (
            vec_ref[_V_C_RE_NIM] * vec(_V_BR) + vec_ref[_V_C_NIM_NRE] * vec(_V_BI_PP))
    f = lax.dot_general(cb_ref[...], pow_ref[T:2 * T, :], _NT, precision=lax.Precision.HIGHEST,
                        preferred_element_type=F32)
    for c in range(S5_GROUP_CH):
        even, odd = c * S5_GROUP_CH, c * S5_GROUP_CH + S5_PAIRS
        f_ref[c * S5_PAIRS:(c + 1) * S5_PAIRS, :] = jnp.concatenate(
            [f[even:even + S5_PAIRS], f[odd:odd + S5_PAIRS]], axis=1)

    for p in range(S5_PAIRS):
        halves = _regroup_halves(us_ref[2 * p], us_ref[2 * p + 1])
        for hh in range(S5_HALVES):
            upk_ref[:, hh * batch:(hh + 1) * batch, p * LANES:(p + 1) * LANES] = (
                halves[hh].reshape(n_slab_chunks, batch, LANES))
    lhs_ref[...] = upk_ref[...].reshape(n_rows, S5_LANES).astype(BF16)

    e = _dot(lhs_ref[...], win_ref[...])
    a_chunk = pow_ref[2 * T:2 * T + 1, :]
    a_sw = swap(a_chunk)
    a1 = jnp.where(first_state_half, a_chunk, a_sw)
    a2 = jnp.where(first_state_half, -a_sw, a_chunk)
    s = jnp.zeros((batch, 2 * S5_STATE), F32)
    s_sw = jnp.zeros((batch, 2 * S5_STATE), F32)
    for c in range(n_rows // batch):
        rows = slice(c * batch, (c + 1) * batch)
        sprev_ref[rows, :] = s
        e_c = e[rows, :]
        s, s_sw = (a1 * s + a2 * s_sw + e_c[:, :2 * S5_STATE],
                   a1 * s_sw - a2 * s + e_c[:, 2 * S5_STATE:])
    sprev = sprev_ref[...].astype(BF16)

    causal = ((lax.broadcasted_iota(jnp.int32, (T, LANES), 1) & (T - 1))
              >= lax.broadcasted_iota(jnp.int32, (T, LANES), 0))

    def toeplitz(ci, pair):
        f_row = jnp.broadcast_to(f_ref[ci * S5_PAIRS + pair:ci * S5_PAIRS + pair + 1, :], (T, LANES))
        return jnp.where(causal, pltpu.roll(f_row, 0, 1, stride=1, stride_axis=0), 0.0).astype(BF16)

    ci_per_tile = 2 * LANES // T
    for n in range(S5_PAIRS // 2):
        acc = lax.dot_general(sprev, wout_ref[2 * n * LANES:(2 * n + 2) * LANES, :], _NT,
                              preferred_element_type=F32)
        for kt in range(S5_GROUP_CH // ci_per_tile):
            w = jnp.concatenate(
                [jnp.concatenate([toeplitz(ci_per_tile * kt + r, 2 * n), toeplitz(ci_per_tile * kt + r, 2 * n + 1)],
                                 axis=1) for r in range(ci_per_tile)], axis=0)
            acc = acc + _dot(lhs_ref[:, 2 * kt * LANES:(2 * kt + 2) * LANES], w)
        for q in range(2):
            pair = 2 * n + q
            u_pk = upk_ref[:, :, pair * LANES:(pair + 1) * LANES].reshape(n_rows, LANES)
            y = acc[:, q * LANES:(q + 1) * LANES] + vec_ref[_V_D, pair:pair + 1, :] * u_pk
            y = jax.nn.gelu(y, approximate=True).reshape(n_slab_chunks, S5_HALVES * batch, LANES)
            y0 = y[:, 0:batch, :].reshape(n_slab_rows, LANES)
            y1 = y[:, batch:2 * batch, :].reshape(n_slab_rows, LANES)
            ys_ref[2 * pair], ys_ref[2 * pair + 1] = _regroup_halves(y0, y1)


def _conv_mix_kernel(zs_ref, pc_ref, o_ref, *, batch):
    n_rows = zs_ref.shape[1]
    row = lax.broadcasted_iota(jnp.int32, (CHUNK, CHUNK), 0)
    col = lax.broadcasted_iota(jnp.int32, (CHUNK, CHUNK), 1)
    lag = row - col
    band_cur = (lag >= -(CONV_K - 1)) & (lag <= 0)
    band_prev = lag >= CHUNK - (CONV_K - 1)

    def conv_channel(ch, carry):
        z = zs_ref[ch]
        z_prev = jnp.concatenate([jnp.zeros((batch, CHUNK), F32), z[:n_rows - batch]], axis=0)
        lhs = jnp.concatenate([z_prev, z], axis=1).astype(BF16)
        taps = jnp.broadcast_to(pc_ref[ch, _PC_TAPS:_PC_TAPS + 1, :], (CHUNK, CHUNK))
        diag = pltpu.roll(taps, 0, 1, stride=1, stride_axis=0)
        toe = jnp.concatenate([jnp.where(band_prev, diag, 0.0), jnp.where(band_cur, diag, 0.0)],
                              axis=0).astype(BF16)
        o_ref[ch] = _dot(lhs, toe) + pc_ref[ch, _PC_BIAS:_PC_BIAS + 1, :]
        return carry

    lax.fori_loop(0, CONV_HEAD_DIM, conv_channel, 0, unroll=True)

    inv_n = 1.0 / CONV_HEAD_DIM
    for r0 in range(0, n_rows, LN_ROWS):
        rows = pl.ds(r0, LN_ROWS)
        zero = jnp.zeros((LN_ROWS, CHUNK), F32)
        mu = lax.fori_loop(0, CONV_HEAD_DIM, lambda ch, acc: acc + o_ref[ch, rows, :], zero,
                           unroll=LN_UNROLL) * inv_n
        var = lax.fori_loop(0, CONV_HEAD_DIM, lambda ch, acc: acc + jnp.square(o_ref[ch, rows, :] - mu),
                            zero, unroll=LN_UNROLL) * inv_n
        rstd = lax.rsqrt(var + EPS)

        def norm_channel(ch, carry):
            half = ((o_ref[ch, rows, :] - mu) * rstd * pc_ref[ch, _PC_HALF_GAIN:_PC_HALF_GAIN + 1, :]
                    + pc_ref[ch, _PC_HALF_BETA:_PC_HALF_BETA + 1, :])
            o_ref[ch, rows, :] = half + half * jnp.tanh(half)
            return carry

        lax.fori_loop(0, CONV_HEAD_DIM, norm_channel, 0, unroll=LN_UNROLL)


def _out_ffn2_norm_kernel(x1_ref, ys_ref, yc_ref, wglu_t_ref, bglu_ref, wo_ref,
                          g2_ref, wg_ref, wu_ref, wd_ref, gf_ref, o_ref):
    first_seq = pl.program_id(1) * TILE_SEQS
    ys_t = _load_slabs(ys_ref, first_seq)
    yc_t = _load_slabs(yc_ref, first_seq)
    x2 = x1_ref[...].reshape(ROW_TILE, D_MODEL) + lax.dot_general(
        yc_t.astype(BF16), wo_ref[S5_WIDTH:, :], _TN, preferred_element_type=F32)
    gate_t = jax.nn.sigmoid(_dot(wglu_t_ref[...], ys_t.astype(BF16)) + bglu_ref[...])
    x2 = x2 + lax.dot_general((ys_t * gate_t).astype(BF16), wo_ref[0:S5_WIDTH, :], _TN,
                              preferred_element_type=F32)
    h, r = _rms_split(x2, g2_ref[...])
    x3 = x2 + 0.5 * _swiglu(h, r, wg_ref, wu_ref, wd_ref)
    o_ref[...] = _rms_norm(x3, gf_ref[...]).reshape(TILE_SEQS, CHUNK, D_MODEL)


def _resident(shape):
    nd = len(shape)
    return pl.BlockSpec(shape, lambda *_: (0,) * nd, pipeline_mode=pl.Buffered(1))


def _tile_spec():
    return pl.BlockSpec((TILE_SEQS, CHUNK, D_MODEL), lambda c, h: (h, c, 0))


def _slab_spec(n_ch):
    return pl.BlockSpec((n_ch, SUBLANES, LANES), lambda c, h: (0, c, 0))


_DENSE_PARAMS = pltpu.CompilerParams(dimension_semantics=("arbitrary", "arbitrary"),
                                     vmem_limit_bytes=VMEM_LIMIT)
_MIX_PARAMS = pltpu.CompilerParams(dimension_semantics=("arbitrary",), vmem_limit_bytes=VMEM_LIMIT)


def _cast_spec(w, n_steps):
    n_blocks = n_steps
    while w.shape[0] % n_blocks or (w.shape[0] // n_blocks) % (2 * SUBLANES):
        n_blocks //= 2
    steps_per_block = n_steps // n_blocks
    return pl.BlockSpec((w.shape[0] // n_blocks, w.shape[1]), lambda c, h: (c // steps_per_block, 0))


def _ffn1_in_proj(x, g1, wg, wu, wd, gm, win_t, later_weights):
    batch, seq, _ = x.shape
    assert batch == SUBLANES and seq % CHUNK == 0
    n_rows = (seq // CHUNK) * batch
    cast_specs = [_cast_spec(w, seq // CHUNK) for w in later_weights]
    return pl.pallas_call(
        _ffn1_in_proj_kernel,
        grid=(seq // CHUNK, batch // TILE_SEQS),
        in_specs=[
            _tile_spec(),
            _resident((1, D_MODEL)),
            _resident((D_MODEL, D_FF)),
            _resident((D_MODEL, D_FF)),
            _resident((D_FF, D_MODEL)),
            _resident((1, D_MODEL)),
            _resident((IN_COLS, D_MODEL)),
        ] + cast_specs,
        out_specs=[_tile_spec(), _slab_spec(S5_WIDTH), _slab_spec(CONV_WIDTH)] + cast_specs,
        out_shape=[
            jax.ShapeDtypeStruct((batch, seq, D_MODEL), F32),
            jax.ShapeDtypeStruct((S5_WIDTH, n_rows, LANES), F32),
            jax.ShapeDtypeStruct((CONV_WIDTH, n_rows, LANES), F32),
        ] + [jax.ShapeDtypeStruct(w.shape, BF16) for w in later_weights],
        compiler_params=_DENSE_PARAMS,
        name="ffn1_in_proj",
    )(x, g1, wg, wu, wd, gm, win_t, *later_weights)


def _s5_mix(us, tables, *, batch):
    n_rows = us.shape[1]
    slab = pl.BlockSpec((S5_STEP_GROUPS * S5_GROUP_CH, n_rows, LANES), lambda g: (g, 0, 0))

    def per_group(t):
        nd = t.ndim - 1
        return pl.BlockSpec((S5_STEP_GROUPS,) + t.shape[1:], lambda g: (g,) + (0,) * nd)

    return pl.pallas_call(
        functools.partial(_s5_mix_kernel, batch=batch),
        grid=(S5_GROUPS // S5_STEP_GROUPS,),
        in_specs=[slab] + [per_group(t) for t in tables],
        out_specs=slab,
        out_shape=jax.ShapeDtypeStruct((S5_WIDTH, n_rows, LANES), F32),
        scratch_shapes=[pltpu.VMEM((S5_GROUP_CH * S5_GROUP_CH, 2 * S5_STATE), F32),
                        pltpu.VMEM((S5_GROUP_CH * S5_PAIRS, LANES), F32),
                        pltpu.VMEM((n_rows // batch, S5_HALVES * batch, S5_LANES), F32),
                        pltpu.VMEM((S5_HALVES * n_rows, S5_LANES), BF16),
                        pltpu.VMEM((S5_LANES, 4 * S5_STATE), BF16),
                        pltpu.VMEM((S5_LANES, 2 * S5_STATE), BF16),
                        pltpu.VMEM((S5_HALVES * n_rows, 2 * S5_STATE), F32)],
        compiler_params=_MIX_PARAMS,
        name="s5_mix",
    )(us, *tables)


def _conv_mix(zs, per_channel, *, batch):
    n_rows = zs.shape[1]
    slab = pl.BlockSpec((CONV_HEAD_DIM, n_rows, LANES), lambda h: (h, 0, 0))
    per_ch = pl.BlockSpec((CONV_HEAD_DIM, _PC_COUNT, LANES), lambda h: (h, 0, 0))
    return pl.pallas_call(
        functools.partial(_conv_mix_kernel, batch=batch),
        grid=(CONV_HEADS,),
        in_specs=[slab, per_ch],
        out_specs=slab,
        out_shape=jax.ShapeDtypeStruct((CONV_WIDTH, n_rows, LANES), F32),
        compiler_params=_MIX_PARAMS,
        name="conv_mix",
    )(zs, per_channel)


def _out_ffn2_norm(x1, ys, yc, wglu_t, bglu, wo, g2, wg, wu, wd, gf):
    batch, seq, _ = x1.shape
    return pl.pallas_call(
        _out_ffn2_norm_kernel,
        grid=(seq // CHUNK, batch // TILE_SEQS),
        in_specs=[
            _tile_spec(),
            _slab_spec(S5_WIDTH),
            _slab_spec(CONV_WIDTH),
            _resident((S5_WIDTH, S5_WIDTH)),
            _resident((S5_WIDTH, 1)),
            _resident((S5_WIDTH + CONV_WIDTH, D_MODEL)),
            _resident((1, D_MODEL)),
            _resident((D_MODEL, D_FF)),
            _resident((D_MODEL, D_FF)),
            _resident((D_FF, D_MODEL)),
            _resident((1, D_MODEL)),
        ],
        out_specs=_tile_spec(),
        out_shape=jax.ShapeDtypeStruct((batch, seq, D_MODEL), F32),
        compiler_params=_DENSE_PARAMS,
        name="out_ffn2_norm",
    )(x1, ys, yc, wglu_t, bglu, wo, g2, wg, wu, wd, gf)


def _s5_tables(lam_re, lam_im, log_dt, b_re, b_im, c_re, c_im, d_skip):
    T, C, P, G = S5_T, S5_GROUP_CH, S5_STATE, S5_GROUPS
    dt = jnp.exp(log_dt)[:, None]
    mag = jnp.exp(lam_re * dt)
    abar_re = mag * jnp.cos(lam_im * dt)
    abar_im = mag * jnp.sin(lam_im * dt)
    den = lam_re * lam_re + lam_im * lam_im
    num_re = abar_re - 1.0
    num_im = abar_im
    f_re = ((num_re * lam_re + num_im * lam_im) / den)[..., None]
    f_im = ((num_im * lam_re - num_re * lam_im) / den)[..., None]
    bb_re = f_re * b_re - f_im * b_im
    bb_im = f_re * b_im + f_im * b_re
    n_q = T // SUBLANES
    expo = jnp.concatenate([jnp.arange(SUBLANES), SUBLANES * jnp.arange(1, n_q + 1)]).astype(F32)[None, :, None]
    bmag = jnp.exp(expo * (lam_re * dt)[:, None, :])
    bang = expo * (lam_im * dt)[:, None, :]
    base_re, base_im = bmag * jnp.cos(bang), bmag * jnp.sin(bang)
    lo_re, lo_im = base_re[:, :SUBLANES], base_im[:, :SUBLANES]
    hi_re = jnp.concatenate([base_re[:, :1], base_re[:, SUBLANES:]], axis=1)
    hi_im = jnp.concatenate([base_im[:, :1], base_im[:, SUBLANES:]], axis=1)

    def outer(q_re, q_im, r_re, r_im):
        re = q_re[:, :, None] * r_re[:, None] - q_im[:, :, None] * r_im[:, None]
        im = q_re[:, :, None] * r_im[:, None] + q_im[:, :, None] * r_re[:, None]
        return jnp.concatenate([re, im], axis=-1).reshape(G, -1, 2 * P)

    cat = lambda a, b: jnp.concatenate([a, b], axis=-1)
    powers = jnp.concatenate([
        outer(hi_re[:, n_q - 1::-1], hi_im[:, n_q - 1::-1], lo_re[:, ::-1], lo_im[:, ::-1]),
        outer(hi_re, hi_im, lo_re, lo_im)], axis=1)
    bt_re = bb_re.transpose(0, 2, 1)
    bt_im = bb_im.transpose(0, 2, 1)
    d_rows = jnp.pad(jnp.repeat(d_skip.reshape(G, C // 2, 2), T, axis=-1), ((0, 0), (0, C // 2), (0, 0)))
    cp_re = jnp.concatenate([c_re[:, 0::2], c_re[:, 1::2]], axis=1)
    cp_im = jnp.concatenate([c_im[:, 0::2], c_im[:, 1::2]], axis=1)
    vecs = [None] * _V_COUNT
    vecs[_V_BR] = cat(bt_re, bt_re)
    vecs[_V_BI_NP] = cat(-bt_im, bt_im)
    vecs[_V_BI_PN] = cat(bt_im, -bt_im)
    vecs[_V_BI_PP] = cat(bt_im, bt_im)
    vecs[_V_CR_PN] = cat(c_re, -c_re)
    vecs[_V_CI_NN] = cat(-c_im, -c_im)
    vecs[_V_C_RE_NIM] = cat(cp_re, -cp_im)
    vecs[_V_C_NIM_NRE] = cat(-cp_im, -cp_re)
    vecs[_V_D] = d_rows
    return powers, jnp.stack(vecs, axis=1)


def _conv_tables(w_dw, b_dw, ln_g, ln_b):
    taps = jnp.pad(w_dw[::-1].T, ((0, 0), (0, LANES - CONV_K)))
    lanes = lambda v: jnp.broadcast_to(v[:, None], (CONV_WIDTH, LANES))
    rows = [None] * _PC_COUNT
    rows[_PC_TAPS], rows[_PC_BIAS] = taps, lanes(b_dw)
    rows[_PC_HALF_GAIN], rows[_PC_HALF_BETA] = lanes(0.5 * ln_g), lanes(0.5 * ln_b)
    return jnp.stack(rows, axis=1)


def kernel(x, ffn1_norm, ffn1_w_gate, ffn1_w_up, ffn1_w_down, mix_norm, w_in, s5_lam_re, s5_lam_im, s5_log_dt, s5_b_re, s5_b_im, s5_c_re, s5_c_im, s5_d, s5_w_glu, s5_b_glu, conv_w_dw, conv_b_dw, conv_ln_g, conv_ln_b, w_out, ffn2_norm, ffn2_w_gate, ffn2_w_up, ffn2_w_down, final_norm):
    batch, seq, d_model = x.shape
    assert ffn1_norm.shape[0] == 1 and d_model == D_MODEL
    l = 0
    row = lambda v: v.reshape(1, -1)

    x1, us, zs, wg2, wu2, wd2, wo = _ffn1_in_proj(
        x, row(ffn1_norm[l]), ffn1_w_gate[l].astype(BF16), ffn1_w_up[l].astype(BF16),
        ffn1_w_down[l].astype(BF16), row(mix_norm[l]), w_in[l].T.astype(BF16),
        (ffn2_w_gate[l], ffn2_w_up[l], ffn2_w_down[l], w_out[l]))

    tables = _s5_tables(
        s5_lam_re[l], s5_lam_im[l], s5_log_dt[l], s5_b_re[l], s5_b_im[l], s5_c_re[l], s5_c_im[l], s5_d[l])
    ys = _s5_mix(us, tables, batch=batch)

    yc = _conv_mix(zs, _conv_tables(conv_w_dw[l], conv_b_dw[l], conv_ln_g[l], conv_ln_b[l]), batch=batch)

    return _out_ffn2_norm(
        x1, ys, yc, s5_w_glu[l].T.astype(BF16), s5_b_glu[l].reshape(-1, 1), wo,
        row(ffn2_norm[l]), wg2, wu2, wd2, row(final_norm))
```

```python
import functools

import jax
import jax.numpy as jnp
from jax import lax
from jax.experimental import pallas as pl
from jax.experimental.pallas import tpu as pltpu

D_MODEL = 1024
D_FF = 2816
S5_WIDTH = 512
S5_GROUP_CH = 16
S5_GROUPS = S5_WIDTH // S5_GROUP_CH
S5_STATE = 64
CONV_WIDTH = 512
CONV_HEAD_DIM = 64
CONV_HEADS = CONV_WIDTH // CONV_HEAD_DIM
CONV_K = 31
IN_COLS = S5_WIDTH + 2 * CONV_WIDTH
EPS = 1e-6

LANES = 128
SUBLANES = 8
CHUNK = LANES
S5_T = 64
S5_HALVES = CHUNK // S5_T
S5_PAIRS = S5_GROUP_CH // 2
S5_LANES = S5_GROUP_CH * S5_T
S5_STEP_GROUPS = 2
TILE_SEQS = 8
ROW_TILE = TILE_SEQS * CHUNK
FF_SPLITS = ((0, 1536), (1536, 2816))
LN_ROWS = 64
LN_UNROLL = 32
VMEM_LIMIT = 60 * 1024 * 1024

BF16 = jnp.bfloat16
F32 = jnp.float32

(_V_BR,
 _V_BI_NP,
 _V_BI_PN,
 _V_BI_PP,
 _V_CR_PN,
 _V_CI_NN,
 _V_C_RE_NIM,
 _V_C_NIM_NRE,
 _V_D,
 _V_COUNT) = range(10)

_PC_TAPS, _PC_BIAS, _PC_HALF_GAIN, _PC_HALF_BETA, _PC_COUNT = range(5)

_GELU_C = 0.7978845608028654
_GELU_C3 = _GELU_C * 0.044715

_NT = (((1,), (1,)), ((), ()))
_TN = (((0,), (0,)), ((), ()))


def _rms_norm(x, g):
    return x * lax.rsqrt(jnp.mean(x * x, axis=-1, keepdims=True) + EPS) * g


def _dot(a, b):
    return jnp.dot(a, b, preferred_element_type=F32)


def _rms_split(x, g):
    return (x * g).astype(BF16), lax.rsqrt(jnp.mean(x * x, axis=-1, keepdims=True) + EPS)


def _swiglu(h, row_scale, wg_ref, wu_ref, wd_ref):
    acc = None
    for lo, hi in FF_SPLITS:
        g = _dot(h, wg_ref[:, lo:hi]) * row_scale
        u = _dot(h, wu_ref[:, lo:hi]) * row_scale
        a = (g * jax.nn.sigmoid(g) * u).astype(BF16)
        part = _dot(a, wd_ref[lo:hi, :])
        acc = part if acc is None else acc + part
    return acc


def _store_slabs(slab_ref, val, first_seq):
    n_ch = slab_ref.shape[0]
    rows = slab_ref.reshape(n_ch * SUBLANES, LANES)
    for ct in range(n_ch // SUBLANES):
        for s in range(TILE_SEQS):
            rows[pl.ds(ct * SUBLANES * SUBLANES + first_seq + s, SUBLANES, stride=SUBLANES), :] = (
                val[ct * SUBLANES:(ct + 1) * SUBLANES, s * LANES:(s + 1) * LANES])


def _load_slabs(slab_ref, first_seq):
    n_ch = slab_ref.shape[0]
    rows = slab_ref.reshape(n_ch * SUBLANES, LANES)
    bands = []
    for ct in range(n_ch // SUBLANES):
        bands.append(jnp.concatenate(
            [rows[pl.ds(ct * SUBLANES * SUBLANES + first_seq + s, SUBLANES, stride=SUBLANES), :]
             for s in range(TILE_SEQS)], axis=1))
    return jnp.concatenate(bands, axis=0)


def _ffn1_in_proj_kernel(x_ref, g1_ref, wg_ref, wu_ref, wd_ref, gm_ref, win_t_ref, *rest):
    n_cast = (len(rest) - 3) // 2
    x1_ref, us_ref, zs_ref = rest[n_cast:n_cast + 3]
    first_seq = pl.program_id(1) * TILE_SEQS
    x = x_ref[...].reshape(ROW_TILE, D_MODEL)
    h, r = _rms_split(x, g1_ref[...])
    x1 = x + 0.5 * _swiglu(h, r, wg_ref, wu_ref, wd_ref)
    x1_ref[...] = x1.reshape(TILE_SEQS, CHUNK, D_MODEL)
    h2, r2 = _rms_split(x1, gm_ref[...])
    r2_lanes = jnp.transpose(jnp.broadcast_to(r2, (ROW_TILE, LANES)))[0:1, :]

    def in_proj_t(lo, hi):
        return lax.dot_general(win_t_ref[lo:hi, :], h2, _NT, preferred_element_type=F32) * r2_lanes

    gate = jax.nn.sigmoid(in_proj_t(S5_WIDTH + CONV_WIDTH, IN_COLS))
    _store_slabs(us_ref, in_proj_t(0, S5_WIDTH), first_seq)
    z_t = in_proj_t(S5_WIDTH, S5_WIDTH + CONV_WIDTH) * gate
    _store_slabs(zs_ref, z_t, first_seq)
    for src_ref, dst_ref in zip(rest[:n_cast], rest[n_cast + 3:]):
        dst_ref[...] = src_ref[...].astype(BF16)


def _regroup_halves(a, b):
    first_half = lax.broadcasted_iota(jnp.int32, (1, LANES), 1) < S5_T
    straight = jnp.where(first_half, a, b)
    crossed = pltpu.roll(jnp.where(first_half, b, a), S5_T, 1)
    return jnp.where(first_half, straight, crossed), jnp.where(first_half, crossed, straight)


def _s5_mix_kernel(us_ref, pow_ref, vec_ref, ys_ref, *scratch, batch):
    for g in range(S5_STEP_GROUPS):
        ch = pl.ds(g * S5_GROUP_CH, S5_GROUP_CH)
        _s5_group(us_ref.at[ch], pow_ref.at[g], vec_ref.at[g], ys_ref.at[ch], *scratch, batch=batch)


def _s5_group(us_ref, pow_ref, vec_ref, ys_ref,
              cb_ref, f_ref, upk_ref, lhs_ref, win_ref, wout_ref, sprev_ref, *, batch):
    T = S5_T
    n_slab_rows = us_ref.shape[1]
    n_slab_chunks = n_slab_rows // batch
    n_rows = S5_HALVES * n_slab_rows
    lane = lax.broadcasted_iota(jnp.int32, (1, LANES), 1)
    first_state_half = lane < S5_STATE
    swap = lambda v: pltpu.roll(v, S5_STATE, 1)
    pa_rev = pow_ref[0:T, :]
    pb_rev = swap(pa_rev)
    pa_fwd = pow_ref[T + 1:2 * T + 1, :]
    pb_fwd = swap(pa_fwd)
    for c in range(S5_GROUP_CH):
        rows = slice(c * T, (c + 1) * T)
        vec = lambda i: vec_ref[i, c:c + 1, :]
        win_ref[rows, 0:2 * S5_STATE] = (pa_rev * vec(_V_BR) + pb_rev * vec(_V_BI_NP)).astype(BF16)
        win_ref[rows, 2 * S5_STATE:] = (pb_rev * vec(_V_BR) + pa_rev * vec(_V_BI_PN)).astype(BF16)
        wout_ref[rows, :] = (pa_fwd * vec(_V_CR_PN) + pb_fwd * vec(_V_CI_NN)).astype(BF16)
        cb_ref[c * S5_GROUP_CH:(c + 1) * S5_GROUP_CH, :] = (
            vec_ref[_V_C_RE_NIM] * vec(_V_BR) + vec_ref[_V_C_NIM_NRE] * vec(_V_BI_PP))
    f = lax.dot_general(cb_ref[...], pow_ref[T:2 * T, :], _NT, precision=lax.Precision.HIGHEST,
                        preferred_element_type=F32)
    for c in range(S5_GROUP_CH):
        even, odd = c * S5_GROUP_CH, c * S5_GROUP_CH + S5_PAIRS
        f_ref[c * S5_PAIRS:(c + 1) * S5_PAIRS, :] = jnp.concatenate(
            [f[even:even + S5_PAIRS], f[odd:odd + S5_PAIRS]], axis=1)

    for p in range(S5_PAIRS):
        halves = _regroup_halves(us_ref[2 * p], us_ref[2 * p + 1])
        for hh in range(S5_HALVES):
            upk_ref[:, hh * batch:(hh + 1) * batch, p * LANES:(p + 1) * LANES] = (
                halves[hh].reshape(n_slab_chunks, batch, LANES))
    lhs_ref[...] = upk_ref[...].reshape(n_rows, S5_LANES).astype(BF16)

    e = _dot(lhs_ref[...], win_ref[...])
    a_chunk = pow_ref[2 * T:2 * T + 1, :]
    a_sw = swap(a_chunk)
    a1 = jnp.where(first_state_half, a_chunk, a_sw)
    a2 = jnp.where(first_state_half, -a_sw, a_chunk)
    s = jnp.zeros((batch, 2 * S5_STATE), F32)
    s_sw = jnp.zeros((batch, 2 * S5_STATE), F32)
    for c in range(n_rows // batch):
        rows = slice(c * batch, (c + 1) * batch)
        sprev_ref[rows, :] = s
        e_c = e[rows, :]
        s, s_sw = (a1 * s + a2 * s_sw + e_c[:, :2 * S5_STATE],
                   a1 * s_sw - a2 * s + e_c[:, 2 * S5_STATE:])
    sprev = sprev_ref[...].astype(BF16)

    causal = ((lax.broadcasted_iota(jnp.int32, (T, LANES), 1) & (T - 1))
              >= lax.broadcasted_iota(jnp.int32, (T, LANES), 0))

    def toeplitz(ci, pair):
        f_row = jnp.broadcast_to(f_ref[ci * S5_PAIRS + pair:ci * S5_PAIRS + pair + 1, :], (T, LANES))
        return jnp.where(causal, pltpu.roll(f_row, 0, 1, stride=1, stride_axis=0), 0.0).astype(BF16)

    ci_per_tile = 2 * LANES // T
    for n in range(S5_PAIRS // 2):
        acc = lax.dot_general(sprev, wout_ref[2 * n * LANES:(2 * n + 2) * LANES, :], _NT,
                              preferred_element_type=F32)
        for kt in range(S5_GROUP_CH // ci_per_tile):
            w = jnp.concatenate(
                [jnp.concatenate([toeplitz(ci_per_tile * kt + r, 2 * n), toeplitz(ci_per_tile * kt + r, 2 * n + 1)],
                                 axis=1) for r in range(ci_per_tile)], axis=0)
            acc = acc + _dot(lhs_ref[:, 2 * kt * LANES:(2 * kt + 2) * LANES], w)
        for q in range(2):
            pair = 2 * n + q
            u_pk = upk_ref[:, :, pair * LANES:(pair + 1) * LANES].reshape(n_rows, LANES)
            y = acc[:, q * LANES:(q + 1) * LANES] + vec_ref[_V_D, pair:pair + 1, :] * u_pk
            half = 0.5 * y
            y = (half + half * jnp.tanh(y * (_GELU_C + _GELU_C3 * (y * y)))).reshape(
                n_slab_chunks, S5_HALVES * batch, LANES)
            y0 = y[:, 0:batch, :].reshape(n_slab_rows, LANES)
            y1 = y[:, batch:2 * batch, :].reshape(n_slab_rows, LANES)
            ys_ref[2 * pair], ys_ref[2 * pair + 1] = _regroup_halves(y0, y1)


def _conv_mix_kernel(zs_ref, pc_ref, o_ref, *, batch):
    n_rows = zs_ref.shape[1]
    row = lax.broadcasted_iota(jnp.int32, (CHUNK, CHUNK), 0)
    col = lax.broadcasted_iota(jnp.int32, (CHUNK, CHUNK), 1)
    lag = row - col
    band_cur = (lag >= -(CONV_K - 1)) & (lag <= 0)
    band_prev = lag >= CHUNK - (CONV_K - 1)

    def conv_channel(ch, carry):
        z = zs_ref[ch]
        z_prev = jnp.concatenate([jnp.zeros((batch, CHUNK), F32), z[:n_rows - batch]], axis=0)
        lhs = jnp.concatenate([z_prev, z], axis=1).astype(BF16)
        taps = jnp.broadcast_to(pc_ref[ch, _PC_TAPS:_PC_TAPS + 1, :], (CHUNK, CHUNK))
        diag = pltpu.roll(taps, 0, 1, stride=1, stride_axis=0)
        toe = jnp.concatenate([jnp.where(band_prev, diag, 0.0), jnp.where(band_cur, diag, 0.0)],
                              axis=0).astype(BF16)
        o_ref[ch] = _dot(lhs, toe) + pc_ref[ch, _PC_BIAS:_PC_BIAS + 1, :]
        return carry

    lax.fori_loop(0, CONV_HEAD_DIM, conv_channel, 0, unroll=True)

    inv_n = 1.0 / CONV_HEAD_DIM
    for r0 in range(0, n_rows, LN_ROWS):
        rows = pl.ds(r0, LN_ROWS)
        zero = jnp.zeros((LN_ROWS, CHUNK), F32)
        mu = lax.fori_loop(0, CONV_HEAD_DIM, lambda ch, acc: acc + o_ref[ch, rows, :], zero,
                           unroll=LN_UNROLL) * inv_n
        var = lax.fori_loop(0, CONV_HEAD_DIM, lambda ch, acc: acc + jnp.square(o_ref[ch, rows, :] - mu),
                            zero, unroll=LN_UNROLL) * inv_n
        rstd = lax.rsqrt(var + EPS)

        def norm_channel(ch, carry):
            half = ((o_ref[ch, rows, :] - mu) * rstd * pc_ref[ch, _PC_HALF_GAIN:_PC_HALF_GAIN + 1, :]
                    + pc_ref[ch, _PC_HALF_BETA:_PC_HALF_BETA + 1, :])
            o_ref[ch, rows, :] = half + half * jnp.tanh(half)
            return carry

        lax.fori_loop(0, CONV_HEAD_DIM, norm_channel, 0, unroll=LN_UNROLL)


def _out_ffn2_norm_kernel(x1_ref, ys_ref, yc_ref, wglu_t_ref, bglu_ref, wo_ref,
                          g2_ref, wg_ref, wu_ref, wd_ref, gf_ref, o_ref):
    first_seq = pl.program_id(1) * TILE_SEQS
    ys_t = _load_slabs(ys_ref, first_seq)
    yc_t = _load_slabs(yc_ref, first_seq)
    x2 = x1_ref[...].reshape(ROW_TILE, D_MODEL) + lax.dot_general(
        yc_t.astype(BF16), wo_ref[S5_WIDTH:, :], _TN, preferred_element_type=F32)
    gate_t = jax.nn.sigmoid(_dot(wglu_t_ref[...], ys_t.astype(BF16)) + bglu_ref[...])
    x2 = x2 + lax.dot_general((ys_t * gate_t).astype(BF16), wo_ref[0:S5_WIDTH, :], _TN,
                              preferred_element_type=F32)
    h, r = _rms_split(x2, g2_ref[...])
    x3 = x2 + 0.5 * _swiglu(h, r, wg_ref, wu_ref, wd_ref)
    o_ref[...] = _rms_norm(x3, gf_ref[...]).reshape(TILE_SEQS, CHUNK, D_MODEL)


def _resident(shape):
    nd = len(shape)
    return pl.BlockSpec(shape, lambda *_: (0,) * nd, pipeline_mode=pl.Buffered(1))


def _tile_spec():
    return pl.BlockSpec((TILE_SEQS, CHUNK, D_MODEL), lambda c, h: (h, c, 0))


def _slab_spec(n_ch):
    return pl.BlockSpec((n_ch, SUBLANES, LANES), lambda c, h: (0, c, 0))


_DENSE_PARAMS = pltpu.CompilerParams(dimension_semantics=("arbitrary", "arbitrary"),
                                     vmem_limit_bytes=VMEM_LIMIT)
_MIX_PARAMS = pltpu.CompilerParams(dimension_semantics=("arbitrary",), vmem_limit_bytes=VMEM_LIMIT)


def _cast_spec(w, n_steps):
    n_blocks = n_steps
    while w.shape[0] % n_blocks or (w.shape[0] // n_blocks) % (2 * SUBLANES):
        n_blocks //= 2
    steps_per_block = n_steps // n_blocks
    return pl.BlockSpec((w.shape[0] // n_blocks, w.shape[1]), lambda c, h: (c // steps_per_block, 0))


def _ffn1_in_proj(x, g1, wg, wu, wd, gm, win_t, later_weights):
    batch, seq, _ = x.shape
    assert batch == SUBLANES and seq % CHUNK == 0
    n_rows = (seq // CHUNK) * batch
    cast_specs = [_cast_spec(w, seq // CHUNK) for w in later_weights]
    return pl.pallas_call(
        _ffn1_in_proj_kernel,
        grid=(seq // CHUNK, batch // TILE_SEQS),
        in_specs=[
            _tile_spec(),
            _resident((1, D_MODEL)),
            _resident((D_MODEL, D_FF)),
            _resident((D_MODEL, D_FF)),
            _resident((D_FF, D_MODEL)),
            _resident((1, D_MODEL)),
            _resident((IN_COLS, D_MODEL)),
        ] + cast_specs,
        out_specs=[_tile_spec(), _slab_spec(S5_WIDTH), _slab_spec(CONV_WIDTH)] + cast_specs,
        out_shape=[
            jax.ShapeDtypeStruct((batch, seq, D_MODEL), F32),
            jax.ShapeDtypeStruct((S5_WIDTH, n_rows, LANES), F32),
            jax.ShapeDtypeStruct((CONV_WIDTH, n_rows, LANES), F32),
        ] + [jax.ShapeDtypeStruct(w.shape, BF16) for w in later_weights],
        compiler_params=_DENSE_PARAMS,
        name="ffn1_in_proj",
    )(x, g1, wg, wu, wd, gm, win_t, *later_weights)


def _s5_mix(us, tables, *, batch):
    n_rows = us.shape[1]
    slab = pl.BlockSpec((S5_STEP_GROUPS * S5_GROUP_CH, n_rows, LANES), lambda g: (g, 0, 0))

    def per_group(t):
        nd = t.ndim - 1
        return pl.BlockSpec((S5_STEP_GROUPS,) + t.shape[1:], lambda g: (g,) + (0,) * nd)

    return pl.pallas_call(
        functools.partial(_s5_mix_kernel, batch=batch),
        grid=(S5_GROUPS // S5_STEP_GROUPS,),
        in_specs=[slab] + [per_group(t) for t in tables],
        out_specs=slab,
        out_shape=jax.ShapeDtypeStruct((S5_WIDTH, n_rows, LANES), F32),
        scratch_shapes=[pltpu.VMEM((S5_GROUP_CH * S5_GROUP_CH, 2 * S5_STATE), F32),
                        pltpu.VMEM((S5_GROUP_CH * S5_PAIRS, LANES), F32),
                        pltpu.VMEM((n_rows // batch, S5_HALVES * batch, S5_LANES), F32),
                        pltpu.VMEM((S5_HALVES * n_rows, S5_LANES), BF16),
                        pltpu.VMEM((S5_LANES, 4 * S5_STATE), BF16),
                        pltpu.VMEM((S5_LANES, 2 * S5_STATE), BF16),
                        pltpu.VMEM((S5_HALVES * n_rows, 2 * S5_STATE), F32)],
        compiler_params=_MIX_PARAMS,
        name="s5_mix",
    )(us, *tables)


def _conv_mix(zs, per_channel, *, batch):
    n_rows = zs.shape[1]
    slab = pl.BlockSpec((CONV_HEAD_DIM, n_rows, LANES), lambda h: (h, 0, 0))
    per_ch = pl.BlockSpec((CONV_HEAD_DIM, _PC_COUNT, LANES), lambda h: (h, 0, 0))
    return pl.pallas_call(
        functools.partial(_conv_mix_kernel, batch=batch),
        grid=(CONV_HEADS,),
        in_specs=[slab, per_ch],
        out_specs=slab,
        out_shape=jax.ShapeDtypeStruct((CONV_WIDTH, n_rows, LANES), F32),
        compiler_params=_MIX_PARAMS,
        name="conv_mix",
    )(zs, per_channel)


def _out_ffn2_norm(x1, ys, yc, wglu_t, bglu, wo, g2, wg, wu, wd, gf):
    batch, seq, _ = x1.shape
    return pl.pallas_call(
        _out_ffn2_norm_kernel,
        grid=(seq // CHUNK, batch // TILE_SEQS),
        in_specs=[
            _tile_spec(),
            _slab_spec(S5_WIDTH),
            _slab_spec(CONV_WIDTH),
            _resident((S5_WIDTH, S5_WIDTH)),
            _resident((S5_WIDTH, 1)),
            _resident((S5_WIDTH + CONV_WIDTH, D_MODEL)),
            _resident((1, D_MODEL)),
            _resident((D_MODEL, D_FF)),
            _resident((D_MODEL, D_FF)),
            _resident((D_FF, D_MODEL)),
            _resident((1, D_MODEL)),
        ],
        out_specs=_tile_spec(),
        out_shape=jax.ShapeDtypeStruct((batch, seq, D_MODEL), F32),
        compiler_params=_DENSE_PARAMS,
        name="out_ffn2_norm",
    )(x1, ys, yc, wglu_t, bglu, wo, g2, wg, wu, wd, gf)


def _s5_tables(lam_re, lam_im, log_dt, b_re, b_im, c_re, c_im, d_skip):
    T, C, P, G = S5_T, S5_GROUP_CH, S5_STATE, S5_GROUPS
    dt = jnp.exp(log_dt)[:, None]
    mag = jnp.exp(lam_re * dt)
    abar_re = mag * jnp.cos(lam_im * dt)
    abar_im = mag * jnp.sin(lam_im * dt)
    den = lam_re * lam_re + lam_im * lam_im
    num_re = abar_re - 1.0
    num_im = abar_im
    f_re = ((num_re * lam_re + num_im * lam_im) / den)[..., None]
    f_im = ((num_im * lam_re - num_re * lam_im) / den)[..., None]
    bb_re = f_re * b_re - f_im * b_im
    bb_im = f_re * b_im + f_im * b_re
    n_q = T // SUBLANES
    expo = jnp.concatenate([jnp.arange(SUBLANES), SUBLANES * jnp.arange(1, n_q + 1)]).astype(F32)[None, :, None]
    bmag = jnp.exp(expo * (lam_re * dt)[:, None, :])
    bang = expo * (lam_im * dt)[:, None, :]
    base_re, base_im = bmag * jnp.cos(bang), bmag * jnp.sin(bang)
    lo_re, lo_im = base_re[:, :SUBLANES], base_im[:, :SUBLANES]
    hi_re = jnp.concatenate([base_re[:, :1], base_re[:, SUBLANES:]], axis=1)
    hi_im = jnp.concatenate([base_im[:, :1], base_im[:, SUBLANES:]], axis=1)

    def outer(q_re, q_im, r_re, r_im):
        re = q_re[:, :, None] * r_re[:, None] - q_im[:, :, None] * r_im[:, None]
        im = q_re[:, :, None] * r_im[:, None] + q_im[:, :, None] * r_re[:, None]
        return jnp.concatenate([re, im], axis=-1).reshape(G, -1, 2 * P)

    cat = lambda a, b: jnp.concatenate([a, b], axis=-1)
    powers = jnp.concatenate([
        outer(hi_re[:, n_q - 1::-1], hi_im[:, n_q - 1::-1], lo_re[:, ::-1], lo_im[:, ::-1]),
        outer(hi_re, hi_im, lo_re, lo_im)], axis=1)
    bt_re = bb_re.transpose(0, 2, 1)
    bt_im = bb_im.transpose(0, 2, 1)
    d_rows = jnp.pad(jnp.repeat(d_skip.reshape(G, C // 2, 2), T, axis=-1), ((0, 0), (0, C // 2), (0, 0)))
    cp_re = jnp.concatenate([c_re[:, 0::2], c_re[:, 1::2]], axis=1)
    cp_im = jnp.concatenate([c_im[:, 0::2], c_im[:, 1::2]], axis=1)
    vecs = [None] * _V_COUNT
    vecs[_V_BR] = cat(bt_re, bt_re)
    vecs[_V_BI_NP] = cat(-bt_im, bt_im)
    vecs[_V_BI_PN] = cat(bt_im, -bt_im)
    vecs[_V_BI_PP] = cat(bt_im, bt_im)
    vecs[_V_CR_PN] = cat(c_re, -c_re)
    vecs[_V_CI_NN] = cat(-c_im, -c_im)
    vecs[_V_C_RE_NIM] = cat(cp_re, -cp_im)
    vecs[_V_C_NIM_NRE] = cat(-cp_im, -cp_re)
    vecs[_V_D] = d_rows
    return powers, jnp.stack(vecs, axis=1)


def _conv_tables(w_dw, b_dw, ln_g, ln_b):
    taps = jnp.pad(w_dw[::-1].T, ((0, 0), (0, LANES - CONV_K)))
    lanes = lambda v: jnp.broadcast_to(v[:, None], (CONV_WIDTH, LANES))
    rows = [None] * _PC_COUNT
    rows[_PC_TAPS], rows[_PC_BIAS] = taps, lanes(b_dw)
    rows[_PC_HALF_GAIN], rows[_PC_HALF_BETA] = lanes(0.5 * ln_g), lanes(0.5 * ln_b)
    return jnp.stack(rows, axis=1)


def kernel(x, ffn1_norm, ffn1_w_gate, ffn1_w_up, ffn1_w_down, mix_norm, w_in, s5_lam_re, s5_lam_im, s5_log_dt, s5_b_re, s5_b_im, s5_c_re, s5_c_im, s5_d, s5_w_glu, s5_b_glu, conv_w_dw, conv_b_dw, conv_ln_g, conv_ln_b, w_out, ffn2_norm, ffn2_w_gate, ffn2_w_up, ffn2_w_down, final_norm):
    batch, seq, d_model = x.shape
    assert ffn1_norm.shape[0] == 1 and d_model == D_MODEL
    l = 0
    row = lambda v: v.reshape(1, -1)

    x1, us, zs, wg2, wu2, wd2, wo = _ffn1_in_proj(
        x, row(ffn1_norm[l]), ffn1_w_gate[l].astype(BF16), ffn1_w_up[l].astype(BF16),
        ffn1_w_down[l].astype(BF16), row(mix_norm[l]), w_in[l].T.astype(BF16),
        (ffn2_w_gate[l], ffn2_w_up[l], ffn2_w_down[l], w_out[l]))

    tables = _s5_tables(
        s5_lam_re[l], s5_lam_im[l], s5_log_dt[l], s5_b_re[l], s5_b_im[l], s5_c_re[l], s5_c_im[l], s5_d[l])
    ys = _s5_mix(us, tables, batch=batch)

    yc = _conv_mix(zs, _conv_tables(conv_w_dw[l], conv_b_dw[l], conv_ln_g[l], conv_ln_b[l]), batch=batch)

    return _out_ffn2_norm(
        x1, ys, yc, s5_w_glu[l].T.astype(BF16), s5_b_glu[l].reshape(-1, 1), wo,
        row(ffn2_norm[l]), wg2, wu2, wd2, row(final_norm))
```
